```python
import jax, jax.numpy as jnp
from jax import lax
import numpy as np

D_MODEL = 1024
BATCH = 32
SEQ = 256
DEPTH = 1
DEC_BATCH = 8
DEC_SEQ = 1024
PAST_LEN = 256

GRID_W = 64
D_MIX = D_MODEL
CONV_CH = D_MIX // 2
RWKV_DIM = D_MIX - CONV_CH
HEAD_DIM = 64
RWKV_HEADS = RWKV_DIM // HEAD_DIM
CONV_WIDTH = 31
DECAY_LORA = 64
ICLR_LORA = 64
GATE_LORA = 128
N_EXPERTS = 32
TOP_K = 4
D_FF = D_MODEL
SWIGLU_LIMIT = 7.0
SWIGLU_ALPHA = 1.702
ROUTE_BLOCK = 128
RMS_EPS = 1e-6
LN_EPS = 1e-5
GN_EPS = 64e-5
N_MOD = 6
IN_COLS = 2 * CONV_CH + 3 * RWKV_DIM

kernel_name = 'hybrid_conv_rwkv7_moe_diffusion_step'


def rms_norm(x, g):
    xf = x.astype(jnp.float32)
    y = xf * lax.rsqrt(jnp.mean(xf * xf, axis=-1, keepdims=True) + RMS_EPS)
    return (y * g.astype(jnp.float32)).astype(x.dtype)


def layer_norm(x, g, b):
    xf = x.astype(jnp.float32)
    mu = jnp.mean(xf, axis=-1, keepdims=True)
    xc = xf - mu
    var = jnp.mean(xc * xc, axis=-1, keepdims=True)
    return (xc * lax.rsqrt(var + LN_EPS) * g.astype(jnp.float32) + b.astype(jnp.float32)).astype(x.dtype)


def conv_module(glu, conv_w, conv_b, ln_g, ln_b, n_seg):
    b, t, ch = glu.shape
    u = glu.reshape(b * n_seg, t // n_seg, ch)
    u = lax.conv_general_dilated(
        u, conv_w[:, None, :].astype(u.dtype), window_strides=(1,),
        padding=[(CONV_WIDTH // 2, CONV_WIDTH // 2)],
        dimension_numbers=('NWC', 'WIO', 'NWC'), feature_group_count=ch)
    u = (u + conv_b).reshape(b, t, ch)
    return jax.nn.silu(layer_norm(u, ln_g, ln_b))


def wkv_scan(r, decay, a_vec, b_vec, v, k, s0, reverse):
    def step(S, inp):
        r_t, w_t, a_t, b_t, v_t, k_t = inp
        sa = jnp.einsum('bhij,bhj->bhi', S, a_t)
        S = S * w_t[:, :, None, :] + sa[..., None] * b_t[:, :, None, :] + v_t[..., None] * k_t[:, :, None, :]
        return S, jnp.einsum('bhij,bhj->bhi', S, r_t)
    xs = tuple(jnp.moveaxis(z, 1, 0) for z in (r, decay, a_vec, b_vec, v, k))
    s_fin, ys = lax.scan(step, s0.astype(jnp.float32), xs, reverse=reverse)
    return jnp.moveaxis(ys, 0, 1), s_fin


def rwkv_group(h, r, k, v, s0_f, s0_b, p):
    b, t, _ = h.shape
    f32 = jnp.float32
    shp = (b, t, RWKV_HEADS, HEAD_DIM)
    rh = r.astype(f32).reshape(shp)
    kh = k.astype(f32).reshape(shp)
    vh = v.astype(f32).reshape(shp)
    kk = (k * p['rw_k_k']).astype(f32).reshape(shp)
    kk = kk * lax.rsqrt(jnp.maximum(jnp.sum(kk * kk, axis=-1, keepdims=True), 1e-24))
    ka = p['rw_k_a'].astype(f32).reshape(RWKV_HEADS, HEAD_DIM)
    rk = p['rw_r_k'].astype(f32)
    ys, bonus, finals = [], [], []
    for d, (s0, rev) in enumerate(((s0_f, False), (s0_b, True))):
        w_raw = p['rw_w0'][d] + jnp.tanh(h @ p['rw_w1'][d]) @ p['rw_w2'][d]
        w_log = -jax.nn.softplus(-w_raw.astype(f32)) - 0.5
        decay = jnp.exp(-jnp.exp(w_log)).reshape(shp)
        a = jax.nn.sigmoid((p['rw_a0'][d] + (h @ p['rw_a1'][d]) @ p['rw_a2'][d]).astype(f32)).reshape(shp)
        k_d = kh * (1.0 + (a - 1.0) * ka)
        y, s_fin = wkv_scan(rh, decay, -kk, kk * a, vh, k_d, s0, rev)
        ys.append(y)
        bonus.append(jnp.sum(rh * k_d * rk, axis=-1, keepdims=True) * vh)
        finals.append(s_fin)
    o = ys[0] + ys[1]
    mu = jnp.mean(o, axis=-1, keepdims=True)
    oc = o - mu
    o = oc * lax.rsqrt(jnp.mean(oc * oc, axis=-1, keepdims=True) + GN_EPS)
    o = o.reshape(b, t, RWKV_DIM) * p['rw_gn_g'].astype(f32) + p['rw_gn_b'].astype(f32)
    o = o + (bonus[0] + bonus[1]).reshape(b, t, RWKV_DIM)
    g = jax.nn.sigmoid(h @ p['rw_g1']) @ p['rw_g2']
    return (o * g.astype(f32)).astype(h.dtype), finals[0], finals[1]


def routed_ffn(h, p):
    shp = h.shape
    tok = h.reshape(-1, D_MODEL)
    n_tok = tok.shape[0]
    logits = (tok @ p['router_w'] + p['router_b']).astype(jnp.float32)
    top_val, top_idx = lax.top_k(logits, TOP_K)
    gate = jax.nn.softmax(top_val, axis=-1)
    n_assign = n_tok * TOP_K
    flat_e = top_idx.reshape(n_assign).astype(jnp.int32)
    flat_tok = jnp.arange(n_assign, dtype=jnp.int32) // TOP_K
    order = jnp.argsort(flat_e)
    e_sorted = flat_e[order]
    tok_sorted = flat_tok[order]
    gate_sorted = gate.reshape(n_assign)[order]
    counts = jnp.bincount(flat_e, length=N_EXPERTS).astype(jnp.int32)
    padded = (counts + ROUTE_BLOCK - 1) // ROUTE_BLOCK * ROUTE_BLOCK
    pad_end = jnp.cumsum(padded)
    pad_start = pad_end - padded
    start = jnp.cumsum(counts) - counts
    dest = pad_start[e_sorted] + (jnp.arange(n_assign, dtype=jnp.int32) - start[e_sorted])
    n_blocks = -(-n_assign // ROUTE_BLOCK) + N_EXPERTS
    tok_buf = jnp.full((n_blocks * ROUTE_BLOCK,), n_tok, jnp.int32).at[dest].set(tok_sorted)
    gate_buf = jnp.zeros((n_blocks * ROUTE_BLOCK,), jnp.float32).at[dest].set(gate_sorted)
    blk_e = jnp.searchsorted(pad_end, jnp.arange(n_blocks, dtype=jnp.int32) * ROUTE_BLOCK, side='right')
    blk_e = jnp.minimum(blk_e, N_EXPERTS - 1).astype(jnp.int32)
    tok_pad = jnp.concatenate([tok, jnp.zeros((1, D_MODEL), tok.dtype)], axis=0)

    def expert_block(args):
        idx, g, e = args
        xb = tok_pad[idx]
        gu = xb @ p['w_gu'][e] + p['b_gu'][e]
        x_glu, x_lin = gu[:, :D_FF], gu[:, D_FF:]
        x_glu = jnp.minimum(x_glu, SWIGLU_LIMIT)
        x_lin = jnp.clip(x_lin, -SWIGLU_LIMIT, SWIGLU_LIMIT)
        act = x_glu * jax.nn.sigmoid(SWIGLU_ALPHA * x_glu) * (x_lin + 1.0)
        out = act @ p['w_down'][e] + p['b_down'][e]
        return out.astype(jnp.float32) * g[:, None]

    outs = lax.map(expert_block, (tok_buf.reshape(n_blocks, ROUTE_BLOCK),
                                  gate_buf.reshape(n_blocks, ROUTE_BLOCK), blk_e))
    y = jax.ops.segment_sum(outs.reshape(n_blocks * ROUTE_BLOCK, D_MODEL), tok_buf,
                            num_segments=n_tok + 1)[:n_tok]
    return y.astype(h.dtype).reshape(shp)


def trunk_layer(x, mod, s0_f, s0_b, n_seg, p):
    sh1, sc1, gt1, sh2, sc2, gt2 = [mod[:, i][:, None, :] for i in range(N_MOD)]
    h = rms_norm(x, p['pre_mix_g']) * (1.0 + sc1) + sh1
    proj = h @ p['w_in']
    c_val, c_gate, r, k, v = jnp.split(
        proj, [CONV_CH, 2 * CONV_CH, 2 * CONV_CH + RWKV_DIM, 2 * CONV_CH + 2 * RWKV_DIM], axis=-1)
    conv_out = conv_module(c_val * jax.nn.sigmoid(c_gate), p['conv_w'], p['conv_b'],
                           p['conv_ln_g'], p['conv_ln_b'], n_seg)
    rwkv_out, s_f, s_b = rwkv_group(h, r, k, v, s0_f, s0_b, p)
    mix = jnp.concatenate([conv_out, rwkv_out], axis=-1) @ p['w_out']
    x = x + gt1 * rms_norm(mix, p['post_mix_g'])
    h2 = rms_norm(x, p['pre_ffn_g']) * (1.0 + sc2) + sh2
    x = x + gt2 * rms_norm(routed_ffn(h2, p), p['post_ffn_g'])
    return x, s_f, s_b


def _normal(key, shape, scale):
    return scale * jax.random.normal(key, shape, jnp.float32)


def setup_inputs(seed: int = 0) -> dict:
    key = jax.random.key(seed)
    ks = jax.random.split(key, 37)
    L, D = DEPTH, D_MODEL
    st_shape = (DEC_BATCH, DEPTH, RWKV_HEADS, HEAD_DIM, HEAD_DIM)
    return {
        'x_prompt': _normal(ks[0], (BATCH, SEQ, D), 1.0),
        'x_sample': _normal(ks[1], (DEC_BATCH, DEC_SEQ, D), 1.0),
        'state_wkv_fwd': _normal(ks[2], st_shape, 0.3),
        'state_wkv_bwd': _normal(ks[3], st_shape, 0.3),
        'c': _normal(ks[4], (DEC_BATCH, D), 1.0),
        'c_ctx': _normal(ks[5], (D,), 1.0),
        'ada_w': _normal(ks[6], (L, D, N_MOD * D), 0.5 * D ** -0.5),
        'ada_b': _normal(ks[7], (L, N_MOD * D), 0.02),
        'pre_mix_g': 1.0 + _normal(ks[8], (L, D), 0.02),
        'post_mix_g': 1.0 + _normal(ks[9], (L, D), 0.02),
        'pre_ffn_g': 1.0 + _normal(ks[10], (L, D), 0.02),
        'post_ffn_g': 1.0 + _normal(ks[11], (L, D), 0.02),
        'w_in': _normal(ks[12], (L, D, IN_COLS), D ** -0.5),
        'w_out': _normal(ks[13], (L, D_MIX, D), D_MIX ** -0.5),
        'conv_w': _normal(ks[14], (L, CONV_WIDTH, CONV_CH), CONV_WIDTH ** -0.5),
        'conv_b': _normal(ks[15], (L, CONV_CH), 0.02),
        'conv_ln_g': 1.0 + _normal(ks[16], (L, CONV_CH), 0.02),
        'conv_ln_b': _normal(ks[17], (L, CONV_CH), 0.02),
        'rw_w0': 0.5 + _normal(ks[18], (L, 2, RWKV_DIM), 0.5),
        'rw_w1': _normal(ks[19], (L, 2, D, DECAY_LORA), D ** -0.5),
        'rw_w2': _normal(ks[20], (L, 2, DECAY_LORA, RWKV_DIM), 0.5 * DECAY_LORA ** -0.5),
        'rw_a0': _normal(ks[21], (L, 2, RWKV_DIM), 0.3),
        'rw_a1': _normal(ks[22], (L, 2, D, ICLR_LORA), D ** -0.5),
        'rw_a2': _normal(ks[23], (L, 2, ICLR_LORA, RWKV_DIM), 0.5 * ICLR_LORA ** -0.5),
        'rw_g1': _normal(ks[24], (L, D, GATE_LORA), D ** -0.5),
        'rw_g2': _normal(ks[25], (L, GATE_LORA, RWKV_DIM), GATE_LORA ** -0.5),
        'rw_k_k': 0.85 + _normal(ks[26], (L, RWKV_DIM), 0.05),
        'rw_k_a': 1.0 + _normal(ks[27], (L, RWKV_DIM), 0.05),
        'rw_r_k': _normal(ks[28], (L, RWKV_HEADS, HEAD_DIM), 0.1),
        'rw_gn_g': 1.0 + _normal(ks[29], (L, RWKV_DIM), 0.02),
        'rw_gn_b': _normal(ks[30], (L, RWKV_DIM), 0.02),
        'router_w': _normal(ks[31], (L, D, N_EXPERTS), D ** -0.5),
        'router_b': _normal(ks[32], (L, N_EXPERTS), 0.01),
        'w_gu': _normal(ks[33], (L, N_EXPERTS, D, 2 * D_FF), D ** -0.5),
        'b_gu': _normal(ks[34], (L, N_EXPERTS, 2 * D_FF), 0.02),
        'w_down': _normal(ks[35], (L, N_EXPERTS, D_FF, D), D_FF ** -0.5),
        'b_down': _normal(ks[36], (L, N_EXPERTS, D), 0.02),
    }


def reference(x_prompt, x_sample, state_wkv_fwd, state_wkv_bwd, c, c_ctx, ada_w, ada_b,
              pre_mix_g, post_mix_g, pre_ffn_g, post_ffn_g, w_in, w_out, conv_w, conv_b,
              conv_ln_g, conv_ln_b, rw_w0, rw_w1, rw_w2, rw_a0, rw_a1, rw_a2, rw_g1, rw_g2,
              rw_k_k, rw_k_a, rw_r_k, rw_gn_g, rw_gn_b, router_w, router_b, w_gu, b_gu,
              w_down, b_down):
    rows = x_sample.shape[1] // GRID_W
    zero_state = jnp.zeros((x_prompt.shape[0], RWKV_HEADS, HEAD_DIM, HEAD_DIM), jnp.float32)
    y_prompt, y_sample = x_prompt, x_sample
    new_f, new_b = [], []
    for l in range(DEPTH):
        p = {
            'pre_mix_g': pre_mix_g[l], 'post_mix_g': post_mix_g[l],
            'pre_ffn_g': pre_ffn_g[l], 'post_ffn_g': post_ffn_g[l],
            'w_in': w_in[l], 'w_out': w_out[l],
            'conv_w': conv_w[l], 'conv_b': conv_b[l],
            'conv_ln_g': conv_ln_g[l], 'conv_ln_b': conv_ln_b[l],
            'rw_w0': rw_w0[l], 'rw_w1': rw_w1[l], 'rw_w2': rw_w2[l],
            'rw_a0': rw_a0[l], 'rw_a1': rw_a1[l], 'rw_a2': rw_a2[l],
            'rw_g1': rw_g1[l], 'rw_g2': rw_g2[l],
            'rw_k_k': rw_k_k[l], 'rw_k_a': rw_k_a[l], 'rw_r_k': rw_r_k[l],
            'rw_gn_g': rw_gn_g[l], 'rw_gn_b': rw_gn_b[l],
            'router_w': router_w[l], 'router_b': router_b[l],
            'w_gu': w_gu[l], 'b_gu': b_gu[l], 'w_down': w_down[l], 'b_down': b_down[l],
        }
        mod_ctx = (jax.nn.silu(c_ctx) @ ada_w[l] + ada_b[l]).reshape(1, N_MOD, D_MODEL)
        mod_lat = (jax.nn.silu(c) @ ada_w[l] + ada_b[l]).reshape(c.shape[0], N_MOD, D_MODEL)
        y_prompt, s_f, s_b = trunk_layer(y_prompt, mod_ctx, zero_state, zero_state, 1, p)
        new_f.append(s_f)
        new_b.append(s_b)
        y_sample, _, _ = trunk_layer(y_sample, mod_lat, state_wkv_fwd[:, l], state_wkv_bwd[:, l], rows, p)
    new_state_wkv_fwd = jnp.stack(new_f, axis=1)
    new_state_wkv_bwd = jnp.stack(new_b, axis=1)
    return (y_prompt, y_sample, new_state_wkv_fwd, new_state_wkv_bwd)
```

```python
import functools
import math

import jax
import jax.numpy as jnp
from jax import lax
from jax.experimental import pallas as pl
from jax.experimental.pallas import tpu as pltpu

F32 = jnp.float32
BF16 = jnp.bfloat16

D_MODEL = 1024
CONV_CH = 512
RWKV_DIM = 512
HEAD_DIM = 64
RWKV_HEADS = 8
N_PAIRS = RWKV_HEADS // 2
CONV_WIDTH = 31
N_EXPERTS = 32
TOP_K = 4
D_FF = 1024
SWIGLU_LIMIT = 7.0
SWIGLU_ALPHA = 1.702
RMS_EPS = 1e-6
LN_EPS = 1e-5
GN_EPS = 64e-5
N_MOD = 6
GRID_W = 64

LANES = 128
CHUNK = 64
LORA_COLS = 384
TM_IN = 512
TM_CONV = 256
TM_OUT = 256
TM_FIN = 512
MOE_GROUP = 2048
MOE_BLOCK = 128
MOE_WIN = 256
NEG_BIG = -1e30
EXP_M05 = math.exp(-0.5)
VMEM_LIMIT = 56 * 1024 * 1024


def _sigmoid(x):
    return 1.0 / (1.0 + jnp.exp(-x))


def _mm(a, b):
    return jnp.dot(a.astype(BF16), b.astype(BF16), preferred_element_type=F32)


def _mm_nt(a, b):
    return lax.dot_general(a.astype(BF16), b.astype(BF16), (((1,), (1,)), ((), ())),
                           preferred_element_type=F32)


def _mm_tn(a, b):
    return lax.dot_general(a.astype(BF16), b.astype(BF16), (((0,), (0,)), ((), ())),
                           preferred_element_type=F32)


def _split2(x):
    hi = x.astype(BF16)
    lo = (x - hi.astype(F32)).astype(BF16)
    return hi, lo


def _split3(x):
    h1 = x.astype(BF16)
    r1 = x - h1.astype(F32)
    h2 = r1.astype(BF16)
    h3 = (r1 - h2.astype(F32)).astype(BF16)
    return h1, h2, h3


def _rms(x, g):
    return x * lax.rsqrt(jnp.mean(x * x, axis=-1, keepdims=True) + RMS_EPS) * g


def _params(*sem):
    return pltpu.CompilerParams(dimension_semantics=sem, vmem_limit_bytes=VMEM_LIMIT)


def _mod_kernel(c_ref, w_ref, b_ref, o_ref):
    c = c_ref[...]
    s1, s2, s3 = _split3(c * _sigmoid(c))
    w1, w2, w3 = _split3(w_ref[...])
    dot = functools.partial(jnp.dot, preferred_element_type=F32)
    acc = dot(s1, w1) + (dot(s1, w2) + dot(s2, w1)) + (dot(s2, w2) + dot(s1, w3) + dot(s3, w1))
    o_ref[...] = acc + b_ref[...]


def _mod_call(c_rows, ada_w, ada_b):
    m, d = c_rows.shape
    n = ada_w.shape[1]
    tn = 512
    return pl.pallas_call(
        _mod_kernel,
        grid=(n // tn,),
        in_specs=[pl.BlockSpec((m, d), lambda j: (0, 0)),
                  pl.BlockSpec((d, tn), lambda j: (0, j)),
                  pl.BlockSpec((1, tn), lambda j: (0, j))],
        out_specs=pl.BlockSpec((m, tn), lambda j: (0, j)),
        out_shape=jax.ShapeDtypeStruct((m, n), F32),
        compiler_params=_params("arbitrary"),
        name="mod",
    )(c_rows, ada_w, ada_b.reshape(1, n))


def _in_kernel(x_ref, mod_ref, g_ref, w_ref, cv_ref, rkv_ref, lo_ref):
    m = mod_ref[0]
    h = _rms(x_ref[...], g_ref[...]) * (1.0 + m[1:2]) + m[0:1]
    proj = jnp.dot(h.astype(BF16), w_ref[...], preferred_element_type=F32)
    cv_ref[...] = proj[:, :2 * CONV_CH]
    rkv_ref[...] = proj[:, 2 * CONV_CH:2 * CONV_CH + 3 * RWKV_DIM]
    lo_ref[...] = proj[:, 2 * CONV_CH + 3 * RWKV_DIM:]


def _mod_index(tile, seq_len, n_mod_rows):
    def index(i):
        return ((i * tile) // seq_len) % n_mod_rows, 0, 0
    return index


def _in_call(x, mod, g, w_all, seq_len):
    n, d = x.shape
    ncol = w_all.shape[1]
    return pl.pallas_call(
        _in_kernel,
        grid=(n // TM_IN,),
        in_specs=[pl.BlockSpec((TM_IN, d), lambda i: (i, 0)),
                  pl.BlockSpec((1, N_MOD, d), _mod_index(TM_IN, seq_len, mod.shape[0])),
                  pl.BlockSpec((1, d), lambda i: (0, 0)),
                  pl.BlockSpec((d, ncol), lambda i: (0, 0))],
        out_specs=[pl.BlockSpec((TM_IN, 2 * CONV_CH), lambda i: (i, 0)),
                   pl.BlockSpec((TM_IN, 3 * RWKV_DIM), lambda i: (i, 0)),
                   pl.BlockSpec((TM_IN, LORA_COLS), lambda i: (i, 0))],
        out_shape=[jax.ShapeDtypeStruct((n, 2 * CONV_CH), F32),
                   jax.ShapeDtypeStruct((n, 3 * RWKV_DIM), F32),
                   jax.ShapeDtypeStruct((n, LORA_COLS), F32)],
        compiler_params=_params("arbitrary"),
        name="in_proj",
    )(x, mod, g, w_all)


CONV_HALO = 16
CONV_ROWS = 64


def _conv_kernel(cv_ref, w_ref, b_ref, g_ref, be_ref, o_ref, pad_ref, acc_ref, *, seg_len):
    nseg = TM_CONV // seg_len
    stride = seg_len + 2 * CONV_HALO
    cv = cv_ref[...]
    u = cv[:, :CONV_CH] * _sigmoid(cv[:, CONV_CH:])
    pad_ref[...] = jnp.zeros(pad_ref.shape, F32)
    for s in range(nseg):
        pad_ref[s * stride + CONV_HALO:s * stride + CONV_HALO + seg_len, :] = (
            u[s * seg_len:(s + 1) * seg_len])
    first = CONV_HALO - CONV_WIDTH // 2
    for s in range(nseg):
        for rc in range(seg_len // CONV_ROWS):
            for lc in range(CONV_CH // LANES):
                cols = slice(lc * LANES, (lc + 1) * LANES)
                acc = jnp.zeros((CONV_ROWS, LANES), F32)
                for t in range(CONV_WIDTH):
                    base = s * stride + first + t + rc * CONV_ROWS
                    acc = acc + w_ref[t:t + 1, cols] * pad_ref[base:base + CONV_ROWS, cols]
                row0 = s * seg_len + rc * CONV_ROWS
                acc_ref[row0:row0 + CONV_ROWS, cols] = acc
    y = acc_ref[...] + b_ref[...]
    mu = jnp.mean(y, axis=-1, keepdims=True)
    yc = y - mu
    var = jnp.mean(yc * yc, axis=-1, keepdims=True)
    z = yc * lax.rsqrt(var + LN_EPS) * g_ref[...] + be_ref[...]
    o_ref[...] = z * _sigmoid(z)


def _conv_call(cv, conv_w, conv_b, ln_g, ln_b, seg_len):
    n = cv.shape[0]
    nseg = TM_CONV // seg_len
    row = lambda i: (i, 0)
    const = lambda i: (0, 0)
    return pl.pallas_call(
        functools.partial(_conv_kernel, seg_len=seg_len),
        grid=(n // TM_CONV,),
        in_specs=[pl.BlockSpec((TM_CONV, 2 * CONV_CH), row),
                  pl.BlockSpec((CONV_WIDTH, CONV_CH), const),
                  pl.BlockSpec((1, CONV_CH), const),
                  pl.BlockSpec((1, CONV_CH), const),
                  pl.BlockSpec((1, CONV_CH), const)],
        out_specs=pl.BlockSpec((TM_CONV, CONV_CH), row),
        out_shape=jax.ShapeDtypeStruct((n, CONV_CH), F32),
        scratch_shapes=[pltpu.VMEM((nseg * (seg_len + 2 * CONV_HALO), CONV_CH), F32),
                        pltpu.VMEM((TM_CONV, CONV_CH), F32)],
        compiler_params=_params("arbitrary"),
        name="conv_module",
    )(cv, conv_w, conv_b, ln_g, ln_b)


def _head_sums(x, ones_bd):
    hi, lo = _split2(x)
    return (jnp.dot(hi, ones_bd, preferred_element_type=F32)
            + jnp.dot(lo, ones_bd, preferred_element_type=F32))


def _wkv_masks():
    c = CHUNK
    r2 = lax.broadcasted_iota(jnp.int32, (2 * c, 2 * c), 0)
    c2 = lax.broadcasted_iota(jnp.int32, (2 * c, 2 * c), 1)
    same = (r2 // c) == (c2 // c)
    rr, cc = r2 % c, c2 % c
    r1 = lax.broadcasted_iota(jnp.int32, (c, 2 * c), 0)
    c1 = lax.broadcasted_iota(jnp.int32, (c, 2 * c), 1)
    cc1 = c1 % c
    rs = lax.broadcasted_iota(jnp.int32, (c, c), 0)
    cs = lax.broadcasted_iota(jnp.int32, (c, c), 1)
    masks = {"same": same, "eye": r2 == c2, "left": c1 < c,
             "ones_bd": same.astype(BF16)}
    for name, lt in (("f", lambda a, b: a < b), ("b", lambda a, b: a > b)):
        le = (lambda a, b, lt=lt: lt(a, b) | (a == b))
        masks["strict_" + name] = same & lt(cc, rr)
        masks["inc_" + name] = le(cc1, r1)
        masks["ak0_" + name] = (c1 >= c) & lt(cc1, r1)
        masks["ak1_" + name] = (c1 < c) & lt(cc1, r1)
        masks["tri_" + name] = le(cs, rs).astype(BF16)
    return masks


def _wkv_pair(r, lw_terms, a, b, v, k, st, mk, d):
    c = CHUNK
    logp, lw = lw_terms
    tot = logp[0:1] if d == "b" else logp[c - 1:c]
    at = a * jnp.exp(logp - lw)
    rt = r * jnp.exp(logp)
    e_inv = jnp.exp(-logp)
    bt, kt = b * e_inv, k * e_inv
    e_hat = jnp.exp(tot - logp)
    bh, kh = b * e_hat, k * e_hat
    left = mk["left"]
    zero = jnp.zeros_like(at)
    at0, at1 = jnp.where(left, at, zero), jnp.where(left, zero, at)
    rt0, rt1 = jnp.where(left, rt, zero), jnp.where(left, zero, rt)
    g0 = _mm_nt(jnp.concatenate([at0, rt0], axis=0), jnp.concatenate([bt, kt], axis=0))
    g1 = _mm_nt(jnp.concatenate([at1, rt1], axis=0), jnp.concatenate([kt, bt], axis=0))
    lmat = jnp.where(mk["strict_" + d], jnp.concatenate([g0[:c], g1[:c]], axis=0), 0.0)
    tinv = jnp.where(mk["eye"], 1.0, lmat)
    lpow = lmat
    for _ in range(5):
        lpow = _mm(lpow, lpow)
        tinv = tinv + _mm(lpow, tinv)
    ak0 = jnp.where(mk["ak0_" + d], g0[:c], 0.0)
    ak1 = jnp.where(mk["ak1_" + d], g1[:c], 0.0)
    zv = jnp.zeros_like(v)
    x0 = _mm(ak0, jnp.concatenate([zv, v], axis=0))
    x1 = _mm(ak1, jnp.concatenate([v, zv], axis=0))
    rhs = jnp.concatenate([
        jnp.concatenate([at0, jnp.where(left, x0, zero)], axis=1),
        jnp.concatenate([at1, jnp.where(left, zero, x1)], axis=1)], axis=0)
    aw_blk = _mm(tinv, rhs)
    aw = aw_blk[:c] + aw_blk[c:]
    inc = mk["inc_" + d]
    rb = jnp.where(inc, jnp.where(left, g0[c:], g1[c:]), 0.0)
    rk = jnp.where(inc, jnp.where(left, g1[c:], g0[c:]), 0.0)
    v0, v1 = jnp.where(left, v, zv), jnp.where(left, zv, v)
    vblk = jnp.concatenate([jnp.zeros((2 * c, LANES), F32),
                            jnp.concatenate([v1, v0], axis=0)], axis=1)
    qz = _mm(jnp.concatenate([rb, rk], axis=1), jnp.concatenate([aw_blk, vblk], axis=0))
    q = rt + qz[:, :LANES]
    zz = qz[:, LANES:]
    rhs3 = jnp.concatenate([aw, jnp.concatenate([zv, v], axis=1)], axis=0)
    mn = _mm_tn(jnp.concatenate([bh, kh], axis=0), rhs3)
    same = mk["same"]
    mt = jnp.where(same, mn[:, :LANES], 0.0) + jnp.where(mk["eye"], jnp.exp(tot), 0.0)
    nt = jnp.where(same, mn[:, LANES:], 0.0)
    y = _mm(q, st) + zz
    st_new = _mm(mt, st) + nt
    return y, st_new


def _wkv_kernel(rkv_f_ref, lo_f_ref, rkv_b_ref, lo_b_ref, s0f_ref, s0b_ref,
                w2_ref, w0_ref, a2_ref, a0_ref, kk_ref, ka_ref,
                yf_ref, yb_ref, sf_ref, sb_ref, st_ref):
    ci = pl.program_id(1)
    nc = pl.num_programs(1)

    @pl.when(ci == 0)
    def _():
        st_ref[0] = s0f_ref[0]
        st_ref[1] = s0b_ref[0]

    mk = _wkv_masks()
    for di, (d, rkv_ref, lo_ref, y_ref) in enumerate(
            (("f", rkv_f_ref, lo_f_ref, yf_ref), ("b", rkv_b_ref, lo_b_ref, yb_ref))):
        rkv = rkv_ref[...]
        lo = lo_ref[...]
        r = rkv[:, :RWKV_DIM]
        k = rkv[:, RWKV_DIM:2 * RWKV_DIM]
        v = rkv[:, 2 * RWKV_DIM:]
        w_raw = w0_ref[di:di + 1, :] + _mm(jnp.tanh(lo[:, :LANES]), w2_ref[di])
        lw = -EXP_M05 * _sigmoid(w_raw)
        ag = _sigmoid(a0_ref[di:di + 1, :] + _mm(lo[:, LANES:2 * LANES], a2_ref[di]))
        kd = k * (1.0 + (ag - 1.0) * ka_ref[...])
        kk_raw = k * kk_ref[...]
        h1, h2, h3 = _split3(lw)
        cs = jnp.dot(mk["tri_" + d], jnp.concatenate([h1, h2, h3], axis=1),
                     preferred_element_type=F32)
        logp = cs[:, :RWKV_DIM] + cs[:, RWKV_DIM:2 * RWKV_DIM] + cs[:, 2 * RWKV_DIM:]
        for p in range(N_PAIRS):
            cols = slice(p * LANES, (p + 1) * LANES)
            kkp = kk_raw[:, cols]
            ss = _head_sums(kkp * kkp, mk["ones_bd"])
            kkp = kkp * lax.rsqrt(jnp.maximum(ss, 1e-24))
            y, st_new = _wkv_pair(r[:, cols], (logp[:, cols], lw[:, cols]), -kkp,
                                  kkp * ag[:, cols], v[:, cols], kd[:, cols],
                                  st_ref[di, p], mk, d)
            y_ref[:, cols] = y
            st_ref[di, p] = st_new

    @pl.when(ci == nc - 1)
    def _():
        sf_ref[0] = st_ref[0]
        sb_ref[0] = st_ref[1]


def _wkv_call(rkv, lora, s0f, s0b, w2p, w0, a2p, a0, k_k, k_a, batch, seq_len):
    n = rkv.shape[0]
    nc = seq_len // CHUNK
    fwd = lambda b, c: (b * nc + c, 0)
    bwd = lambda b, c: (b * nc + (nc - 1 - c), 0)
    st = lambda b, c: (b, 0, 0, 0)
    c2 = lambda b, c: (0, 0)
    c3 = lambda b, c: (0, 0, 0)
    st_shape = jax.ShapeDtypeStruct((batch, N_PAIRS, LANES, LANES), F32)
    return pl.pallas_call(
        _wkv_kernel,
        grid=(batch, nc),
        in_specs=[pl.BlockSpec((CHUNK, 3 * RWKV_DIM), fwd),
                  pl.BlockSpec((CHUNK, LORA_COLS), fwd),
                  pl.BlockSpec((CHUNK, 3 * RWKV_DIM), bwd),
                  pl.BlockSpec((CHUNK, LORA_COLS), bwd),
                  pl.BlockSpec((1, N_PAIRS, LANES, LANES), st),
                  pl.BlockSpec((1, N_PAIRS, LANES, LANES), st),
                  pl.BlockSpec((2, LANES, RWKV_DIM), c3),
                  pl.BlockSpec((2, RWKV_DIM), c2),
                  pl.BlockSpec((2, LANES, RWKV_DIM), c3),
                  pl.BlockSpec((2, RWKV_DIM), c2),
                  pl.BlockSpec((1, RWKV_DIM), c2),
                  pl.BlockSpec((1, RWKV_DIM), c2)],
        out_specs=[pl.BlockSpec((CHUNK, RWKV_DIM), fwd),
                   pl.BlockSpec((CHUNK, RWKV_DIM), bwd),
                   pl.BlockSpec((1, N_PAIRS, LANES, LANES), st),
                   pl.BlockSpec((1, N_PAIRS, LANES, LANES), st)],
        out_shape=[jax.ShapeDtypeStruct((n, RWKV_DIM), F32),
                   jax.ShapeDtypeStruct((n, RWKV_DIM), F32), st_shape, st_shape],
        scratch_shapes=[pltpu.VMEM((2, N_PAIRS, LANES, LANES), F32)],
        compiler_params=_params("arbitrary", "arbitrary"),
        name="wkv_chunked",
    )(rkv, lora, rkv, lora, s0f, s0b, w2p, w0, a2p, a0, k_k, k_a)


def _out_kernel(x_ref, mod_ref, conv_ref, yf_ref, yb_ref, rkv_ref, lo_ref,
                a2_ref, a0_ref, ka_ref, rk_ref, gng_ref, gnb_ref, g2_ref, wout_ref,
                postg_ref, preg_ref, rw_ref, rb_ref,
                x1_ref, h2_ref, rank_ref, gate_ref, cnt_ref, *, group_tiles):
    i = pl.program_id(0)

    @pl.when(i % group_tiles == 0)
    def _():
        cnt_ref[...] = jnp.zeros(cnt_ref.shape, F32)

    m = mod_ref[0]
    rkv = rkv_ref[...]
    lo = lo_ref[...]
    r = rkv[:, :RWKV_DIM]
    k = rkv[:, RWKV_DIM:2 * RWKV_DIM]
    v = rkv[:, 2 * RWKV_DIM:]
    ha = lo[:, LANES:2 * LANES]
    ag_f = _sigmoid(a0_ref[0:1, :] + _mm(ha, a2_ref[0]))
    ag_b = _sigmoid(a0_ref[1:2, :] + _mm(ha, a2_ref[1]))
    rkk = r * k * rk_ref[...] * (2.0 + (ag_f + ag_b - 2.0) * ka_ref[...])
    o = yf_ref[...] + yb_ref[...]
    gate = _mm(_sigmoid(lo[:, 2 * LANES:]), g2_ref[...])
    r2 = lax.broadcasted_iota(jnp.int32, (LANES, LANES), 0)
    c2 = lax.broadcasted_iota(jnp.int32, (LANES, LANES), 1)
    ones_bd = ((r2 // HEAD_DIM) == (c2 // HEAD_DIM)).astype(BF16)
    parts = []
    for p in range(N_PAIRS):
        cols = slice(p * LANES, (p + 1) * LANES)
        op = o[:, cols]
        mu = _head_sums(op, ones_bd) * (1.0 / HEAD_DIM)
        oc = op - mu
        var = _head_sums(oc * oc, ones_bd) * (1.0 / HEAD_DIM)
        on = oc * lax.rsqrt(var + GN_EPS) * gng_ref[:, cols] + gnb_ref[:, cols]
        bonus = _head_sums(rkk[:, cols], ones_bd) * v[:, cols]
        parts.append((on + bonus) * gate[:, cols])
    mix_in = jnp.concatenate([conv_ref[...]] + parts, axis=1)
    mix = jnp.dot(mix_in.astype(BF16), wout_ref[...], preferred_element_type=F32)
    x1 = x_ref[...] + m[2:3] * _rms(mix, postg_ref[...])
    x1_ref[...] = x1
    h2 = _rms(x1, preg_ref[...]) * (1.0 + m[4:5]) + m[3:4]
    h2_ref[...] = h2.astype(BF16)
    hh, hl = _split2(h2)
    dot = functools.partial(jnp.dot, preferred_element_type=F32)
    logits = dot(hh, rw_ref[0]) + (dot(hh, rw_ref[1]) + dot(hl, rw_ref[0])) + rb_ref[...]
    lane = lax.broadcasted_iota(jnp.int32, logits.shape, 1)
    work = logits
    picks, vals = [], []
    for _ in range(TOP_K):
        mx = jnp.max(work, axis=-1, keepdims=True)
        idx = jnp.min(jnp.where(work == mx, lane, LANES), axis=-1, keepdims=True)
        pick = lane == idx
        picks.append(pick)
        vals.append(mx)
        work = jnp.where(pick, 2.0 * NEG_BIG, work)
    exps = [jnp.exp(val - vals[0]) for val in vals]
    den = exps[0] + exps[1] + exps[2] + exps[3]
    gate_d = jnp.zeros(logits.shape, F32)
    sel = jnp.zeros(logits.shape, F32)
    for pick, e in zip(picks, exps):
        gate_d = jnp.where(pick, e / den, gate_d)
        sel = jnp.where(pick, 1.0, sel)
    tr = lax.broadcasted_iota(jnp.int32, (TM_OUT, TM_OUT), 0)
    tc = lax.broadcasted_iota(jnp.int32, (TM_OUT, TM_OUT), 1)
    before = (tc < tr).astype(BF16)
    rank = dot(before, sel.astype(BF16)) + cnt_ref[...]
    cnt_ref[...] = cnt_ref[...] + jnp.sum(sel, axis=0, keepdims=True)
    rank_ref[...] = jnp.where(sel > 0.0, rank, -1.0).T
    gate_ref[...] = gate_d.T


def _out_call(x, mod, conv_out, yf, yb, rkv, lora, wts, seq_len):
    n, d = x.shape
    row = lambda i: (i, 0)
    c2 = lambda i: (0, 0)
    c3 = lambda i: (0, 0, 0)
    col = lambda i: (0, i)
    full = lambda a: pl.BlockSpec(a.shape, c3 if a.ndim == 3 else c2)
    consts = [wts["a2p"], wts["a0"], wts["k_a"], wts["r_k"], wts["gn_g"], wts["gn_b"], wts["g2"],
              wts["w_out"], wts["post_mix_g"], wts["pre_ffn_g"], wts["router_w"], wts["router_b"]]
    return pl.pallas_call(
        functools.partial(_out_kernel, group_tiles=MOE_GROUP // TM_OUT),
        grid=(n // TM_OUT,),
        in_specs=[pl.BlockSpec((TM_OUT, d), row),
                  pl.BlockSpec((1, N_MOD, d), _mod_index(TM_OUT, seq_len, mod.shape[0])),
                  pl.BlockSpec((TM_OUT, CONV_CH), row),
                  pl.BlockSpec((TM_OUT, RWKV_DIM), row),
                  pl.BlockSpec((TM_OUT, RWKV_DIM), row),
                  pl.BlockSpec((TM_OUT, 3 * RWKV_DIM), row),
                  pl.BlockSpec((TM_OUT, LORA_COLS), row)] + [full(a) for a in consts],
        out_specs=[pl.BlockSpec((TM_OUT, d), row),
                   pl.BlockSpec((TM_OUT, d), row),
                   pl.BlockSpec((LANES, TM_OUT), col),
                   pl.BlockSpec((LANES, TM_OUT), col)],
        out_shape=[jax.ShapeDtypeStruct((n, d), F32),
                   jax.ShapeDtypeStruct((n, d), BF16),
                   jax.ShapeDtypeStruct((LANES, n), F32),
                   jax.ShapeDtypeStruct((LANES, n), F32)],
        scratch_shapes=[pltpu.VMEM((1, LANES), F32)],
        compiler_params=_params("arbitrary"),
        name="out_router",
    )(x, mod, conv_out, yf, yb, rkv, lora, *consts)


N_WIN = MOE_GROUP // MOE_WIN


def _moe_kernel(cw_ref, h2_ref, rank_ref, gate_ref, wgu_ref, bgu_ref, wd_ref, bd_ref,
                y_ref, xs_ref, gs_ref):
    g = pl.program_id(0)
    e = pl.program_id(1)

    @pl.when(e == 0)
    def _():
        y_ref[...] = jnp.zeros(y_ref.shape, F32)

    cbase = (g * N_EXPERTS + e) * (N_WIN + 1)
    n_rows = cw_ref[cbase + N_WIN]
    n_blk = (n_rows + MOE_BLOCK - 1) // MOE_BLOCK
    slot = lax.broadcasted_iota(jnp.int32, (MOE_BLOCK, MOE_WIN), 0).astype(F32)

    def block(b, carry):
        base = b * MOE_BLOCK
        basef = base.astype(F32)
        xs_ref[...] = jnp.zeros(xs_ref.shape, F32)
        gs_ref[...] = jnp.zeros(gs_ref.shape, F32)

        def window_hits(w):
            return (cw_ref[cbase + w + 1] > base) & (cw_ref[cbase + w] < base + MOE_BLOCK)

        for w in range(N_WIN):
            toks = slice(w * MOE_WIN, (w + 1) * MOE_WIN)

            @pl.when(window_hits(w))
            def _(toks=toks):
                hit = (rank_ref[0, :, toks] - basef) == slot
                xs_ref[...] += jnp.dot(hit.astype(BF16), h2_ref[toks, :],
                                       preferred_element_type=F32)
                gs_ref[...] += jnp.sum(jnp.where(hit, gate_ref[0, :, toks], 0.0),
                                       axis=-1, keepdims=True)

        gu = jnp.dot(xs_ref[...].astype(BF16), wgu_ref[0], preferred_element_type=F32) + bgu_ref[0]
        x_glu = jnp.minimum(gu[:, :D_FF], SWIGLU_LIMIT)
        x_lin = jnp.clip(gu[:, D_FF:], -SWIGLU_LIMIT, SWIGLU_LIMIT)
        act = x_glu * _sigmoid(SWIGLU_ALPHA * x_glu) * (x_lin + 1.0)
        out = jnp.dot(act.astype(BF16), wd_ref[0], preferred_element_type=F32) + bd_ref[0]
        outg = (out * gs_ref[...]).astype(BF16)

        for w in range(N_WIN):
            toks = slice(w * MOE_WIN, (w + 1) * MOE_WIN)

            @pl.when(window_hits(w))
            def _(toks=toks):
                hit = (rank_ref[0, :, toks] - basef) == slot
                y_ref[toks, :] += lax.dot_general(hit.astype(BF16), outg,
                                                  (((0,), (0,)), ((), ())),
                                                  preferred_element_type=F32)
        return carry

    lax.fori_loop(0, n_blk, block, 0)


def _moe_call(cw, h2, rank3, gate3, wgu, bgu, wd, bd):
    n, d = h2.shape
    groups = n // MOE_GROUP
    grid_spec = pltpu.PrefetchScalarGridSpec(
        num_scalar_prefetch=1,
        grid=(groups, N_EXPERTS),
        in_specs=[pl.BlockSpec((MOE_GROUP, d), lambda g, e, cw: (g, 0)),
                  pl.BlockSpec((1, 1, MOE_GROUP), lambda g, e, cw: (e, 0, g)),
                  pl.BlockSpec((1, 1, MOE_GROUP), lambda g, e, cw: (e, 0, g)),
                  pl.BlockSpec((1, d, 2 * D_FF), lambda g, e, cw: (e, 0, 0)),
                  pl.BlockSpec((1, 1, 2 * D_FF), lambda g, e, cw: (e, 0, 0)),
                  pl.BlockSpec((1, D_FF, d), lambda g, e, cw: (e, 0, 0)),
                  pl.BlockSpec((1, 1, d), lambda g, e, cw: (e, 0, 0))],
        out_specs=pl.BlockSpec((MOE_GROUP, d), lambda g, e, cw: (g, 0)),
        scratch_shapes=[pltpu.VMEM((MOE_BLOCK, d), F32), pltpu.VMEM((MOE_BLOCK, 1), F32)])
    return pl.pallas_call(
        _moe_kernel,
        grid_spec=grid_spec,
        out_shape=jax.ShapeDtypeStruct((n, d), F32),
        compiler_params=_params("arbitrary", "arbitrary"),
        name="moe_experts",
    )(cw, h2, rank3, gate3, wgu, bgu, wd, bd)


def _final_kernel(x1_ref, y_ref, mod_ref, g_ref, o_ref):
    m = mod_ref[0]
    o_ref[...] = x1_ref[...] + m[5:6] * _rms(y_ref[...], g_ref[...])


def _final_call(x1, y, mod, g, seq_len):
    n, d = x1.shape
    row = lambda i: (i, 0)
    return pl.pallas_call(
        _final_kernel,
        grid=(n // TM_FIN,),
        in_specs=[pl.BlockSpec((TM_FIN, d), row),
                  pl.BlockSpec((TM_FIN, d), row),
                  pl.BlockSpec((1, N_MOD, d), _mod_index(TM_FIN, seq_len, mod.shape[0])),
                  pl.BlockSpec((1, d), lambda i: (0, 0))],
        out_specs=pl.BlockSpec((TM_FIN, d), row),
        out_shape=jax.ShapeDtypeStruct((n, d), F32),
        compiler_params=_params("arbitrary"),
        name="final_residual",
    )(x1, y, mod, g)


def _pack_state(s):
    b = s.shape[0]
    st = jnp.swapaxes(s.astype(F32), -1, -2).reshape(b, N_PAIRS, 2, HEAD_DIM, HEAD_DIM)
    z = jnp.zeros_like(st[:, :, 0])
    top = jnp.concatenate([st[:, :, 0], z], axis=-1)
    bot = jnp.concatenate([z, st[:, :, 1]], axis=-1)
    return jnp.concatenate([top, bot], axis=-2)


def _unpack_state(st):
    b = st.shape[0]
    h0 = st[:, :, :HEAD_DIM, :HEAD_DIM]
    h1 = st[:, :, HEAD_DIM:, HEAD_DIM:]
    s = jnp.stack([h0, h1], axis=2).reshape(b, RWKV_HEADS, HEAD_DIM, HEAD_DIM)
    return jnp.swapaxes(s, -1, -2)


def _window_counts(rank_t, n_tok):
    sel = (rank_t[:N_EXPERTS] >= 0.0).astype(jnp.int32)
    per_win = sel.reshape(N_EXPERTS, n_tok // MOE_GROUP, N_WIN, MOE_WIN).sum(-1)
    cum = jnp.cumsum(per_win, axis=-1)
    cum = jnp.concatenate([jnp.zeros_like(cum[..., :1]), cum], axis=-1)
    return jnp.transpose(cum, (1, 0, 2)).reshape(-1)


def _layer(x, mod, s0f, s0b, seg_len, wts):
    batch, seq_len, d = x.shape
    n = batch * seq_len
    x2 = x.reshape(n, d)
    cv, rkv, lora = _in_call(x2, mod, wts["pre_mix_g"], wts["w_all"], seq_len)
    conv_out = _conv_call(cv, wts["conv_w"], wts["conv_b"], wts["conv_ln_g"], wts["conv_ln_b"],
                          seg_len)
    yf, yb, sf, sb = _wkv_call(rkv, lora, s0f, s0b, wts["w2p"], wts["w0"], wts["a2p"], wts["a0"],
                               wts["k_k"], wts["k_a"], batch, seq_len)
    x1, h2, rank_t, gate_t = _out_call(x2, mod, conv_out, yf, yb, rkv, lora, wts, seq_len)
    cw = _window_counts(rank_t, n)
    rank3 = rank_t[:N_EXPERTS].reshape(N_EXPERTS, 1, n)
    gate3 = gate_t[:N_EXPERTS].reshape(N_EXPERTS, 1, n)
    y_ffn = _moe_call(cw, h2, rank3, gate3, wts["w_gu"], wts["b_gu"], wts["w_down"], wts["b_down"])
    out = _final_call(x1, y_ffn, mod, wts["post_ffn_g"], seq_len)
    return out.reshape(batch, seq_len, d), sf, sb


def _prep_weights(l, pre_mix_g, post_mix_g, pre_ffn_g, post_ffn_g, w_in, w_out, conv_w, conv_b,
                  conv_ln_g, conv_ln_b, rw_w0, rw_w1, rw_w2, rw_a0, rw_a1, rw_a2, rw_g1, rw_g2,
                  rw_k_k, rw_k_a, rw_r_k, rw_gn_g, rw_gn_b, router_w, router_b, w_gu, b_gu,
                  w_down, b_down):
    row = lambda a: a.reshape(1, -1).astype(F32)
    zpad = jnp.zeros((HEAD_DIM, RWKV_DIM), F32)
    w_all = jnp.concatenate([w_in[l], rw_w1[l, 0], rw_w1[l, 1], rw_a1[l, 0], rw_a1[l, 1],
                             rw_g1[l]], axis=1).astype(BF16)
    w2p = jnp.stack([jnp.concatenate([rw_w2[l, 0], zpad], axis=0),
                     jnp.concatenate([zpad, rw_w2[l, 1]], axis=0)]).astype(BF16)
    a2p = jnp.stack([jnp.concatenate([rw_a2[l, 0], zpad], axis=0),
                     jnp.concatenate([zpad, rw_a2[l, 1]], axis=0)]).astype(BF16)
    rw_pad = jnp.pad(router_w[l].astype(F32), ((0, 0), (0, LANES - N_EXPERTS)))
    rw_hi = rw_pad.astype(BF16)
    rw_lo = (rw_pad - rw_hi.astype(F32)).astype(BF16)
    rb_pad = jnp.concatenate([router_b[l].astype(F32),
                              jnp.full((LANES - N_EXPERTS,), NEG_BIG, F32)]).reshape(1, LANES)
    return {
        "pre_mix_g": row(pre_mix_g[l]), "post_mix_g": row(post_mix_g[l]),
        "pre_ffn_g": row(pre_ffn_g[l]), "post_ffn_g": row(post_ffn_g[l]),
        "w_all": w_all, "w_out": w_out[l].astype(BF16),
        "conv_w": conv_w[l].astype(F32), "conv_b": row(conv_b[l]),
        "conv_ln_g": row(conv_ln_g[l]), "conv_ln_b": row(conv_ln_b[l]),
        "w2p": w2p, "w0": rw_w0[l].astype(F32), "a2p": a2p, "a0": rw_a0[l].astype(F32),
        "k_k": row(rw_k_k[l]), "k_a": row(rw_k_a[l]), "r_k": row(rw_r_k[l]),
        "gn_g": row(rw_gn_g[l]), "gn_b": row(rw_gn_b[l]), "g2": rw_g2[l].astype(BF16),
        "router_w": jnp.stack([rw_hi, rw_lo]), "router_b": rb_pad,
        "w_gu": w_gu[l].astype(BF16), "b_gu": b_gu[l].reshape(N_EXPERTS, 1, -1).astype(F32),
        "w_down": w_down[l].astype(BF16), "b_down": b_down[l].reshape(N_EXPERTS, 1, -1).astype(F32),
    }


def kernel(x_prompt, x_sample, state_wkv_fwd, state_wkv_bwd, c, c_ctx, ada_w, ada_b, pre_mix_g, post_mix_g, pre_ffn_g, post_ffn_g, w_in, w_out, conv_w, conv_b, conv_ln_g, conv_ln_b, rw_w0, rw_w1, rw_w2, rw_a0, rw_a1, rw_a2, rw_g1, rw_g2, rw_k_k, rw_k_a, rw_r_k, rw_gn_g, rw_gn_b, router_w, router_b, w_gu, b_gu, w_down, b_down):
    depth = ada_w.shape[0]
    dec_batch = c.shape[0]
    mod_rows = 16
    c_rows = jnp.concatenate([c, c_ctx[None, :],
                              jnp.zeros((mod_rows - dec_batch - 1, D_MODEL), F32)], axis=0)
    zero_state = jnp.zeros((x_prompt.shape[0], N_PAIRS, LANES, LANES), F32)
    y_prompt, y_sample = x_prompt, x_sample
    new_f, new_b = [], []
    for l in range(depth):
        wts = _prep_weights(l, pre_mix_g, post_mix_g, pre_ffn_g, post_ffn_g, w_in, w_out, conv_w,
                            conv_b, conv_ln_g, conv_ln_b, rw_w0, rw_w1, rw_w2, rw_a0, rw_a1, rw_a2,
                            rw_g1, rw_g2, rw_k_k, rw_k_a, rw_r_k, rw_gn_g, rw_gn_b, router_w,
                            router_b, w_gu, b_gu, w_down, b_down)
        mod = _mod_call(c_rows, ada_w[l], ada_b[l]).reshape(mod_rows, N_MOD, D_MODEL)
        mod_lat = mod[:dec_batch]
        mod_ctx = mod[dec_batch:dec_batch + 1]
        y_prompt, s_f, s_b = _layer(y_prompt, mod_ctx, zero_state, zero_state,
                                    y_prompt.shape[1], wts)
        new_f.append(_unpack_state(s_f))
        new_b.append(_unpack_state(s_b))
        y_sample, _, _ = _layer(y_sample, mod_lat, _pack_state(state_wkv_fwd[:, l]),
                                _pack_state(state_wkv_bwd[:, l]), GRID_W, wts)
    return (y_prompt, y_sample, jnp.stack(new_f, axis=1), jnp.stack(new_b, axis=1))
```

```python
import functools
import math

import jax
import jax.numpy as jnp
from jax import lax
from jax.experimental import pallas as pl
from jax.experimental.pallas import tpu as pltpu

F32 = jnp.float32
BF16 = jnp.bfloat16

D_MODEL = 1024
CONV_CH = 512
RWKV_DIM = 512
HEAD_DIM = 64
RWKV_HEADS = 8
N_PAIRS = RWKV_HEADS // 2
CONV_WIDTH = 31
N_EXPERTS = 32
TOP_K = 4
D_FF = 1024
SWIGLU_LIMIT = 7.0
SWIGLU_ALPHA = 1.702
RMS_EPS = 1e-6
LN_EPS = 1e-5
GN_EPS = 64e-5
N_MOD = 6
GRID_W = 64

LANES = 128
CHUNK = 64
LORA_COLS = 384
TM_IN = 512
TM_CONV = 256
TM_OUT = 256
TM_FIN = 512
MOE_GROUP = 2048
MOE_BLOCK = 128
MOE_WIN = 256
NEG_BIG = -1e30
EXP_M05 = math.exp(-0.5)
VMEM_LIMIT = 56 * 1024 * 1024


def _sigmoid(x):
    return 1.0 / (1.0 + jnp.exp(-x))


def _mm(a, b):
    return jnp.dot(a.astype(BF16), b.astype(BF16), preferred_element_type=F32)


def _mm_nt(a, b):
    return lax.dot_general(a.astype(BF16), b.astype(BF16), (((1,), (1,)), ((), ())),
                           preferred_element_type=F32)


def _mm_tn(a, b):
    return lax.dot_general(a.astype(BF16), b.astype(BF16), (((0,), (0,)), ((), ())),
                           preferred_element_type=F32)


def _split2(x):
    hi = x.astype(BF16)
    lo = (x - hi.astype(F32)).astype(BF16)
    return hi, lo


def _split3(x):
    h1 = x.astype(BF16)
    r1 = x - h1.astype(F32)
    h2 = r1.astype(BF16)
    h3 = (r1 - h2.astype(F32)).astype(BF16)
    return h1, h2, h3


def _rms(x, g):
    return x * lax.rsqrt(jnp.mean(x * x, axis=-1, keepdims=True) + RMS_EPS) * g


def _params(*sem):
    return pltpu.CompilerParams(dimension_semantics=sem, vmem_limit_bytes=VMEM_LIMIT)


def _mod_kernel(c_ref, w_ref, b_ref, o_ref):
    c = c_ref[...]
    s1, s2, s3 = _split3(c * _sigmoid(c))
    w1, w2, w3 = _split3(w_ref[...])
    dot = functools.partial(jnp.dot, preferred_element_type=F32)
    acc = dot(s1, w1) + (dot(s1, w2) + dot(s2, w1)) + (dot(s2, w2) + dot(s1, w3) + dot(s3, w1))
    o_ref[...] = acc + b_ref[...]


def _mod_call(c_rows, ada_w, ada_b):
    m, d = c_rows.shape
    n = ada_w.shape[1]
    tn = 512
    return pl.pallas_call(
        _mod_kernel,
        grid=(n // tn,),
        in_specs=[pl.BlockSpec((m, d), lambda j: (0, 0)),
                  pl.BlockSpec((d, tn), lambda j: (0, j)),
                  pl.BlockSpec((1, tn), lambda j: (0, j))],
        out_specs=pl.BlockSpec((m, tn), lambda j: (0, j)),
        out_shape=jax.ShapeDtypeStruct((m, n), F32),
        compiler_params=_params("arbitrary"),
        name="mod",
    )(c_rows, ada_w, ada_b.reshape(1, n))


def _in_kernel(x_ref, mod_ref, g_ref, w_ref, cv_ref, rkv_ref, lo_ref):
    m = mod_ref[0]
    h = _rms(x_ref[...], g_ref[...]) * (1.0 + m[1:2]) + m[0:1]
    proj = jnp.dot(h.astype(BF16), w_ref[...], preferred_element_type=F32)
    cv_ref[...] = proj[:, :2 * CONV_CH]
    rkv_ref[...] = proj[:, 2 * CONV_CH:2 * CONV_CH + 3 * RWKV_DIM]
    lo_ref[...] = proj[:, 2 * CONV_CH + 3 * RWKV_DIM:]


def _mod_index(tile, seq_len, n_mod_rows):
    def index(i):
        return ((i * tile) // seq_len) % n_mod_rows, 0, 0
    return index


def _in_call(x, mod, g, w_all, seq_len):
    n, d = x.shape
    ncol = w_all.shape[1]
    return pl.pallas_call(
        _in_kernel,
        grid=(n // TM_IN,),
        in_specs=[pl.BlockSpec((TM_IN, d), lambda i: (i, 0)),
                  pl.BlockSpec((1, N_MOD, d), _mod_index(TM_IN, seq_len, mod.shape[0])),
                  pl.BlockSpec((1, d), lambda i: (0, 0)),
                  pl.BlockSpec((d, ncol), lambda i: (0, 0))],
        out_specs=[pl.BlockSpec((TM_IN, 2 * CONV_CH), lambda i: (i, 0)),
                   pl.BlockSpec((TM_IN, 3 * RWKV_DIM), lambda i: (i, 0)),
                   pl.BlockSpec((TM_IN, LORA_COLS), lambda i: (i, 0))],
        out_shape=[jax.ShapeDtypeStruct((n, 2 * CONV_CH), F32),
                   jax.ShapeDtypeStruct((n, 3 * RWKV_DIM), F32),
                   jax.ShapeDtypeStruct((n, LORA_COLS), F32)],
        compiler_params=_params("arbitrary"),
        name="in_proj",
    )(x, mod, g, w_all)


CONV_HALO = 16
CONV_ROWS = 64


def _conv_kernel(cv_ref, w_ref, b_ref, g_ref, be_ref, o_ref, pad_ref, acc_ref, *, seg_len):
    nseg = TM_CONV // seg_len
    stride = seg_len + 2 * CONV_HALO
    cv = cv_ref[...]
    u = cv[:, :CONV_CH] * _sigmoid(cv[:, CONV_CH:])
    pad_ref[...] = jnp.zeros(pad_ref.shape, F32)
    for s in range(nseg):
        pad_ref[s * stride + CONV_HALO:s * stride + CONV_HALO + seg_len, :] = (
            u[s * seg_len:(s + 1) * seg_len])
    first = CONV_HALO - CONV_WIDTH // 2
    for s in range(nseg):
        for rc in range(seg_len // CONV_ROWS):
            for lc in range(CONV_CH // LANES):
                cols = slice(lc * LANES, (lc + 1) * LANES)
                acc = jnp.zeros((CONV_ROWS, LANES), F32)
                for t in range(CONV_WIDTH):
                    base = s * stride + first + t + rc * CONV_ROWS
                    acc = acc + w_ref[t:t + 1, cols] * pad_ref[base:base + CONV_ROWS, cols]
                row0 = s * seg_len + rc * CONV_ROWS
                acc_ref[row0:row0 + CONV_ROWS, cols] = acc
    y = acc_ref[...] + b_ref[...]
    mu = jnp.mean(y, axis=-1, keepdims=True)
    yc = y - mu
    var = jnp.mean(yc * yc, axis=-1, keepdims=True)
    z = yc * lax.rsqrt(var + LN_EPS) * g_ref[...] + be_ref[...]
    o_ref[...] = z * _sigmoid(z)


def _conv_call(cv, conv_w, conv_b, ln_g, ln_b, seg_len):
    n = cv.shape[0]
    nseg = TM_CONV // seg_len
    row = lambda i: (i, 0)
    const = lambda i: (0, 0)
    return pl.pallas_call(
        functools.partial(_conv_kernel, seg_len=seg_len),
        grid=(n // TM_CONV,),
        in_specs=[pl.BlockSpec((TM_CONV, 2 * CONV_CH), row),
                  pl.BlockSpec((CONV_WIDTH, CONV_CH), const),
                  pl.BlockSpec((1, CONV_CH), const),
                  pl.BlockSpec((1, CONV_CH), const),
                  pl.BlockSpec((1, CONV_CH), const)],
        out_specs=pl.BlockSpec((TM_CONV, CONV_CH), row),
        out_shape=jax.ShapeDtypeStruct((n, CONV_CH), F32),
        scratch_shapes=[pltpu.VMEM((nseg * (seg_len + 2 * CONV_HALO), CONV_CH), F32),
                        pltpu.VMEM((TM_CONV, CONV_CH), F32)],
        compiler_params=_params("arbitrary"),
        name="conv_module",
    )(cv, conv_w, conv_b, ln_g, ln_b)


def _head_sums(x, ones_bd):
    hi, lo = _split2(x)
    return (jnp.dot(hi, ones_bd, preferred_element_type=F32)
            + jnp.dot(lo, ones_bd, preferred_element_type=F32))


def _wkv_masks():
    c = CHUNK
    r2 = lax.broadcasted_iota(jnp.int32, (2 * c, 2 * c), 0)
    c2 = lax.broadcasted_iota(jnp.int32, (2 * c, 2 * c), 1)
    same = (r2 // c) == (c2 // c)
    rr, cc = r2 % c, c2 % c
    r1 = lax.broadcasted_iota(jnp.int32, (c, 2 * c), 0)
    c1 = lax.broadcasted_iota(jnp.int32, (c, 2 * c), 1)
    cc1 = c1 % c
    rs = lax.broadcasted_iota(jnp.int32, (c, c), 0)
    cs = lax.broadcasted_iota(jnp.int32, (c, c), 1)
    masks = {"same": same, "eye": r2 == c2, "left": c1 < c,
             "ones_bd": same.astype(BF16)}
    for name, lt in (("f", lambda a, b: a < b), ("b", lambda a, b: a > b)):
        le = (lambda a, b, lt=lt: lt(a, b) | (a == b))
        masks["strict_" + name] = same & lt(cc, rr)
        masks["inc_" + name] = le(cc1, r1)
        masks["ak0_" + name] = (c1 >= c) & lt(cc1, r1)
        masks["ak1_" + name] = (c1 < c) & lt(cc1, r1)
        masks["tri_" + name] = le(cs, rs).astype(BF16)
    return masks


def _wkv_chains(chains, mk):
    c = CHUNK
    left = mk["left"]
    for ch in chains:
        zero = jnp.zeros_like(ch["at"])
        ch["at0"], ch["at1"] = jnp.where(left, ch["at"], zero), jnp.where(left, zero, ch["at"])
        rt0, rt1 = jnp.where(left, ch["rt"], zero), jnp.where(left, zero, ch["rt"])
        bt, kt = ch["bt"].astype(BF16), ch["kt"].astype(BF16)
        ch["g0"] = _mm_nt(jnp.concatenate([ch["at0"], rt0], axis=0), jnp.concatenate([bt, kt], axis=0))
        ch["g1"] = _mm_nt(jnp.concatenate([ch["at1"], rt1], axis=0), jnp.concatenate([kt, bt], axis=0))
    for ch in chains:
        g0, g1 = ch["g0"], ch["g1"]
        lmat = jnp.where(mk["strict_" + ch["d"]], jnp.concatenate([g0[:c], g1[:c]], axis=0), 0.0)
        ch["tinv"] = jnp.where(mk["eye"], 1.0, lmat)
        ch["lpow"] = _mm(lmat, lmat)
    for step in range(5):
        for ch in chains:
            lp = ch["lpow"].astype(BF16)
            if step < 4:
                prod = jnp.dot(lp, jnp.concatenate([lp, ch["tinv"].astype(BF16)], axis=1),
                               preferred_element_type=F32)
                ch["lpow"] = prod[:, :LANES]
                ch["tinv"] = ch["tinv"] + prod[:, LANES:]
            else:
                ch["tinv"] = ch["tinv"] + _mm(lp, ch["tinv"])
    for ch in chains:
        d, v = ch["d"], ch["v"]
        zv = jnp.zeros_like(v)
        ak0 = jnp.where(mk["ak0_" + d], ch["g0"][:c], 0.0)
        ak1 = jnp.where(mk["ak1_" + d], ch["g1"][:c], 0.0)
        ch["x0"] = _mm(ak0, jnp.concatenate([zv, v], axis=0))
        ch["x1"] = _mm(ak1, jnp.concatenate([v, zv], axis=0))
    for ch in chains:
        zero = jnp.zeros_like(ch["x0"])
        rhs = jnp.concatenate([
            jnp.concatenate([ch["at0"], jnp.where(left, ch["x0"], zero)], axis=1),
            jnp.concatenate([ch["at1"], jnp.where(left, zero, ch["x1"])], axis=1)], axis=0)
        ch["aw_blk"] = _mm(ch["tinv"], rhs)
    for ch in chains:
        d, v, aw_blk = ch["d"], ch["v"], ch["aw_blk"]
        zv = jnp.zeros_like(v)
        aw = aw_blk[:c] + aw_blk[c:]
        inc = mk["inc_" + d]
        g0, g1 = ch["g0"], ch["g1"]
        rb = jnp.where(inc, jnp.where(left, g0[c:], g1[c:]), 0.0)
        rk = jnp.where(inc, jnp.where(left, g1[c:], g0[c:]), 0.0)
        v0, v1 = jnp.where(left, v, zv), jnp.where(left, zv, v)
        vblk = jnp.concatenate([jnp.zeros((2 * c, LANES), F32),
                                jnp.concatenate([v1, v0], axis=0)], axis=1)
        qz = _mm(jnp.concatenate([rb, rk], axis=1), jnp.concatenate([aw_blk, vblk], axis=0))
        ch["q"] = ch["rt"] + qz[:, :LANES]
        ch["zz"] = qz[:, LANES:]
        rhs3 = jnp.concatenate([aw, jnp.concatenate([zv, v], axis=1)], axis=0)
        ch["mn"] = _mm_tn(jnp.concatenate([ch["bh"], ch["kh"]], axis=0), rhs3)
    same = mk["same"]
    for ch in chains:
        mn = ch["mn"]
        mt = jnp.where(same, mn[:, :LANES], 0.0) + jnp.where(mk["eye"], ch["ptot"], 0.0)
        nt = jnp.where(same, mn[:, LANES:], 0.0)
        both = _mm(jnp.concatenate([ch["q"], mt], axis=0), ch["st"])
        ch["y"] = both[:c] + ch["zz"]
        ch["st_new"] = both[c:] + nt


def _wkv_kernel(rkv_f_ref, lo_f_ref, rkv_b_ref, lo_b_ref, s0f_ref, s0b_ref,
                w2_ref, w0_ref, a2_ref, a0_ref, kk_ref, ka_ref,
                yf_ref, yb_ref, sf_ref, sb_ref, st_ref):
    ci = pl.program_id(1)
    nc = pl.num_programs(1)

    @pl.when(ci == 0)
    def _():
        st_ref[0] = s0f_ref[0]
        st_ref[1] = s0b_ref[0]

    mk = _wkv_masks()
    c = CHUNK
    chains = []
    for di, (d, rkv_ref, lo_ref, y_ref) in enumerate(
            (("f", rkv_f_ref, lo_f_ref, yf_ref), ("b", rkv_b_ref, lo_b_ref, yb_ref))):
        rkv = rkv_ref[...]
        lo = lo_ref[...]
        r = rkv[:, :RWKV_DIM]
        k = rkv[:, RWKV_DIM:2 * RWKV_DIM]
        v = rkv[:, 2 * RWKV_DIM:]
        w_raw = w0_ref[di:di + 1, :] + _mm(jnp.tanh(lo[:, :LANES]), w2_ref[di])
        lw = -EXP_M05 * _sigmoid(w_raw)
        ag = _sigmoid(a0_ref[di:di + 1, :] + _mm(lo[:, LANES:2 * LANES], a2_ref[di]))
        kd = k * (1.0 + (ag - 1.0) * ka_ref[...])
        kk_raw = k * kk_ref[...]
        h1, h2, h3 = _split3(lw)
        cs = jnp.dot(mk["tri_" + d], jnp.concatenate([h1, h2, h3], axis=1),
                     preferred_element_type=F32)
        logp = cs[:, :RWKV_DIM] + cs[:, RWKV_DIM:2 * RWKV_DIM] + cs[:, 2 * RWKV_DIM:]
        tot = logp[0:1] if d == "b" else logp[c - 1:c]
        e_inv = jnp.exp(-logp)
        e_hat = jnp.exp(tot - logp)
        rt = r * jnp.exp(logp)
        e_ex = jnp.exp(logp - lw)
        kt, kh = kd * e_inv, kd * e_hat
        ptot = jnp.exp(tot)
        for p in range(N_PAIRS):
            cols = slice(p * LANES, (p + 1) * LANES)
            chains.append({"d": d, "di": di, "p": p, "y_ref": y_ref, "cols": cols,
                           "kk": kk_raw[:, cols], "ag": ag[:, cols], "e_ex": e_ex[:, cols],
                           "e_inv": e_inv[:, cols], "e_hat": e_hat[:, cols],
                           "rt": rt[:, cols], "kt": kt[:, cols], "kh": kh[:, cols],
                           "v": v[:, cols], "ptot": ptot[:, cols], "st": st_ref[di, p]})
    for ch in chains:
        kk = ch["kk"]
        kk = kk * lax.rsqrt(jnp.maximum(_head_sums(kk * kk, mk["ones_bd"]), 1e-24))
        kb = kk * ch["ag"]
        ch["at"] = -kk * ch["e_ex"]
        ch["bt"] = kb * ch["e_inv"]
        ch["bh"] = kb * ch["e_hat"]
    _wkv_chains(chains, mk)
    for ch in chains:
        ch["y_ref"][:, ch["cols"]] = ch["y"]
        st_ref[ch["di"], ch["p"]] = ch["st_new"]

    @pl.when(ci == nc - 1)
    def _():
        sf_ref[0] = st_ref[0]
        sb_ref[0] = st_ref[1]


def _wkv_call(rkv, lora, s0f, s0b, w2p, w0, a2p, a0, k_k, k_a, batch, seq_len):
    n = rkv.shape[0]
    nc = seq_len // CHUNK
    fwd = lambda b, c: (b * nc + c, 0)
    bwd = lambda b, c: (b * nc + (nc - 1 - c), 0)
    st = lambda b, c: (b, 0, 0, 0)
    c2 = lambda b, c: (0, 0)
    c3 = lambda b, c: (0, 0, 0)
    st_shape = jax.ShapeDtypeStruct((batch, N_PAIRS, LANES, LANES), F32)
    return pl.pallas_call(
        _wkv_kernel,
        grid=(batch, nc),
        in_specs=[pl.BlockSpec((CHUNK, 3 * RWKV_DIM), fwd),
                  pl.BlockSpec((CHUNK, LORA_COLS), fwd),
                  pl.BlockSpec((CHUNK, 3 * RWKV_DIM), bwd),
                  pl.BlockSpec((CHUNK, LORA_COLS), bwd),
                  pl.BlockSpec((1, N_PAIRS, LANES, LANES), st),
                  pl.BlockSpec((1, N_PAIRS, LANES, LANES), st),
                  pl.BlockSpec((2, LANES, RWKV_DIM), c3),
                  pl.BlockSpec((2, RWKV_DIM), c2),
                  pl.BlockSpec((2, LANES, RWKV_DIM), c3),
                  pl.BlockSpec((2, RWKV_DIM), c2),
                  pl.BlockSpec((1, RWKV_DIM), c2),
                  pl.BlockSpec((1, RWKV_DIM), c2)],
        out_specs=[pl.BlockSpec((CHUNK, RWKV_DIM), fwd),
                   pl.BlockSpec((CHUNK, RWKV_DIM), bwd),
                   pl.BlockSpec((1, N_PAIRS, LANES, LANES), st),
                   pl.BlockSpec((1, N_PAIRS, LANES, LANES), st)],
        out_shape=[jax.ShapeDtypeStruct((n, RWKV_DIM), F32),
                   jax.ShapeDtypeStruct((n, RWKV_DIM), F32), st_shape, st_shape],
        scratch_shapes=[pltpu.VMEM((2, N_PAIRS, LANES, LANES), F32)],
        compiler_params=_params("arbitrary", "arbitrary"),
        name="wkv_chunked",
    )(rkv, lora, rkv, lora, s0f, s0b, w2p, w0, a2p, a0, k_k, k_a)


def _out_kernel(x_ref, mod_ref, conv_ref, yf_ref, yb_ref, rkv_ref, lo_ref,
                a2_ref, a0_ref, ka_ref, rk_ref, gng_ref, gnb_ref, g2_ref, wout_ref,
                postg_ref, preg_ref, rw_ref, rb_ref,
                x1_ref, h2_ref, rank_ref, gate_ref, cnt_ref, *, group_tiles):
    i = pl.program_id(0)

    @pl.when(i % group_tiles == 0)
    def _():
        cnt_ref[...] = jnp.zeros(cnt_ref.shape, F32)

    m = mod_ref[0]
    rkv = rkv_ref[...]
    lo = lo_ref[...]
    r = rkv[:, :RWKV_DIM]
    k = rkv[:, RWKV_DIM:2 * RWKV_DIM]
    v = rkv[:, 2 * RWKV_DIM:]
    ha = lo[:, LANES:2 * LANES]
    ag_f = _sigmoid(a0_ref[0:1, :] + _mm(ha, a2_ref[0]))
    ag_b = _sigmoid(a0_ref[1:2, :] + _mm(ha, a2_ref[1]))
    rkk = r * k * rk_ref[...] * (2.0 + (ag_f + ag_b - 2.0) * ka_ref[...])
    o = yf_ref[...] + yb_ref[...]
    gate = _mm(_sigmoid(lo[:, 2 * LANES:]), g2_ref[...])
    r2 = lax.broadcasted_iota(jnp.int32, (LANES, LANES), 0)
    c2 = lax.broadcasted_iota(jnp.int32, (LANES, LANES), 1)
    ones_bd = ((r2 // HEAD_DIM) == (c2 // HEAD_DIM)).astype(BF16)
    parts = []
    for p in range(N_PAIRS):
        cols = slice(p * LANES, (p + 1) * LANES)
        op = o[:, cols]
        mu = _head_sums(op, ones_bd) * (1.0 / HEAD_DIM)
        oc = op - mu
        var = _head_sums(oc * oc, ones_bd) * (1.0 / HEAD_DIM)
        on = oc * lax.rsqrt(var + GN_EPS) * gng_ref[:, cols] + gnb_ref[:, cols]
        bonus = _head_sums(rkk[:, cols], ones_bd) * v[:, cols]
        parts.append((on + bonus) * gate[:, cols])
    mix_in = jnp.concatenate([conv_ref[...]] + parts, axis=1)
    mix = jnp.dot(mix_in.astype(BF16), wout_ref[...], preferred_element_type=F32)
    x1 = x_ref[...] + m[2:3] * _rms(mix, postg_ref[...])
    x1_ref[...] = x1
    h2 = _rms(x1, preg_ref[...]) * (1.0 + m[4:5]) + m[3:4]
    h2_ref[...] = h2.astype(BF16)
    hh, hl = _split2(h2)
    dot = functools.partial(jnp.dot, preferred_element_type=F32)
    logits = dot(hh, rw_ref[0]) + (dot(hh, rw_ref[1]) + dot(hl, rw_ref[0])) + rb_ref[...]
    lane = lax.broadcasted_iota(jnp.int32, logits.shape, 1)
    work = logits
    picks, vals = [], []
    for _ in range(TOP_K):
        mx = jnp.max(work, axis=-1, keepdims=True)
        idx = jnp.min(jnp.where(work == mx, lane, LANES), axis=-1, keepdims=True)
        pick = lane == idx
        picks.append(pick)
        vals.append(mx)
        work = jnp.where(pick, 2.0 * NEG_BIG, work)
    exps = [jnp.exp(val - vals[0]) for val in vals]
    den = exps[0] + exps[1] + exps[2] + exps[3]
    gate_d = jnp.zeros(logits.shape, F32)
    sel = jnp.zeros(logits.shape, F32)
    for pick, e in zip(picks, exps):
        gate_d = jnp.where(pick, e / den, gate_d)
        sel = jnp.where(pick, 1.0, sel)
    tr = lax.broadcasted_iota(jnp.int32, (TM_OUT, TM_OUT), 0)
    tc = lax.broadcasted_iota(jnp.int32, (TM_OUT, TM_OUT), 1)
    before = (tc < tr).astype(BF16)
    rank = dot(before, sel.astype(BF16)) + cnt_ref[...]
    cnt_ref[...] = cnt_ref[...] + jnp.sum(sel, axis=0, keepdims=True)
    rank_ref[...] = jnp.where(sel > 0.0, rank, -1.0).T
    gate_ref[...] = gate_d.T


def _out_call(x, mod, conv_out, yf, yb, rkv, lora, wts, seq_len):
    n, d = x.shape
    row = lambda i: (i, 0)
    c2 = lambda i: (0, 0)
    c3 = lambda i: (0, 0, 0)
    col = lambda i: (0, i)
    full = lambda a: pl.BlockSpec(a.shape, c3 if a.ndim == 3 else c2)
    consts = [wts["a2p"], wts["a0"], wts["k_a"], wts["r_k"], wts["gn_g"], wts["gn_b"], wts["g2"],
              wts["w_out"], wts["post_mix_g"], wts["pre_ffn_g"], wts["router_w"], wts["router_b"]]
    return pl.pallas_call(
        functools.partial(_out_kernel, group_tiles=MOE_GROUP // TM_OUT),
        grid=(n // TM_OUT,),
        in_specs=[pl.BlockSpec((TM_OUT, d), row),
                  pl.BlockSpec((1, N_MOD, d), _mod_index(TM_OUT, seq_len, mod.shape[0])),
                  pl.BlockSpec((TM_OUT, CONV_CH), row),
                  pl.BlockSpec((TM_OUT, RWKV_DIM), row),
                  pl.BlockSpec((TM_OUT, RWKV_DIM), row),
                  pl.BlockSpec((TM_OUT, 3 * RWKV_DIM), row),
                  pl.BlockSpec((TM_OUT, LORA_COLS), row)] + [full(a) for a in consts],
        out_specs=[pl.BlockSpec((TM_OUT, d), row),
                   pl.BlockSpec((TM_OUT, d), row),
                   pl.BlockSpec((LANES, TM_OUT), col),
                   pl.BlockSpec((LANES, TM_OUT), col)],
        out_shape=[jax.ShapeDtypeStruct((n, d), F32),
                   jax.ShapeDtypeStruct((n, d), BF16),
                   jax.ShapeDtypeStruct((LANES, n), F32),
                   jax.ShapeDtypeStruct((LANES, n), F32)],
        scratch_shapes=[pltpu.VMEM((1, LANES), F32)],
        compiler_params=_params("arbitrary"),
        name="out_router",
    )(x, mod, conv_out, yf, yb, rkv, lora, *consts)


N_WIN = MOE_GROUP // MOE_WIN


def _moe_kernel(cw_ref, h2_ref, rank_ref, gate_ref, wgu_ref, bgu_ref, wd_ref, bd_ref,
                y_ref, xs_ref, gs_ref):
    g = pl.program_id(0)
    e = pl.program_id(1)

    @pl.when(e == 0)
    def _():
        y_ref[...] = jnp.zeros(y_ref.shape, F32)

    cbase = (g * N_EXPERTS + e) * (N_WIN + 1)
    n_rows = cw_ref[cbase + N_WIN]
    n_blk = (n_rows + MOE_BLOCK - 1) // MOE_BLOCK
    slot = lax.broadcasted_iota(jnp.int32, (MOE_BLOCK, MOE_WIN), 0).astype(F32)

    def block(b, carry):
        base = b * MOE_BLOCK
        basef = base.astype(F32)
        xs_ref[...] = jnp.zeros(xs_ref.shape, F32)
        gs_ref[...] = jnp.zeros(gs_ref.shape, F32)

        def window_hits(w):
            return (cw_ref[cbase + w + 1] > base) & (cw_ref[cbase + w] < base + MOE_BLOCK)

        for w in range(N_WIN):
            toks = slice(w * MOE_WIN, (w + 1) * MOE_WIN)

            @pl.when(window_hits(w))
            def _(toks=toks):
                hit = (rank_ref[0, :, toks] - basef) == slot
                xs_ref[...] += jnp.dot(hit.astype(BF16), h2_ref[toks, :],
                                       preferred_element_type=F32)
                gs_ref[...] += jnp.sum(jnp.where(hit, gate_ref[0, :, toks], 0.0),
                                       axis=-1, keepdims=True)

        gu = jnp.dot(xs_ref[...].astype(BF16), wgu_ref[0], preferred_element_type=F32) + bgu_ref[0]
        x_glu = jnp.minimum(gu[:, :D_FF], SWIGLU_LIMIT)
        x_lin = jnp.clip(gu[:, D_FF:], -SWIGLU_LIMIT, SWIGLU_LIMIT)
        act = x_glu * _sigmoid(SWIGLU_ALPHA * x_glu) * (x_lin + 1.0)
        out = jnp.dot(act.astype(BF16), wd_ref[0], preferred_element_type=F32) + bd_ref[0]
        outg = (out * gs_ref[...]).astype(BF16)

        for w in range(N_WIN):
            toks = slice(w * MOE_WIN, (w + 1) * MOE_WIN)

            @pl.when(window_hits(w))
            def _(toks=toks):
                hit = (rank_ref[0, :, toks] - basef) == slot
                y_ref[toks, :] += lax.dot_general(hit.astype(BF16), outg,
                                                  (((0,), (0,)), ((), ())),
                                                  preferred_element_type=F32)
        return carry

    lax.fori_loop(0, n_blk, block, 0)


def _moe_call(cw, h2, rank3, gate3, wgu, bgu, wd, bd):
    n, d = h2.shape
    groups = n // MOE_GROUP
    grid_spec = pltpu.PrefetchScalarGridSpec(
        num_scalar_prefetch=1,
        grid=(groups, N_EXPERTS),
        in_specs=[pl.BlockSpec((MOE_GROUP, d), lambda g, e, cw: (g, 0)),
                  pl.BlockSpec((1, 1, MOE_GROUP), lambda g, e, cw: (e, 0, g)),
                  pl.BlockSpec((1, 1, MOE_GROUP), lambda g, e, cw: (e, 0, g)),
                  pl.BlockSpec((1, d, 2 * D_FF), lambda g, e, cw: (e, 0, 0)),
                  pl.BlockSpec((1, 1, 2 * D_FF), lambda g, e, cw: (e, 0, 0)),
                  pl.BlockSpec((1, D_FF, d), lambda g, e, cw: (e, 0, 0)),
                  pl.BlockSpec((1, 1, d), lambda g, e, cw: (e, 0, 0))],
        out_specs=pl.BlockSpec((MOE_GROUP, d), lambda g, e, cw: (g, 0)),
        scratch_shapes=[pltpu.VMEM((MOE_BLOCK, d), F32), pltpu.VMEM((MOE_BLOCK, 1), F32)])
    return pl.pallas_call(
        _moe_kernel,
        grid_spec=grid_spec,
        out_shape=jax.ShapeDtypeStruct((n, d), F32),
        compiler_params=_params("arbitrary", "arbitrary"),
        name="moe_experts",
    )(cw, h2, rank3, gate3, wgu, bgu, wd, bd)


def _final_kernel(x1_ref, y_ref, mod_ref, g_ref, o_ref):
    m = mod_ref[0]
    o_ref[...] = x1_ref[...] + m[5:6] * _rms(y_ref[...], g_ref[...])


def _final_call(x1, y, mod, g, seq_len):
    n, d = x1.shape
    row = lambda i: (i, 0)
    return pl.pallas_call(
        _final_kernel,
        grid=(n // TM_FIN,),
        in_specs=[pl.BlockSpec((TM_FIN, d), row),
                  pl.BlockSpec((TM_FIN, d), row),
                  pl.BlockSpec((1, N_MOD, d), _mod_index(TM_FIN, seq_len, mod.shape[0])),
                  pl.BlockSpec((1, d), lambda i: (0, 0))],
        out_specs=pl.BlockSpec((TM_FIN, d), row),
        out_shape=jax.ShapeDtypeStruct((n, d), F32),
        compiler_params=_params("arbitrary"),
        name="final_residual",
    )(x1, y, mod, g)


def _pack_state(s):
    b = s.shape[0]
    st = jnp.swapaxes(s.astype(F32), -1, -2).reshape(b, N_PAIRS, 2, HEAD_DIM, HEAD_DIM)
    z = jnp.zeros_like(st[:, :, 0])
    top = jnp.concatenate([st[:, :, 0], z], axis=-1)
    bot = jnp.concatenate([z, st[:, :, 1]], axis=-1)
    return jnp.concatenate([top, bot], axis=-2)


def _unpack_state(st):
    b = st.shape[0]
    h0 = st[:, :, :HEAD_DIM, :HEAD_DIM]
    h1 = st[:, :, HEAD_DIM:, HEAD_DIM:]
    s = jnp.stack([h0, h1], axis=2).reshape(b, RWKV_HEADS, HEAD_DIM, HEAD_DIM)
    return jnp.swapaxes(s, -1, -2)


def _window_counts(rank_t, n_tok):
    sel = (rank_t[:N_EXPERTS] >= 0.0).astype(jnp.int32)
    per_win = sel.reshape(N_EXPERTS, n_tok // MOE_GROUP, N_WIN, MOE_WIN).sum(-1)
    cum = jnp.cumsum(per_win, axis=-1)
    cum = jnp.concatenate([jnp.zeros_like(cum[..., :1]), cum], axis=-1)
    return jnp.transpose(cum, (1, 0, 2)).reshape(-1)


def _layer(x, mod, s0f, s0b, seg_len, wts):
    batch, seq_len, d = x.shape
    n = batch * seq_len
    x2 = x.reshape(n, d)
    cv, rkv, lora = _in_call(x2, mod, wts["pre_mix_g"], wts["w_all"], seq_len)
    conv_out = _conv_call(cv, wts["conv_w"], wts["conv_b"], wts["conv_ln_g"], wts["conv_ln_b"],
                          seg_len)
    yf, yb, sf, sb = _wkv_call(rkv, lora, s0f, s0b, wts["w2p"], wts["w0"], wts["a2p"], wts["a0"],
                               wts["k_k"], wts["k_a"], batch, seq_len)
    x1, h2, rank_t, gate_t = _out_call(x2, mod, conv_out, yf, yb, rkv, lora, wts, seq_len)
    cw = _window_counts(rank_t, n)
    rank3 = rank_t[:N_EXPERTS].reshape(N_EXPERTS, 1, n)
    gate3 = gate_t[:N_EXPERTS].reshape(N_EXPERTS, 1, n)
    y_ffn = _moe_call(cw, h2, rank3, gate3, wts["w_gu"], wts["b_gu"], wts["w_down"], wts["b_down"])
    out = _final_call(x1, y_ffn, mod, wts["post_ffn_g"], seq_len)
    return out.reshape(batch, seq_len, d), sf, sb


def _prep_weights(l, pre_mix_g, post_mix_g, pre_ffn_g, post_ffn_g, w_in, w_out, conv_w, conv_b,
                  conv_ln_g, conv_ln_b, rw_w0, rw_w1, rw_w2, rw_a0, rw_a1, rw_a2, rw_g1, rw_g2,
                  rw_k_k, rw_k_a, rw_r_k, rw_gn_g, rw_gn_b, router_w, router_b, w_gu, b_gu,
                  w_down, b_down):
    row = lambda a: a.reshape(1, -1).astype(F32)
    zpad = jnp.zeros((HEAD_DIM, RWKV_DIM), F32)
    w_all = jnp.concatenate([w_in[l], rw_w1[l, 0], rw_w1[l, 1], rw_a1[l, 0], rw_a1[l, 1],
                             rw_g1[l]], axis=1).astype(BF16)
    w2p = jnp.stack([jnp.concatenate([rw_w2[l, 0], zpad], axis=0),
                     jnp.concatenate([zpad, rw_w2[l, 1]], axis=0)]).astype(BF16)
    a2p = jnp.stack([jnp.concatenate([rw_a2[l, 0], zpad], axis=0),
                     jnp.concatenate([zpad, rw_a2[l, 1]], axis=0)]).astype(BF16)
    rw_pad = jnp.pad(router_w[l].astype(F32), ((0, 0), (0, LANES - N_EXPERTS)))
    rw_hi = rw_pad.astype(BF16)
    rw_lo = (rw_pad - rw_hi.astype(F32)).astype(BF16)
    rb_pad = jnp.concatenate([router_b[l].astype(F32),
                              jnp.full((LANES - N_EXPERTS,), NEG_BIG, F32)]).reshape(1, LANES)
    return {
        "pre_mix_g": row(pre_mix_g[l]), "post_mix_g": row(post_mix_g[l]),
        "pre_ffn_g": row(pre_ffn_g[l]), "post_ffn_g": row(post_ffn_g[l]),
        "w_all": w_all, "w_out": w_out[l].astype(BF16),
        "conv_w": conv_w[l].astype(F32), "conv_b": row(conv_b[l]),
        "conv_ln_g": row(conv_ln_g[l]), "conv_ln_b": row(conv_ln_b[l]),
        "w2p": w2p, "w0": rw_w0[l].astype(F32), "a2p": a2p, "a0": rw_a0[l].astype(F32),
        "k_k": row(rw_k_k[l]), "k_a": row(rw_k_a[l]), "r_k": row(rw_r_k[l]),
        "gn_g": row(rw_gn_g[l]), "gn_b": row(rw_gn_b[l]), "g2": rw_g2[l].astype(BF16),
        "router_w": jnp.stack([rw_hi, rw_lo]), "router_b": rb_pad,
        "w_gu": w_gu[l].astype(BF16), "b_gu": b_gu[l].reshape(N_EXPERTS, 1, -1).astype(F32),
        "w_down": w_down[l].astype(BF16), "b_down": b_down[l].reshape(N_EXPERTS, 1, -1).astype(F32),
    }


def kernel(x_prompt, x_sample, state_wkv_fwd, state_wkv_bwd, c, c_ctx, ada_w, ada_b, pre_mix_g, post_mix_g, pre_ffn_g, post_ffn_g, w_in, w_out, conv_w, conv_b, conv_ln_g, conv_ln_b, rw_w0, rw_w1, rw_w2, rw_a0, rw_a1, rw_a2, rw_g1, rw_g2, rw_k_k, rw_k_a, rw_r_k, rw_gn_g, rw_gn_b, router_w, router_b, w_gu, b_gu, w_down, b_down):
    depth = ada_w.shape[0]
    dec_batch = c.shape[0]
    mod_rows = 16
    c_rows = jnp.concatenate([c, c_ctx[None, :],
                              jnp.zeros((mod_rows - dec_batch - 1, D_MODEL), F32)], axis=0)
    zero_state = jnp.zeros((x_prompt.shape[0], N_PAIRS, LANES, LANES), F32)
    y_prompt, y_sample = x_prompt, x_sample
    new_f, new_b = [], []
    for l in range(depth):
        wts = _prep_weights(l, pre_mix_g, post_mix_g, pre_ffn_g, post_ffn_g, w_in, w_out, conv_w,
                            conv_b, conv_ln_g, conv_ln_b, rw_w0, rw_w1, rw_w2, rw_a0, rw_a1, rw_a2,
                            rw_g1, rw_g2, rw_k_k, rw_k_a, rw_r_k, rw_gn_g, rw_gn_b, router_w,
                            router_b, w_gu, b_gu, w_down, b_down)
        mod = _mod_call(c_rows, ada_w[l], ada_b[l]).reshape(mod_rows, N_MOD, D_MODEL)
        mod_lat = mod[:dec_batch]
        mod_ctx = mod[dec_batch:dec_batch + 1]
        y_prompt, s_f, s_b = _layer(y_prompt, mod_ctx, zero_state, zero_state,
                                    y_prompt.shape[1], wts)
        new_f.append(_unpack_state(s_f))
        new_b.append(_unpack_state(s_b))
        y_sample, _, _ = _layer(y_sample, mod_lat, _pack_state(state_wkv_fwd[:, l]),
                                _pack_state(state_wkv_bwd[:, l]), GRID_W, wts)
    return (y_prompt, y_sample, jnp.stack(new_f, axis=1), jnp.stack(new_b, axis=1))
```

```python
import functools
import math

import jax
import jax.numpy as jnp
from jax import lax
from jax.experimental import pallas as pl
from jax.experimental.pallas import tpu as pltpu
from jax.experimental.pallas import tpu_sc as plsc

F32 = jnp.float32
BF16 = jnp.bfloat16

D_MODEL = 1024
CONV_CH = 512
RWKV_DIM = 512
HEAD_DIM = 64
RWKV_HEADS = 8
N_PAIRS = RWKV_HEADS // 2
CONV_WIDTH = 31
N_EXPERTS = 32
TOP_K = 4
D_FF = 1024
SWIGLU_LIMIT = 7.0
SWIGLU_ALPHA = 1.702
RMS_EPS = 1e-6
LN_EPS = 1e-5
GN_EPS = 64e-5
N_MOD = 6
GRID_W = 64

LANES = 128
CHUNK = 64
LORA_COLS = 384
TM_IN = 512
TM_CONV = 256
TM_OUT = 256
TM_FIN = 512
MOE_BLOCK = 256
SC_CORES = 2
SC_SUBCORES = 16
SC_ROWS = 64
NEG_BIG = -1e30
EXP_M05 = math.exp(-0.5)
VMEM_LIMIT = 56 * 1024 * 1024


def _sigmoid(x):
    return 1.0 / (1.0 + jnp.exp(-x))


def _mm(a, b):
    return jnp.dot(a.astype(BF16), b.astype(BF16), preferred_element_type=F32)


def _mm_nt(a, b):
    return lax.dot_general(a.astype(BF16), b.astype(BF16), (((1,), (1,)), ((), ())),
                           preferred_element_type=F32)


def _mm_tn(a, b):
    return lax.dot_general(a.astype(BF16), b.astype(BF16), (((0,), (0,)), ((), ())),
                           preferred_element_type=F32)


def _split2(x):
    hi = x.astype(BF16)
    lo = (x - hi.astype(F32)).astype(BF16)
    return hi, lo


def _split3(x):
    h1 = x.astype(BF16)
    r1 = x - h1.astype(F32)
    h2 = r1.astype(BF16)
    h3 = (r1 - h2.astype(F32)).astype(BF16)
    return h1, h2, h3


def _rms(x, g):
    return x * lax.rsqrt(jnp.mean(x * x, axis=-1, keepdims=True) + RMS_EPS) * g


def _params(*sem):
    return pltpu.CompilerParams(dimension_semantics=sem, vmem_limit_bytes=VMEM_LIMIT)


def _mod_kernel(c_ref, w_ref, b_ref, o_ref):
    c = c_ref[...]
    s1, s2, s3 = _split3(c * _sigmoid(c))
    w1, w2, w3 = _split3(w_ref[...])
    dot = functools.partial(jnp.dot, preferred_element_type=F32)
    acc = dot(s1, w1) + (dot(s1, w2) + dot(s2, w1)) + (dot(s2, w2) + dot(s1, w3) + dot(s3, w1))
    o_ref[...] = acc + b_ref[...]


def _mod_call(c_rows, ada_w, ada_b):
    m, d = c_rows.shape
    n = ada_w.shape[1]
    tn = 512
    return pl.pallas_call(
        _mod_kernel,
        grid=(n // tn,),
        in_specs=[pl.BlockSpec((m, d), lambda j: (0, 0)),
                  pl.BlockSpec((d, tn), lambda j: (0, j)),
                  pl.BlockSpec((1, tn), lambda j: (0, j))],
        out_specs=pl.BlockSpec((m, tn), lambda j: (0, j)),
        out_shape=jax.ShapeDtypeStruct((m, n), F32),
        compiler_params=_params("arbitrary"),
        name="mod",
    )(c_rows, ada_w, ada_b.reshape(1, n))


def _in_kernel(x_ref, mod_ref, g_ref, w_ref, cv_ref, rkv_ref, lo_ref):
    m = mod_ref[0]
    h = _rms(x_ref[...], g_ref[...]) * (1.0 + m[1:2]) + m[0:1]
    proj = jnp.dot(h.astype(BF16), w_ref[...], preferred_element_type=F32)
    cv_ref[...] = proj[:, :2 * CONV_CH]
    rkv_ref[...] = proj[:, 2 * CONV_CH:2 * CONV_CH + 3 * RWKV_DIM]
    lo_ref[...] = proj[:, 2 * CONV_CH + 3 * RWKV_DIM:]


def _mod_index(tile, seq_len, n_mod_rows):
    def index(i):
        return ((i * tile) // seq_len) % n_mod_rows, 0, 0
    return index


def _in_call(x, mod, g, w_all, seq_len):
    n, d = x.shape
    ncol = w_all.shape[1]
    return pl.pallas_call(
        _in_kernel,
        grid=(n // TM_IN,),
        in_specs=[pl.BlockSpec((TM_IN, d), lambda i: (i, 0)),
                  pl.BlockSpec((1, N_MOD, d), _mod_index(TM_IN, seq_len, mod.shape[0])),
                  pl.BlockSpec((1, d), lambda i: (0, 0)),
                  pl.BlockSpec((d, ncol), lambda i: (0, 0))],
        out_specs=[pl.BlockSpec((TM_IN, 2 * CONV_CH), lambda i: (i, 0)),
                   pl.BlockSpec((TM_IN, 3 * RWKV_DIM), lambda i: (i, 0)),
                   pl.BlockSpec((TM_IN, LORA_COLS), lambda i: (i, 0))],
        out_shape=[jax.ShapeDtypeStruct((n, 2 * CONV_CH), F32),
                   jax.ShapeDtypeStruct((n, 3 * RWKV_DIM), F32),
                   jax.ShapeDtypeStruct((n, LORA_COLS), F32)],
        compiler_params=_params("arbitrary"),
        name="in_proj",
    )(x, mod, g, w_all)


CONV_HALO = 16
CONV_ROWS = 64


def _conv_kernel(cv_ref, w_ref, b_ref, g_ref, be_ref, o_ref, pad_ref, acc_ref, *, seg_len):
    nseg = TM_CONV // seg_len
    stride = seg_len + 2 * CONV_HALO
    cv = cv_ref[...]
    u = cv[:, :CONV_CH] * _sigmoid(cv[:, CONV_CH:])
    pad_ref[...] = jnp.zeros(pad_ref.shape, F32)
    for s in range(nseg):
        pad_ref[s * stride + CONV_HALO:s * stride + CONV_HALO + seg_len, :] = (
            u[s * seg_len:(s + 1) * seg_len])
    first = CONV_HALO - CONV_WIDTH // 2
    for s in range(nseg):
        for rc in range(seg_len // CONV_ROWS):
            for lc in range(CONV_CH // LANES):
                cols = slice(lc * LANES, (lc + 1) * LANES)
                acc = jnp.zeros((CONV_ROWS, LANES), F32)
                for t in range(CONV_WIDTH):
                    base = s * stride + first + t + rc * CONV_ROWS
                    acc = acc + w_ref[t:t + 1, cols] * pad_ref[base:base + CONV_ROWS, cols]
                row0 = s * seg_len + rc * CONV_ROWS
                acc_ref[row0:row0 + CONV_ROWS, cols] = acc
    y = acc_ref[...] + b_ref[...]
    mu = jnp.mean(y, axis=-1, keepdims=True)
    yc = y - mu
    var = jnp.mean(yc * yc, axis=-1, keepdims=True)
    z = yc * lax.rsqrt(var + LN_EPS) * g_ref[...] + be_ref[...]
    o_ref[...] = z * _sigmoid(z)


def _conv_call(cv, conv_w, conv_b, ln_g, ln_b, seg_len):
    n = cv.shape[0]
    nseg = TM_CONV // seg_len
    row = lambda i: (i, 0)
    const = lambda i: (0, 0)
    return pl.pallas_call(
        functools.partial(_conv_kernel, seg_len=seg_len),
        grid=(n // TM_CONV,),
        in_specs=[pl.BlockSpec((TM_CONV, 2 * CONV_CH), row),
                  pl.BlockSpec((CONV_WIDTH, CONV_CH), const),
                  pl.BlockSpec((1, CONV_CH), const),
                  pl.BlockSpec((1, CONV_CH), const),
                  pl.BlockSpec((1, CONV_CH), const)],
        out_specs=pl.BlockSpec((TM_CONV, CONV_CH), row),
        out_shape=jax.ShapeDtypeStruct((n, CONV_CH), F32),
        scratch_shapes=[pltpu.VMEM((nseg * (seg_len + 2 * CONV_HALO), CONV_CH), F32),
                        pltpu.VMEM((TM_CONV, CONV_CH), F32)],
        compiler_params=_params("arbitrary"),
        name="conv_module",
    )(cv, conv_w, conv_b, ln_g, ln_b)


def _head_sums(x, ones_bd):
    hi, lo = _split2(x)
    return (jnp.dot(hi, ones_bd, preferred_element_type=F32)
            + jnp.dot(lo, ones_bd, preferred_element_type=F32))


def _wkv_masks():
    c = CHUNK
    r2 = lax.broadcasted_iota(jnp.int32, (2 * c, 2 * c), 0)
    c2 = lax.broadcasted_iota(jnp.int32, (2 * c, 2 * c), 1)
    same = (r2 // c) == (c2 // c)
    rr, cc = r2 % c, c2 % c
    r1 = lax.broadcasted_iota(jnp.int32, (c, 2 * c), 0)
    c1 = lax.broadcasted_iota(jnp.int32, (c, 2 * c), 1)
    cc1 = c1 % c
    rs = lax.broadcasted_iota(jnp.int32, (c, c), 0)
    cs = lax.broadcasted_iota(jnp.int32, (c, c), 1)
    masks = {"same": same, "eye": r2 == c2, "left": c1 < c,
             "ones_bd": same.astype(BF16)}
    for name, lt in (("f", lambda a, b: a < b), ("b", lambda a, b: a > b)):
        le = (lambda a, b, lt=lt: lt(a, b) | (a == b))
        masks["strict_" + name] = same & lt(cc, rr)
        masks["inc_" + name] = le(cc1, r1)
        masks["ak0_" + name] = (c1 >= c) & lt(cc1, r1)
        masks["ak1_" + name] = (c1 < c) & lt(cc1, r1)
        masks["tri_" + name] = le(cs, rs).astype(BF16)
    return masks


def _wkv_chains(chains, mk):
    c = CHUNK
    left = mk["left"]
    for ch in chains:
        zero = jnp.zeros_like(ch["at"])
        ch["at0"], ch["at1"] = jnp.where(left, ch["at"], zero), jnp.where(left, zero, ch["at"])
        rt0, rt1 = jnp.where(left, ch["rt"], zero), jnp.where(left, zero, ch["rt"])
        bt, kt = ch["bt"].astype(BF16), ch["kt"].astype(BF16)
        ch["g0"] = _mm_nt(jnp.concatenate([ch["at0"], rt0], axis=0), jnp.concatenate([bt, kt], axis=0))
        ch["g1"] = _mm_nt(jnp.concatenate([ch["at1"], rt1], axis=0), jnp.concatenate([kt, bt], axis=0))
    for ch in chains:
        g0, g1 = ch["g0"], ch["g1"]
        lmat = jnp.where(mk["strict_" + ch["d"]], jnp.concatenate([g0[:c], g1[:c]], axis=0), 0.0)
        ch["tinv"] = jnp.where(mk["eye"], 1.0, lmat)
        ch["lpow"] = _mm(lmat, lmat)
    for step in range(5):
        for ch in chains:
            lp = ch["lpow"].astype(BF16)
            if step < 4:
                prod = jnp.dot(lp, jnp.concatenate([lp, ch["tinv"].astype(BF16)], axis=1),
                               preferred_element_type=F32)
                ch["lpow"] = prod[:, :LANES]
                ch["tinv"] = ch["tinv"] + prod[:, LANES:]
            else:
                ch["tinv"] = ch["tinv"] + _mm(lp, ch["tinv"])
    for ch in chains:
        d, v = ch["d"], ch["v"]
        zv = jnp.zeros_like(v)
        ak0 = jnp.where(mk["ak0_" + d], ch["g0"][:c], 0.0)
        ak1 = jnp.where(mk["ak1_" + d], ch["g1"][:c], 0.0)
        ch["x0"] = _mm(ak0, jnp.concatenate([zv, v], axis=0))
        ch["x1"] = _mm(ak1, jnp.concatenate([v, zv], axis=0))
    for ch in chains:
        zero = jnp.zeros_like(ch["x0"])
        rhs = jnp.concatenate([
            jnp.concatenate([ch["at0"], jnp.where(left, ch["x0"], zero)], axis=1),
            jnp.concatenate([ch["at1"], jnp.where(left, zero, ch["x1"])], axis=1)], axis=0)
        ch["aw_blk"] = _mm(ch["tinv"], rhs)
    for ch in chains:
        d, v, aw_blk = ch["d"], ch["v"], ch["aw_blk"]
        zv = jnp.zeros_like(v)
        aw = aw_blk[:c] + aw_blk[c:]
        inc = mk["inc_" + d]
        g0, g1 = ch["g0"], ch["g1"]
        rb = jnp.where(inc, jnp.where(left, g0[c:], g1[c:]), 0.0)
        rk = jnp.where(inc, jnp.where(left, g1[c:], g0[c:]), 0.0)
        v0, v1 = jnp.where(left, v, zv), jnp.where(left, zv, v)
        vblk = jnp.concatenate([jnp.zeros((2 * c, LANES), F32),
                                jnp.concatenate([v1, v0], axis=0)], axis=1)
        qz = _mm(jnp.concatenate([rb, rk], axis=1), jnp.concatenate([aw_blk, vblk], axis=0))
        ch["q"] = ch["rt"] + qz[:, :LANES]
        ch["zz"] = qz[:, LANES:]
        rhs3 = jnp.concatenate([aw, jnp.concatenate([zv, v], axis=1)], axis=0)
        ch["mn"] = _mm_tn(jnp.concatenate([ch["bh"], ch["kh"]], axis=0), rhs3)
    same = mk["same"]
    for ch in chains:
        mn = ch["mn"]
        mt = jnp.where(same, mn[:, :LANES], 0.0) + jnp.where(mk["eye"], ch["ptot"], 0.0)
        nt = jnp.where(same, mn[:, LANES:], 0.0)
        both = _mm(jnp.concatenate([ch["q"], mt], axis=0), ch["st"])
        ch["y"] = both[:c] + ch["zz"]
        ch["st_new"] = both[c:] + nt


def _wkv_kernel(rkv_f_ref, lo_f_ref, rkv_b_ref, lo_b_ref, s0f_ref, s0b_ref,
                w2_ref, w0_ref, a2_ref, a0_ref, kk_ref, ka_ref,
                yf_ref, yb_ref, sf_ref, sb_ref, st_ref):
    ci = pl.program_id(1)
    nc = pl.num_programs(1)

    @pl.when(ci == 0)
    def _():
        st_ref[0] = s0f_ref[0]
        st_ref[1] = s0b_ref[0]

    mk = _wkv_masks()
    c = CHUNK
    chains = []
    for di, (d, rkv_ref, lo_ref, y_ref) in enumerate(
            (("f", rkv_f_ref, lo_f_ref, yf_ref), ("b", rkv_b_ref, lo_b_ref, yb_ref))):
        rkv = rkv_ref[...]
        lo = lo_ref[...]
        r = rkv[:, :RWKV_DIM]
        k = rkv[:, RWKV_DIM:2 * RWKV_DIM]
        v = rkv[:, 2 * RWKV_DIM:]
        w_raw = w0_ref[di:di + 1, :] + _mm(jnp.tanh(lo[:, :LANES]), w2_ref[di])
        lw = -EXP_M05 * _sigmoid(w_raw)
        ag = _sigmoid(a0_ref[di:di + 1, :] + _mm(lo[:, LANES:2 * LANES], a2_ref[di]))
        kd = k * (1.0 + (ag - 1.0) * ka_ref[...])
        kk_raw = k * kk_ref[...]
        h1, h2, h3 = _split3(lw)
        cs = jnp.dot(mk["tri_" + d], jnp.concatenate([h1, h2, h3], axis=1),
                     preferred_element_type=F32)
        logp = cs[:, :RWKV_DIM] + cs[:, RWKV_DIM:2 * RWKV_DIM] + cs[:, 2 * RWKV_DIM:]
        tot = logp[0:1] if d == "b" else logp[c - 1:c]
        e_inv = jnp.exp(-logp)
        e_hat = jnp.exp(tot - logp)
        rt = r * jnp.exp(logp)
        e_ex = jnp.exp(logp - lw)
        kt, kh = kd * e_inv, kd * e_hat
        ptot = jnp.exp(tot)
        for p in range(N_PAIRS):
            cols = slice(p * LANES, (p + 1) * LANES)
            chains.append({"d": d, "di": di, "p": p, "y_ref": y_ref, "cols": cols,
                           "kk": kk_raw[:, cols], "ag": ag[:, cols], "e_ex": e_ex[:, cols],
                           "e_inv": e_inv[:, cols], "e_hat": e_hat[:, cols],
                           "rt": rt[:, cols], "kt": kt[:, cols], "kh": kh[:, cols],
                           "v": v[:, cols], "ptot": ptot[:, cols], "st": st_ref[di, p]})
    for ch in chains:
        kk = ch["kk"]
        kk = kk * lax.rsqrt(jnp.maximum(_head_sums(kk * kk, mk["ones_bd"]), 1e-24))
        kb = kk * ch["ag"]
        ch["at"] = -kk * ch["e_ex"]
        ch["bt"] = kb * ch["e_inv"]
        ch["bh"] = kb * ch["e_hat"]
    _wkv_chains(chains, mk)
    for ch in chains:
        ch["y_ref"][:, ch["cols"]] = ch["y"]
        st_ref[ch["di"], ch["p"]] = ch["st_new"]

    @pl.when(ci == nc - 1)
    def _():
        sf_ref[0] = st_ref[0]
        sb_ref[0] = st_ref[1]


def _wkv_call(rkv, lora, s0f, s0b, w2p, w0, a2p, a0, k_k, k_a, batch, seq_len):
    n = rkv.shape[0]
    nc = seq_len // CHUNK
    fwd = lambda b, c: (b * nc + c, 0)
    bwd = lambda b, c: (b * nc + (nc - 1 - c), 0)
    st = lambda b, c: (b, 0, 0, 0)
    c2 = lambda b, c: (0, 0)
    c3 = lambda b, c: (0, 0, 0)
    st_shape = jax.ShapeDtypeStruct((batch, N_PAIRS, LANES, LANES), F32)
    return pl.pallas_call(
        _wkv_kernel,
        grid=(batch, nc),
        in_specs=[pl.BlockSpec((CHUNK, 3 * RWKV_DIM), fwd),
                  pl.BlockSpec((CHUNK, LORA_COLS), fwd),
                  pl.BlockSpec((CHUNK, 3 * RWKV_DIM), bwd),
                  pl.BlockSpec((CHUNK, LORA_COLS), bwd),
                  pl.BlockSpec((1, N_PAIRS, LANES, LANES), st),
                  pl.BlockSpec((1, N_PAIRS, LANES, LANES), st),
                  pl.BlockSpec((2, LANES, RWKV_DIM), c3),
                  pl.BlockSpec((2, RWKV_DIM), c2),
                  pl.BlockSpec((2, LANES, RWKV_DIM), c3),
                  pl.BlockSpec((2, RWKV_DIM), c2),
                  pl.BlockSpec((1, RWKV_DIM), c2),
                  pl.BlockSpec((1, RWKV_DIM), c2)],
        out_specs=[pl.BlockSpec((CHUNK, RWKV_DIM), fwd),
                   pl.BlockSpec((CHUNK, RWKV_DIM), bwd),
                   pl.BlockSpec((1, N_PAIRS, LANES, LANES), st),
                   pl.BlockSpec((1, N_PAIRS, LANES, LANES), st)],
        out_shape=[jax.ShapeDtypeStruct((n, RWKV_DIM), F32),
                   jax.ShapeDtypeStruct((n, RWKV_DIM), F32), st_shape, st_shape],
        scratch_shapes=[pltpu.VMEM((2, N_PAIRS, LANES, LANES), F32)],
        compiler_params=_params("arbitrary", "arbitrary"),
        name="wkv_chunked",
    )(rkv, lora, rkv, lora, s0f, s0b, w2p, w0, a2p, a0, k_k, k_a)


def _out_kernel(x_ref, mod_ref, conv_ref, yf_ref, yb_ref, rkv_ref, lo_ref,
                a2_ref, a0_ref, ka_ref, rk_ref, gng_ref, gnb_ref, g2_ref, wout_ref,
                postg_ref, preg_ref, rw_ref, rb_ref,
                x1_ref, h2_ref, route_ref, cnt_ref):
    i = pl.program_id(0)

    @pl.when(i == 0)
    def _():
        cnt_ref[...] = jnp.zeros(cnt_ref.shape, F32)

    m = mod_ref[0]
    rkv = rkv_ref[...]
    lo = lo_ref[...]
    r = rkv[:, :RWKV_DIM]
    k = rkv[:, RWKV_DIM:2 * RWKV_DIM]
    v = rkv[:, 2 * RWKV_DIM:]
    ha = lo[:, LANES:2 * LANES]
    ag_f = _sigmoid(a0_ref[0:1, :] + _mm(ha, a2_ref[0]))
    ag_b = _sigmoid(a0_ref[1:2, :] + _mm(ha, a2_ref[1]))
    rkk = r * k * rk_ref[...] * (2.0 + (ag_f + ag_b - 2.0) * ka_ref[...])
    o = yf_ref[...] + yb_ref[...]
    gate = _mm(_sigmoid(lo[:, 2 * LANES:]), g2_ref[...])
    r2 = lax.broadcasted_iota(jnp.int32, (LANES, LANES), 0)
    c2 = lax.broadcasted_iota(jnp.int32, (LANES, LANES), 1)
    ones_bd = ((r2 // HEAD_DIM) == (c2 // HEAD_DIM)).astype(BF16)
    parts = []
    for p in range(N_PAIRS):
        cols = slice(p * LANES, (p + 1) * LANES)
        op = o[:, cols]
        mu = _head_sums(op, ones_bd) * (1.0 / HEAD_DIM)
        oc = op - mu
        var = _head_sums(oc * oc, ones_bd) * (1.0 / HEAD_DIM)
        on = oc * lax.rsqrt(var + GN_EPS) * gng_ref[:, cols] + gnb_ref[:, cols]
        bonus = _head_sums(rkk[:, cols], ones_bd) * v[:, cols]
        parts.append((on + bonus) * gate[:, cols])
    mix_in = jnp.concatenate([conv_ref[...]] + parts, axis=1)
    mix = jnp.dot(mix_in.astype(BF16), wout_ref[...], preferred_element_type=F32)
    x1 = x_ref[...] + m[2:3] * _rms(mix, postg_ref[...])
    x1_ref[...] = x1
    h2 = _rms(x1, preg_ref[...]) * (1.0 + m[4:5]) + m[3:4]
    h2_ref[...] = h2
    hh, hl = _split2(h2)
    dot = functools.partial(jnp.dot, preferred_element_type=F32)
    logits = dot(hh, rw_ref[0]) + (dot(hh, rw_ref[1]) + dot(hl, rw_ref[0])) + rb_ref[...]
    lane = lax.broadcasted_iota(jnp.int32, logits.shape, 1)
    work = logits
    picks, vals, idxs = [], [], []
    for _ in range(TOP_K):
        mx = jnp.max(work, axis=-1, keepdims=True)
        idx = jnp.min(jnp.where(work == mx, lane, LANES), axis=-1, keepdims=True)
        pick = lane == idx
        picks.append(pick)
        vals.append(mx)
        idxs.append(idx)
        work = jnp.where(pick, 2.0 * NEG_BIG, work)
    exps = [jnp.exp(val - vals[0]) for val in vals]
    den = exps[0] + exps[1] + exps[2] + exps[3]
    sel = jnp.zeros(logits.shape, F32)
    for pick in picks:
        sel = jnp.where(pick, 1.0, sel)
    tr = lax.broadcasted_iota(jnp.int32, (TM_OUT, TM_OUT), 0)
    tc = lax.broadcasted_iota(jnp.int32, (TM_OUT, TM_OUT), 1)
    before = (tc < tr).astype(BF16)
    rank = dot(before, sel.astype(BF16)) + cnt_ref[...]
    cnt_ref[...] = cnt_ref[...] + jnp.sum(sel, axis=0, keepdims=True)
    route = jnp.zeros(logits.shape, F32)
    for q, (pick, idx, e) in enumerate(zip(picks, idxs, exps)):
        rank_q = jnp.sum(jnp.where(pick, rank, 0.0), axis=-1, keepdims=True)
        route = jnp.where(lane == q, idx.astype(F32), route)
        route = jnp.where(lane == TOP_K + q, e / den, route)
        route = jnp.where(lane == 2 * TOP_K + q, rank_q, route)
    route_ref[...] = route


def _out_call(x, mod, conv_out, yf, yb, rkv, lora, wts, seq_len):
    n, d = x.shape
    row = lambda i: (i, 0)
    c2 = lambda i: (0, 0)
    c3 = lambda i: (0, 0, 0)
    full = lambda a: pl.BlockSpec(a.shape, c3 if a.ndim == 3 else c2)
    consts = [wts["a2p"], wts["a0"], wts["k_a"], wts["r_k"], wts["gn_g"], wts["gn_b"], wts["g2"],
              wts["w_out"], wts["post_mix_g"], wts["pre_ffn_g"], wts["router_w"], wts["router_b"]]
    return pl.pallas_call(
        _out_kernel,
        grid=(n // TM_OUT,),
        in_specs=[pl.BlockSpec((TM_OUT, d), row),
                  pl.BlockSpec((1, N_MOD, d), _mod_index(TM_OUT, seq_len, mod.shape[0])),
                  pl.BlockSpec((TM_OUT, CONV_CH), row),
                  pl.BlockSpec((TM_OUT, RWKV_DIM), row),
                  pl.BlockSpec((TM_OUT, RWKV_DIM), row),
                  pl.BlockSpec((TM_OUT, 3 * RWKV_DIM), row),
                  pl.BlockSpec((TM_OUT, LORA_COLS), row)] + [full(a) for a in consts],
        out_specs=[pl.BlockSpec((TM_OUT, d), row),
                   pl.BlockSpec((TM_OUT, d), row),
                   pl.BlockSpec((TM_OUT, LANES), row),
                   pl.BlockSpec((1, LANES), c2)],
        out_shape=[jax.ShapeDtypeStruct((n, d), F32),
                   jax.ShapeDtypeStruct((n, d), F32),
                   jax.ShapeDtypeStruct((n, LANES), F32),
                   jax.ShapeDtypeStruct((1, LANES), F32)],
        compiler_params=_params("arbitrary"),
        name="out_router",
    )(x, mod, conv_out, yf, yb, rkv, lora, *consts)


def _sc_gather_rows(table, idx):
    n_rows, d = idx.shape[0], table.shape[1]
    workers = SC_CORES * SC_SUBCORES
    per_worker = n_rows // workers
    assert per_worker * workers == n_rows and per_worker % SC_ROWS == 0
    mesh = plsc.VectorSubcoreMesh(core_axis_name="c", subcore_axis_name="s",
                                  num_cores=SC_CORES, num_subcores=SC_SUBCORES)

    @functools.partial(
        pl.kernel, mesh=mesh,
        out_type=jax.ShapeDtypeStruct((n_rows, d), table.dtype),
        scratch_types=[pltpu.VMEM((SC_ROWS,), jnp.int32),
                       pltpu.VMEM((SC_ROWS, d), table.dtype),
                       pltpu.SemaphoreType.DMA],
        name="sc_gather_rows")
    def gather(table_hbm, idx_hbm, out_hbm, idx_v, rows_v, sem):
        wid = lax.axis_index("s") * SC_CORES + lax.axis_index("c")
        base = wid * per_worker

        @pl.loop(0, per_worker // SC_ROWS)
        def _(j):
            off = base + j * SC_ROWS
            pltpu.sync_copy(idx_hbm.at[pl.ds(off, SC_ROWS)], idx_v)
            pltpu.async_copy(table_hbm.at[idx_v], rows_v, sem).wait()
            pltpu.sync_copy(rows_v, out_hbm.at[pl.ds(off, SC_ROWS)])

    return gather(table, idx)


def _moe_kernel(blk_e_ref, blk_first_ref, n_used_ref, xs_ref, wgu_ref, bgu_ref, wd_ref, bd_ref,
                o_ref, wgu_bf_ref, wd_bf_ref):
    b = pl.program_id(0)

    @pl.when(blk_first_ref[b] == 1)
    def _():
        wgu_bf_ref[...] = wgu_ref[0].astype(BF16)
        wd_bf_ref[...] = wd_ref[0].astype(BF16)

    @pl.when(b < n_used_ref[0])
    def _():
        gu = jnp.dot(xs_ref[...].astype(BF16), wgu_bf_ref[...],
                     preferred_element_type=F32) + bgu_ref[0]
        x_glu = jnp.minimum(gu[:, :D_FF], SWIGLU_LIMIT)
        x_lin = jnp.clip(gu[:, D_FF:], -SWIGLU_LIMIT, SWIGLU_LIMIT)
        act = x_glu * _sigmoid(SWIGLU_ALPHA * x_glu) * (x_lin + 1.0)
        o_ref[...] = jnp.dot(act.astype(BF16), wd_bf_ref[...],
                             preferred_element_type=F32) + bd_ref[0]

    @pl.when(b >= n_used_ref[0])
    def _():
        o_ref[...] = jnp.zeros(o_ref.shape, F32)


def _moe_call(blk_e, blk_first, n_used, xs, wgu, bgu, wd, bd):
    n_slots, d = xs.shape
    ex = lambda b, be, bf, nu: (be[b], 0, 0)
    row = lambda b, be, bf, nu: (b, 0)
    grid_spec = pltpu.PrefetchScalarGridSpec(
        num_scalar_prefetch=3,
        grid=(n_slots // MOE_BLOCK,),
        in_specs=[pl.BlockSpec((MOE_BLOCK, d), row),
                  pl.BlockSpec((1, d, 2 * D_FF), ex),
                  pl.BlockSpec((1, 1, 2 * D_FF), ex),
                  pl.BlockSpec((1, D_FF, d), ex),
                  pl.BlockSpec((1, 1, d), ex)],
        out_specs=pl.BlockSpec((MOE_BLOCK, d), row),
        scratch_shapes=[pltpu.VMEM((d, 2 * D_FF), BF16), pltpu.VMEM((D_FF, d), BF16)])
    return pl.pallas_call(
        _moe_kernel,
        grid_spec=grid_spec,
        out_shape=jax.ShapeDtypeStruct((n_slots, d), F32),
        compiler_params=_params("arbitrary"),
        name="moe_experts",
    )(blk_e, blk_first, n_used, xs, wgu, bgu, wd, bd)


def _final_kernel(x1_ref, y4_ref, route_ref, mod_ref, g_ref, o_ref):
    m = mod_ref[0]
    d = x1_ref.shape[1]
    route = route_ref[...]
    y = jnp.zeros(x1_ref.shape, F32)
    for q in range(TOP_K):
        y = y + route[:, TOP_K + q:TOP_K + q + 1] * y4_ref[:, q * d:(q + 1) * d]
    o_ref[...] = x1_ref[...] + m[5:6] * _rms(y, g_ref[...])


def _final_call(x1, y4, route, mod, g, seq_len, row_offset):
    n, d = x1.shape
    row = lambda i: (i, 0)
    shifted = lambda i: (i + row_offset // TM_FIN, 0)
    return pl.pallas_call(
        _final_kernel,
        grid=(n // TM_FIN,),
        in_specs=[pl.BlockSpec((TM_FIN, d), row),
                  pl.BlockSpec((TM_FIN, TOP_K * d), shifted),
                  pl.BlockSpec((TM_FIN, LANES), row),
                  pl.BlockSpec((1, N_MOD, d), _mod_index(TM_FIN, seq_len, mod.shape[0])),
                  pl.BlockSpec((1, d), lambda i: (0, 0))],
        out_specs=pl.BlockSpec((TM_FIN, d), row),
        out_shape=jax.ShapeDtypeStruct((n, d), F32),
        compiler_params=_params("arbitrary"),
        name="final_residual",
    )(x1, y4, route, mod, g)


def _pack_state(s):
    b = s.shape[0]
    st = jnp.swapaxes(s.astype(F32), -1, -2).reshape(b, N_PAIRS, 2, HEAD_DIM, HEAD_DIM)
    z = jnp.zeros_like(st[:, :, 0])
    top = jnp.concatenate([st[:, :, 0], z], axis=-1)
    bot = jnp.concatenate([z, st[:, :, 1]], axis=-1)
    return jnp.concatenate([top, bot], axis=-2)


def _unpack_state(st):
    b = st.shape[0]
    h0 = st[:, :, :HEAD_DIM, :HEAD_DIM]
    h1 = st[:, :, HEAD_DIM:, HEAD_DIM:]
    s = jnp.stack([h0, h1], axis=2).reshape(b, RWKV_HEADS, HEAD_DIM, HEAD_DIM)
    return jnp.swapaxes(s, -1, -2)


def _mix_sublayer(x, mod, s0f, s0b, seg_len, wts):
    batch, seq_len, d = x.shape
    x2 = x.reshape(batch * seq_len, d)
    cv, rkv, lora = _in_call(x2, mod, wts["pre_mix_g"], wts["w_all"], seq_len)
    conv_out = _conv_call(cv, wts["conv_w"], wts["conv_b"], wts["conv_ln_g"], wts["conv_ln_b"],
                          seg_len)
    yf, yb, sf, sb = _wkv_call(rkv, lora, s0f, s0b, wts["w2p"], wts["w0"], wts["a2p"], wts["a0"],
                               wts["k_k"], wts["k_a"], batch, seq_len)
    x1, h2, route, cnt = _out_call(x2, mod, conv_out, yf, yb, rkv, lora, wts, seq_len)
    return x1, h2, route, cnt, sf, sb


def _dispatch_plan(routes, cnts):
    experts = jnp.arange(N_EXPERTS, dtype=jnp.int32)
    counts = [c[0, :N_EXPERTS].astype(jnp.int32) for c in cnts]
    total = sum(counts)
    padded = (total + MOE_BLOCK - 1) // MOE_BLOCK * MOE_BLOCK
    pad_end = jnp.cumsum(padded)
    pad_start = pad_end - padded
    dests = []
    prior = jnp.zeros_like(total)
    for route, count in zip(routes, counts):
        idx = route[:, :TOP_K].astype(jnp.int32)
        rank = route[:, 2 * TOP_K:3 * TOP_K].astype(jnp.int32)
        first = jnp.sum(jnp.where(idx[..., None] == experts, pad_start + prior, 0), axis=-1)
        dests.append(first + rank)
        prior = prior + count
    dest = jnp.concatenate(dests, axis=0).reshape(-1)
    n_tok = dest.shape[0] // TOP_K
    n_slots = n_tok * TOP_K + N_EXPERTS * MOE_BLOCK
    n_blocks = n_slots // MOE_BLOCK
    tok_of_slot = (jnp.arange(n_slots, dtype=jnp.int32) % n_tok).at[dest].set(
        jnp.arange(n_tok * TOP_K, dtype=jnp.int32) // TOP_K, unique_indices=True)
    blk_e = jnp.searchsorted(pad_end, jnp.arange(n_blocks, dtype=jnp.int32) * MOE_BLOCK,
                             side="right")
    blk_e = jnp.minimum(blk_e, N_EXPERTS - 1).astype(jnp.int32)
    blk_first = jnp.concatenate([jnp.ones((1,), jnp.int32),
                                 (blk_e[1:] != blk_e[:-1]).astype(jnp.int32)])
    n_used = (pad_end[-1:] // MOE_BLOCK).astype(jnp.int32)
    return dest, tok_of_slot, blk_e, blk_first, n_used


def _routed_ffn(h2s, routes, cnts, wts):
    dest, tok_of_slot, blk_e, blk_first, n_used = _dispatch_plan(routes, cnts)
    h2 = jnp.concatenate(h2s, axis=0)
    xs = _sc_gather_rows(h2, tok_of_slot)
    out_sorted = _moe_call(blk_e, blk_first, n_used, xs, wts["w_gu"], wts["b_gu"],
                           wts["w_down"], wts["b_down"])
    y4 = _sc_gather_rows(out_sorted, dest)
    return y4.reshape(h2.shape[0], TOP_K * h2.shape[1])


def _prep_weights(l, pre_mix_g, post_mix_g, pre_ffn_g, post_ffn_g, w_in, w_out, conv_w, conv_b,
                  conv_ln_g, conv_ln_b, rw_w0, rw_w1, rw_w2, rw_a0, rw_a1, rw_a2, rw_g1, rw_g2,
                  rw_k_k, rw_k_a, rw_r_k, rw_gn_g, rw_gn_b, router_w, router_b, w_gu, b_gu,
                  w_down, b_down):
    row = lambda a: a.reshape(1, -1).astype(F32)
    zpad = jnp.zeros((HEAD_DIM, RWKV_DIM), F32)
    w_all = jnp.concatenate([w_in[l], rw_w1[l, 0], rw_w1[l, 1], rw_a1[l, 0], rw_a1[l, 1],
                             rw_g1[l]], axis=1).astype(BF16)
    w2p = jnp.stack([jnp.concatenate([rw_w2[l, 0], zpad], axis=0),
                     jnp.concatenate([zpad, rw_w2[l, 1]], axis=0)]).astype(BF16)
    a2p = jnp.stack([jnp.concatenate([rw_a2[l, 0], zpad], axis=0),
                     jnp.concatenate([zpad, rw_a2[l, 1]], axis=0)]).astype(BF16)
    rw_pad = jnp.pad(router_w[l].astype(F32), ((0, 0), (0, LANES - N_EXPERTS)))
    rw_hi = rw_pad.astype(BF16)
    rw_lo = (rw_pad - rw_hi.astype(F32)).astype(BF16)
    rb_pad = jnp.concatenate([router_b[l].astype(F32),
                              jnp.full((LANES - N_EXPERTS,), NEG_BIG, F32)]).reshape(1, LANES)
    return {
        "pre_mix_g": row(pre_mix_g[l]), "post_mix_g": row(post_mix_g[l]),
        "pre_ffn_g": row(pre_ffn_g[l]), "post_ffn_g": row(post_ffn_g[l]),
        "w_all": w_all, "w_out": w_out[l].astype(BF16),
        "conv_w": conv_w[l].astype(F32), "conv_b": row(conv_b[l]),
        "conv_ln_g": row(conv_ln_g[l]), "conv_ln_b": row(conv_ln_b[l]),
        "w2p": w2p, "w0": rw_w0[l].astype(F32), "a2p": a2p, "a0": rw_a0[l].astype(F32),
        "k_k": row(rw_k_k[l]), "k_a": row(rw_k_a[l]), "r_k": row(rw_r_k[l]),
        "gn_g": row(rw_gn_g[l]), "gn_b": row(rw_gn_b[l]), "g2": rw_g2[l].astype(BF16),
        "router_w": jnp.stack([rw_hi, rw_lo]), "router_b": rb_pad,
        "w_gu": w_gu[l].astype(F32), "b_gu": b_gu[l].reshape(N_EXPERTS, 1, -1).astype(F32),
        "w_down": w_down[l].astype(F32), "b_down": b_down[l].reshape(N_EXPERTS, 1, -1).astype(F32),
    }


def kernel(x_prompt, x_sample, state_wkv_fwd, state_wkv_bwd, c, c_ctx, ada_w, ada_b, pre_mix_g, post_mix_g, pre_ffn_g, post_ffn_g, w_in, w_out, conv_w, conv_b, conv_ln_g, conv_ln_b, rw_w0, rw_w1, rw_w2, rw_a0, rw_a1, rw_a2, rw_g1, rw_g2, rw_k_k, rw_k_a, rw_r_k, rw_gn_g, rw_gn_b, router_w, router_b, w_gu, b_gu, w_down, b_down):
    depth = ada_w.shape[0]
    dec_batch = c.shape[0]
    mod_rows = 16
    c_rows = jnp.concatenate([c, c_ctx[None, :],
                              jnp.zeros((mod_rows - dec_batch - 1, D_MODEL), F32)], axis=0)
    zero_state = jnp.zeros((x_prompt.shape[0], N_PAIRS, LANES, LANES), F32)
    y_prompt, y_sample = x_prompt, x_sample
    new_f, new_b = [], []
    for l in range(depth):
        wts = _prep_weights(l, pre_mix_g, post_mix_g, pre_ffn_g, post_ffn_g, w_in, w_out, conv_w,
                            conv_b, conv_ln_g, conv_ln_b, rw_w0, rw_w1, rw_w2, rw_a0, rw_a1, rw_a2,
                            rw_g1, rw_g2, rw_k_k, rw_k_a, rw_r_k, rw_gn_g, rw_gn_b, router_w,
                            router_b, w_gu, b_gu, w_down, b_down)
        mod = _mod_call(c_rows, ada_w[l], ada_b[l]).reshape(mod_rows, N_MOD, D_MODEL)
        mod_lat = mod[:dec_batch]
        mod_ctx = mod[dec_batch:dec_batch + 1]
        ctx = _mix_sublayer(y_prompt, mod_ctx, zero_state, zero_state, y_prompt.shape[1], wts)
        lat = _mix_sublayer(y_sample, mod_lat, _pack_state(state_wkv_fwd[:, l]),
                            _pack_state(state_wkv_bwd[:, l]), GRID_W, wts)
        new_f.append(_unpack_state(ctx[4]))
        new_b.append(_unpack_state(ctx[5]))
        y4 = _routed_ffn([ctx[1], lat[1]], [ctx[2], lat[2]], [ctx[3], lat[3]], wts)
        n_ctx = ctx[0].shape[0]
        y_prompt = _final_call(ctx[0], y4, ctx[2], mod_ctx, wts["post_ffn_g"], y_prompt.shape[1],
                               0).reshape(y_prompt.shape)
        y_sample = _final_call(lat[0], y4, lat[2], mod_lat, wts["post_ffn_g"], y_sample.shape[1],
                               n_ctx).reshape(y_sample.shape)
    return (y_prompt, y_sample, jnp.stack(new_f, axis=1), jnp.stack(new_b, axis=1))
```

```python
import functools
import math

import jax
import jax.numpy as jnp
from jax import lax
from jax.experimental import pallas as pl
from jax.experimental.pallas import tpu as pltpu
from jax.experimental.pallas import tpu_sc as plsc

F32 = jnp.float32
BF16 = jnp.bfloat16

D_MODEL = 1024
CONV_CH = 512
RWKV_DIM = 512
HEAD_DIM = 64
RWKV_HEADS = 8
N_PAIRS = RWKV_HEADS // 2
CONV_WIDTH = 31
N_EXPERTS = 32
TOP_K = 4
D_FF = 1024
SWIGLU_LIMIT = 7.0
SWIGLU_ALPHA = 1.702
RMS_EPS = 1e-6
LN_EPS = 1e-5
GN_EPS = 64e-5
N_MOD = 6
GRID_W = 64

LANES = 128
CHUNK = 64
LORA_COLS = 384
TM_IN = 512
TM_CONV = 256
TM_OUT = 256
TM_FIN = 512
MOE_BLOCK = 256
SC_CORES = 2
SC_SUBCORES = 16
SC_ROWS = 64
NEG_BIG = -1e30
EXP_M05 = math.exp(-0.5)
VMEM_LIMIT = 56 * 1024 * 1024


def _sigmoid(x):
    return 1.0 / (1.0 + jnp.exp(-x))


def _mm(a, b):
    return jnp.dot(a.astype(BF16), b.astype(BF16), preferred_element_type=F32)


def _mm_nt(a, b):
    return lax.dot_general(a.astype(BF16), b.astype(BF16), (((1,), (1,)), ((), ())),
                           preferred_element_type=F32)


def _mm_tn(a, b):
    return lax.dot_general(a.astype(BF16), b.astype(BF16), (((0,), (0,)), ((), ())),
                           preferred_element_type=F32)


def _split2(x):
    hi = x.astype(BF16)
    lo = (x - hi.astype(F32)).astype(BF16)
    return hi, lo


def _split3(x):
    h1 = x.astype(BF16)
    r1 = x - h1.astype(F32)
    h2 = r1.astype(BF16)
    h3 = (r1 - h2.astype(F32)).astype(BF16)
    return h1, h2, h3


def _rms(x, g):
    return x * lax.rsqrt(jnp.mean(x * x, axis=-1, keepdims=True) + RMS_EPS) * g


def _params(*sem):
    return pltpu.CompilerParams(dimension_semantics=sem, vmem_limit_bytes=VMEM_LIMIT)


def _mod_kernel(c_ref, w_ref, b_ref, o_ref):
    c = c_ref[...]
    s1, s2, s3 = _split3(c * _sigmoid(c))
    w1, w2, w3 = _split3(w_ref[...])
    dot = functools.partial(jnp.dot, preferred_element_type=F32)
    acc = dot(s1, w1) + (dot(s1, w2) + dot(s2, w1)) + (dot(s2, w2) + dot(s1, w3) + dot(s3, w1))
    o_ref[...] = acc + b_ref[...]


def _mod_call(c_rows, ada_w, ada_b):
    m, d = c_rows.shape
    n = ada_w.shape[1]
    tn = 512
    return pl.pallas_call(
        _mod_kernel,
        grid=(n // tn,),
        in_specs=[pl.BlockSpec((m, d), lambda j: (0, 0)),
                  pl.BlockSpec((d, tn), lambda j: (0, j)),
                  pl.BlockSpec((1, tn), lambda j: (0, j))],
        out_specs=pl.BlockSpec((m, tn), lambda j: (0, j)),
        out_shape=jax.ShapeDtypeStruct((m, n), F32),
        compiler_params=_params("arbitrary"),
        name="mod",
    )(c_rows, ada_w, ada_b.reshape(1, n))


def _in_kernel(x_ref, mod_ref, g_ref, w_ref, cv_ref, rkv_ref, lo_ref):
    m = mod_ref[0]
    h = _rms(x_ref[...], g_ref[...]) * (1.0 + m[1:2]) + m[0:1]
    proj = jnp.dot(h.astype(BF16), w_ref[...], preferred_element_type=F32)
    cv_ref[...] = proj[:, :2 * CONV_CH]
    rkv_ref[...] = proj[:, 2 * CONV_CH:2 * CONV_CH + 3 * RWKV_DIM]
    lo_ref[...] = proj[:, 2 * CONV_CH + 3 * RWKV_DIM:]


def _mod_index(tile, seq_len, n_mod_rows):
    def index(i):
        return ((i * tile) // seq_len) % n_mod_rows, 0, 0
    return index


def _in_call(x, mod, g, w_all, seq_len):
    n, d = x.shape
    ncol = w_all.shape[1]
    return pl.pallas_call(
        _in_kernel,
        grid=(n // TM_IN,),
        in_specs=[pl.BlockSpec((TM_IN, d), lambda i: (i, 0)),
                  pl.BlockSpec((1, N_MOD, d), _mod_index(TM_IN, seq_len, mod.shape[0])),
                  pl.BlockSpec((1, d), lambda i: (0, 0)),
                  pl.BlockSpec((d, ncol), lambda i: (0, 0))],
        out_specs=[pl.BlockSpec((TM_IN, 2 * CONV_CH), lambda i: (i, 0)),
                   pl.BlockSpec((TM_IN, 3 * RWKV_DIM), lambda i: (i, 0)),
                   pl.BlockSpec((TM_IN, LORA_COLS), lambda i: (i, 0))],
        out_shape=[jax.ShapeDtypeStruct((n, 2 * CONV_CH), F32),
                   jax.ShapeDtypeStruct((n, 3 * RWKV_DIM), F32),
                   jax.ShapeDtypeStruct((n, LORA_COLS), F32)],
        compiler_params=_params("arbitrary"),
        name="in_proj",
    )(x, mod, g, w_all)


CONV_HALO = 16
CONV_ROWS = 64


def _conv_kernel(cv_ref, w_ref, b_ref, g_ref, be_ref, o_ref, pad_ref, acc_ref, *, seg_len):
    nseg = TM_CONV // seg_len
    stride = seg_len + 2 * CONV_HALO
    cv = cv_ref[...]
    u = cv[:, :CONV_CH] * _sigmoid(cv[:, CONV_CH:])
    pad_ref[...] = jnp.zeros(pad_ref.shape, F32)
    for s in range(nseg):
        pad_ref[s * stride + CONV_HALO:s * stride + CONV_HALO + seg_len, :] = (
            u[s * seg_len:(s + 1) * seg_len])
    first = CONV_HALO - CONV_WIDTH // 2
    for s in range(nseg):
        for rc in range(seg_len // CONV_ROWS):
            for lc in range(CONV_CH // LANES):
                cols = slice(lc * LANES, (lc + 1) * LANES)
                acc = jnp.zeros((CONV_ROWS, LANES), F32)
                for t in range(CONV_WIDTH):
                    base = s * stride + first + t + rc * CONV_ROWS
                    acc = acc + w_ref[t:t + 1, cols] * pad_ref[base:base + CONV_ROWS, cols]
                row0 = s * seg_len + rc * CONV_ROWS
                acc_ref[row0:row0 + CONV_ROWS, cols] = acc
    y = acc_ref[...] + b_ref[...]
    mu = jnp.mean(y, axis=-1, keepdims=True)
    yc = y - mu
    var = jnp.mean(yc * yc, axis=-1, keepdims=True)
    z = yc * lax.rsqrt(var + LN_EPS) * g_ref[...] + be_ref[...]
    o_ref[...] = z * _sigmoid(z)


def _conv_call(cv, conv_w, conv_b, ln_g, ln_b, seg_len):
    n = cv.shape[0]
    nseg = TM_CONV // seg_len
    row = lambda i: (i, 0)
    const = lambda i: (0, 0)
    return pl.pallas_call(
        functools.partial(_conv_kernel, seg_len=seg_len),
        grid=(n // TM_CONV,),
        in_specs=[pl.BlockSpec((TM_CONV, 2 * CONV_CH), row),
                  pl.BlockSpec((CONV_WIDTH, CONV_CH), const),
                  pl.BlockSpec((1, CONV_CH), const),
                  pl.BlockSpec((1, CONV_CH), const),
                  pl.BlockSpec((1, CONV_CH), const)],
        out_specs=pl.BlockSpec((TM_CONV, CONV_CH), row),
        out_shape=jax.ShapeDtypeStruct((n, CONV_CH), F32),
        scratch_shapes=[pltpu.VMEM((nseg * (seg_len + 2 * CONV_HALO), CONV_CH), F32),
                        pltpu.VMEM((TM_CONV, CONV_CH), F32)],
        compiler_params=_params("arbitrary"),
        name="conv_module",
    )(cv, conv_w, conv_b, ln_g, ln_b)


def _head_sums(x, ones_bd):
    hi, lo = _split2(x)
    return (jnp.dot(hi, ones_bd, preferred_element_type=F32)
            + jnp.dot(lo, ones_bd, preferred_element_type=F32))


def _wkv_masks():
    c = CHUNK
    r2 = lax.broadcasted_iota(jnp.int32, (2 * c, 2 * c), 0)
    c2 = lax.broadcasted_iota(jnp.int32, (2 * c, 2 * c), 1)
    same = (r2 // c) == (c2 // c)
    rr, cc = r2 % c, c2 % c
    r1 = lax.broadcasted_iota(jnp.int32, (c, 2 * c), 0)
    c1 = lax.broadcasted_iota(jnp.int32, (c, 2 * c), 1)
    cc1 = c1 % c
    rs = lax.broadcasted_iota(jnp.int32, (c, c), 0)
    cs = lax.broadcasted_iota(jnp.int32, (c, c), 1)
    masks = {"same": same, "eye": r2 == c2, "left": c1 < c,
             "ones_bd": same.astype(BF16)}
    for name, lt in (("f", lambda a, b: a < b), ("b", lambda a, b: a > b)):
        le = (lambda a, b, lt=lt: lt(a, b) | (a == b))
        masks["strict_" + name] = same & lt(cc, rr)
        masks["inc_" + name] = le(cc1, r1)
        masks["ak0_" + name] = (c1 >= c) & lt(cc1, r1)
        masks["ak1_" + name] = (c1 < c) & lt(cc1, r1)
        masks["tri_" + name] = le(cs, rs).astype(BF16)
    return masks


def _wkv_chains(chains, mk):
    c = CHUNK
    left = mk["left"]
    for ch in chains:
        zero = jnp.zeros_like(ch["at"])
        ch["at0"], ch["at1"] = jnp.where(left, ch["at"], zero), jnp.where(left, zero, ch["at"])
        rt0, rt1 = jnp.where(left, ch["rt"], zero), jnp.where(left, zero, ch["rt"])
        bt, kt = ch["bt"].astype(BF16), ch["kt"].astype(BF16)
        ch["g0"] = _mm_nt(jnp.concatenate([ch["at0"], rt0], axis=0), jnp.concatenate([bt, kt], axis=0))
        ch["g1"] = _mm_nt(jnp.concatenate([ch["at1"], rt1], axis=0), jnp.concatenate([kt, bt], axis=0))
    for ch in chains:
        g0, g1 = ch["g0"], ch["g1"]
        lmat = jnp.where(mk["strict_" + ch["d"]], jnp.concatenate([g0[:c], g1[:c]], axis=0), 0.0)
        ch["tinv"] = jnp.where(mk["eye"], 1.0, lmat)
        ch["lpow"] = _mm(lmat, lmat)
    for step in range(5):
        for ch in chains:
            lp = ch["lpow"].astype(BF16)
            if step < 4:
                prod = jnp.dot(lp, jnp.concatenate([lp, ch["tinv"].astype(BF16)], axis=1),
                               preferred_element_type=F32)
                ch["lpow"] = prod[:, :LANES]
                ch["tinv"] = ch["tinv"] + prod[:, LANES:]
            else:
                ch["tinv"] = ch["tinv"] + _mm(lp, ch["tinv"])
    for ch in chains:
        d, v = ch["d"], ch["v"]
        zv = jnp.zeros_like(v)
        ak0 = jnp.where(mk["ak0_" + d], ch["g0"][:c], 0.0)
        ak1 = jnp.where(mk["ak1_" + d], ch["g1"][:c], 0.0)
        ch["x0"] = _mm(ak0, jnp.concatenate([zv, v], axis=0))
        ch["x1"] = _mm(ak1, jnp.concatenate([v, zv], axis=0))
    for ch in chains:
        zero = jnp.zeros_like(ch["x0"])
        rhs = jnp.concatenate([
            jnp.concatenate([ch["at0"], jnp.where(left, ch["x0"], zero)], axis=1),
            jnp.concatenate([ch["at1"], jnp.where(left, zero, ch["x1"])], axis=1)], axis=0)
        ch["aw_blk"] = _mm(ch["tinv"], rhs)
    for ch in chains:
        d, v, aw_blk = ch["d"], ch["v"], ch["aw_blk"]
        zv = jnp.zeros_like(v)
        aw = aw_blk[:c] + aw_blk[c:]
        inc = mk["inc_" + d]
        g0, g1 = ch["g0"], ch["g1"]
        rb = jnp.where(inc, jnp.where(left, g0[c:], g1[c:]), 0.0)
        rk = jnp.where(inc, jnp.where(left, g1[c:], g0[c:]), 0.0)
        v0, v1 = jnp.where(left, v, zv), jnp.where(left, zv, v)
        vblk = jnp.concatenate([jnp.zeros((2 * c, LANES), F32),
                                jnp.concatenate([v1, v0], axis=0)], axis=1)
        qz = _mm(jnp.concatenate([rb, rk], axis=1), jnp.concatenate([aw_blk, vblk], axis=0))
        ch["q"] = ch["rt"] + qz[:, :LANES]
        ch["zz"] = qz[:, LANES:]
        rhs3 = jnp.concatenate([aw, jnp.concatenate([zv, v], axis=1)], axis=0)
        ch["mn"] = _mm_tn(jnp.concatenate([ch["bh"], ch["kh"]], axis=0), rhs3)
    same = mk["same"]
    for ch in chains:
        mn = ch["mn"]
        mt = jnp.where(same, mn[:, :LANES], 0.0) + jnp.where(mk["eye"], ch["ptot"], 0.0)
        nt = jnp.where(same, mn[:, LANES:], 0.0)
        both = _mm(jnp.concatenate([ch["q"], mt], axis=0), ch["st"])
        ch["y"] = both[:c] + ch["zz"]
        ch["st_new"] = both[c:] + nt


def _wkv_kernel(rkv_f_ref, lo_f_ref, rkv_b_ref, lo_b_ref, s0f_ref, s0b_ref,
                w2_ref, w0_ref, a2_ref, a0_ref, kk_ref, ka_ref,
                yf_ref, yb_ref, sf_ref, sb_ref, st_ref):
    ci = pl.program_id(1)
    nc = pl.num_programs(1)

    @pl.when(ci == 0)
    def _():
        st_ref[0] = s0f_ref[0]
        st_ref[1] = s0b_ref[0]

    mk = _wkv_masks()
    c = CHUNK
    chains = []
    for di, (d, rkv_ref, lo_ref, y_ref) in enumerate(
            (("f", rkv_f_ref, lo_f_ref, yf_ref), ("b", rkv_b_ref, lo_b_ref, yb_ref))):
        rkv = rkv_ref[...]
        lo = lo_ref[...]
        r = rkv[:, :RWKV_DIM]
        k = rkv[:, RWKV_DIM:2 * RWKV_DIM]
        v = rkv[:, 2 * RWKV_DIM:]
        w_raw = w0_ref[di:di + 1, :] + _mm(jnp.tanh(lo[:, :LANES]), w2_ref[di])
        lw = -EXP_M05 * _sigmoid(w_raw)
        ag = _sigmoid(a0_ref[di:di + 1, :] + _mm(lo[:, LANES:2 * LANES], a2_ref[di]))
        kd = k * (1.0 + (ag - 1.0) * ka_ref[...])
        kk_raw = k * kk_ref[...]
        h1, h2, h3 = _split3(lw)
        cs = jnp.dot(mk["tri_" + d], jnp.concatenate([h1, h2, h3], axis=1),
                     preferred_element_type=F32)
        logp = cs[:, :RWKV_DIM] + cs[:, RWKV_DIM:2 * RWKV_DIM] + cs[:, 2 * RWKV_DIM:]
        tot = logp[0:1] if d == "b" else logp[c - 1:c]
        e_inv = jnp.exp(-logp)
        e_hat = jnp.exp(tot - logp)
        rt = r * jnp.exp(logp)
        e_ex = jnp.exp(logp - lw)
        kt, kh = kd * e_inv, kd * e_hat
        ptot = jnp.exp(tot)
        for p in range(N_PAIRS):
            cols = slice(p * LANES, (p + 1) * LANES)
            chains.append({"d": d, "di": di, "p": p, "y_ref": y_ref, "cols": cols,
                           "kk": kk_raw[:, cols], "ag": ag[:, cols], "e_ex": e_ex[:, cols],
                           "e_inv": e_inv[:, cols], "e_hat": e_hat[:, cols],
                           "rt": rt[:, cols], "kt": kt[:, cols], "kh": kh[:, cols],
                           "v": v[:, cols], "ptot": ptot[:, cols], "st": st_ref[di, p]})
    for ch in chains:
        kk = ch["kk"]
        kk = kk * lax.rsqrt(jnp.maximum(_head_sums(kk * kk, mk["ones_bd"]), 1e-24))
        kb = kk * ch["ag"]
        ch["at"] = -kk * ch["e_ex"]
        ch["bt"] = kb * ch["e_inv"]
        ch["bh"] = kb * ch["e_hat"]
    _wkv_chains(chains, mk)
    for ch in chains:
        ch["y_ref"][:, ch["cols"]] = ch["y"]
        st_ref[ch["di"], ch["p"]] = ch["st_new"]

    @pl.when(ci == nc - 1)
    def _():
        sf_ref[0] = st_ref[0]
        sb_ref[0] = st_ref[1]


def _wkv_call(rkv, lora, s0f, s0b, w2p, w0, a2p, a0, k_k, k_a, batch, seq_len):
    n = rkv.shape[0]
    nc = seq_len // CHUNK
    fwd = lambda b, c: (b * nc + c, 0)
    bwd = lambda b, c: (b * nc + (nc - 1 - c), 0)
    st = lambda b, c: (b, 0, 0, 0)
    c2 = lambda b, c: (0, 0)
    c3 = lambda b, c: (0, 0, 0)
    st_shape = jax.ShapeDtypeStruct((batch, N_PAIRS, LANES, LANES), F32)
    return pl.pallas_call(
        _wkv_kernel,
        grid=(batch, nc),
        in_specs=[pl.BlockSpec((CHUNK, 3 * RWKV_DIM), fwd),
                  pl.BlockSpec((CHUNK, LORA_COLS), fwd),
                  pl.BlockSpec((CHUNK, 3 * RWKV_DIM), bwd),
                  pl.BlockSpec((CHUNK, LORA_COLS), bwd),
                  pl.BlockSpec((1, N_PAIRS, LANES, LANES), st),
                  pl.BlockSpec((1, N_PAIRS, LANES, LANES), st),
                  pl.BlockSpec((2, LANES, RWKV_DIM), c3),
                  pl.BlockSpec((2, RWKV_DIM), c2),
                  pl.BlockSpec((2, LANES, RWKV_DIM), c3),
                  pl.BlockSpec((2, RWKV_DIM), c2),
                  pl.BlockSpec((1, RWKV_DIM), c2),
                  pl.BlockSpec((1, RWKV_DIM), c2)],
        out_specs=[pl.BlockSpec((CHUNK, RWKV_DIM), fwd),
                   pl.BlockSpec((CHUNK, RWKV_DIM), bwd),
                   pl.BlockSpec((1, N_PAIRS, LANES, LANES), st),
                   pl.BlockSpec((1, N_PAIRS, LANES, LANES), st)],
        out_shape=[jax.ShapeDtypeStruct((n, RWKV_DIM), F32),
                   jax.ShapeDtypeStruct((n, RWKV_DIM), F32), st_shape, st_shape],
        scratch_shapes=[pltpu.VMEM((2, N_PAIRS, LANES, LANES), F32)],
        compiler_params=_params("arbitrary", "arbitrary"),
        name="wkv_chunked",
    )(rkv, lora, rkv, lora, s0f, s0b, w2p, w0, a2p, a0, k_k, k_a)


def _out_kernel(x_ref, mod_ref, conv_ref, yf_ref, yb_ref, rkv_ref, lo_ref,
                a2_ref, a0_ref, ka_ref, rk_ref, gng_ref, gnb_ref, g2_ref, wout_ref,
                postg_ref, preg_ref, rw_ref, rb_ref,
                x1_ref, h2_ref, route_ref, cnt_ref):
    i = pl.program_id(0)

    @pl.when(i == 0)
    def _():
        cnt_ref[...] = jnp.zeros(cnt_ref.shape, F32)

    m = mod_ref[0]
    rkv = rkv_ref[...]
    lo = lo_ref[...]
    r = rkv[:, :RWKV_DIM]
    k = rkv[:, RWKV_DIM:2 * RWKV_DIM]
    v = rkv[:, 2 * RWKV_DIM:]
    ha = lo[:, LANES:2 * LANES]
    ag_f = _sigmoid(a0_ref[0:1, :] + _mm(ha, a2_ref[0]))
    ag_b = _sigmoid(a0_ref[1:2, :] + _mm(ha, a2_ref[1]))
    rkk = r * k * rk_ref[...] * (2.0 + (ag_f + ag_b - 2.0) * ka_ref[...])
    o = yf_ref[...] + yb_ref[...]
    gate = _mm(_sigmoid(lo[:, 2 * LANES:]), g2_ref[...])
    r2 = lax.broadcasted_iota(jnp.int32, (LANES, LANES), 0)
    c2 = lax.broadcasted_iota(jnp.int32, (LANES, LANES), 1)
    ones_bd = ((r2 // HEAD_DIM) == (c2 // HEAD_DIM)).astype(BF16)
    parts = []
    for p in range(N_PAIRS):
        cols = slice(p * LANES, (p + 1) * LANES)
        op = o[:, cols]
        mu = _head_sums(op, ones_bd) * (1.0 / HEAD_DIM)
        oc = op - mu
        var = _head_sums(oc * oc, ones_bd) * (1.0 / HEAD_DIM)
        on = oc * lax.rsqrt(var + GN_EPS) * gng_ref[:, cols] + gnb_ref[:, cols]
        bonus = _head_sums(rkk[:, cols], ones_bd) * v[:, cols]
        parts.append((on + bonus) * gate[:, cols])
    mix_in = jnp.concatenate([conv_ref[...]] + parts, axis=1)
    mix = jnp.dot(mix_in.astype(BF16), wout_ref[...], preferred_element_type=F32)
    x1 = x_ref[...] + m[2:3] * _rms(mix, postg_ref[...])
    x1_ref[...] = x1
    h2 = _rms(x1, preg_ref[...]) * (1.0 + m[4:5]) + m[3:4]
    h2_ref[...] = h2
    hh, hl = _split2(h2)
    dot = functools.partial(jnp.dot, preferred_element_type=F32)
    logits = dot(hh, rw_ref[0]) + (dot(hh, rw_ref[1]) + dot(hl, rw_ref[0])) + rb_ref[...]
    lane = lax.broadcasted_iota(jnp.int32, logits.shape, 1)
    work = logits
    picks, vals, idxs = [], [], []
    for _ in range(TOP_K):
        mx = jnp.max(work, axis=-1, keepdims=True)
        idx = jnp.min(jnp.where(work == mx, lane, LANES), axis=-1, keepdims=True)
        pick = lane == idx
        picks.append(pick)
        vals.append(mx)
        idxs.append(idx)
        work = jnp.where(pick, 2.0 * NEG_BIG, work)
    exps = [jnp.exp(val - vals[0]) for val in vals]
    den = exps[0] + exps[1] + exps[2] + exps[3]
    sel = jnp.zeros(logits.shape, F32)
    for pick in picks:
        sel = jnp.where(pick, 1.0, sel)
    tr = lax.broadcasted_iota(jnp.int32, (TM_OUT, TM_OUT), 0)
    tc = lax.broadcasted_iota(jnp.int32, (TM_OUT, TM_OUT), 1)
    before = (tc < tr).astype(BF16)
    rank = dot(before, sel.astype(BF16)) + cnt_ref[...]
    cnt_ref[...] = cnt_ref[...] + jnp.sum(sel, axis=0, keepdims=True)
    route = jnp.zeros(logits.shape, F32)
    for q, (pick, idx, e) in enumerate(zip(picks, idxs, exps)):
        rank_q = jnp.sum(jnp.where(pick, rank, 0.0), axis=-1, keepdims=True)
        route = jnp.where(lane == q, idx.astype(F32), route)
        route = jnp.where(lane == TOP_K + q, e / den, route)
        route = jnp.where(lane == 2 * TOP_K + q, rank_q, route)
    route_ref[...] = route


def _out_call(x, mod, conv_out, yf, yb, rkv, lora, wts, seq_len):
    n, d = x.shape
    row = lambda i: (i, 0)
    c2 = lambda i: (0, 0)
    c3 = lambda i: (0, 0, 0)
    full = lambda a: pl.BlockSpec(a.shape, c3 if a.ndim == 3 else c2)
    consts = [wts["a2p"], wts["a0"], wts["k_a"], wts["r_k"], wts["gn_g"], wts["gn_b"], wts["g2"],
              wts["w_out"], wts["post_mix_g"], wts["pre_ffn_g"], wts["router_w"], wts["router_b"]]
    return pl.pallas_call(
        _out_kernel,
        grid=(n // TM_OUT,),
        in_specs=[pl.BlockSpec((TM_OUT, d), row),
                  pl.BlockSpec((1, N_MOD, d), _mod_index(TM_OUT, seq_len, mod.shape[0])),
                  pl.BlockSpec((TM_OUT, CONV_CH), row),
                  pl.BlockSpec((TM_OUT, RWKV_DIM), row),
                  pl.BlockSpec((TM_OUT, RWKV_DIM), row),
                  pl.BlockSpec((TM_OUT, 3 * RWKV_DIM), row),
                  pl.BlockSpec((TM_OUT, LORA_COLS), row)] + [full(a) for a in consts],
        out_specs=[pl.BlockSpec((TM_OUT, d), row),
                   pl.BlockSpec((TM_OUT, d), row),
                   pl.BlockSpec((TM_OUT, LANES), row),
                   pl.BlockSpec((1, LANES), c2)],
        out_shape=[jax.ShapeDtypeStruct((n, d), F32),
                   jax.ShapeDtypeStruct((n, d), F32),
                   jax.ShapeDtypeStruct((n, LANES), F32),
                   jax.ShapeDtypeStruct((1, LANES), F32)],
        compiler_params=_params("arbitrary"),
        name="out_router",
    )(x, mod, conv_out, yf, yb, rkv, lora, *consts)


def _sc_rows(src, idx, n_out, scatter):
    n_idx, (n_src, d) = idx.shape[0], src.shape
    workers = SC_CORES * SC_SUBCORES
    per_worker = n_idx // workers
    assert per_worker * workers == n_idx and per_worker % SC_ROWS == 0 and n_src % SC_ROWS == 0
    mesh = plsc.VectorSubcoreMesh(core_axis_name="c", subcore_axis_name="s",
                                  num_cores=SC_CORES, num_subcores=SC_SUBCORES)

    @functools.partial(
        pl.kernel, mesh=mesh,
        out_type=jax.ShapeDtypeStruct((n_out, d), src.dtype),
        scratch_types=[pltpu.VMEM((SC_ROWS,), jnp.int32),
                       pltpu.VMEM((SC_ROWS, d), src.dtype),
                       pltpu.SemaphoreType.DMA],
        name="sc_scatter_rows" if scatter else "sc_gather_rows")
    def move(src_hbm, idx_hbm, out_hbm, idx_v, rows_v, sem):
        wid = lax.axis_index("s") * SC_CORES + lax.axis_index("c")
        base = wid * per_worker

        @pl.loop(0, per_worker // SC_ROWS)
        def _(j):
            off = base + j * SC_ROWS
            pltpu.sync_copy(idx_hbm.at[pl.ds(off, SC_ROWS)], idx_v)
            if scatter:
                pltpu.sync_copy(src_hbm.at[pl.ds(lax.rem(off, n_src), SC_ROWS)], rows_v)
                pltpu.async_copy(rows_v, out_hbm.at[idx_v], sem).wait()
            else:
                pltpu.async_copy(src_hbm.at[idx_v], rows_v, sem).wait()
                pltpu.sync_copy(rows_v, out_hbm.at[pl.ds(off, SC_ROWS)])

    return move(src, idx)


def _moe_kernel(blk_e_ref, blk_first_ref, blk_rows_ref, xs_ref, wgu_ref, bgu_ref, wd_ref, bd_ref,
                o_ref, wgu_bf_ref, wd_bf_ref):
    b = pl.program_id(0)
    n_rows = blk_rows_ref[b]

    @pl.when(blk_first_ref[b] == 1)
    def _():
        wgu_bf_ref[...] = wgu_ref[0].astype(BF16)
        wd_bf_ref[...] = wd_ref[0].astype(BF16)

    @pl.when(n_rows > 0)
    def _():
        live = lax.broadcasted_iota(jnp.int32, xs_ref.shape, 0) < n_rows
        xs = jnp.where(live, xs_ref[...], 0.0)
        gu = jnp.dot(xs.astype(BF16), wgu_bf_ref[...],
                     preferred_element_type=F32) + bgu_ref[0]
        x_glu = jnp.minimum(gu[:, :D_FF], SWIGLU_LIMIT)
        x_lin = jnp.clip(gu[:, D_FF:], -SWIGLU_LIMIT, SWIGLU_LIMIT)
        act = x_glu * _sigmoid(SWIGLU_ALPHA * x_glu) * (x_lin + 1.0)
        o_ref[...] = jnp.dot(act.astype(BF16), wd_bf_ref[...],
                             preferred_element_type=F32) + bd_ref[0]

    @pl.when(n_rows == 0)
    def _():
        o_ref[...] = jnp.zeros(o_ref.shape, F32)


def _moe_call(blk_e, blk_first, blk_rows, xs, wgu, bgu, wd, bd):
    n_slots, d = xs.shape
    ex = lambda b, be, bf, nu: (be[b], 0, 0)
    row = lambda b, be, bf, nu: (b, 0)
    grid_spec = pltpu.PrefetchScalarGridSpec(
        num_scalar_prefetch=3,
        grid=(n_slots // MOE_BLOCK,),
        in_specs=[pl.BlockSpec((MOE_BLOCK, d), row),
                  pl.BlockSpec((1, d, 2 * D_FF), ex),
                  pl.BlockSpec((1, 1, 2 * D_FF), ex),
                  pl.BlockSpec((1, D_FF, d), ex),
                  pl.BlockSpec((1, 1, d), ex)],
        out_specs=pl.BlockSpec((MOE_BLOCK, d), row),
        scratch_shapes=[pltpu.VMEM((d, 2 * D_FF), BF16), pltpu.VMEM((D_FF, d), BF16)])
    return pl.pallas_call(
        _moe_kernel,
        grid_spec=grid_spec,
        out_shape=jax.ShapeDtypeStruct((n_slots, d), F32),
        compiler_params=_params("arbitrary"),
        name="moe_experts",
    )(blk_e, blk_first, blk_rows, xs, wgu, bgu, wd, bd)


def _final_kernel(x1_ref, y4_ref, route_ref, mod_ref, g_ref, o_ref):
    m = mod_ref[0]
    route = route_ref[...]
    y = jnp.zeros(x1_ref.shape, F32)
    for q in range(TOP_K):
        y = y + route[:, TOP_K + q:TOP_K + q + 1] * y4_ref[q]
    o_ref[...] = x1_ref[...] + m[5:6] * _rms(y, g_ref[...])


def _final_call(x1, y4, route, mod, g, seq_len, row_offset):
    n, d = x1.shape
    row = lambda i: (i, 0)
    shifted = lambda i: (0, i + row_offset // TM_FIN, 0)
    return pl.pallas_call(
        _final_kernel,
        grid=(n // TM_FIN,),
        in_specs=[pl.BlockSpec((TM_FIN, d), row),
                  pl.BlockSpec((TOP_K, TM_FIN, d), shifted),
                  pl.BlockSpec((TM_FIN, LANES), row),
                  pl.BlockSpec((1, N_MOD, d), _mod_index(TM_FIN, seq_len, mod.shape[0])),
                  pl.BlockSpec((1, d), lambda i: (0, 0))],
        out_specs=pl.BlockSpec((TM_FIN, d), row),
        out_shape=jax.ShapeDtypeStruct((n, d), F32),
        compiler_params=_params("arbitrary"),
        name="final_residual",
    )(x1, y4, route, mod, g)


def _pack_state(s):
    b = s.shape[0]
    st = jnp.swapaxes(s.astype(F32), -1, -2).reshape(b, N_PAIRS, 2, HEAD_DIM, HEAD_DIM)
    z = jnp.zeros_like(st[:, :, 0])
    top = jnp.concatenate([st[:, :, 0], z], axis=-1)
    bot = jnp.concatenate([z, st[:, :, 1]], axis=-1)
    return jnp.concatenate([top, bot], axis=-2)


def _unpack_state(st):
    b = st.shape[0]
    h0 = st[:, :, :HEAD_DIM, :HEAD_DIM]
    h1 = st[:, :, HEAD_DIM:, HEAD_DIM:]
    s = jnp.stack([h0, h1], axis=2).reshape(b, RWKV_HEADS, HEAD_DIM, HEAD_DIM)
    return jnp.swapaxes(s, -1, -2)


def _mix_sublayer(x, mod, s0f, s0b, seg_len, wts):
    batch, seq_len, d = x.shape
    x2 = x.reshape(batch * seq_len, d)
    cv, rkv, lora = _in_call(x2, mod, wts["pre_mix_g"], wts["w_all"], seq_len)
    conv_out = _conv_call(cv, wts["conv_w"], wts["conv_b"], wts["conv_ln_g"], wts["conv_ln_b"],
                          seg_len)
    yf, yb, sf, sb = _wkv_call(rkv, lora, s0f, s0b, wts["w2p"], wts["w0"], wts["a2p"], wts["a0"],
                               wts["k_k"], wts["k_a"], batch, seq_len)
    x1, h2, route, cnt = _out_call(x2, mod, conv_out, yf, yb, rkv, lora, wts, seq_len)
    return x1, h2, route, cnt, sf, sb


def _dispatch_plan(routes, cnts):
    experts = jnp.arange(N_EXPERTS, dtype=jnp.int32)
    counts = [c[0, :N_EXPERTS].astype(jnp.int32) for c in cnts]
    total = sum(counts)
    padded = (total + MOE_BLOCK - 1) // MOE_BLOCK * MOE_BLOCK
    pad_end = jnp.cumsum(padded)
    pad_start = pad_end - padded
    dests = []
    prior = jnp.zeros_like(total)
    for route, count in zip(routes, counts):
        idx = route[:, :TOP_K].astype(jnp.int32)
        rank = route[:, 2 * TOP_K:3 * TOP_K].astype(jnp.int32)
        first = jnp.sum(jnp.where(idx[..., None] == experts, pad_start + prior, 0), axis=-1)
        dests.append(first + rank)
        prior = prior + count
    dest = jnp.concatenate(dests, axis=0).T.reshape(-1)
    n_slots = dest.shape[0] + N_EXPERTS * MOE_BLOCK
    blk_row0 = jnp.arange(n_slots // MOE_BLOCK, dtype=jnp.int32) * MOE_BLOCK
    blk_e = jnp.sum((pad_end[None, :] <= blk_row0[:, None]).astype(jnp.int32), axis=-1)
    blk_e = jnp.minimum(blk_e, N_EXPERTS - 1)
    blk_first = jnp.concatenate([jnp.ones((1,), jnp.int32),
                                 (blk_e[1:] != blk_e[:-1]).astype(jnp.int32)])
    of_blk = lambda per_expert: jnp.sum(
        jnp.where(blk_e[:, None] == experts[None, :], per_expert[None, :], 0), axis=-1)
    blk_rows = jnp.clip(of_blk(pad_start + total) - blk_row0, 0, MOE_BLOCK).astype(jnp.int32)
    return dest, n_slots, blk_e, blk_first, blk_rows


def _routed_ffn(h2s, routes, cnts, wts):
    dest, n_slots, blk_e, blk_first, blk_rows = _dispatch_plan(routes, cnts)
    h2 = jnp.concatenate(h2s, axis=0)
    xs = _sc_rows(h2, dest, n_slots, scatter=True)
    out_sorted = _moe_call(blk_e, blk_first, blk_rows, xs, wts["w_gu"], wts["b_gu"],
                           wts["w_down"], wts["b_down"])
    y4 = _sc_rows(out_sorted, dest, dest.shape[0], scatter=False)
    return y4.reshape(TOP_K, h2.shape[0], h2.shape[1])


def _prep_weights(l, pre_mix_g, post_mix_g, pre_ffn_g, post_ffn_g, w_in, w_out, conv_w, conv_b,
                  conv_ln_g, conv_ln_b, rw_w0, rw_w1, rw_w2, rw_a0, rw_a1, rw_a2, rw_g1, rw_g2,
                  rw_k_k, rw_k_a, rw_r_k, rw_gn_g, rw_gn_b, router_w, router_b, w_gu, b_gu,
                  w_down, b_down):
    row = lambda a: a.reshape(1, -1).astype(F32)
    zpad = jnp.zeros((HEAD_DIM, RWKV_DIM), F32)
    w_all = jnp.concatenate([w_in[l], rw_w1[l, 0], rw_w1[l, 1], rw_a1[l, 0], rw_a1[l, 1],
                             rw_g1[l]], axis=1).astype(BF16)
    w2p = jnp.stack([jnp.concatenate([rw_w2[l, 0], zpad], axis=0),
                     jnp.concatenate([zpad, rw_w2[l, 1]], axis=0)]).astype(BF16)
    a2p = jnp.stack([jnp.concatenate([rw_a2[l, 0], zpad], axis=0),
                     jnp.concatenate([zpad, rw_a2[l, 1]], axis=0)]).astype(BF16)
    rw_pad = jnp.pad(router_w[l].astype(F32), ((0, 0), (0, LANES - N_EXPERTS)))
    rw_hi = rw_pad.astype(BF16)
    rw_lo = (rw_pad - rw_hi.astype(F32)).astype(BF16)
    rb_pad = jnp.concatenate([router_b[l].astype(F32),
                              jnp.full((LANES - N_EXPERTS,), NEG_BIG, F32)]).reshape(1, LANES)
    return {
        "pre_mix_g": row(pre_mix_g[l]), "post_mix_g": row(post_mix_g[l]),
        "pre_ffn_g": row(pre_ffn_g[l]), "post_ffn_g": row(post_ffn_g[l]),
        "w_all": w_all, "w_out": w_out[l].astype(BF16),
        "conv_w": conv_w[l].astype(F32), "conv_b": row(conv_b[l]),
        "conv_ln_g": row(conv_ln_g[l]), "conv_ln_b": row(conv_ln_b[l]),
        "w2p": w2p, "w0": rw_w0[l].astype(F32), "a2p": a2p, "a0": rw_a0[l].astype(F32),
        "k_k": row(rw_k_k[l]), "k_a": row(rw_k_a[l]), "r_k": row(rw_r_k[l]),
        "gn_g": row(rw_gn_g[l]), "gn_b": row(rw_gn_b[l]), "g2": rw_g2[l].astype(BF16),
        "router_w": jnp.stack([rw_hi, rw_lo]), "router_b": rb_pad,
        "w_gu": w_gu[l].astype(F32), "b_gu": b_gu[l].reshape(N_EXPERTS, 1, -1).astype(F32),
        "w_down": w_down[l].astype(F32), "b_down": b_down[l].reshape(N_EXPERTS, 1, -1).astype(F32),
    }


def kernel(x_prompt, x_sample, state_wkv_fwd, state_wkv_bwd, c, c_ctx, ada_w, ada_b, pre_mix_g, post_mix_g, pre_ffn_g, post_ffn_g, w_in, w_out, conv_w, conv_b, conv_ln_g, conv_ln_b, rw_w0, rw_w1, rw_w2, rw_a0, rw_a1, rw_a2, rw_g1, rw_g2, rw_k_k, rw_k_a, rw_r_k, rw_gn_g, rw_gn_b, router_w, router_b, w_gu, b_gu, w_down, b_down):
    depth = ada_w.shape[0]
    dec_batch = c.shape[0]
    mod_rows = 16
    c_rows = jnp.concatenate([c, c_ctx[None, :],
                              jnp.zeros((mod_rows - dec_batch - 1, D_MODEL), F32)], axis=0)
    zero_state = jnp.zeros((x_prompt.shape[0], N_PAIRS, LANES, LANES), F32)
    y_prompt, y_sample = x_prompt, x_sample
    new_f, new_b = [], []
    for l in range(depth):
        wts = _prep_weights(l, pre_mix_g, post_mix_g, pre_ffn_g, post_ffn_g, w_in, w_out, conv_w,
                            conv_b, conv_ln_g, conv_ln_b, rw_w0, rw_w1, rw_w2, rw_a0, rw_a1, rw_a2,
                            rw_g1, rw_g2, rw_k_k, rw_k_a, rw_r_k, rw_gn_g, rw_gn_b, router_w,
                            router_b, w_gu, b_gu, w_down, b_down)
        mod = _mod_call(c_rows, ada_w[l], ada_b[l]).reshape(mod_rows, N_MOD, D_MODEL)
        mod_lat = mod[:dec_batch]
        mod_ctx = mod[dec_batch:dec_batch + 1]
        ctx = _mix_sublayer(y_prompt, mod_ctx, zero_state, zero_state, y_prompt.shape[1], wts)
        lat = _mix_sublayer(y_sample, mod_lat, _pack_state(state_wkv_fwd[:, l]),
                            _pack_state(state_wkv_bwd[:, l]), GRID_W, wts)
        new_f.append(_unpack_state(ctx[4]))
        new_b.append(_unpack_state(ctx[5]))
        y4 = _routed_ffn([ctx[1], lat[1]], [ctx[2], lat[2]], [ctx[3], lat[3]], wts)
        n_ctx = ctx[0].shape[0]
        y_prompt = _final_call(ctx[0], y4, ctx[2], mod_ctx, wts["post_ffn_g"], y_prompt.shape[1],
                               0).reshape(y_prompt.shape)
        y_sample = _final_call(lat[0], y4, lat[2], mod_lat, wts["post_ffn_g"], y_sample.shape[1],
                               n_ctx).reshape(y_sample.shape)
    return (y_prompt, y_sample, jnp.stack(new_f, axis=1), jnp.stack(new_b, axis=1))
```

```python
import functools
import math

import jax
import jax.numpy as jnp
from jax import lax
from jax.experimental import pallas as pl
from jax.experimental.pallas import tpu as pltpu
from jax.experimental.pallas import tpu_sc as plsc

F32 = jnp.float32
BF16 = jnp.bfloat16

D_MODEL = 1024
CONV_CH = 512
RWKV_DIM = 512
HEAD_DIM = 64
RWKV_HEADS = 8
N_PAIRS = RWKV_HEADS // 2
CONV_WIDTH = 31
N_EXPERTS = 32
TOP_K = 4
D_FF = 1024
SWIGLU_LIMIT = 7.0
SWIGLU_ALPHA = 1.702
RMS_EPS = 1e-6
LN_EPS = 1e-5
GN_EPS = 64e-5
N_MOD = 6
GRID_W = 64

LANES = 128
SUBLANES = 8
CHUNK = 64
LORA_COLS = 384
TM_IN = 512
TM_CONV = 256
TM_OUT = 256
TM_FIN = 512
MOE_BLOCK = 256
SC_CORES = 2
SC_SUBCORES = 16
SC_ROWS = 128
NEG_BIG = -1e30
EXP_M05 = math.exp(-0.5)
VMEM_LIMIT = 56 * 1024 * 1024


def _sigmoid(x):
    return 1.0 / (1.0 + jnp.exp(-x))


def _mm(a, b):
    return jnp.dot(a.astype(BF16), b.astype(BF16), preferred_element_type=F32)


def _mm_nt(a, b):
    return lax.dot_general(a.astype(BF16), b.astype(BF16), (((1,), (1,)), ((), ())),
                           preferred_element_type=F32)


def _mm_tn(a, b):
    return lax.dot_general(a.astype(BF16), b.astype(BF16), (((0,), (0,)), ((), ())),
                           preferred_element_type=F32)


def _split2(x):
    hi = x.astype(BF16)
    lo = (x - hi.astype(F32)).astype(BF16)
    return hi, lo


def _split3(x):
    h1 = x.astype(BF16)
    r1 = x - h1.astype(F32)
    h2 = r1.astype(BF16)
    h3 = (r1 - h2.astype(F32)).astype(BF16)
    return h1, h2, h3


def _pack_rows(x):
    n = x.shape[1] // 2
    bits = lambda v: lax.bitcast_convert_type(v.astype(BF16).astype(F32), jnp.uint32)
    word = (bits(x[:, n:]) & jnp.uint32(0xFFFF0000)) | (bits(x[:, :n]) >> 16)
    return lax.bitcast_convert_type(word, jnp.int32)


def _unpack_rows(w):
    u = lax.bitcast_convert_type(w, jnp.uint32)
    lo = lax.bitcast_convert_type(u << 16, F32)
    hi = lax.bitcast_convert_type(u & jnp.uint32(0xFFFF0000), F32)
    return jnp.concatenate([lo, hi], axis=1)


def _rms(x, g):
    return x * lax.rsqrt(jnp.mean(x * x, axis=-1, keepdims=True) + RMS_EPS) * g


def _params(*sem):
    return pltpu.CompilerParams(dimension_semantics=sem, vmem_limit_bytes=VMEM_LIMIT)


def _mod_kernel(c_ref, w_ref, b_ref, o_ref):
    c = c_ref[...]
    s1, s2, s3 = _split3(c * _sigmoid(c))
    w1, w2, w3 = _split3(w_ref[...])
    dot = functools.partial(jnp.dot, preferred_element_type=F32)
    acc = dot(s1, w1) + (dot(s1, w2) + dot(s2, w1)) + (dot(s2, w2) + dot(s1, w3) + dot(s3, w1))
    o_ref[...] = acc + b_ref[...]


def _mod_call(c_rows, ada_w, ada_b):
    m, d = c_rows.shape
    n = ada_w.shape[1]
    tn = 512
    return pl.pallas_call(
        _mod_kernel,
        grid=(n // tn,),
        in_specs=[pl.BlockSpec((m, d), lambda j: (0, 0)),
                  pl.BlockSpec((d, tn), lambda j: (0, j)),
                  pl.BlockSpec((1, tn), lambda j: (0, j))],
        out_specs=pl.BlockSpec((m, tn), lambda j: (0, j)),
        out_shape=jax.ShapeDtypeStruct((m, n), F32),
        compiler_params=_params("arbitrary"),
        name="mod",
    )(c_rows, ada_w, ada_b.reshape(1, n))


def _in_kernel(x_ref, mod_ref, g_ref, w_ref, cv_ref, rkv_ref, lo_ref):
    m = mod_ref[0]
    h = _rms(x_ref[...], g_ref[...]) * (1.0 + m[1:2]) + m[0:1]
    proj = jnp.dot(h.astype(BF16), w_ref[...], preferred_element_type=F32)
    cv_ref[...] = proj[:, :2 * CONV_CH]
    rkv_ref[...] = proj[:, 2 * CONV_CH:2 * CONV_CH + 3 * RWKV_DIM]
    lo_ref[...] = proj[:, 2 * CONV_CH + 3 * RWKV_DIM:]


def _mod_index(tile, seq_len, n_mod_rows):
    def index(i):
        return ((i * tile) // seq_len) % n_mod_rows, 0, 0
    return index


def _in_call(x, mod, g, w_all, seq_len):
    n, d = x.shape
    ncol = w_all.shape[1]
    return pl.pallas_call(
        _in_kernel,
        grid=(n // TM_IN,),
        in_specs=[pl.BlockSpec((TM_IN, d), lambda i: (i, 0)),
                  pl.BlockSpec((1, N_MOD, d), _mod_index(TM_IN, seq_len, mod.shape[0])),
                  pl.BlockSpec((1, d), lambda i: (0, 0)),
                  pl.BlockSpec((d, ncol), lambda i: (0, 0))],
        out_specs=[pl.BlockSpec((TM_IN, 2 * CONV_CH), lambda i: (i, 0)),
                   pl.BlockSpec((TM_IN, 3 * RWKV_DIM), lambda i: (i, 0)),
                   pl.BlockSpec((TM_IN, LORA_COLS), lambda i: (i, 0))],
        out_shape=[jax.ShapeDtypeStruct((n, 2 * CONV_CH), F32),
                   jax.ShapeDtypeStruct((n, 3 * RWKV_DIM), F32),
                   jax.ShapeDtypeStruct((n, LORA_COLS), F32)],
        compiler_params=_params("arbitrary"),
        name="in_proj",
    )(x, mod, g, w_all)


CONV_HALO = 16
CONV_ROWS = 32


def _conv_kernel(cv_ref, w_ref, b_ref, g_ref, be_ref, o_ref, pad_ref, acc_ref, *, seg_len):
    nseg = TM_CONV // seg_len
    stride = seg_len + 2 * CONV_HALO
    rows = nseg * stride
    cv = cv_ref[...]
    u = cv[:, :CONV_CH] * _sigmoid(cv[:, CONV_CH:])
    pad_ref[0] = jnp.zeros(pad_ref.shape[1:], F32)
    for s in range(nseg):
        pad_ref[0, s * stride + CONV_HALO:s * stride + CONV_HALO + seg_len, :] = (
            u[s * seg_len:(s + 1) * seg_len])
    for r in range(1, SUBLANES):
        pad_ref[r, 0:rows, :] = pad_ref[0, r:r + rows, :]
    first = CONV_HALO - CONV_WIDTH // 2
    chunks_per_seg = seg_len // CONV_ROWS

    def row_chunk(i, carry):
        pad0 = i * CONV_ROWS + (i // chunks_per_seg) * (2 * CONV_HALO)
        for lc in range(CONV_CH // LANES):
            cols = slice(lc * LANES, (lc + 1) * LANES)
            acc = jnp.zeros((CONV_ROWS, LANES), F32)
            for t in range(CONV_WIDTH):
                q, r = divmod(first + t, SUBLANES)
                start = pl.multiple_of(pad0 + q * SUBLANES, SUBLANES)
                acc = acc + w_ref[t:t + 1, cols] * pad_ref[r, pl.ds(start, CONV_ROWS), cols]
            acc_ref[pl.ds(pl.multiple_of(i * CONV_ROWS, CONV_ROWS), CONV_ROWS), cols] = acc
        return carry

    lax.fori_loop(0, TM_CONV // CONV_ROWS, row_chunk, 0)
    y = acc_ref[...] + b_ref[...]
    mu = jnp.mean(y, axis=-1, keepdims=True)
    yc = y - mu
    var = jnp.mean(yc * yc, axis=-1, keepdims=True)
    z = yc * lax.rsqrt(var + LN_EPS) * g_ref[...] + be_ref[...]
    o_ref[...] = z * _sigmoid(z)


def _conv_call(cv, conv_w, conv_b, ln_g, ln_b, seg_len):
    n = cv.shape[0]
    nseg = TM_CONV // seg_len
    row = lambda i: (i, 0)
    const = lambda i: (0, 0)
    return pl.pallas_call(
        functools.partial(_conv_kernel, seg_len=seg_len),
        grid=(n // TM_CONV,),
        in_specs=[pl.BlockSpec((TM_CONV, 2 * CONV_CH), row),
                  pl.BlockSpec((CONV_WIDTH, CONV_CH), const),
                  pl.BlockSpec((1, CONV_CH), const),
                  pl.BlockSpec((1, CONV_CH), const),
                  pl.BlockSpec((1, CONV_CH), const)],
        out_specs=pl.BlockSpec((TM_CONV, CONV_CH), row),
        out_shape=jax.ShapeDtypeStruct((n, CONV_CH), F32),
        scratch_shapes=[pltpu.VMEM((SUBLANES, nseg * (seg_len + 2 * CONV_HALO) + SUBLANES, CONV_CH),
                                   F32),
                        pltpu.VMEM((TM_CONV, CONV_CH), F32)],
        compiler_params=_params("arbitrary"),
        name="conv_module",
    )(cv, conv_w, conv_b, ln_g, ln_b)


def _head_sums(x, ones_bd):
    hi, lo = _split2(x)
    return jnp.dot(jnp.concatenate([hi, lo], axis=1), jnp.concatenate([ones_bd, ones_bd], axis=0),
                   preferred_element_type=F32)


def _wkv_masks():
    c = CHUNK
    r2 = lax.broadcasted_iota(jnp.int32, (2 * c, 2 * c), 0)
    c2 = lax.broadcasted_iota(jnp.int32, (2 * c, 2 * c), 1)
    same = (r2 // c) == (c2 // c)
    rr, cc = r2 % c, c2 % c
    r1 = lax.broadcasted_iota(jnp.int32, (c, 2 * c), 0)
    c1 = lax.broadcasted_iota(jnp.int32, (c, 2 * c), 1)
    cc1 = c1 % c
    rs = lax.broadcasted_iota(jnp.int32, (c, c), 0)
    cs = lax.broadcasted_iota(jnp.int32, (c, c), 1)
    masks = {"same": same, "eye": r2 == c2, "left": c1 < c,
             "ones_bd": same.astype(BF16)}
    for name, lt in (("f", lambda a, b: a < b), ("b", lambda a, b: a > b)):
        le = (lambda a, b, lt=lt: lt(a, b) | (a == b))
        masks["strict_" + name] = same & lt(cc, rr)
        masks["inc_" + name] = le(cc1, r1)
        masks["ak0_" + name] = (c1 >= c) & lt(cc1, r1)
        masks["ak1_" + name] = (c1 < c) & lt(cc1, r1)
        masks["tri_" + name] = le(cs, rs).astype(BF16)
    return masks


def _wkv_chains(chains, mk):
    c = CHUNK
    left = mk["left"]
    for ch in chains:
        zero = jnp.zeros_like(ch["at"])
        ch["at0"], ch["at1"] = jnp.where(left, ch["at"], zero), jnp.where(left, zero, ch["at"])
        rt0, rt1 = jnp.where(left, ch["rt"], zero), jnp.where(left, zero, ch["rt"])
        bt, kt = ch["bt"].astype(BF16), ch["kt"].astype(BF16)
        ch["g0"] = _mm_nt(jnp.concatenate([ch["at0"], rt0], axis=0), jnp.concatenate([bt, kt], axis=0))
        ch["g1"] = _mm_nt(jnp.concatenate([ch["at1"], rt1], axis=0), jnp.concatenate([kt, bt], axis=0))
    for ch in chains:
        g0, g1 = ch["g0"], ch["g1"]
        lmat = jnp.where(mk["strict_" + ch["d"]], jnp.concatenate([g0[:c], g1[:c]], axis=0), 0.0)
        ch["tinv"] = jnp.where(mk["eye"], 1.0, lmat)
        ch["lpow"] = _mm(lmat, lmat)
    for step in range(5):
        for ch in chains:
            lp = ch["lpow"].astype(BF16)
            if step < 4:
                prod = jnp.dot(lp, jnp.concatenate([lp, ch["tinv"].astype(BF16)], axis=1),
                               preferred_element_type=F32)
                ch["lpow"] = prod[:, :LANES]
                ch["tinv"] = ch["tinv"] + prod[:, LANES:]
            else:
                ch["tinv"] = ch["tinv"] + _mm(lp, ch["tinv"])
    for ch in chains:
        d, v = ch["d"], ch["v"]
        zv = jnp.zeros_like(v)
        ak0 = jnp.where(mk["ak0_" + d], ch["g0"][:c], 0.0)
        ak1 = jnp.where(mk["ak1_" + d], ch["g1"][:c], 0.0)
        ch["x0"] = _mm(ak0, jnp.concatenate([zv, v], axis=0))
        ch["x1"] = _mm(ak1, jnp.concatenate([v, zv], axis=0))
    for ch in chains:
        zero = jnp.zeros_like(ch["x0"])
        rhs = jnp.concatenate([
            jnp.concatenate([ch["at0"], jnp.where(left, ch["x0"], zero)], axis=1),
            jnp.concatenate([ch["at1"], jnp.where(left, zero, ch["x1"])], axis=1)], axis=0)
        ch["aw_blk"] = _mm(ch["tinv"], rhs)
    for ch in chains:
        d, v, aw_blk = ch["d"], ch["v"], ch["aw_blk"]
        zv = jnp.zeros_like(v)
        aw = aw_blk[:c] + aw_blk[c:]
        inc = mk["inc_" + d]
        g0, g1 = ch["g0"], ch["g1"]
        rb = jnp.where(inc, jnp.where(left, g0[c:], g1[c:]), 0.0)
        rk = jnp.where(inc, jnp.where(left, g1[c:], g0[c:]), 0.0)
        v0, v1 = jnp.where(left, v, zv), jnp.where(left, zv, v)
        vblk = jnp.concatenate([jnp.zeros((2 * c, LANES), F32),
                                jnp.concatenate([v1, v0], axis=0)], axis=1)
        qz = _mm(jnp.concatenate([rb, rk], axis=1), jnp.concatenate([aw_blk, vblk], axis=0))
        ch["q"] = ch["rt"] + qz[:, :LANES]
        ch["zz"] = qz[:, LANES:]
        rhs3 = jnp.concatenate([aw, jnp.concatenate([zv, v], axis=1)], axis=0)
        ch["mn"] = _mm_tn(jnp.concatenate([ch["bh"], ch["kh"]], axis=0), rhs3)
    same = mk["same"]
    for ch in chains:
        mn = ch["mn"]
        mt = jnp.where(same, mn[:, :LANES], 0.0) + jnp.where(mk["eye"], ch["ptot"], 0.0)
        nt = jnp.where(same, mn[:, LANES:], 0.0)
        both = _mm(jnp.concatenate([ch["q"], mt], axis=0), ch["st"])
        ch["y"] = both[:c] + ch["zz"]
        ch["st_new"] = both[c:] + nt


def _wkv_kernel(rkv_f_ref, lo_f_ref, rkv_b_ref, lo_b_ref, s0f_ref, s0b_ref,
                w2_ref, w0_ref, a2_ref, a0_ref, kk_ref, ka_ref,
                yf_ref, yb_ref, sf_ref, sb_ref, st_ref):
    ci = pl.program_id(1)
    nc = pl.num_programs(1)

    @pl.when(ci == 0)
    def _():
        st_ref[0] = s0f_ref[0]
        st_ref[1] = s0b_ref[0]

    mk = _wkv_masks()
    c = CHUNK
    chains = []
    for di, (d, rkv_ref, lo_ref, y_ref) in enumerate(
            (("f", rkv_f_ref, lo_f_ref, yf_ref), ("b", rkv_b_ref, lo_b_ref, yb_ref))):
        rkv = rkv_ref[...]
        lo = lo_ref[...]
        r = rkv[:, :RWKV_DIM]
        k = rkv[:, RWKV_DIM:2 * RWKV_DIM]
        v = rkv[:, 2 * RWKV_DIM:]
        w_raw = w0_ref[di:di + 1, :] + _mm(jnp.tanh(lo[:, :LANES]), w2_ref[di])
        lw = -EXP_M05 * _sigmoid(w_raw)
        ag = _sigmoid(a0_ref[di:di + 1, :] + _mm(lo[:, LANES:2 * LANES], a2_ref[di]))
        kd = k * (1.0 + (ag - 1.0) * ka_ref[...])
        kk_raw = k * kk_ref[...]
        h1, h2, h3 = _split3(lw)
        cs = jnp.dot(mk["tri_" + d], jnp.concatenate([h1, h2, h3], axis=1),
                     preferred_element_type=F32)
        logp = cs[:, :RWKV_DIM] + cs[:, RWKV_DIM:2 * RWKV_DIM] + cs[:, 2 * RWKV_DIM:]
        tot = logp[0:1] if d == "b" else logp[c - 1:c]
        e_inv = jnp.exp(-logp)
        e_hat = jnp.exp(tot - logp)
        rt = r * jnp.exp(logp)
        e_ex = jnp.exp(logp - lw)
        kt, kh = kd * e_inv, kd * e_hat
        ptot = jnp.exp(tot)
        for p in range(N_PAIRS):
            cols = slice(p * LANES, (p + 1) * LANES)
            chains.append({"d": d, "di": di, "p": p, "y_ref": y_ref, "cols": cols,
                           "kk": kk_raw[:, cols], "ag": ag[:, cols], "e_ex": e_ex[:, cols],
                           "e_inv": e_inv[:, cols], "e_hat": e_hat[:, cols],
                           "rt": rt[:, cols], "kt": kt[:, cols], "kh": kh[:, cols],
                           "v": v[:, cols], "ptot": ptot[:, cols], "st": st_ref[di, p]})
    for ch in chains:
        kk = ch["kk"]
        kk = kk * lax.rsqrt(jnp.maximum(_head_sums(kk * kk, mk["ones_bd"]), 1e-24))
        kb = kk * ch["ag"]
        ch["at"] = -kk * ch["e_ex"]
        ch["bt"] = kb * ch["e_inv"]
        ch["bh"] = kb * ch["e_hat"]
    _wkv_chains(chains, mk)
    for ch in chains:
        ch["y_ref"][:, ch["cols"]] = ch["y"]
        st_ref[ch["di"], ch["p"]] = ch["st_new"]

    @pl.when(ci == nc - 1)
    def _():
        sf_ref[0] = st_ref[0]
        sb_ref[0] = st_ref[1]


def _wkv_call(rkv, lora, s0f, s0b, w2p, w0, a2p, a0, k_k, k_a, batch, seq_len):
    n = rkv.shape[0]
    nc = seq_len // CHUNK
    fwd = lambda b, c: (b * nc + c, 0)
    bwd = lambda b, c: (b * nc + (nc - 1 - c), 0)
    st = lambda b, c: (b, 0, 0, 0)
    c2 = lambda b, c: (0, 0)
    c3 = lambda b, c: (0, 0, 0)
    st_shape = jax.ShapeDtypeStruct((batch, N_PAIRS, LANES, LANES), F32)
    return pl.pallas_call(
        _wkv_kernel,
        grid=(batch, nc),
        in_specs=[pl.BlockSpec((CHUNK, 3 * RWKV_DIM), fwd),
                  pl.BlockSpec((CHUNK, LORA_COLS), fwd),
                  pl.BlockSpec((CHUNK, 3 * RWKV_DIM), bwd),
                  pl.BlockSpec((CHUNK, LORA_COLS), bwd),
                  pl.BlockSpec((1, N_PAIRS, LANES, LANES), st),
                  pl.BlockSpec((1, N_PAIRS, LANES, LANES), st),
                  pl.BlockSpec((2, LANES, RWKV_DIM), c3),
                  pl.BlockSpec((2, RWKV_DIM), c2),
                  pl.BlockSpec((2, LANES, RWKV_DIM), c3),
                  pl.BlockSpec((2, RWKV_DIM), c2),
                  pl.BlockSpec((1, RWKV_DIM), c2),
                  pl.BlockSpec((1, RWKV_DIM), c2)],
        out_specs=[pl.BlockSpec((CHUNK, RWKV_DIM), fwd),
                   pl.BlockSpec((CHUNK, RWKV_DIM), bwd),
                   pl.BlockSpec((1, N_PAIRS, LANES, LANES), st),
                   pl.BlockSpec((1, N_PAIRS, LANES, LANES), st)],
        out_shape=[jax.ShapeDtypeStruct((n, RWKV_DIM), F32),
                   jax.ShapeDtypeStruct((n, RWKV_DIM), F32), st_shape, st_shape],
        scratch_shapes=[pltpu.VMEM((2, N_PAIRS, LANES, LANES), F32)],
        compiler_params=_params("arbitrary", "arbitrary"),
        name="wkv_chunked",
    )(rkv, lora, rkv, lora, s0f, s0b, w2p, w0, a2p, a0, k_k, k_a)


def _out_kernel(x_ref, mod_ref, conv_ref, yf_ref, yb_ref, rkv_ref, lo_ref,
                a2_ref, a0_ref, ka_ref, rk_ref, gng_ref, gnb_ref, g2_ref, wout_ref,
                postg_ref, preg_ref, rw_ref, rb_ref,
                x1_ref, h2_ref, route_ref, cnt_ref):
    i = pl.program_id(0)

    @pl.when(i == 0)
    def _():
        cnt_ref[...] = jnp.zeros(cnt_ref.shape, F32)

    m = mod_ref[0]
    rkv = rkv_ref[...]
    lo = lo_ref[...]
    r = rkv[:, :RWKV_DIM]
    k = rkv[:, RWKV_DIM:2 * RWKV_DIM]
    v = rkv[:, 2 * RWKV_DIM:]
    ha = lo[:, LANES:2 * LANES]
    ag_f = _sigmoid(a0_ref[0:1, :] + _mm(ha, a2_ref[0]))
    ag_b = _sigmoid(a0_ref[1:2, :] + _mm(ha, a2_ref[1]))
    rkk = r * k * rk_ref[...] * (2.0 + (ag_f + ag_b - 2.0) * ka_ref[...])
    o = yf_ref[...] + yb_ref[...]
    gate = _mm(_sigmoid(lo[:, 2 * LANES:]), g2_ref[...])
    r2 = lax.broadcasted_iota(jnp.int32, (LANES, LANES), 0)
    c2 = lax.broadcasted_iota(jnp.int32, (LANES, LANES), 1)
    ones_bd = ((r2 // HEAD_DIM) == (c2 // HEAD_DIM)).astype(BF16)
    parts = []
    for p in range(N_PAIRS):
        cols = slice(p * LANES, (p + 1) * LANES)
        op = o[:, cols]
        mu = _head_sums(op, ones_bd) * (1.0 / HEAD_DIM)
        oc = op - mu
        var = _head_sums(oc * oc, ones_bd) * (1.0 / HEAD_DIM)
        on = oc * lax.rsqrt(var + GN_EPS) * gng_ref[:, cols] + gnb_ref[:, cols]
        bonus = _head_sums(rkk[:, cols], ones_bd) * v[:, cols]
        parts.append((on + bonus) * gate[:, cols])
    mix_in = jnp.concatenate([conv_ref[...]] + parts, axis=1)
    mix = jnp.dot(mix_in.astype(BF16), wout_ref[...], preferred_element_type=F32)
    x1 = x_ref[...] + m[2:3] * _rms(mix, postg_ref[...])
    x1_ref[...] = x1
    h2 = _rms(x1, preg_ref[...]) * (1.0 + m[4:5]) + m[3:4]
    h2_ref[...] = _pack_rows(h2)
    hh, hl = _split2(h2)
    dot = functools.partial(jnp.dot, preferred_element_type=F32)
    logits = dot(hh, rw_ref[0]) + (dot(hh, rw_ref[1]) + dot(hl, rw_ref[0])) + rb_ref[...]
    lane = lax.broadcasted_iota(jnp.int32, logits.shape, 1)
    work = logits
    picks, vals, idxs = [], [], []
    for _ in range(TOP_K):
        mx = jnp.max(work, axis=-1, keepdims=True)
        idx = jnp.min(jnp.where(work == mx, lane, LANES), axis=-1, keepdims=True)
        pick = lane == idx
        picks.append(pick)
        vals.append(mx)
        idxs.append(idx)
        work = jnp.where(pick, 2.0 * NEG_BIG, work)
    exps = [jnp.exp(val - vals[0]) for val in vals]
    den = exps[0] + exps[1] + exps[2] + exps[3]
    sel = jnp.zeros(logits.shape, F32)
    for pick in picks:
        sel = jnp.where(pick, 1.0, sel)
    tr = lax.broadcasted_iota(jnp.int32, (TM_OUT, TM_OUT), 0)
    tc = lax.broadcasted_iota(jnp.int32, (TM_OUT, TM_OUT), 1)
    before = (tc < tr).astype(BF16)
    rank = dot(before, sel.astype(BF16)) + cnt_ref[...]
    cnt_ref[...] = cnt_ref[...] + jnp.sum(sel, axis=0, keepdims=True)
    route = jnp.zeros(logits.shape, F32)
    for q, (pick, idx, e) in enumerate(zip(picks, idxs, exps)):
        rank_q = jnp.sum(jnp.where(pick, rank, 0.0), axis=-1, keepdims=True)
        route = jnp.where(lane == q, idx.astype(F32), route)
        route = jnp.where(lane == TOP_K + q, e / den, route)
        route = jnp.where(lane == 2 * TOP_K + q, rank_q, route)
    route_ref[...] = route


def _out_call(x, mod, conv_out, yf, yb, rkv, lora, wts, seq_len):
    n, d = x.shape
    row = lambda i: (i, 0)
    c2 = lambda i: (0, 0)
    c3 = lambda i: (0, 0, 0)
    full = lambda a: pl.BlockSpec(a.shape, c3 if a.ndim == 3 else c2)
    consts = [wts["a2p"], wts["a0"], wts["k_a"], wts["r_k"], wts["gn_g"], wts["gn_b"], wts["g2"],
              wts["w_out"], wts["post_mix_g"], wts["pre_ffn_g"], wts["router_w"], wts["router_b"]]
    return pl.pallas_call(
        _out_kernel,
        grid=(n // TM_OUT,),
        in_specs=[pl.BlockSpec((TM_OUT, d), row),
                  pl.BlockSpec((1, N_MOD, d), _mod_index(TM_OUT, seq_len, mod.shape[0])),
                  pl.BlockSpec((TM_OUT, CONV_CH), row),
                  pl.BlockSpec((TM_OUT, RWKV_DIM), row),
                  pl.BlockSpec((TM_OUT, RWKV_DIM), row),
                  pl.BlockSpec((TM_OUT, 3 * RWKV_DIM), row),
                  pl.BlockSpec((TM_OUT, LORA_COLS), row)] + [full(a) for a in consts],
        out_specs=[pl.BlockSpec((TM_OUT, d), row),
                   pl.BlockSpec((TM_OUT, d // 2), row),
                   pl.BlockSpec((TM_OUT, LANES), row),
                   pl.BlockSpec((1, LANES), c2)],
        out_shape=[jax.ShapeDtypeStruct((n, d), F32),
                   jax.ShapeDtypeStruct((n, d // 2), jnp.int32),
                   jax.ShapeDtypeStruct((n, LANES), F32),
                   jax.ShapeDtypeStruct((1, LANES), F32)],
        compiler_params=_params("arbitrary"),
        name="out_router",
    )(x, mod, conv_out, yf, yb, rkv, lora, *consts)


def _sc_rows(src, idx, n_out, scatter):
    n_idx, (n_src, d) = idx.shape[0], src.shape
    workers = SC_CORES * SC_SUBCORES
    per_worker = n_idx // workers
    assert per_worker * workers == n_idx and per_worker % SC_ROWS == 0 and n_src % SC_ROWS == 0
    mesh = plsc.VectorSubcoreMesh(core_axis_name="c", subcore_axis_name="s",
                                  num_cores=SC_CORES, num_subcores=SC_SUBCORES)

    @functools.partial(
        pl.kernel, mesh=mesh,
        out_type=jax.ShapeDtypeStruct((n_out, d), src.dtype),
        scratch_types=[pltpu.VMEM((SC_ROWS,), jnp.int32),
                       pltpu.VMEM((SC_ROWS, d), src.dtype),
                       pltpu.SemaphoreType.DMA],
        name="sc_scatter_rows" if scatter else "sc_gather_rows")
    def move(src_hbm, idx_hbm, out_hbm, idx_v, rows_v, sem):
        wid = lax.axis_index("s") * SC_CORES + lax.axis_index("c")
        base = wid * per_worker

        @pl.loop(0, per_worker // SC_ROWS)
        def _(j):
            off = base + j * SC_ROWS
            pltpu.sync_copy(idx_hbm.at[pl.ds(off, SC_ROWS)], idx_v)
            if scatter:
                pltpu.sync_copy(src_hbm.at[pl.ds(lax.rem(off, n_src), SC_ROWS)], rows_v)
                pltpu.async_copy(rows_v, out_hbm.at[idx_v], sem).wait()
            else:
                pltpu.async_copy(src_hbm.at[idx_v], rows_v, sem).wait()
                pltpu.sync_copy(rows_v, out_hbm.at[pl.ds(off, SC_ROWS)])

    return move(src, idx)


def _moe_kernel(blk_e_ref, blk_first_ref, blk_rows_ref, xs_ref, wgu_ref, bgu_ref, wd_ref, bd_ref,
                o_ref, wgu_bf_ref, wd_bf_ref):
    b = pl.program_id(0)
    n_rows = blk_rows_ref[b]

    @pl.when(blk_first_ref[b] == 1)
    def _():
        wgu_bf_ref[...] = wgu_ref[0].astype(BF16)
        wd_bf_ref[...] = wd_ref[0].astype(BF16)

    @pl.when(n_rows > 0)
    def _():
        live = lax.broadcasted_iota(jnp.int32, xs_ref.shape, 0) < n_rows
        xs = _unpack_rows(jnp.where(live, xs_ref[...], 0))
        gu = jnp.dot(xs.astype(BF16), wgu_bf_ref[...],
                     preferred_element_type=F32) + bgu_ref[0]
        x_glu = jnp.minimum(gu[:, :D_FF], SWIGLU_LIMIT)
        x_lin = jnp.clip(gu[:, D_FF:], -SWIGLU_LIMIT, SWIGLU_LIMIT)
        act = x_glu * _sigmoid(SWIGLU_ALPHA * x_glu) * (x_lin + 1.0)
        o_ref[...] = _pack_rows(jnp.dot(act.astype(BF16), wd_bf_ref[...],
                                        preferred_element_type=F32) + bd_ref[0])

    @pl.when(n_rows == 0)
    def _():
        o_ref[...] = jnp.zeros(o_ref.shape, jnp.int32)


def _moe_call(blk_e, blk_first, blk_rows, xs, wgu, bgu, wd, bd):
    n_slots, half = xs.shape
    d = 2 * half
    ex = lambda b, be, bf, nu: (be[b], 0, 0)
    row = lambda b, be, bf, nu: (b, 0)
    grid_spec = pltpu.PrefetchScalarGridSpec(
        num_scalar_prefetch=3,
        grid=(n_slots // MOE_BLOCK,),
        in_specs=[pl.BlockSpec((MOE_BLOCK, half), row),
                  pl.BlockSpec((1, d, 2 * D_FF), ex),
                  pl.BlockSpec((1, 1, 2 * D_FF), ex),
                  pl.BlockSpec((1, D_FF, d), ex),
                  pl.BlockSpec((1, 1, d), ex)],
        out_specs=pl.BlockSpec((MOE_BLOCK, half), row),
        scratch_shapes=[pltpu.VMEM((d, 2 * D_FF), BF16), pltpu.VMEM((D_FF, d), BF16)])
    return pl.pallas_call(
        _moe_kernel,
        grid_spec=grid_spec,
        out_shape=jax.ShapeDtypeStruct((n_slots, half), jnp.int32),
        compiler_params=_params("arbitrary"),
        name="moe_experts",
    )(blk_e, blk_first, blk_rows, xs, wgu, bgu, wd, bd)


def _final_kernel(x1_ref, y4_ref, route_ref, mod_ref, g_ref, o_ref):
    m = mod_ref[0]
    route = route_ref[...]
    y = jnp.zeros(x1_ref.shape, F32)
    for q in range(TOP_K):
        y = y + route[:, TOP_K + q:TOP_K + q + 1] * _unpack_rows(y4_ref[q])
    o_ref[...] = x1_ref[...] + m[5:6] * _rms(y, g_ref[...])


def _final_call(x1, y4, route, mod, g, seq_len, row_offset):
    n, d = x1.shape
    row = lambda i: (i, 0)
    shifted = lambda i: (0, i + row_offset // TM_FIN, 0)
    return pl.pallas_call(
        _final_kernel,
        grid=(n // TM_FIN,),
        in_specs=[pl.BlockSpec((TM_FIN, d), row),
                  pl.BlockSpec((TOP_K, TM_FIN, d // 2), shifted),
                  pl.BlockSpec((TM_FIN, LANES), row),
                  pl.BlockSpec((1, N_MOD, d), _mod_index(TM_FIN, seq_len, mod.shape[0])),
                  pl.BlockSpec((1, d), lambda i: (0, 0))],
        out_specs=pl.BlockSpec((TM_FIN, d), row),
        out_shape=jax.ShapeDtypeStruct((n, d), F32),
        compiler_params=_params("arbitrary"),
        name="final_residual",
    )(x1, y4, route, mod, g)


def _pack_state(s):
    b = s.shape[0]
    st = jnp.swapaxes(s.astype(F32), -1, -2).reshape(b, N_PAIRS, 2, HEAD_DIM, HEAD_DIM)
    z = jnp.zeros_like(st[:, :, 0])
    top = jnp.concatenate([st[:, :, 0], z], axis=-1)
    bot = jnp.concatenate([z, st[:, :, 1]], axis=-1)
    return jnp.concatenate([top, bot], axis=-2)


def _unpack_state(st):
    b = st.shape[0]
    h0 = st[:, :, :HEAD_DIM, :HEAD_DIM]
    h1 = st[:, :, HEAD_DIM:, HEAD_DIM:]
    s = jnp.stack([h0, h1], axis=2).reshape(b, RWKV_HEADS, HEAD_DIM, HEAD_DIM)
    return jnp.swapaxes(s, -1, -2)


def _mix_sublayer(x, mod, s0f, s0b, seg_len, wts):
    batch, seq_len, d = x.shape
    x2 = x.reshape(batch * seq_len, d)
    cv, rkv, lora = _in_call(x2, mod, wts["pre_mix_g"], wts["w_all"], seq_len)
    conv_out = _conv_call(cv, wts["conv_w"], wts["conv_b"], wts["conv_ln_g"], wts["conv_ln_b"],
                          seg_len)
    yf, yb, sf, sb = _wkv_call(rkv, lora, s0f, s0b, wts["w2p"], wts["w0"], wts["a2p"], wts["a0"],
                               wts["k_k"], wts["k_a"], batch, seq_len)
    x1, h2, route, cnt = _out_call(x2, mod, conv_out, yf, yb, rkv, lora, wts, seq_len)
    return x1, h2, route, cnt, sf, sb


def _dispatch_plan(routes, cnts):
    experts = jnp.arange(N_EXPERTS, dtype=jnp.int32)
    counts = [c[0, :N_EXPERTS].astype(jnp.int32) for c in cnts]
    total = sum(counts)
    padded = (total + MOE_BLOCK - 1) // MOE_BLOCK * MOE_BLOCK
    pad_end = jnp.cumsum(padded)
    pad_start = pad_end - padded
    dests = []
    prior = jnp.zeros_like(total)
    for route, count in zip(routes, counts):
        idx = route[:, :TOP_K].astype(jnp.int32)
        rank = route[:, 2 * TOP_K:3 * TOP_K].astype(jnp.int32)
        first = jnp.sum(jnp.where(idx[..., None] == experts, pad_start + prior, 0), axis=-1)
        dests.append(first + rank)
        prior = prior + count
    dest = jnp.concatenate(dests, axis=0).T.reshape(-1)
    n_slots = dest.shape[0] + N_EXPERTS * MOE_BLOCK
    blk_row0 = jnp.arange(n_slots // MOE_BLOCK, dtype=jnp.int32) * MOE_BLOCK
    blk_e = jnp.sum((pad_end[None, :] <= blk_row0[:, None]).astype(jnp.int32), axis=-1)
    blk_e = jnp.minimum(blk_e, N_EXPERTS - 1)
    blk_first = jnp.concatenate([jnp.ones((1,), jnp.int32),
                                 (blk_e[1:] != blk_e[:-1]).astype(jnp.int32)])
    of_blk = lambda per_expert: jnp.sum(
        jnp.where(blk_e[:, None] == experts[None, :], per_expert[None, :], 0), axis=-1)
    blk_rows = jnp.clip(of_blk(pad_start + total) - blk_row0, 0, MOE_BLOCK).astype(jnp.int32)
    return dest, n_slots, blk_e, blk_first, blk_rows


def _routed_ffn(h2s, routes, cnts, wts):
    dest, n_slots, blk_e, blk_first, blk_rows = _dispatch_plan(routes, cnts)
    h2 = jnp.concatenate(h2s, axis=0)
    xs = _sc_rows(h2, dest, n_slots, scatter=True)
    out_sorted = _moe_call(blk_e, blk_first, blk_rows, xs, wts["w_gu"], wts["b_gu"],
                           wts["w_down"], wts["b_down"])
    y4 = _sc_rows(out_sorted, dest, dest.shape[0], scatter=False)
    return y4.reshape(TOP_K, h2.shape[0], h2.shape[1])


def _prep_weights(l, pre_mix_g, post_mix_g, pre_ffn_g, post_ffn_g, w_in, w_out, conv_w, conv_b,
                  conv_ln_g, conv_ln_b, rw_w0, rw_w1, rw_w2, rw_a0, rw_a1, rw_a2, rw_g1, rw_g2,
                  rw_k_k, rw_k_a, rw_r_k, rw_gn_g, rw_gn_b, router_w, router_b, w_gu, b_gu,
                  w_down, b_down):
    row = lambda a: a.reshape(1, -1).astype(F32)
    zpad = jnp.zeros((HEAD_DIM, RWKV_DIM), F32)
    w_all = jnp.concatenate([w_in[l], rw_w1[l, 0], rw_w1[l, 1], rw_a1[l, 0], rw_a1[l, 1],
                             rw_g1[l]], axis=1).astype(BF16)
    w2p = jnp.stack([jnp.concatenate([rw_w2[l, 0], zpad], axis=0),
                     jnp.concatenate([zpad, rw_w2[l, 1]], axis=0)]).astype(BF16)
    a2p = jnp.stack([jnp.concatenate([rw_a2[l, 0], zpad], axis=0),
                     jnp.concatenate([zpad, rw_a2[l, 1]], axis=0)]).astype(BF16)
    rw_pad = jnp.pad(router_w[l].astype(F32), ((0, 0), (0, LANES - N_EXPERTS)))
    rw_hi = rw_pad.astype(BF16)
    rw_lo = (rw_pad - rw_hi.astype(F32)).astype(BF16)
    rb_pad = jnp.concatenate([router_b[l].astype(F32),
                              jnp.full((LANES - N_EXPERTS,), NEG_BIG, F32)]).reshape(1, LANES)
    return {
        "pre_mix_g": row(pre_mix_g[l]), "post_mix_g": row(post_mix_g[l]),
        "pre_ffn_g": row(pre_ffn_g[l]), "post_ffn_g": row(post_ffn_g[l]),
        "w_all": w_all, "w_out": w_out[l].astype(BF16),
        "conv_w": conv_w[l].astype(F32), "conv_b": row(conv_b[l]),
        "conv_ln_g": row(conv_ln_g[l]), "conv_ln_b": row(conv_ln_b[l]),
        "w2p": w2p, "w0": rw_w0[l].astype(F32), "a2p": a2p, "a0": rw_a0[l].astype(F32),
        "k_k": row(rw_k_k[l]), "k_a": row(rw_k_a[l]), "r_k": row(rw_r_k[l]),
        "gn_g": row(rw_gn_g[l]), "gn_b": row(rw_gn_b[l]), "g2": rw_g2[l].astype(BF16),
        "router_w": jnp.stack([rw_hi, rw_lo]), "router_b": rb_pad,
        "w_gu": w_gu[l].astype(F32), "b_gu": b_gu[l].reshape(N_EXPERTS, 1, -1).astype(F32),
        "w_down": w_down[l].astype(F32), "b_down": b_down[l].reshape(N_EXPERTS, 1, -1).astype(F32),
    }


def kernel(x_prompt, x_sample, state_wkv_fwd, state_wkv_bwd, c, c_ctx, ada_w, ada_b, pre_mix_g, post_mix_g, pre_ffn_g, post_ffn_g, w_in, w_out, conv_w, conv_b, conv_ln_g, conv_ln_b, rw_w0, rw_w1, rw_w2, rw_a0, rw_a1, rw_a2, rw_g1, rw_g2, rw_k_k, rw_k_a, rw_r_k, rw_gn_g, rw_gn_b, router_w, router_b, w_gu, b_gu, w_down, b_down):
    depth = ada_w.shape[0]
    dec_batch = c.shape[0]
    mod_rows = 16
    c_rows = jnp.concatenate([c, c_ctx[None, :],
                              jnp.zeros((mod_rows - dec_batch - 1, D_MODEL), F32)], axis=0)
    zero_state = jnp.zeros((x_prompt.shape[0], N_PAIRS, LANES, LANES), F32)
    y_prompt, y_sample = x_prompt, x_sample
    new_f, new_b = [], []
    for l in range(depth):
        wts = _prep_weights(l, pre_mix_g, post_mix_g, pre_ffn_g, post_ffn_g, w_in, w_out, conv_w,
                            conv_b, conv_ln_g, conv_ln_b, rw_w0, rw_w1, rw_w2, rw_a0, rw_a1, rw_a2,
                            rw_g1, rw_g2, rw_k_k, rw_k_a, rw_r_k, rw_gn_g, rw_gn_b, router_w,
                            router_b, w_gu, b_gu, w_down, b_down)
        mod = _mod_call(c_rows, ada_w[l], ada_b[l]).reshape(mod_rows, N_MOD, D_MODEL)
        mod_lat = mod[:dec_batch]
        mod_ctx = mod[dec_batch:dec_batch + 1]
        ctx = _mix_sublayer(y_prompt, mod_ctx, zero_state, zero_state, y_prompt.shape[1], wts)
        lat = _mix_sublayer(y_sample, mod_lat, _pack_state(state_wkv_fwd[:, l]),
                            _pack_state(state_wkv_bwd[:, l]), GRID_W, wts)
        new_f.append(_unpack_state(ctx[4]))
        new_b.append(_unpack_state(ctx[5]))
        y4 = _routed_ffn([ctx[1], lat[1]], [ctx[2], lat[2]], [ctx[3], lat[3]], wts)
        n_ctx = ctx[0].shape[0]
        y_prompt = _final_call(ctx[0], y4, ctx[2], mod_ctx, wts["post_ffn_g"], y_prompt.shape[1],
                               0).reshape(y_prompt.shape)
        y_sample = _final_call(lat[0], y4, lat[2], mod_lat, wts["post_ffn_g"], y_sample.shape[1],
                               n_ctx).reshape(y_sample.shape)
    return (y_prompt, y_sample, jnp.stack(new_f, axis=1), jnp.stack(new_b, axis=1))
```

```python
import functools
import math

import jax
import jax.numpy as jnp
from jax import lax
from jax.experimental import pallas as pl
from jax.experimental.pallas import tpu as pltpu
from jax.experimental.pallas import tpu_sc as plsc

F32 = jnp.float32
BF16 = jnp.bfloat16

D_MODEL = 1024
CONV_CH = 512
RWKV_DIM = 512
HEAD_DIM = 64
RWKV_HEADS = 8
N_PAIRS = RWKV_HEADS // 2
CONV_WIDTH = 31
N_EXPERTS = 32
TOP_K = 4
D_FF = 1024
SWIGLU_LIMIT = 7.0
SWIGLU_ALPHA = 1.702
RMS_EPS = 1e-6
LN_EPS = 1e-5
GN_EPS = 64e-5
N_MOD = 6
GRID_W = 64

LANES = 128
SUBLANES = 8
CHUNK = 64
LORA_COLS = 384
TM_IN = 512
TM_CONV = 256
TM_OUT = 256
TM_FIN = 512
MOE_BLOCK = 256
SC_CORES = 2
SC_SUBCORES = 16
SC_ROWS = 128
NEG_BIG = -1e30
EXP_M05 = math.exp(-0.5)
VMEM_LIMIT = 56 * 1024 * 1024


def _sigmoid(x):
    return 1.0 / (1.0 + jnp.exp(-x))


def _mm(a, b):
    return jnp.dot(a.astype(BF16), b.astype(BF16), preferred_element_type=F32)


def _mm_nt(a, b):
    return lax.dot_general(a.astype(BF16), b.astype(BF16), (((1,), (1,)), ((), ())),
                           preferred_element_type=F32)


def _mm_tn(a, b):
    return lax.dot_general(a.astype(BF16), b.astype(BF16), (((0,), (0,)), ((), ())),
                           preferred_element_type=F32)


def _split2(x):
    hi = x.astype(BF16)
    lo = (x - hi.astype(F32)).astype(BF16)
    return hi, lo


def _split3(x):
    h1 = x.astype(BF16)
    r1 = x - h1.astype(F32)
    h2 = r1.astype(BF16)
    h3 = (r1 - h2.astype(F32)).astype(BF16)
    return h1, h2, h3


def _pack_rows(x):
    n = x.shape[1] // 2
    bits = lambda v: lax.bitcast_convert_type(v.astype(BF16).astype(F32), jnp.uint32)
    word = (bits(x[:, n:]) & jnp.uint32(0xFFFF0000)) | (bits(x[:, :n]) >> 16)
    return lax.bitcast_convert_type(word, jnp.int32)


def _unpack_rows(w):
    u = lax.bitcast_convert_type(w, jnp.uint32)
    lo = lax.bitcast_convert_type(u << 16, F32)
    hi = lax.bitcast_convert_type(u & jnp.uint32(0xFFFF0000), F32)
    return jnp.concatenate([lo, hi], axis=1)


def _rms(x, g):
    return x * lax.rsqrt(jnp.mean(x * x, axis=-1, keepdims=True) + RMS_EPS) * g


def _params(*sem):
    return pltpu.CompilerParams(dimension_semantics=sem, vmem_limit_bytes=VMEM_LIMIT)


def _mod_kernel(c_ref, w_ref, b_ref, o_ref):
    c = c_ref[...]
    s1, s2, s3 = _split3(c * _sigmoid(c))
    w1, w2, w3 = _split3(w_ref[...])
    dot = functools.partial(jnp.dot, preferred_element_type=F32)
    acc = dot(s1, w1) + (dot(s1, w2) + dot(s2, w1)) + (dot(s2, w2) + dot(s1, w3) + dot(s3, w1))
    o_ref[...] = acc + b_ref[...]


def _mod_call(c_rows, ada_w, ada_b):
    m, d = c_rows.shape
    n = ada_w.shape[1]
    tn = 512
    return pl.pallas_call(
        _mod_kernel,
        grid=(n // tn,),
        in_specs=[pl.BlockSpec((m, d), lambda j: (0, 0)),
                  pl.BlockSpec((d, tn), lambda j: (0, j)),
                  pl.BlockSpec((1, tn), lambda j: (0, j))],
        out_specs=pl.BlockSpec((m, tn), lambda j: (0, j)),
        out_shape=jax.ShapeDtypeStruct((m, n), F32),
        compiler_params=_params("arbitrary"),
        name="mod",
    )(c_rows, ada_w, ada_b.reshape(1, n))


def _in_kernel(x_ref, mod_ref, g_ref, w_ref, cv_ref, rkv_ref, lo_ref):
    m = mod_ref[0]
    h = _rms(x_ref[...], g_ref[...]) * (1.0 + m[1:2]) + m[0:1]
    proj = jnp.dot(h.astype(BF16), w_ref[...], preferred_element_type=F32)
    cv_ref[...] = proj[:, :2 * CONV_CH]
    rkv_ref[...] = proj[:, 2 * CONV_CH:2 * CONV_CH + 3 * RWKV_DIM]
    lo_ref[...] = proj[:, 2 * CONV_CH + 3 * RWKV_DIM:]


def _mod_index(tile, seq_len, n_mod_rows):
    def index(i):
        return ((i * tile) // seq_len) % n_mod_rows, 0, 0
    return index


def _in_call(x, mod, g, w_all, seq_len):
    n, d = x.shape
    ncol = w_all.shape[1]
    return pl.pallas_call(
        _in_kernel,
        grid=(n // TM_IN,),
        in_specs=[pl.BlockSpec((TM_IN, d), lambda i: (i, 0)),
                  pl.BlockSpec((1, N_MOD, d), _mod_index(TM_IN, seq_len, mod.shape[0])),
                  pl.BlockSpec((1, d), lambda i: (0, 0)),
                  pl.BlockSpec((d, ncol), lambda i: (0, 0))],
        out_specs=[pl.BlockSpec((TM_IN, 2 * CONV_CH), lambda i: (i, 0)),
                   pl.BlockSpec((TM_IN, 3 * RWKV_DIM), lambda i: (i, 0)),
                   pl.BlockSpec((TM_IN, LORA_COLS), lambda i: (i, 0))],
        out_shape=[jax.ShapeDtypeStruct((n, 2 * CONV_CH), F32),
                   jax.ShapeDtypeStruct((n, 3 * RWKV_DIM), F32),
                   jax.ShapeDtypeStruct((n, LORA_COLS), F32)],
        compiler_params=_params("arbitrary"),
        name="in_proj",
    )(x, mod, g, w_all)


CONV_HALO = 16
CONV_ROWS = 32


def _conv_kernel(cv_ref, w_ref, b_ref, g_ref, be_ref, o_ref, pad_ref, acc_ref, *, seg_len):
    nseg = TM_CONV // seg_len
    stride = seg_len + 2 * CONV_HALO
    rows = nseg * stride
    cv = cv_ref[...]
    u = cv[:, :CONV_CH] * _sigmoid(cv[:, CONV_CH:])
    pad_ref[0] = jnp.zeros(pad_ref.shape[1:], F32)
    for s in range(nseg):
        pad_ref[0, s * stride + CONV_HALO:s * stride + CONV_HALO + seg_len, :] = (
            u[s * seg_len:(s + 1) * seg_len])
    for r in range(1, SUBLANES):
        pad_ref[r, 0:rows, :] = pad_ref[0, r:r + rows, :]
    first = CONV_HALO - CONV_WIDTH // 2
    chunks_per_seg = seg_len // CONV_ROWS

    def row_chunk(i, carry):
        pad0 = i * CONV_ROWS + (i // chunks_per_seg) * (2 * CONV_HALO)
        for lc in range(CONV_CH // LANES):
            cols = slice(lc * LANES, (lc + 1) * LANES)
            acc = jnp.zeros((CONV_ROWS, LANES), F32)
            for t in range(CONV_WIDTH):
                q, r = divmod(first + t, SUBLANES)
                start = pl.multiple_of(pad0 + q * SUBLANES, SUBLANES)
                acc = acc + w_ref[t:t + 1, cols] * pad_ref[r, pl.ds(start, CONV_ROWS), cols]
            acc_ref[pl.ds(pl.multiple_of(i * CONV_ROWS, CONV_ROWS), CONV_ROWS), cols] = acc
        return carry

    lax.fori_loop(0, TM_CONV // CONV_ROWS, row_chunk, 0)
    y = acc_ref[...] + b_ref[...]
    mu = jnp.mean(y, axis=-1, keepdims=True)
    yc = y - mu
    var = jnp.mean(yc * yc, axis=-1, keepdims=True)
    z = yc * lax.rsqrt(var + LN_EPS) * g_ref[...] + be_ref[...]
    o_ref[...] = z * _sigmoid(z)


def _conv_call(cv, conv_w, conv_b, ln_g, ln_b, seg_len):
    n = cv.shape[0]
    nseg = TM_CONV // seg_len
    row = lambda i: (i, 0)
    const = lambda i: (0, 0)
    return pl.pallas_call(
        functools.partial(_conv_kernel, seg_len=seg_len),
        grid=(n // TM_CONV,),
        in_specs=[pl.BlockSpec((TM_CONV, 2 * CONV_CH), row),
                  pl.BlockSpec((CONV_WIDTH, CONV_CH), const),
                  pl.BlockSpec((1, CONV_CH), const),
                  pl.BlockSpec((1, CONV_CH), const),
                  pl.BlockSpec((1, CONV_CH), const)],
        out_specs=pl.BlockSpec((TM_CONV, CONV_CH), row),
        out_shape=jax.ShapeDtypeStruct((n, CONV_CH), F32),
        scratch_shapes=[pltpu.VMEM((SUBLANES, nseg * (seg_len + 2 * CONV_HALO) + SUBLANES, CONV_CH),
                                   F32),
                        pltpu.VMEM((TM_CONV, CONV_CH), F32)],
        compiler_params=_params("arbitrary"),
        name="conv_module",
    )(cv, conv_w, conv_b, ln_g, ln_b)


def _head_sums(x, ones_bd):
    hi, lo = _split2(x)
    return jnp.dot(jnp.concatenate([hi, lo], axis=1), jnp.concatenate([ones_bd, ones_bd], axis=0),
                   preferred_element_type=F32)


def _wkv_masks():
    c = CHUNK
    r2 = lax.broadcasted_iota(jnp.int32, (2 * c, 2 * c), 0)
    c2 = lax.broadcasted_iota(jnp.int32, (2 * c, 2 * c), 1)
    same = (r2 // c) == (c2 // c)
    rr, cc = r2 % c, c2 % c
    r1 = lax.broadcasted_iota(jnp.int32, (c, 2 * c), 0)
    c1 = lax.broadcasted_iota(jnp.int32, (c, 2 * c), 1)
    cc1 = c1 % c
    rs = lax.broadcasted_iota(jnp.int32, (c, c), 0)
    cs = lax.broadcasted_iota(jnp.int32, (c, c), 1)
    masks = {"same": same, "eye": r2 == c2, "left": c1 < c,
             "ones_bd": same.astype(BF16)}
    for name, lt in (("f", lambda a, b: a < b), ("b", lambda a, b: a > b)):
        le = (lambda a, b, lt=lt: lt(a, b) | (a == b))
        masks["strict_" + name] = same & lt(cc, rr)
        masks["inc_" + name] = le(cc1, r1)
        masks["ak0_" + name] = (c1 >= c) & lt(cc1, r1)
        masks["ak1_" + name] = (c1 < c) & lt(cc1, r1)
        masks["tri_" + name] = le(cs, rs).astype(BF16)
    return masks


def _wkv_chains(chains, mk):
    c = CHUNK
    left = mk["left"]
    for ch in chains:
        zero = jnp.zeros_like(ch["at"])
        ch["at0"], ch["at1"] = jnp.where(left, ch["at"], zero), jnp.where(left, zero, ch["at"])
        rt0, rt1 = jnp.where(left, ch["rt"], zero), jnp.where(left, zero, ch["rt"])
        bt, kt = ch["bt"].astype(BF16), ch["kt"].astype(BF16)
        ch["g0"] = _mm_nt(jnp.concatenate([ch["at0"], rt0], axis=0), jnp.concatenate([bt, kt], axis=0))
        ch["g1"] = _mm_nt(jnp.concatenate([ch["at1"], rt1], axis=0), jnp.concatenate([kt, bt], axis=0))
    yield
    for ch in chains:
        g0, g1 = ch["g0"], ch["g1"]
        lmat = jnp.where(mk["strict_" + ch["d"]], jnp.concatenate([g0[:c], g1[:c]], axis=0), 0.0)
        ch["tinv"] = jnp.where(mk["eye"], 1.0, lmat)
        ch["lpow"] = _mm(lmat, lmat)
    for step in range(5):
        yield
        for ch in chains:
            lp = ch["lpow"].astype(BF16)
            if step < 4:
                prod = jnp.dot(lp, jnp.concatenate([lp, ch["tinv"].astype(BF16)], axis=1),
                               preferred_element_type=F32)
                ch["lpow"] = prod[:, :LANES]
                ch["tinv"] = ch["tinv"] + prod[:, LANES:]
            else:
                ch["tinv"] = ch["tinv"] + _mm(lp, ch["tinv"])
    yield
    for ch in chains:
        d, v = ch["d"], ch["v"]
        zv = jnp.zeros_like(v)
        ak0 = jnp.where(mk["ak0_" + d], ch["g0"][:c], 0.0)
        ak1 = jnp.where(mk["ak1_" + d], ch["g1"][:c], 0.0)
        ch["x0"] = _mm(ak0, jnp.concatenate([zv, v], axis=0))
        ch["x1"] = _mm(ak1, jnp.concatenate([v, zv], axis=0))
    yield
    for ch in chains:
        zero = jnp.zeros_like(ch["x0"])
        rhs = jnp.concatenate([
            jnp.concatenate([ch["at0"], jnp.where(left, ch["x0"], zero)], axis=1),
            jnp.concatenate([ch["at1"], jnp.where(left, zero, ch["x1"])], axis=1)], axis=0)
        ch["aw_blk"] = _mm(ch["tinv"], rhs)
    yield
    for ch in chains:
        d, v, aw_blk = ch["d"], ch["v"], ch["aw_blk"]
        zv = jnp.zeros_like(v)
        aw = aw_blk[:c] + aw_blk[c:]
        inc = mk["inc_" + d]
        g0, g1 = ch["g0"], ch["g1"]
        rb = jnp.where(inc, jnp.where(left, g0[c:], g1[c:]), 0.0)
        rk = jnp.where(inc, jnp.where(left, g1[c:], g0[c:]), 0.0)
        v0, v1 = jnp.where(left, v, zv), jnp.where(left, zv, v)
        vblk = jnp.concatenate([jnp.zeros((2 * c, LANES), F32),
                                jnp.concatenate([v1, v0], axis=0)], axis=1)
        qz = _mm(jnp.concatenate([rb, rk], axis=1), jnp.concatenate([aw_blk, vblk], axis=0))
        ch["q"] = ch["rt"] + qz[:, :LANES]
        ch["zz"] = qz[:, LANES:]
        rhs3 = jnp.concatenate([aw, jnp.concatenate([zv, v], axis=1)], axis=0)
        ch["mn"] = _mm_tn(jnp.concatenate([ch["bh"], ch["kh"]], axis=0), rhs3)
    yield
    same = mk["same"]
    for ch in chains:
        mn = ch["mn"]
        mt = jnp.where(same, mn[:, :LANES], 0.0) + jnp.where(mk["eye"], ch["ptot"], 0.0)
        nt = jnp.where(same, mn[:, LANES:], 0.0)
        both = _mm(jnp.concatenate([ch["q"], mt], axis=0), ch["st"])
        ch["y"] = both[:c] + ch["zz"]
        ch["st_new"] = both[c:] + nt


WKV_OPERANDS = ("at", "rt", "bt", "kt", "bh", "kh", "v")


def _wkv_prep(rkv_refs, lo_refs, w2_ref, w0_ref, a2_ref, a0_ref, kk_ref, ka_ref, mk,
              ops_ref, ptot_ref, slot):
    c = CHUNK
    for di, (d, rkv_ref, lo_ref) in enumerate(zip("fb", rkv_refs, lo_refs)):
        rkv = rkv_ref[...]
        lo = lo_ref[...]
        r = rkv[:, :RWKV_DIM]
        k = rkv[:, RWKV_DIM:2 * RWKV_DIM]
        v = rkv[:, 2 * RWKV_DIM:]
        w_raw = w0_ref[di:di + 1, :] + _mm(jnp.tanh(lo[:, :LANES]), w2_ref[di])
        lw = -EXP_M05 * _sigmoid(w_raw)
        yield
        ag = _sigmoid(a0_ref[di:di + 1, :] + _mm(lo[:, LANES:2 * LANES], a2_ref[di]))
        kd = k * (1.0 + (ag - 1.0) * ka_ref[...])
        kk_raw = k * kk_ref[...]
        h1, h2, h3 = _split3(lw)
        cs = jnp.dot(mk["tri_" + d], jnp.concatenate([h1, h2, h3], axis=1),
                     preferred_element_type=F32)
        logp = cs[:, :RWKV_DIM] + cs[:, RWKV_DIM:2 * RWKV_DIM] + cs[:, 2 * RWKV_DIM:]
        yield
        tot = logp[0:1] if d == "b" else logp[c - 1:c]
        e_inv = jnp.exp(-logp)
        e_hat = jnp.exp(tot - logp)
        rt = r * jnp.exp(logp)
        e_ex = jnp.exp(logp - lw)
        kt, kh = kd * e_inv, kd * e_hat
        ptot = jnp.exp(tot)
        for p in range(N_PAIRS):
            yield
            cols = slice(p * LANES, (p + 1) * LANES)
            kk = kk_raw[:, cols]
            kk = kk * lax.rsqrt(jnp.maximum(_head_sums(kk * kk, mk["ones_bd"]), 1e-24))
            kb = kk * ag[:, cols]
            tiles = {"at": -kk * e_ex[:, cols], "rt": rt[:, cols], "bt": kb * e_inv[:, cols],
                     "kt": kt[:, cols], "bh": kb * e_hat[:, cols], "kh": kh[:, cols],
                     "v": v[:, cols]}
            chain = di * N_PAIRS + p
            for j, name in enumerate(WKV_OPERANDS):
                ops_ref[slot, chain, j] = tiles[name]
            ptot_ref[slot, chain] = jnp.broadcast_to(ptot[:, cols], (SUBLANES, LANES))


def _wkv_kernel(rkv_f0_ref, lo_f0_ref, rkv_b0_ref, lo_b0_ref,
                rkv_f1_ref, lo_f1_ref, rkv_b1_ref, lo_b1_ref, s0f_ref, s0b_ref,
                w2_ref, w0_ref, a2_ref, a0_ref, kk_ref, ka_ref,
                yf_ref, yb_ref, sf_ref, sb_ref, st_ref, ops_ref, ptot_ref, *, nc):
    g = pl.program_id(0)
    ci = g % nc
    slot = g % 2
    mk = _wkv_masks()
    prm = (w2_ref, w0_ref, a2_ref, a0_ref, kk_ref, ka_ref, mk, ops_ref, ptot_ref)

    @pl.when(g == 0)
    def _():
        for _ in _wkv_prep((rkv_f0_ref, rkv_b0_ref), (lo_f0_ref, lo_b0_ref), *prm, 0):
            pass

    @pl.when(ci == 0)
    def _():
        st_ref[0] = s0f_ref[0]
        st_ref[1] = s0b_ref[0]

    chains = []
    for di, (d, y_ref) in enumerate((("f", yf_ref), ("b", yb_ref))):
        for p in range(N_PAIRS):
            ch = {"d": d, "di": di, "p": p, "y_ref": y_ref, "st": st_ref[di, p],
                  "ptot": ptot_ref[slot, di * N_PAIRS + p][0:1]}
            for j, name in enumerate(WKV_OPERANDS):
                ch[name] = ops_ref[slot, di * N_PAIRS + p, j]
            chains.append(ch)
    prep = _wkv_prep((rkv_f1_ref, rkv_b1_ref), (lo_f1_ref, lo_b1_ref), *prm, 1 - slot)
    for _ in _wkv_chains(chains, mk):
        next(prep, None)
    for _ in prep:
        pass
    for ch in chains:
        ch["y_ref"][:, ch["p"] * LANES:(ch["p"] + 1) * LANES] = ch["y"]
        st_ref[ch["di"], ch["p"]] = ch["st_new"]

    @pl.when(ci == nc - 1)
    def _():
        sf_ref[0] = st_ref[0]
        sb_ref[0] = st_ref[1]


def _wkv_call(rkv, lora, s0f, s0b, w2p, w0, a2p, a0, k_k, k_a, batch, seq_len):
    n = rkv.shape[0]
    nc = seq_len // CHUNK
    steps = batch * nc
    rev = lambda g: (g // nc) * nc + (nc - 1 - g % nc)
    fwd = lambda g: (g, 0)
    bwd = lambda g: (rev(g), 0)
    fwd_first = lambda g: (0, 0)
    bwd_first = lambda g: (nc - 1, 0)
    fwd_next = lambda g: (jnp.minimum(g + 1, steps - 1), 0)
    bwd_next = lambda g: (rev(jnp.minimum(g + 1, steps - 1)), 0)
    st = lambda g: (g // nc, 0, 0, 0)
    c2 = lambda g: (0, 0)
    c3 = lambda g: (0, 0, 0)
    st_shape = jax.ShapeDtypeStruct((batch, N_PAIRS, LANES, LANES), F32)
    n_chains = 2 * N_PAIRS
    return pl.pallas_call(
        functools.partial(_wkv_kernel, nc=nc),
        grid=(steps,),
        in_specs=[pl.BlockSpec((CHUNK, 3 * RWKV_DIM), fwd_first),
                  pl.BlockSpec((CHUNK, LORA_COLS), fwd_first),
                  pl.BlockSpec((CHUNK, 3 * RWKV_DIM), bwd_first),
                  pl.BlockSpec((CHUNK, LORA_COLS), bwd_first),
                  pl.BlockSpec((CHUNK, 3 * RWKV_DIM), fwd_next),
                  pl.BlockSpec((CHUNK, LORA_COLS), fwd_next),
                  pl.BlockSpec((CHUNK, 3 * RWKV_DIM), bwd_next),
                  pl.BlockSpec((CHUNK, LORA_COLS), bwd_next),
                  pl.BlockSpec((1, N_PAIRS, LANES, LANES), st),
                  pl.BlockSpec((1, N_PAIRS, LANES, LANES), st),
                  pl.BlockSpec((2, LANES, RWKV_DIM), c3),
                  pl.BlockSpec((2, RWKV_DIM), c2),
                  pl.BlockSpec((2, LANES, RWKV_DIM), c3),
                  pl.BlockSpec((2, RWKV_DIM), c2),
                  pl.BlockSpec((1, RWKV_DIM), c2),
                  pl.BlockSpec((1, RWKV_DIM), c2)],
        out_specs=[pl.BlockSpec((CHUNK, RWKV_DIM), fwd),
                   pl.BlockSpec((CHUNK, RWKV_DIM), bwd),
                   pl.BlockSpec((1, N_PAIRS, LANES, LANES), st),
                   pl.BlockSpec((1, N_PAIRS, LANES, LANES), st)],
        out_shape=[jax.ShapeDtypeStruct((n, RWKV_DIM), F32),
                   jax.ShapeDtypeStruct((n, RWKV_DIM), F32), st_shape, st_shape],
        scratch_shapes=[pltpu.VMEM((2, N_PAIRS, LANES, LANES), F32),
                        pltpu.VMEM((2, n_chains, len(WKV_OPERANDS), CHUNK, LANES), F32),
                        pltpu.VMEM((2, n_chains, SUBLANES, LANES), F32)],
        compiler_params=_params("arbitrary"),
        name="wkv_chunked",
    )(rkv, lora, rkv, lora, rkv, lora, rkv, lora, s0f, s0b, w2p, w0, a2p, a0, k_k, k_a)


def _out_kernel(x_ref, mod_ref, conv_ref, yf_ref, yb_ref, rkv_ref, lo_ref,
                a2_ref, a0_ref, ka_ref, rk_ref, gng_ref, gnb_ref, g2_ref, wout_ref,
                postg_ref, preg_ref, rw_ref, rb_ref,
                x1_ref, h2_ref, route_ref, cnt_ref):
    i = pl.program_id(0)

    @pl.when(i == 0)
    def _():
        cnt_ref[...] = jnp.zeros(cnt_ref.shape, F32)

    m = mod_ref[0]
    rkv = rkv_ref[...]
    lo = lo_ref[...]
    r = rkv[:, :RWKV_DIM]
    k = rkv[:, RWKV_DIM:2 * RWKV_DIM]
    v = rkv[:, 2 * RWKV_DIM:]
    ha = lo[:, LANES:2 * LANES]
    ag_f = _sigmoid(a0_ref[0:1, :] + _mm(ha, a2_ref[0]))
    ag_b = _sigmoid(a0_ref[1:2, :] + _mm(ha, a2_ref[1]))
    rkk = r * k * rk_ref[...] * (2.0 + (ag_f + ag_b - 2.0) * ka_ref[...])
    o = yf_ref[...] + yb_ref[...]
    gate = _mm(_sigmoid(lo[:, 2 * LANES:]), g2_ref[...])
    r2 = lax.broadcasted_iota(jnp.int32, (LANES, LANES), 0)
    c2 = lax.broadcasted_iota(jnp.int32, (LANES, LANES), 1)
    ones_bd = ((r2 // HEAD_DIM) == (c2 // HEAD_DIM)).astype(BF16)
    parts = []
    for p in range(N_PAIRS):
        cols = slice(p * LANES, (p + 1) * LANES)
        op = o[:, cols]
        mu = _head_sums(op, ones_bd) * (1.0 / HEAD_DIM)
        oc = op - mu
        var = _head_sums(oc * oc, ones_bd) * (1.0 / HEAD_DIM)
        on = oc * lax.rsqrt(var + GN_EPS) * gng_ref[:, cols] + gnb_ref[:, cols]
        bonus = _head_sums(rkk[:, cols], ones_bd) * v[:, cols]
        parts.append((on + bonus) * gate[:, cols])
    mix_in = jnp.concatenate([conv_ref[...]] + parts, axis=1)
    mix = jnp.dot(mix_in.astype(BF16), wout_ref[...], preferred_element_type=F32)
    x1 = x_ref[...] + m[2:3] * _rms(mix, postg_ref[...])
    x1_ref[...] = x1
    h2 = _rms(x1, preg_ref[...]) * (1.0 + m[4:5]) + m[3:4]
    h2_ref[...] = _pack_rows(h2)
    hh, hl = _split2(h2)
    dot = functools.partial(jnp.dot, preferred_element_type=F32)
    logits = dot(hh, rw_ref[0]) + (dot(hh, rw_ref[1]) + dot(hl, rw_ref[0])) + rb_ref[...]
    lane = lax.broadcasted_iota(jnp.int32, logits.shape, 1)
    work = logits
    picks, vals, idxs = [], [], []
    for _ in range(TOP_K):
        mx = jnp.max(work, axis=-1, keepdims=True)
        idx = jnp.min(jnp.where(work == mx, lane, LANES), axis=-1, keepdims=True)
        pick = lane == idx
        picks.append(pick)
        vals.append(mx)
        idxs.append(idx)
        work = jnp.where(pick, 2.0 * NEG_BIG, work)
    exps = [jnp.exp(val - vals[0]) for val in vals]
    den = exps[0] + exps[1] + exps[2] + exps[3]
    sel = jnp.zeros(logits.shape, F32)
    for pick in picks:
        sel = jnp.where(pick, 1.0, sel)
    tr = lax.broadcasted_iota(jnp.int32, (TM_OUT, TM_OUT), 0)
    tc = lax.broadcasted_iota(jnp.int32, (TM_OUT, TM_OUT), 1)
    before = (tc < tr).astype(BF16)
    rank = dot(before, sel.astype(BF16)) + cnt_ref[...]
    cnt_ref[...] = cnt_ref[...] + jnp.sum(sel, axis=0, keepdims=True)
    route = jnp.zeros(logits.shape, F32)
    for q, (pick, idx, e) in enumerate(zip(picks, idxs, exps)):
        rank_q = jnp.sum(jnp.where(pick, rank, 0.0), axis=-1, keepdims=True)
        route = jnp.where(lane == q, idx.astype(F32), route)
        route = jnp.where(lane == TOP_K + q, e / den, route)
        route = jnp.where(lane == 2 * TOP_K + q, rank_q, route)
    route_ref[...] = route


def _out_call(x, mod, conv_out, yf, yb, rkv, lora, wts, seq_len):
    n, d = x.shape
    row = lambda i: (i, 0)
    c2 = lambda i: (0, 0)
    c3 = lambda i: (0, 0, 0)
    full = lambda a: pl.BlockSpec(a.shape, c3 if a.ndim == 3 else c2)
    consts = [wts["a2p"], wts["a0"], wts["k_a"], wts["r_k"], wts["gn_g"], wts["gn_b"], wts["g2"],
              wts["w_out"], wts["post_mix_g"], wts["pre_ffn_g"], wts["router_w"], wts["router_b"]]
    return pl.pallas_call(
        _out_kernel,
        grid=(n // TM_OUT,),
        in_specs=[pl.BlockSpec((TM_OUT, d), row),
                  pl.BlockSpec((1, N_MOD, d), _mod_index(TM_OUT, seq_len, mod.shape[0])),
                  pl.BlockSpec((TM_OUT, CONV_CH), row),
                  pl.BlockSpec((TM_OUT, RWKV_DIM), row),
                  pl.BlockSpec((TM_OUT, RWKV_DIM), row),
                  pl.BlockSpec((TM_OUT, 3 * RWKV_DIM), row),
                  pl.BlockSpec((TM_OUT, LORA_COLS), row)] + [full(a) for a in consts],
        out_specs=[pl.BlockSpec((TM_OUT, d), row),
                   pl.BlockSpec((TM_OUT, d // 2), row),
                   pl.BlockSpec((TM_OUT, LANES), row),
                   pl.BlockSpec((1, LANES), c2)],
        out_shape=[jax.ShapeDtypeStruct((n, d), F32),
                   jax.ShapeDtypeStruct((n, d // 2), jnp.int32),
                   jax.ShapeDtypeStruct((n, LANES), F32),
                   jax.ShapeDtypeStruct((1, LANES), F32)],
        compiler_params=_params("arbitrary"),
        name="out_router",
    )(x, mod, conv_out, yf, yb, rkv, lora, *consts)


def _sc_rows(src, idx, n_out, scatter):
    n_idx, (n_src, d) = idx.shape[0], src.shape
    workers = SC_CORES * SC_SUBCORES
    per_worker = n_idx // workers
    assert per_worker * workers == n_idx and per_worker % SC_ROWS == 0 and n_src % SC_ROWS == 0
    mesh = plsc.VectorSubcoreMesh(core_axis_name="c", subcore_axis_name="s",
                                  num_cores=SC_CORES, num_subcores=SC_SUBCORES)

    @functools.partial(
        pl.kernel, mesh=mesh,
        out_type=jax.ShapeDtypeStruct((n_out, d), src.dtype),
        scratch_types=[pltpu.VMEM((SC_ROWS,), jnp.int32),
                       pltpu.VMEM((SC_ROWS, d), src.dtype),
                       pltpu.SemaphoreType.DMA],
        name="sc_scatter_rows" if scatter else "sc_gather_rows")
    def move(src_hbm, idx_hbm, out_hbm, idx_v, rows_v, sem):
        wid = lax.axis_index("s") * SC_CORES + lax.axis_index("c")
        base = wid * per_worker

        @pl.loop(0, per_worker // SC_ROWS)
        def _(j):
            off = base + j * SC_ROWS
            pltpu.sync_copy(idx_hbm.at[pl.ds(off, SC_ROWS)], idx_v)
            if scatter:
                pltpu.sync_copy(src_hbm.at[pl.ds(lax.rem(off, n_src), SC_ROWS)], rows_v)
                pltpu.async_copy(rows_v, out_hbm.at[idx_v], sem).wait()
            else:
                pltpu.async_copy(src_hbm.at[idx_v], rows_v, sem).wait()
                pltpu.sync_copy(rows_v, out_hbm.at[pl.ds(off, SC_ROWS)])

    return move(src, idx)


def _moe_kernel(blk_e_ref, blk_first_ref, blk_rows_ref, blk_next_ref,
                xs_ref, wgu_hbm, bgu_ref, wd_hbm, bd_ref,
                o_ref, wgu_f32_ref, wd_f32_ref, wgu_bf_ref, wd_bf_ref, sem):
    b = pl.program_id(0)
    n_rows = blk_rows_ref[b]

    def weight_copies(e):
        return (pltpu.make_async_copy(wgu_hbm.at[e], wgu_f32_ref, sem.at[0]),
                pltpu.make_async_copy(wd_hbm.at[e], wd_f32_ref, sem.at[1]))

    @pl.when(b == 0)
    def _():
        for copy in weight_copies(blk_e_ref[0]):
            copy.start()

    @pl.when(blk_first_ref[b] == 1)
    def _():
        for copy in weight_copies(blk_e_ref[b]):
            copy.wait()
        wgu_bf_ref[...] = wgu_f32_ref[...].astype(BF16)
        wd_bf_ref[...] = wd_f32_ref[...].astype(BF16)

        @pl.when(blk_next_ref[b] >= 0)
        def _():
            for copy in weight_copies(blk_next_ref[b]):
                copy.start()

    @pl.when(n_rows > 0)
    def _():
        live = lax.broadcasted_iota(jnp.int32, xs_ref.shape, 0) < n_rows
        xs = _unpack_rows(jnp.where(live, xs_ref[...], 0))
        gu = jnp.dot(xs.astype(BF16), wgu_bf_ref[...],
                     preferred_element_type=F32) + bgu_ref[0]
        x_glu = jnp.minimum(gu[:, :D_FF], SWIGLU_LIMIT)
        x_lin = jnp.clip(gu[:, D_FF:], -SWIGLU_LIMIT, SWIGLU_LIMIT)
        act = x_glu * _sigmoid(SWIGLU_ALPHA * x_glu) * (x_lin + 1.0)
        o_ref[...] = _pack_rows(jnp.dot(act.astype(BF16), wd_bf_ref[...],
                                        preferred_element_type=F32) + bd_ref[0])

    @pl.when(n_rows == 0)
    def _():
        o_ref[...] = jnp.zeros(o_ref.shape, jnp.int32)


def _moe_call(blk_e, blk_first, blk_rows, blk_next, xs, wgu, bgu, wd, bd):
    n_slots, half = xs.shape
    d = 2 * half
    ex = lambda b, be, bf, br, bn: (be[b], 0, 0)
    row = lambda b, be, bf, br, bn: (b, 0)
    grid_spec = pltpu.PrefetchScalarGridSpec(
        num_scalar_prefetch=4,
        grid=(n_slots // MOE_BLOCK,),
        in_specs=[pl.BlockSpec((MOE_BLOCK, half), row),
                  pl.BlockSpec(memory_space=pl.ANY),
                  pl.BlockSpec((1, 1, 2 * D_FF), ex),
                  pl.BlockSpec(memory_space=pl.ANY),
                  pl.BlockSpec((1, 1, d), ex)],
        out_specs=pl.BlockSpec((MOE_BLOCK, half), row),
        scratch_shapes=[pltpu.VMEM((d, 2 * D_FF), F32), pltpu.VMEM((D_FF, d), F32),
                        pltpu.VMEM((d, 2 * D_FF), BF16), pltpu.VMEM((D_FF, d), BF16),
                        pltpu.SemaphoreType.DMA((2,))])
    return pl.pallas_call(
        _moe_kernel,
        grid_spec=grid_spec,
        out_shape=jax.ShapeDtypeStruct((n_slots, half), jnp.int32),
        compiler_params=_params("arbitrary"),
        name="moe_experts",
    )(blk_e, blk_first, blk_rows, blk_next, xs, wgu, bgu, wd, bd)


def _final_kernel(x1_ref, y4_ref, route_ref, mod_ref, g_ref, o_ref):
    m = mod_ref[0]
    route = route_ref[...]
    y = jnp.zeros(x1_ref.shape, F32)
    for q in range(TOP_K):
        y = y + route[:, TOP_K + q:TOP_K + q + 1] * _unpack_rows(y4_ref[q])
    o_ref[...] = x1_ref[...] + m[5:6] * _rms(y, g_ref[...])


def _final_call(x1, y4, route, mod, g, seq_len, row_offset):
    n, d = x1.shape
    row = lambda i: (i, 0)
    shifted = lambda i: (0, i + row_offset // TM_FIN, 0)
    return pl.pallas_call(
        _final_kernel,
        grid=(n // TM_FIN,),
        in_specs=[pl.BlockSpec((TM_FIN, d), row),
                  pl.BlockSpec((TOP_K, TM_FIN, d // 2), shifted),
                  pl.BlockSpec((TM_FIN, LANES), row),
                  pl.BlockSpec((1, N_MOD, d), _mod_index(TM_FIN, seq_len, mod.shape[0])),
                  pl.BlockSpec((1, d), lambda i: (0, 0))],
        out_specs=pl.BlockSpec((TM_FIN, d), row),
        out_shape=jax.ShapeDtypeStruct((n, d), F32),
        compiler_params=_params("arbitrary"),
        name="final_residual",
    )(x1, y4, route, mod, g)


def _pack_state(s):
    b = s.shape[0]
    st = jnp.swapaxes(s.astype(F32), -1, -2).reshape(b, N_PAIRS, 2, HEAD_DIM, HEAD_DIM)
    z = jnp.zeros_like(st[:, :, 0])
    top = jnp.concatenate([st[:, :, 0], z], axis=-1)
    bot = jnp.concatenate([z, st[:, :, 1]], axis=-1)
    return jnp.concatenate([top, bot], axis=-2)


def _unpack_state(st):
    b = st.shape[0]
    h0 = st[:, :, :HEAD_DIM, :HEAD_DIM]
    h1 = st[:, :, HEAD_DIM:, HEAD_DIM:]
    s = jnp.stack([h0, h1], axis=2).reshape(b, RWKV_HEADS, HEAD_DIM, HEAD_DIM)
    return jnp.swapaxes(s, -1, -2)


def _mix_sublayer(x, mod, s0f, s0b, seg_len, wts):
    batch, seq_len, d = x.shape
    x2 = x.reshape(batch * seq_len, d)
    cv, rkv, lora = _in_call(x2, mod, wts["pre_mix_g"], wts["w_all"], seq_len)
    conv_out = _conv_call(cv, wts["conv_w"], wts["conv_b"], wts["conv_ln_g"], wts["conv_ln_b"],
                          seg_len)
    yf, yb, sf, sb = _wkv_call(rkv, lora, s0f, s0b, wts["w2p"], wts["w0"], wts["a2p"], wts["a0"],
                               wts["k_k"], wts["k_a"], batch, seq_len)
    x1, h2, route, cnt = _out_call(x2, mod, conv_out, yf, yb, rkv, lora, wts, seq_len)
    return x1, h2, route, cnt, sf, sb


def _dispatch_plan(routes, cnts):
    experts = jnp.arange(N_EXPERTS, dtype=jnp.int32)
    counts = [c[0, :N_EXPERTS].astype(jnp.int32) for c in cnts]
    total = sum(counts)
    padded = (total + MOE_BLOCK - 1) // MOE_BLOCK * MOE_BLOCK
    pad_end = jnp.cumsum(padded)
    pad_start = pad_end - padded
    dests = []
    prior = jnp.zeros_like(total)
    for route, count in zip(routes, counts):
        idx = route[:, :TOP_K].astype(jnp.int32)
        rank = route[:, 2 * TOP_K:3 * TOP_K].astype(jnp.int32)
        first = jnp.sum(jnp.where(idx[..., None] == experts, pad_start + prior, 0), axis=-1)
        dests.append(first + rank)
        prior = prior + count
    dest = jnp.concatenate(dests, axis=0).T.reshape(-1)
    n_slots = dest.shape[0] + N_EXPERTS * MOE_BLOCK
    blk_row0 = jnp.arange(n_slots // MOE_BLOCK, dtype=jnp.int32) * MOE_BLOCK
    blk_e = jnp.sum((pad_end[None, :] <= blk_row0[:, None]).astype(jnp.int32), axis=-1)
    blk_e = jnp.minimum(blk_e, N_EXPERTS - 1)
    blk_first = jnp.concatenate([jnp.ones((1,), jnp.int32),
                                 (blk_e[1:] != blk_e[:-1]).astype(jnp.int32)])
    of_blk = lambda per_expert: jnp.sum(
        jnp.where(blk_e[:, None] == experts[None, :], per_expert[None, :], 0), axis=-1)
    blk_rows = jnp.clip(of_blk(pad_start + total) - blk_row0, 0, MOE_BLOCK).astype(jnp.int32)
    pos = jnp.arange(blk_e.shape[0], dtype=jnp.int32)
    later_first = (pos[None, :] > pos[:, None]) & (blk_first[None, :] == 1)
    next_pos = jnp.min(jnp.where(later_first, pos[None, :], pos.shape[0]), axis=-1)
    blk_next = jnp.where(next_pos < pos.shape[0],
                         blk_e[jnp.minimum(next_pos, pos.shape[0] - 1)], -1).astype(jnp.int32)
    return dest, n_slots, blk_e, blk_first, blk_rows, blk_next


def _routed_ffn(h2s, routes, cnts, wts):
    dest, n_slots, blk_e, blk_first, blk_rows, blk_next = _dispatch_plan(routes, cnts)
    h2 = jnp.concatenate(h2s, axis=0)
    xs = _sc_rows(h2, dest, n_slots, scatter=True)
    out_sorted = _moe_call(blk_e, blk_first, blk_rows, blk_next, xs, wts["w_gu"], wts["b_gu"],
                           wts["w_down"], wts["b_down"])
    y4 = _sc_rows(out_sorted, dest, dest.shape[0], scatter=False)
    return y4.reshape(TOP_K, h2.shape[0], h2.shape[1])


def _prep_weights(l, pre_mix_g, post_mix_g, pre_ffn_g, post_ffn_g, w_in, w_out, conv_w, conv_b,
                  conv_ln_g, conv_ln_b, rw_w0, rw_w1, rw_w2, rw_a0, rw_a1, rw_a2, rw_g1, rw_g2,
                  rw_k_k, rw_k_a, rw_r_k, rw_gn_g, rw_gn_b, router_w, router_b, w_gu, b_gu,
                  w_down, b_down):
    row = lambda a: a.reshape(1, -1).astype(F32)
    zpad = jnp.zeros((HEAD_DIM, RWKV_DIM), F32)
    w_all = jnp.concatenate([w_in[l], rw_w1[l, 0], rw_w1[l, 1], rw_a1[l, 0], rw_a1[l, 1],
                             rw_g1[l]], axis=1).astype(BF16)
    w2p = jnp.stack([jnp.concatenate([rw_w2[l, 0], zpad], axis=0),
                     jnp.concatenate([zpad, rw_w2[l, 1]], axis=0)]).astype(BF16)
    a2p = jnp.stack([jnp.concatenate([rw_a2[l, 0], zpad], axis=0),
                     jnp.concatenate([zpad, rw_a2[l, 1]], axis=0)]).astype(BF16)
    rw_pad = jnp.pad(router_w[l].astype(F32), ((0, 0), (0, LANES - N_EXPERTS)))
    rw_hi = rw_pad.astype(BF16)
    rw_lo = (rw_pad - rw_hi.astype(F32)).astype(BF16)
    rb_pad = jnp.concatenate([router_b[l].astype(F32),
                              jnp.full((LANES - N_EXPERTS,), NEG_BIG, F32)]).reshape(1, LANES)
    return {
        "pre_mix_g": row(pre_mix_g[l]), "post_mix_g": row(post_mix_g[l]),
        "pre_ffn_g": row(pre_ffn_g[l]), "post_ffn_g": row(post_ffn_g[l]),
        "w_all": w_all, "w_out": w_out[l].astype(BF16),
        "conv_w": conv_w[l].astype(F32), "conv_b": row(conv_b[l]),
        "conv_ln_g": row(conv_ln_g[l]), "conv_ln_b": row(conv_ln_b[l]),
        "w2p": w2p, "w0": rw_w0[l].astype(F32), "a2p": a2p, "a0": rw_a0[l].astype(F32),
        "k_k": row(rw_k_k[l]), "k_a": row(rw_k_a[l]), "r_k": row(rw_r_k[l]),
        "gn_g": row(rw_gn_g[l]), "gn_b": row(rw_gn_b[l]), "g2": rw_g2[l].astype(BF16),
        "router_w": jnp.stack([rw_hi, rw_lo]), "router_b": rb_pad,
        "w_gu": w_gu[l].astype(F32), "b_gu": b_gu[l].reshape(N_EXPERTS, 1, -1).astype(F32),
        "w_down": w_down[l].astype(F32), "b_down": b_down[l].reshape(N_EXPERTS, 1, -1).astype(F32),
    }


def kernel(x_prompt, x_sample, state_wkv_fwd, state_wkv_bwd, c, c_ctx, ada_w, ada_b, pre_mix_g, post_mix_g, pre_ffn_g, post_ffn_g, w_in, w_out, conv_w, conv_b, conv_ln_g, conv_ln_b, rw_w0, rw_w1, rw_w2, rw_a0, rw_a1, rw_a2, rw_g1, rw_g2, rw_k_k, rw_k_a, rw_r_k, rw_gn_g, rw_gn_b, router_w, router_b, w_gu, b_gu, w_down, b_down):
    depth = ada_w.shape[0]
    dec_batch = c.shape[0]
    mod_rows = 16
    c_rows = jnp.concatenate([c, c_ctx[None, :],
                              jnp.zeros((mod_rows - dec_batch - 1, D_MODEL), F32)], axis=0)
    zero_state = jnp.zeros((x_prompt.shape[0], N_PAIRS, LANES, LANES), F32)
    y_prompt, y_sample = x_prompt, x_sample
    new_f, new_b = [], []
    for l in range(depth):
        wts = _prep_weights(l, pre_mix_g, post_mix_g, pre_ffn_g, post_ffn_g, w_in, w_out, conv_w,
                            conv_b, conv_ln_g, conv_ln_b, rw_w0, rw_w1, rw_w2, rw_a0, rw_a1, rw_a2,
                            rw_g1, rw_g2, rw_k_k, rw_k_a, rw_r_k, rw_gn_g, rw_gn_b, router_w,
                            router_b, w_gu, b_gu, w_down, b_down)
        mod = _mod_call(c_rows, ada_w[l], ada_b[l]).reshape(mod_rows, N_MOD, D_MODEL)
        mod_lat = mod[:dec_batch]
        mod_ctx = mod[dec_batch:dec_batch + 1]
        ctx = _mix_sublayer(y_prompt, mod_ctx, zero_state, zero_state, y_prompt.shape[1], wts)
        lat = _mix_sublayer(y_sample, mod_lat, _pack_state(state_wkv_fwd[:, l]),
                            _pack_state(state_wkv_bwd[:, l]), GRID_W, wts)
        new_f.append(_unpack_state(ctx[4]))
        new_b.append(_unpack_state(ctx[5]))
        y4 = _routed_ffn([ctx[1], lat[1]], [ctx[2], lat[2]], [ctx[3], lat[3]], wts)
        n_ctx = ctx[0].shape[0]
        y_prompt = _final_call(ctx[0], y4, ctx[2], mod_ctx, wts["post_ffn_g"], y_prompt.shape[1],
                               0).reshape(y_prompt.shape)
        y_sample = _final_call(lat[0], y4, lat[2], mod_lat, wts["post_ffn_g"], y_sample.shape[1],
                               n_ctx).reshape(y_sample.shape)
    return (y_prompt, y_sample, jnp.stack(new_f, axis=1), jnp.stack(new_b, axis=1))
```

```python
import functools
import math

import jax
import jax.numpy as jnp
from jax import lax
from jax.experimental import pallas as pl
from jax.experimental.pallas import tpu as pltpu
from jax.experimental.pallas import tpu_sc as plsc

F32 = jnp.float32
BF16 = jnp.bfloat16

D_MODEL = 1024
CONV_CH = 512
RWKV_DIM = 512
HEAD_DIM = 64
RWKV_HEADS = 8
N_PAIRS = RWKV_HEADS // 2
CONV_WIDTH = 31
N_EXPERTS = 32
TOP_K = 4
D_FF = 1024
SWIGLU_LIMIT = 7.0
SWIGLU_ALPHA = 1.702
RMS_EPS = 1e-6
LN_EPS = 1e-5
GN_EPS = 64e-5
N_MOD = 6
GRID_W = 64

LANES = 128
SUBLANES = 8
CHUNK = 64
LORA_COLS = 384
TM_IN = 512
TM_CONV = 256
TM_OUT = 256
TM_FIN = 512
MOE_BLOCK = 256
SC_CORES = 2
SC_SUBCORES = 16
SC_ROWS = 128
NEG_BIG = -1e30
EXP_M05 = math.exp(-0.5)
VMEM_LIMIT = 56 * 1024 * 1024


def _sigmoid(x):
    return 1.0 / (1.0 + jnp.exp(-x))


def _mm(a, b):
    return jnp.dot(a.astype(BF16), b.astype(BF16), preferred_element_type=F32)


def _mm_nt(a, b):
    return lax.dot_general(a.astype(BF16), b.astype(BF16), (((1,), (1,)), ((), ())),
                           preferred_element_type=F32)


def _mm_tn(a, b):
    return lax.dot_general(a.astype(BF16), b.astype(BF16), (((0,), (0,)), ((), ())),
                           preferred_element_type=F32)


def _split2(x):
    hi = x.astype(BF16)
    lo = (x - hi.astype(F32)).astype(BF16)
    return hi, lo


def _split3(x):
    h1 = x.astype(BF16)
    r1 = x - h1.astype(F32)
    h2 = r1.astype(BF16)
    h3 = (r1 - h2.astype(F32)).astype(BF16)
    return h1, h2, h3


def _pack_rows(x):
    n = x.shape[1] // 2
    bits = lambda v: lax.bitcast_convert_type(v.astype(BF16).astype(F32), jnp.uint32)
    word = (bits(x[:, n:]) & jnp.uint32(0xFFFF0000)) | (bits(x[:, :n]) >> 16)
    return lax.bitcast_convert_type(word, jnp.int32)


def _unpack_rows(w):
    u = lax.bitcast_convert_type(w, jnp.uint32)
    lo = lax.bitcast_convert_type(u << 16, F32)
    hi = lax.bitcast_convert_type(u & jnp.uint32(0xFFFF0000), F32)
    return jnp.concatenate([lo, hi], axis=1)


def _rms(x, g):
    return x * lax.rsqrt(jnp.mean(x * x, axis=-1, keepdims=True) + RMS_EPS) * g


def _params(*sem):
    return pltpu.CompilerParams(dimension_semantics=sem, vmem_limit_bytes=VMEM_LIMIT)


def _mod_kernel(c_ref, w_ref, b_ref, o_ref):
    c = c_ref[...]
    s1, s2, s3 = _split3(c * _sigmoid(c))
    w1, w2, w3 = _split3(w_ref[...])
    dot = functools.partial(jnp.dot, preferred_element_type=F32)
    acc = dot(s1, w1) + (dot(s1, w2) + dot(s2, w1)) + (dot(s2, w2) + dot(s1, w3) + dot(s3, w1))
    o_ref[...] = acc + b_ref[...]


def _mod_call(c_rows, ada_w, ada_b):
    m, d = c_rows.shape
    n = ada_w.shape[1]
    tn = 512
    return pl.pallas_call(
        _mod_kernel,
        grid=(n // tn,),
        in_specs=[pl.BlockSpec((m, d), lambda j: (0, 0)),
                  pl.BlockSpec((d, tn), lambda j: (0, j)),
                  pl.BlockSpec((1, tn), lambda j: (0, j))],
        out_specs=pl.BlockSpec((m, tn), lambda j: (0, j)),
        out_shape=jax.ShapeDtypeStruct((m, n), F32),
        compiler_params=_params("arbitrary"),
        name="mod",
    )(c_rows, ada_w, ada_b.reshape(1, n))


def _in_kernel(x_ref, mod_ref, g_ref, w_ref, cv_ref, rkv_ref, lo_ref):
    m = mod_ref[0]
    h = _rms(x_ref[...], g_ref[...]) * (1.0 + m[1:2]) + m[0:1]
    proj = jnp.dot(h.astype(BF16), w_ref[...], preferred_element_type=F32)
    cv_ref[...] = proj[:, :2 * CONV_CH]
    rkv_ref[...] = proj[:, 2 * CONV_CH:2 * CONV_CH + 3 * RWKV_DIM]
    lo_ref[...] = proj[:, 2 * CONV_CH + 3 * RWKV_DIM:]


def _mod_index(tile, seq_len, n_mod_rows):
    def index(i):
        return ((i * tile) // seq_len) % n_mod_rows, 0, 0
    return index


def _in_call(x, mod, g, w_all, seq_len):
    n, d = x.shape
    ncol = w_all.shape[1]
    return pl.pallas_call(
        _in_kernel,
        grid=(n // TM_IN,),
        in_specs=[pl.BlockSpec((TM_IN, d), lambda i: (i, 0)),
                  pl.BlockSpec((1, N_MOD, d), _mod_index(TM_IN, seq_len, mod.shape[0])),
                  pl.BlockSpec((1, d), lambda i: (0, 0)),
                  pl.BlockSpec((d, ncol), lambda i: (0, 0))],
        out_specs=[pl.BlockSpec((TM_IN, 2 * CONV_CH), lambda i: (i, 0)),
                   pl.BlockSpec((TM_IN, 3 * RWKV_DIM), lambda i: (i, 0)),
                   pl.BlockSpec((TM_IN, LORA_COLS), lambda i: (i, 0))],
        out_shape=[jax.ShapeDtypeStruct((n, 2 * CONV_CH), F32),
                   jax.ShapeDtypeStruct((n, 3 * RWKV_DIM), F32),
                   jax.ShapeDtypeStruct((n, LORA_COLS), F32)],
        compiler_params=_params("arbitrary"),
        name="in_proj",
    )(x, mod, g, w_all)


CONV_HALO = 16
CONV_ROWS = 32


def _conv_kernel(cv_ref, w_ref, b_ref, g_ref, be_ref, o_ref, pad_ref, acc_ref, *, seg_len):
    nseg = TM_CONV // seg_len
    stride = seg_len + 2 * CONV_HALO
    rows = nseg * stride
    cv = cv_ref[...]
    u = cv[:, :CONV_CH] * _sigmoid(cv[:, CONV_CH:])
    pad_ref[0] = jnp.zeros(pad_ref.shape[1:], F32)
    for s in range(nseg):
        pad_ref[0, s * stride + CONV_HALO:s * stride + CONV_HALO + seg_len, :] = (
            u[s * seg_len:(s + 1) * seg_len])
    for r in range(1, SUBLANES):
        pad_ref[r, 0:rows, :] = pad_ref[0, r:r + rows, :]
    first = CONV_HALO - CONV_WIDTH // 2
    chunks_per_seg = seg_len // CONV_ROWS

    def row_chunk(i, carry):
        pad0 = i * CONV_ROWS + (i // chunks_per_seg) * (2 * CONV_HALO)
        for lc in range(CONV_CH // LANES):
            cols = slice(lc * LANES, (lc + 1) * LANES)
            acc = jnp.zeros((CONV_ROWS, LANES), F32)
            for t in range(CONV_WIDTH):
                q, r = divmod(first + t, SUBLANES)
                start = pl.multiple_of(pad0 + q * SUBLANES, SUBLANES)
                acc = acc + w_ref[t:t + 1, cols] * pad_ref[r, pl.ds(start, CONV_ROWS), cols]
            acc_ref[pl.ds(pl.multiple_of(i * CONV_ROWS, CONV_ROWS), CONV_ROWS), cols] = acc
        return carry

    lax.fori_loop(0, TM_CONV // CONV_ROWS, row_chunk, 0)
    y = acc_ref[...] + b_ref[...]
    mu = jnp.mean(y, axis=-1, keepdims=True)
    yc = y - mu
    var = jnp.mean(yc * yc, axis=-1, keepdims=True)
    z = yc * lax.rsqrt(var + LN_EPS) * g_ref[...] + be_ref[...]
    o_ref[...] = z * _sigmoid(z)


def _conv_call(cv, conv_w, conv_b, ln_g, ln_b, seg_len):
    n = cv.shape[0]
    nseg = TM_CONV // seg_len
    row = lambda i: (i, 0)
    const = lambda i: (0, 0)
    return pl.pallas_call(
        functools.partial(_conv_kernel, seg_len=seg_len),
        grid=(n // TM_CONV,),
        in_specs=[pl.BlockSpec((TM_CONV, 2 * CONV_CH), row),
                  pl.BlockSpec((CONV_WIDTH, CONV_CH), const),
                  pl.BlockSpec((1, CONV_CH), const),
                  pl.BlockSpec((1, CONV_CH), const),
                  pl.BlockSpec((1, CONV_CH), const)],
        out_specs=pl.BlockSpec((TM_CONV, CONV_CH), row),
        out_shape=jax.ShapeDtypeStruct((n, CONV_CH), F32),
        scratch_shapes=[pltpu.VMEM((SUBLANES, nseg * (seg_len + 2 * CONV_HALO) + SUBLANES, CONV_CH),
                                   F32),
                        pltpu.VMEM((TM_CONV, CONV_CH), F32)],
        compiler_params=_params("arbitrary"),
        name="conv_module",
    )(cv, conv_w, conv_b, ln_g, ln_b)


def _head_sums(x, ones_bd):
    hi, lo = _split2(x)
    return jnp.dot(jnp.concatenate([hi, lo], axis=1), jnp.concatenate([ones_bd, ones_bd], axis=0),
                   preferred_element_type=F32)


def _wkv_masks():
    c = CHUNK
    r2 = lax.broadcasted_iota(jnp.int32, (2 * c, 2 * c), 0)
    c2 = lax.broadcasted_iota(jnp.int32, (2 * c, 2 * c), 1)
    same = (r2 // c) == (c2 // c)
    rr, cc = r2 % c, c2 % c
    r1 = lax.broadcasted_iota(jnp.int32, (c, 2 * c), 0)
    c1 = lax.broadcasted_iota(jnp.int32, (c, 2 * c), 1)
    cc1 = c1 % c
    rs = lax.broadcasted_iota(jnp.int32, (c, c), 0)
    cs = lax.broadcasted_iota(jnp.int32, (c, c), 1)
    masks = {"same": same, "eye": r2 == c2, "left": c1 < c,
             "ones_bd": same.astype(BF16)}
    for name, lt in (("f", lambda a, b: a < b), ("b", lambda a, b: a > b)):
        le = (lambda a, b, lt=lt: lt(a, b) | (a == b))
        masks["strict_" + name] = same & lt(cc, rr)
        masks["inc_" + name] = le(cc1, r1)
        masks["ak0_" + name] = (c1 >= c) & lt(cc1, r1)
        masks["ak1_" + name] = (c1 < c) & lt(cc1, r1)
        masks["tri_" + name] = le(cs, rs).astype(BF16)
    return masks


def _wkv_chains(chains, mk):
    c = CHUNK
    left = mk["left"]
    for ch in chains:
        zero = jnp.zeros_like(ch["at"])
        ch["at0"], ch["at1"] = jnp.where(left, ch["at"], zero), jnp.where(left, zero, ch["at"])
        rt0, rt1 = jnp.where(left, ch["rt"], zero), jnp.where(left, zero, ch["rt"])
        bt, kt = ch["bt"].astype(BF16), ch["kt"].astype(BF16)
        ch["g0"] = _mm_nt(jnp.concatenate([ch["at0"], rt0], axis=0), jnp.concatenate([bt, kt], axis=0))
        ch["g1"] = _mm_nt(jnp.concatenate([ch["at1"], rt1], axis=0), jnp.concatenate([kt, bt], axis=0))
    yield
    for ch in chains:
        g0, g1 = ch["g0"], ch["g1"]
        lmat = jnp.where(mk["strict_" + ch["d"]], jnp.concatenate([g0[:c], g1[:c]], axis=0), 0.0)
        ch["tinv"] = jnp.where(mk["eye"], 1.0, lmat)
        ch["lpow"] = _mm(lmat, lmat)
    for step in range(5):
        yield
        for ch in chains:
            lp = ch["lpow"].astype(BF16)
            if step < 4:
                prod = jnp.dot(lp, jnp.concatenate([lp, ch["tinv"].astype(BF16)], axis=1),
                               preferred_element_type=F32)
                ch["lpow"] = prod[:, :LANES]
                ch["tinv"] = ch["tinv"] + prod[:, LANES:]
            else:
                ch["tinv"] = ch["tinv"] + _mm(lp, ch["tinv"])
    yield
    for ch in chains:
        d, v = ch["d"], ch["v"]
        zv = jnp.zeros_like(v)
        ak0 = jnp.where(mk["ak0_" + d], ch["g0"][:c], 0.0)
        ak1 = jnp.where(mk["ak1_" + d], ch["g1"][:c], 0.0)
        ch["x0"] = _mm(ak0, jnp.concatenate([zv, v], axis=0))
        ch["x1"] = _mm(ak1, jnp.concatenate([v, zv], axis=0))
    yield
    for ch in chains:
        zero = jnp.zeros_like(ch["x0"])
        rhs = jnp.concatenate([
            jnp.concatenate([ch["at0"], jnp.where(left, ch["x0"], zero)], axis=1),
            jnp.concatenate([ch["at1"], jnp.where(left, zero, ch["x1"])], axis=1)], axis=0)
        ch["aw_blk"] = _mm(ch["tinv"], rhs)
    yield
    for ch in chains:
        d, v, aw_blk = ch["d"], ch["v"], ch["aw_blk"]
        zv = jnp.zeros_like(v)
        aw = aw_blk[:c] + aw_blk[c:]
        inc = mk["inc_" + d]
        g0, g1 = ch["g0"], ch["g1"]
        rb = jnp.where(inc, jnp.where(left, g0[c:], g1[c:]), 0.0)
        rk = jnp.where(inc, jnp.where(left, g1[c:], g0[c:]), 0.0)
        v0, v1 = jnp.where(left, v, zv), jnp.where(left, zv, v)
        vblk = jnp.concatenate([jnp.zeros((2 * c, LANES), F32),
                                jnp.concatenate([v1, v0], axis=0)], axis=1)
        qz = _mm(jnp.concatenate([rb, rk], axis=1), jnp.concatenate([aw_blk, vblk], axis=0))
        ch["q"] = ch["rt"] + qz[:, :LANES]
        ch["zz"] = qz[:, LANES:]
        rhs3 = jnp.concatenate([aw, jnp.concatenate([zv, v], axis=1)], axis=0)
        ch["mn"] = _mm_tn(jnp.concatenate([ch["bh"], ch["kh"]], axis=0), rhs3)
    yield
    same = mk["same"]
    for ch in chains:
        mn = ch["mn"]
        mt = jnp.where(same, mn[:, :LANES], 0.0) + jnp.where(mk["eye"], ch["ptot"], 0.0)
        nt = jnp.where(same, mn[:, LANES:], 0.0)
        both = _mm(jnp.concatenate([ch["q"], mt], axis=0), ch["st"])
        ch["y"] = both[:c] + ch["zz"]
        ch["st_new"] = both[c:] + nt


WKV_OPERANDS = ("at", "rt", "bt", "kt", "bh", "kh", "v")


def _wkv_prep(rkv_refs, lo_refs, w2_ref, w0_ref, a2_ref, a0_ref, kk_ref, ka_ref, mk,
              ops_ref, ptot_ref, slot):
    c = CHUNK
    for di, (d, rkv_ref, lo_ref) in enumerate(zip("fb", rkv_refs, lo_refs)):
        rkv = rkv_ref[...]
        lo = lo_ref[...]
        r = rkv[:, :RWKV_DIM]
        k = rkv[:, RWKV_DIM:2 * RWKV_DIM]
        v = rkv[:, 2 * RWKV_DIM:]
        w_raw = w0_ref[di:di + 1, :] + _mm(jnp.tanh(lo[:, :LANES]), w2_ref[di])
        lw = -EXP_M05 * _sigmoid(w_raw)
        yield
        ag = _sigmoid(a0_ref[di:di + 1, :] + _mm(lo[:, LANES:2 * LANES], a2_ref[di]))
        kd = k * (1.0 + (ag - 1.0) * ka_ref[...])
        kk_raw = k * kk_ref[...]
        h1, h2, h3 = _split3(lw)
        cs = jnp.dot(mk["tri_" + d], jnp.concatenate([h1, h2, h3], axis=1),
                     preferred_element_type=F32)
        logp = cs[:, :RWKV_DIM] + cs[:, RWKV_DIM:2 * RWKV_DIM] + cs[:, 2 * RWKV_DIM:]
        yield
        tot = logp[0:1] if d == "b" else logp[c - 1:c]
        e_inv = jnp.exp(-logp)
        e_hat = jnp.exp(tot - logp)
        rt = r * jnp.exp(logp)
        e_ex = jnp.exp(logp - lw)
        kt, kh = kd * e_inv, kd * e_hat
        ptot = jnp.exp(tot)
        for p in range(N_PAIRS):
            yield
            cols = slice(p * LANES, (p + 1) * LANES)
            kk = kk_raw[:, cols]
            kk = kk * lax.rsqrt(jnp.maximum(_head_sums(kk * kk, mk["ones_bd"]), 1e-24))
            kb = kk * ag[:, cols]
            tiles = {"at": -kk * e_ex[:, cols], "rt": rt[:, cols], "bt": kb * e_inv[:, cols],
                     "kt": kt[:, cols], "bh": kb * e_hat[:, cols], "kh": kh[:, cols],
                     "v": v[:, cols]}
            chain = di * N_PAIRS + p
            for j, name in enumerate(WKV_OPERANDS):
                ops_ref[slot, chain, j] = tiles[name]
            ptot_ref[slot, chain] = jnp.broadcast_to(ptot[:, cols], (SUBLANES, LANES))


def _wkv_kernel(rkv_f0_ref, lo_f0_ref, rkv_b0_ref, lo_b0_ref,
                rkv_f1_ref, lo_f1_ref, rkv_b1_ref, lo_b1_ref, s0f_ref, s0b_ref,
                w2_ref, w0_ref, a2_ref, a0_ref, kk_ref, ka_ref,
                yf_ref, yb_ref, sf_ref, sb_ref, st_ref, ops_ref, ptot_ref, *, nc):
    g = pl.program_id(0)
    ci = g % nc
    slot = g % 2
    mk = _wkv_masks()
    prm = (w2_ref, w0_ref, a2_ref, a0_ref, kk_ref, ka_ref, mk, ops_ref, ptot_ref)

    @pl.when(g == 0)
    def _():
        for _ in _wkv_prep((rkv_f0_ref, rkv_b0_ref), (lo_f0_ref, lo_b0_ref), *prm, 0):
            pass

    @pl.when(ci == 0)
    def _():
        st_ref[0] = s0f_ref[0]
        st_ref[1] = s0b_ref[0]

    chains = []
    for di, (d, y_ref) in enumerate((("f", yf_ref), ("b", yb_ref))):
        for p in range(N_PAIRS):
            ch = {"d": d, "di": di, "p": p, "y_ref": y_ref, "st": st_ref[di, p],
                  "ptot": ptot_ref[slot, di * N_PAIRS + p][0:1]}
            for j, name in enumerate(WKV_OPERANDS):
                ch[name] = ops_ref[slot, di * N_PAIRS + p, j]
            chains.append(ch)
    prep = _wkv_prep((rkv_f1_ref, rkv_b1_ref), (lo_f1_ref, lo_b1_ref), *prm, 1 - slot)
    for _ in _wkv_chains(chains, mk):
        next(prep, None)
    for _ in prep:
        pass
    for ch in chains:
        ch["y_ref"][:, ch["p"] * LANES:(ch["p"] + 1) * LANES] = ch["y"]
        st_ref[ch["di"], ch["p"]] = ch["st_new"]

    @pl.when(ci == nc - 1)
    def _():
        for di, s_ref in enumerate((sf_ref, sb_ref)):
            for p in range(N_PAIRS):
                st_t = st_ref[di, p].T
                s_ref[0, 2 * p] = st_t[:HEAD_DIM, :HEAD_DIM]
                s_ref[0, 2 * p + 1] = st_t[HEAD_DIM:, HEAD_DIM:]


def _wkv_call(rkv, lora, s0f, s0b, w2p, w0, a2p, a0, k_k, k_a, batch, seq_len):
    n = rkv.shape[0]
    nc = seq_len // CHUNK
    steps = batch * nc
    rev = lambda g: (g // nc) * nc + (nc - 1 - g % nc)
    fwd = lambda g: (g, 0)
    bwd = lambda g: (rev(g), 0)
    fwd_first = lambda g: (0, 0)
    bwd_first = lambda g: (nc - 1, 0)
    fwd_next = lambda g: (jnp.minimum(g + 1, steps - 1), 0)
    bwd_next = lambda g: (rev(jnp.minimum(g + 1, steps - 1)), 0)
    st = lambda g: (g // nc, 0, 0, 0)
    c2 = lambda g: (0, 0)
    c3 = lambda g: (0, 0, 0)
    st_shape = jax.ShapeDtypeStruct((batch, RWKV_HEADS, HEAD_DIM, HEAD_DIM), F32)
    n_chains = 2 * N_PAIRS
    return pl.pallas_call(
        functools.partial(_wkv_kernel, nc=nc),
        grid=(steps,),
        in_specs=[pl.BlockSpec((CHUNK, 3 * RWKV_DIM), fwd_first),
                  pl.BlockSpec((CHUNK, LORA_COLS), fwd_first),
                  pl.BlockSpec((CHUNK, 3 * RWKV_DIM), bwd_first),
                  pl.BlockSpec((CHUNK, LORA_COLS), bwd_first),
                  pl.BlockSpec((CHUNK, 3 * RWKV_DIM), fwd_next),
                  pl.BlockSpec((CHUNK, LORA_COLS), fwd_next),
                  pl.BlockSpec((CHUNK, 3 * RWKV_DIM), bwd_next),
                  pl.BlockSpec((CHUNK, LORA_COLS), bwd_next),
                  pl.BlockSpec((1, N_PAIRS, LANES, LANES), st),
                  pl.BlockSpec((1, N_PAIRS, LANES, LANES), st),
                  pl.BlockSpec((2, LANES, RWKV_DIM), c3),
                  pl.BlockSpec((2, RWKV_DIM), c2),
                  pl.BlockSpec((2, LANES, RWKV_DIM), c3),
                  pl.BlockSpec((2, RWKV_DIM), c2),
                  pl.BlockSpec((1, RWKV_DIM), c2),
                  pl.BlockSpec((1, RWKV_DIM), c2)],
        out_specs=[pl.BlockSpec((CHUNK, RWKV_DIM), fwd),
                   pl.BlockSpec((CHUNK, RWKV_DIM), bwd),
                   pl.BlockSpec((1, RWKV_HEADS, HEAD_DIM, HEAD_DIM), st),
                   pl.BlockSpec((1, RWKV_HEADS, HEAD_DIM, HEAD_DIM), st)],
        out_shape=[jax.ShapeDtypeStruct((n, RWKV_DIM), F32),
                   jax.ShapeDtypeStruct((n, RWKV_DIM), F32), st_shape, st_shape],
        scratch_shapes=[pltpu.VMEM((2, N_PAIRS, LANES, LANES), F32),
                        pltpu.VMEM((2, n_chains, len(WKV_OPERANDS), CHUNK, LANES), F32),
                        pltpu.VMEM((2, n_chains, SUBLANES, LANES), F32)],
        compiler_params=_params("arbitrary"),
        name="wkv_chunked",
    )(rkv, lora, rkv, lora, rkv, lora, rkv, lora, s0f, s0b, w2p, w0, a2p, a0, k_k, k_a)


OUT_SPLIT = 1


def _out_rows(rows, x_ref, mod_ref, conv_ref, yf_ref, yb_ref, rkv_ref, lo_ref,
              a2_ref, a0_ref, ka_ref, rk_ref, gng_ref, gnb_ref, g2_ref, wout_ref,
              postg_ref, preg_ref, rw_ref, rb_ref, x1_ref, h2_ref, route_ref, cnt_ref):
    n_rows = rows.stop - rows.start
    m = mod_ref[0]
    rkv = rkv_ref[rows, :]
    lo = lo_ref[rows, :]
    r = rkv[:, :RWKV_DIM]
    k = rkv[:, RWKV_DIM:2 * RWKV_DIM]
    v = rkv[:, 2 * RWKV_DIM:]
    ha = lo[:, LANES:2 * LANES]
    ag_f = _sigmoid(a0_ref[0:1, :] + _mm(ha, a2_ref[0]))
    ag_b = _sigmoid(a0_ref[1:2, :] + _mm(ha, a2_ref[1]))
    rkk = r * k * rk_ref[...] * (2.0 + (ag_f + ag_b - 2.0) * ka_ref[...])
    o = yf_ref[rows, :] + yb_ref[rows, :]
    gate = _mm(_sigmoid(lo[:, 2 * LANES:]), g2_ref[...])
    yield
    r2 = lax.broadcasted_iota(jnp.int32, (LANES, LANES), 0)
    c2 = lax.broadcasted_iota(jnp.int32, (LANES, LANES), 1)
    ones_bd = ((r2 // HEAD_DIM) == (c2 // HEAD_DIM)).astype(BF16)
    parts = []
    for p in range(N_PAIRS):
        cols = slice(p * LANES, (p + 1) * LANES)
        op = o[:, cols]
        mu = _head_sums(op, ones_bd) * (1.0 / HEAD_DIM)
        oc = op - mu
        var = _head_sums(oc * oc, ones_bd) * (1.0 / HEAD_DIM)
        on = oc * lax.rsqrt(var + GN_EPS) * gng_ref[:, cols] + gnb_ref[:, cols]
        bonus = _head_sums(rkk[:, cols], ones_bd) * v[:, cols]
        parts.append((on + bonus) * gate[:, cols])
    yield
    mix_in = jnp.concatenate([conv_ref[rows, :]] + parts, axis=1)
    mix = jnp.dot(mix_in.astype(BF16), wout_ref[...], preferred_element_type=F32)
    yield
    x1 = x_ref[rows, :] + m[2:3] * _rms(mix, postg_ref[...])
    x1_ref[rows, :] = x1
    h2 = _rms(x1, preg_ref[...]) * (1.0 + m[4:5]) + m[3:4]
    h2_ref[rows, :] = _pack_rows(h2)
    hh, hl = _split2(h2)
    dot = functools.partial(jnp.dot, preferred_element_type=F32)
    logits = dot(hh, rw_ref[0]) + (dot(hh, rw_ref[1]) + dot(hl, rw_ref[0])) + rb_ref[...]
    yield
    lane = lax.broadcasted_iota(jnp.int32, logits.shape, 1)
    work = logits
    picks, vals, idxs = [], [], []
    for _ in range(TOP_K):
        mx = jnp.max(work, axis=-1, keepdims=True)
        idx = jnp.min(jnp.where(work == mx, lane, LANES), axis=-1, keepdims=True)
        pick = lane == idx
        picks.append(pick)
        vals.append(mx)
        idxs.append(idx)
        work = jnp.where(pick, 2.0 * NEG_BIG, work)
    exps = [jnp.exp(val - vals[0]) for val in vals]
    den = exps[0] + exps[1] + exps[2] + exps[3]
    sel = jnp.zeros(logits.shape, F32)
    for pick in picks:
        sel = jnp.where(pick, 1.0, sel)
    yield
    tr = lax.broadcasted_iota(jnp.int32, (n_rows, n_rows), 0)
    tc = lax.broadcasted_iota(jnp.int32, (n_rows, n_rows), 1)
    before = (tc < tr).astype(BF16)
    rank = dot(before, sel.astype(BF16)) + cnt_ref[...]
    cnt_ref[...] = cnt_ref[...] + jnp.sum(sel, axis=0, keepdims=True)
    route = jnp.zeros(logits.shape, F32)
    for q, (pick, idx, e) in enumerate(zip(picks, idxs, exps)):
        rank_q = jnp.sum(jnp.where(pick, rank, 0.0), axis=-1, keepdims=True)
        route = jnp.where(lane == q, idx.astype(F32), route)
        route = jnp.where(lane == TOP_K + q, e / den, route)
        route = jnp.where(lane == 2 * TOP_K + q, rank_q, route)
    route_ref[rows, :] = route


def _out_kernel(*refs):
    cnt_ref = refs[-1]

    @pl.when(pl.program_id(0) == 0)
    def _():
        cnt_ref[...] = jnp.zeros(cnt_ref.shape, F32)

    n_rows = TM_OUT // OUT_SPLIT
    groups = [_out_rows(slice(s * n_rows, (s + 1) * n_rows), *refs) for s in range(OUT_SPLIT)]
    live = []
    while groups or live:
        if groups:
            live.append(groups.pop(0))
        for gen in list(live):
            if next(gen, "done") == "done":
                live.remove(gen)


def _out_call(x, mod, conv_out, yf, yb, rkv, lora, wts, seq_len):
    n, d = x.shape
    row = lambda i: (i, 0)
    c2 = lambda i: (0, 0)
    c3 = lambda i: (0, 0, 0)
    full = lambda a: pl.BlockSpec(a.shape, c3 if a.ndim == 3 else c2)
    consts = [wts["a2p"], wts["a0"], wts["k_a"], wts["r_k"], wts["gn_g"], wts["gn_b"], wts["g2"],
              wts["w_out"], wts["post_mix_g"], wts["pre_ffn_g"], wts["router_w"], wts["router_b"]]
    return pl.pallas_call(
        _out_kernel,
        grid=(n // TM_OUT,),
        in_specs=[pl.BlockSpec((TM_OUT, d), row),
                  pl.BlockSpec((1, N_MOD, d), _mod_index(TM_OUT, seq_len, mod.shape[0])),
                  pl.BlockSpec((TM_OUT, CONV_CH), row),
                  pl.BlockSpec((TM_OUT, RWKV_DIM), row),
                  pl.BlockSpec((TM_OUT, RWKV_DIM), row),
                  pl.BlockSpec((TM_OUT, 3 * RWKV_DIM), row),
                  pl.BlockSpec((TM_OUT, LORA_COLS), row)] + [full(a) for a in consts],
        out_specs=[pl.BlockSpec((TM_OUT, d), row),
                   pl.BlockSpec((TM_OUT, d // 2), row),
                   pl.BlockSpec((TM_OUT, LANES), row),
                   pl.BlockSpec((1, LANES), c2)],
        out_shape=[jax.ShapeDtypeStruct((n, d), F32),
                   jax.ShapeDtypeStruct((n, d // 2), jnp.int32),
                   jax.ShapeDtypeStruct((n, LANES), F32),
                   jax.ShapeDtypeStruct((1, LANES), F32)],
        compiler_params=_params("arbitrary"),
        name="out_router",
    )(x, mod, conv_out, yf, yb, rkv, lora, *consts)


def _sc_mesh():
    return plsc.VectorSubcoreMesh(core_axis_name="c", subcore_axis_name="s",
                                  num_cores=SC_CORES, num_subcores=SC_SUBCORES)


def _sc_worker_id():
    return lax.axis_index("s") * SC_CORES + lax.axis_index("c")


def _sc_scatter_rows(src_a, src_b, idx, n_out):
    (n_a, d), n_b = src_a.shape, src_b.shape[0]
    n = n_a + n_b
    workers = SC_CORES * SC_SUBCORES
    per_worker = n // workers
    assert idx.shape[0] == TOP_K * n and per_worker * workers == n
    assert per_worker % SC_ROWS == 0 and n_a % per_worker == 0

    @functools.partial(
        pl.kernel, mesh=_sc_mesh(),
        out_type=jax.ShapeDtypeStruct((n_out, d), src_a.dtype),
        scratch_types=[pltpu.VMEM((SC_ROWS,), jnp.int32),
                       pltpu.VMEM((SC_ROWS, d), src_a.dtype),
                       pltpu.SemaphoreType.DMA],
        name="sc_scatter_rows")
    def scatter(a_hbm, b_hbm, idx_hbm, out_hbm, idx_v, rows_v, sem):
        base = _sc_worker_id() * per_worker

        @pl.loop(0, per_worker // SC_ROWS)
        def _(j):
            tok = base + j * SC_ROWS

            @pl.when(tok < n_a)
            def _():
                pltpu.sync_copy(a_hbm.at[pl.ds(tok, SC_ROWS)], rows_v)

            @pl.when(tok >= n_a)
            def _():
                pltpu.sync_copy(b_hbm.at[pl.ds(tok - n_a, SC_ROWS)], rows_v)

            for q in range(TOP_K):
                pltpu.sync_copy(idx_hbm.at[pl.ds(q * n + tok, SC_ROWS)], idx_v)
                pltpu.async_copy(rows_v, out_hbm.at[idx_v], sem).wait()

    return scatter(src_a, src_b, idx)


def _sc_gather_rows(src, idx):
    n_idx, d = idx.shape[0], src.shape[1]
    workers = SC_CORES * SC_SUBCORES
    per_worker = n_idx // workers
    assert per_worker * workers == n_idx and per_worker % SC_ROWS == 0

    @functools.partial(
        pl.kernel, mesh=_sc_mesh(),
        out_type=jax.ShapeDtypeStruct((n_idx, d), src.dtype),
        scratch_types=[pltpu.VMEM((SC_ROWS,), jnp.int32),
                       pltpu.VMEM((SC_ROWS, d), src.dtype),
                       pltpu.SemaphoreType.DMA],
        name="sc_gather_rows")
    def gather(src_hbm, idx_hbm, out_hbm, idx_v, rows_v, sem):
        base = _sc_worker_id() * per_worker

        @pl.loop(0, per_worker // SC_ROWS)
        def _(j):
            off = base + j * SC_ROWS
            pltpu.sync_copy(idx_hbm.at[pl.ds(off, SC_ROWS)], idx_v)
            pltpu.async_copy(src_hbm.at[idx_v], rows_v, sem).wait()
            pltpu.sync_copy(rows_v, out_hbm.at[pl.ds(off, SC_ROWS)])

    return gather(src, idx)


def _moe_kernel(blk_e_ref, blk_first_ref, blk_rows_ref, blk_next_ref,
                xs_ref, wgu_hbm, bgu_ref, wd_hbm, bd_ref,
                o_ref, wgu_f32_ref, wd_f32_ref, wgu_bf_ref, wd_bf_ref, sem):
    b = pl.program_id(0)
    n_rows = blk_rows_ref[b]

    def weight_copies(e):
        return (pltpu.make_async_copy(wgu_hbm.at[e], wgu_f32_ref, sem.at[0]),
                pltpu.make_async_copy(wd_hbm.at[e], wd_f32_ref, sem.at[1]))

    @pl.when(b == 0)
    def _():
        for copy in weight_copies(blk_e_ref[0]):
            copy.start()

    @pl.when(blk_first_ref[b] == 1)
    def _():
        for copy in weight_copies(blk_e_ref[b]):
            copy.wait()
        wgu_bf_ref[...] = wgu_f32_ref[...].astype(BF16)
        wd_bf_ref[...] = wd_f32_ref[...].astype(BF16)

        @pl.when(blk_next_ref[b] >= 0)
        def _():
            for copy in weight_copies(blk_next_ref[b]):
                copy.start()

    @pl.when(n_rows > 0)
    def _():
        live = lax.broadcasted_iota(jnp.int32, xs_ref.shape, 0) < n_rows
        xs = _unpack_rows(jnp.where(live, xs_ref[...], 0))
        gu = jnp.dot(xs.astype(BF16), wgu_bf_ref[...],
                     preferred_element_type=F32) + bgu_ref[0]
        x_glu = jnp.minimum(gu[:, :D_FF], SWIGLU_LIMIT)
        x_lin = jnp.clip(gu[:, D_FF:], -SWIGLU_LIMIT, SWIGLU_LIMIT)
        act = x_glu * _sigmoid(SWIGLU_ALPHA * x_glu) * (x_lin + 1.0)
        o_ref[...] = _pack_rows(jnp.dot(act.astype(BF16), wd_bf_ref[...],
                                        preferred_element_type=F32) + bd_ref[0])

    @pl.when(n_rows == 0)
    def _():
        o_ref[...] = jnp.zeros(o_ref.shape, jnp.int32)


def _moe_call(blk_e, blk_first, blk_rows, blk_next, xs, wgu, bgu, wd, bd):
    n_slots, half = xs.shape
    d = 2 * half
    ex = lambda b, be, bf, br, bn: (be[b], 0, 0)
    row = lambda b, be, bf, br, bn: (b, 0)
    grid_spec = pltpu.PrefetchScalarGridSpec(
        num_scalar_prefetch=4,
        grid=(n_slots // MOE_BLOCK,),
        in_specs=[pl.BlockSpec((MOE_BLOCK, half), row),
                  pl.BlockSpec(memory_space=pl.ANY),
                  pl.BlockSpec((1, 1, 2 * D_FF), ex),
                  pl.BlockSpec(memory_space=pl.ANY),
                  pl.BlockSpec((1, 1, d), ex)],
        out_specs=pl.BlockSpec((MOE_BLOCK, half), row),
        scratch_shapes=[pltpu.VMEM((d, 2 * D_FF), F32), pltpu.VMEM((D_FF, d), F32),
                        pltpu.VMEM((d, 2 * D_FF), BF16), pltpu.VMEM((D_FF, d), BF16),
                        pltpu.SemaphoreType.DMA((2,))])
    return pl.pallas_call(
        _moe_kernel,
        grid_spec=grid_spec,
        out_shape=jax.ShapeDtypeStruct((n_slots, half), jnp.int32),
        compiler_params=_params("arbitrary"),
        name="moe_experts",
    )(blk_e, blk_first, blk_rows, blk_next, xs, wgu, bgu, wd, bd)


def _final_kernel(x1_ref, y4_ref, route_ref, mod_ref, g_ref, o_ref):
    m = mod_ref[0]
    route = route_ref[...]
    y = jnp.zeros(x1_ref.shape, F32)
    for q in range(TOP_K):
        y = y + route[:, TOP_K + q:TOP_K + q + 1] * _unpack_rows(y4_ref[q])
    o_ref[...] = x1_ref[...] + m[5:6] * _rms(y, g_ref[...])


def _final_call(x1, y4, route, mod, g, seq_len, row_offset):
    n, d = x1.shape
    row = lambda i: (i, 0)
    shifted = lambda i: (0, i + row_offset // TM_FIN, 0)
    return pl.pallas_call(
        _final_kernel,
        grid=(n // TM_FIN,),
        in_specs=[pl.BlockSpec((TM_FIN, d), row),
                  pl.BlockSpec((TOP_K, TM_FIN, d // 2), shifted),
                  pl.BlockSpec((TM_FIN, LANES), row),
                  pl.BlockSpec((1, N_MOD, d), _mod_index(TM_FIN, seq_len, mod.shape[0])),
                  pl.BlockSpec((1, d), lambda i: (0, 0))],
        out_specs=pl.BlockSpec((TM_FIN, d), row),
        out_shape=jax.ShapeDtypeStruct((n, d), F32),
        compiler_params=_params("arbitrary"),
        name="final_residual",
    )(x1, y4, route, mod, g)


def _pack_state(s):
    b = s.shape[0]
    st = jnp.swapaxes(s.astype(F32), -1, -2).reshape(b, N_PAIRS, 2, HEAD_DIM, HEAD_DIM)
    z = jnp.zeros_like(st[:, :, 0])
    top = jnp.concatenate([st[:, :, 0], z], axis=-1)
    bot = jnp.concatenate([z, st[:, :, 1]], axis=-1)
    return jnp.concatenate([top, bot], axis=-2)


def _mix_sublayer(x, mod, s0f, s0b, seg_len, wts):
    batch, seq_len, d = x.shape
    x2 = x.reshape(batch * seq_len, d)
    cv, rkv, lora = _in_call(x2, mod, wts["pre_mix_g"], wts["w_all"], seq_len)
    conv_out = _conv_call(cv, wts["conv_w"], wts["conv_b"], wts["conv_ln_g"], wts["conv_ln_b"],
                          seg_len)
    yf, yb, sf, sb = _wkv_call(rkv, lora, s0f, s0b, wts["w2p"], wts["w0"], wts["a2p"], wts["a0"],
                               wts["k_k"], wts["k_a"], batch, seq_len)
    x1, h2, route, cnt = _out_call(x2, mod, conv_out, yf, yb, rkv, lora, wts, seq_len)
    return x1, h2, route, cnt, sf, sb


def _dispatch_plan(routes, cnts):
    experts = jnp.arange(N_EXPERTS, dtype=jnp.int32)
    counts = [c[0, :N_EXPERTS].astype(jnp.int32) for c in cnts]
    total = sum(counts)
    padded = (total + MOE_BLOCK - 1) // MOE_BLOCK * MOE_BLOCK
    pad_end = jnp.cumsum(padded)
    pad_start = pad_end - padded
    dests = []
    prior = jnp.zeros_like(total)
    for route, count in zip(routes, counts):
        idx = route[:, :TOP_K].astype(jnp.int32)
        rank = route[:, 2 * TOP_K:3 * TOP_K].astype(jnp.int32)
        first = jnp.sum(jnp.where(idx[..., None] == experts, pad_start + prior, 0), axis=-1)
        dests.append(first + rank)
        prior = prior + count
    dest = jnp.concatenate(dests, axis=0).T.reshape(-1)
    n_slots = dest.shape[0] + N_EXPERTS * MOE_BLOCK
    blk_row0 = jnp.arange(n_slots // MOE_BLOCK, dtype=jnp.int32) * MOE_BLOCK
    blk_e = jnp.sum((pad_end[None, :] <= blk_row0[:, None]).astype(jnp.int32), axis=-1)
    blk_e = jnp.minimum(blk_e, N_EXPERTS - 1)
    blk_first = jnp.concatenate([jnp.ones((1,), jnp.int32),
                                 (blk_e[1:] != blk_e[:-1]).astype(jnp.int32)])
    of_blk = lambda per_expert: jnp.sum(
        jnp.where(blk_e[:, None] == experts[None, :], per_expert[None, :], 0), axis=-1)
    blk_rows = jnp.clip(of_blk(pad_start + total) - blk_row0, 0, MOE_BLOCK).astype(jnp.int32)
    pos = jnp.arange(blk_e.shape[0], dtype=jnp.int32)
    later_first = (pos[None, :] > pos[:, None]) & (blk_first[None, :] == 1)
    next_pos = jnp.min(jnp.where(later_first, pos[None, :], pos.shape[0]), axis=-1)
    blk_next = jnp.where(next_pos < pos.shape[0],
                         blk_e[jnp.minimum(next_pos, pos.shape[0] - 1)], -1).astype(jnp.int32)
    return dest, n_slots, blk_e, blk_first, blk_rows, blk_next


def _routed_ffn(h2s, routes, cnts, wts):
    dest, n_slots, blk_e, blk_first, blk_rows, blk_next = _dispatch_plan(routes, cnts)
    xs = _sc_scatter_rows(h2s[0], h2s[1], dest, n_slots)
    out_sorted = _moe_call(blk_e, blk_first, blk_rows, blk_next, xs, wts["w_gu"], wts["b_gu"],
                           wts["w_down"], wts["b_down"])
    y4 = _sc_gather_rows(out_sorted, dest)
    return y4.reshape(TOP_K, dest.shape[0] // TOP_K, xs.shape[1])


def _prep_weights(l, pre_mix_g, post_mix_g, pre_ffn_g, post_ffn_g, w_in, w_out, conv_w, conv_b,
                  conv_ln_g, conv_ln_b, rw_w0, rw_w1, rw_w2, rw_a0, rw_a1, rw_a2, rw_g1, rw_g2,
                  rw_k_k, rw_k_a, rw_r_k, rw_gn_g, rw_gn_b, router_w, router_b, w_gu, b_gu,
                  w_down, b_down):
    row = lambda a: a.reshape(1, -1).astype(F32)
    zpad = jnp.zeros((HEAD_DIM, RWKV_DIM), F32)
    w_all = jnp.concatenate([w_in[l], rw_w1[l, 0], rw_w1[l, 1], rw_a1[l, 0], rw_a1[l, 1],
                             rw_g1[l]], axis=1).astype(BF16)
    w2p = jnp.stack([jnp.concatenate([rw_w2[l, 0], zpad], axis=0),
                     jnp.concatenate([zpad, rw_w2[l, 1]], axis=0)]).astype(BF16)
    a2p = jnp.stack([jnp.concatenate([rw_a2[l, 0], zpad], axis=0),
                     jnp.concatenate([zpad, rw_a2[l, 1]], axis=0)]).astype(BF16)
    rw_pad = jnp.pad(router_w[l].astype(F32), ((0, 0), (0, LANES - N_EXPERTS)))
    rw_hi = rw_pad.astype(BF16)
    rw_lo = (rw_pad - rw_hi.astype(F32)).astype(BF16)
    rb_pad = jnp.concatenate([router_b[l].astype(F32),
                              jnp.full((LANES - N_EXPERTS,), NEG_BIG, F32)]).reshape(1, LANES)
    return {
        "pre_mix_g": row(pre_mix_g[l]), "post_mix_g": row(post_mix_g[l]),
        "pre_ffn_g": row(pre_ffn_g[l]), "post_ffn_g": row(post_ffn_g[l]),
        "w_all": w_all, "w_out": w_out[l].astype(BF16),
        "conv_w": conv_w[l].astype(F32), "conv_b": row(conv_b[l]),
        "conv_ln_g": row(conv_ln_g[l]), "conv_ln_b": row(conv_ln_b[l]),
        "w2p": w2p, "w0": rw_w0[l].astype(F32), "a2p": a2p, "a0": rw_a0[l].astype(F32),
        "k_k": row(rw_k_k[l]), "k_a": row(rw_k_a[l]), "r_k": row(rw_r_k[l]),
        "gn_g": row(rw_gn_g[l]), "gn_b": row(rw_gn_b[l]), "g2": rw_g2[l].astype(BF16),
        "router_w": jnp.stack([rw_hi, rw_lo]), "router_b": rb_pad,
        "w_gu": w_gu[l].astype(F32), "b_gu": b_gu[l].reshape(N_EXPERTS, 1, -1).astype(F32),
        "w_down": w_down[l].astype(F32), "b_down": b_down[l].reshape(N_EXPERTS, 1, -1).astype(F32),
    }


def kernel(x_prompt, x_sample, state_wkv_fwd, state_wkv_bwd, c, c_ctx, ada_w, ada_b, pre_mix_g, post_mix_g, pre_ffn_g, post_ffn_g, w_in, w_out, conv_w, conv_b, conv_ln_g, conv_ln_b, rw_w0, rw_w1, rw_w2, rw_a0, rw_a1, rw_a2, rw_g1, rw_g2, rw_k_k, rw_k_a, rw_r_k, rw_gn_g, rw_gn_b, router_w, router_b, w_gu, b_gu, w_down, b_down):
    depth = ada_w.shape[0]
    dec_batch = c.shape[0]
    mod_rows = 16
    c_rows = jnp.concatenate([c, c_ctx[None, :],
                              jnp.zeros((mod_rows - dec_batch - 1, D_MODEL), F32)], axis=0)
    zero_state = jnp.zeros((x_prompt.shape[0], N_PAIRS, LANES, LANES), F32)
    y_prompt, y_sample = x_prompt, x_sample
    new_f, new_b = [], []
    for l in range(depth):
        wts = _prep_weights(l, pre_mix_g, post_mix_g, pre_ffn_g, post_ffn_g, w_in, w_out, conv_w,
                            conv_b, conv_ln_g, conv_ln_b, rw_w0, rw_w1, rw_w2, rw_a0, rw_a1, rw_a2,
                            rw_g1, rw_g2, rw_k_k, rw_k_a, rw_r_k, rw_gn_g, rw_gn_b, router_w,
                            router_b, w_gu, b_gu, w_down, b_down)
        mod = _mod_call(c_rows, ada_w[l], ada_b[l]).reshape(mod_rows, N_MOD, D_MODEL)
        mod_lat = mod[:dec_batch]
        mod_ctx = mod[dec_batch:dec_batch + 1]
        ctx = _mix_sublayer(y_prompt, mod_ctx, zero_state, zero_state, y_prompt.shape[1], wts)
        lat = _mix_sublayer(y_sample, mod_lat, _pack_state(state_wkv_fwd[:, l]),
                            _pack_state(state_wkv_bwd[:, l]), GRID_W, wts)
        new_f.append(ctx[4])
        new_b.append(ctx[5])
        y4 = _routed_ffn([ctx[1], lat[1]], [ctx[2], lat[2]], [ctx[3], lat[3]], wts)
        n_ctx = ctx[0].shape[0]
        y_prompt = _final_call(ctx[0], y4, ctx[2], mod_ctx, wts["post_ffn_g"], y_prompt.shape[1],
                               0).reshape(y_prompt.shape)
        y_sample = _final_call(lat[0], y4, lat[2], mod_lat, wts["post_ffn_g"], y_sample.shape[1],
                               n_ctx).reshape(y_sample.shape)
    return (y_prompt, y_sample, jnp.stack(new_f, axis=1), jnp.stack(new_b, axis=1))
```

```python
import functools
import math

import jax
import jax.numpy as jnp
from jax import lax
from jax.experimental import pallas as pl
from jax.experimental.pallas import tpu as pltpu
from jax.experimental.pallas import tpu_sc as plsc

F32 = jnp.float32
BF16 = jnp.bfloat16

D_MODEL = 1024
CONV_CH = 512
RWKV_DIM = 512
HEAD_DIM = 64
RWKV_HEADS = 8
N_PAIRS = RWKV_HEADS // 2
CONV_WIDTH = 31
N_EXPERTS = 32
TOP_K = 4
D_FF = 1024
SWIGLU_LIMIT = 7.0
SWIGLU_ALPHA = 1.702
RMS_EPS = 1e-6
LN_EPS = 1e-5
GN_EPS = 64e-5
N_MOD = 6
GRID_W = 64

LANES = 128
SUBLANES = 8
CHUNK = 64
WKV_SEQS = 2
LORA_COLS = 384
TM_IN = 512
TM_CONV = 256
TM_OUT = 256
TM_FIN = 512
MOE_BLOCK = 256
SC_CORES = 2
SC_SUBCORES = 16
SC_ROWS = 128
NEG_BIG = -1e30
EXP_M05 = math.exp(-0.5)
VMEM_LIMIT = 56 * 1024 * 1024


def _sigmoid(x):
    return 1.0 / (1.0 + jnp.exp(-x))


def _mm(a, b):
    return jnp.dot(a.astype(BF16), b.astype(BF16), preferred_element_type=F32)


def _mm_nt(a, b):
    return lax.dot_general(a.astype(BF16), b.astype(BF16), (((1,), (1,)), ((), ())),
                           preferred_element_type=F32)


def _mm_tn(a, b):
    return lax.dot_general(a.astype(BF16), b.astype(BF16), (((0,), (0,)), ((), ())),
                           preferred_element_type=F32)


def _split2(x):
    hi = x.astype(BF16)
    lo = (x - hi.astype(F32)).astype(BF16)
    return hi, lo


def _split3(x):
    h1 = x.astype(BF16)
    r1 = x - h1.astype(F32)
    h2 = r1.astype(BF16)
    h3 = (r1 - h2.astype(F32)).astype(BF16)
    return h1, h2, h3


def _pack_rows(x):
    n = x.shape[1] // 2
    bits = lambda v: lax.bitcast_convert_type(v.astype(BF16).astype(F32), jnp.uint32)
    word = (bits(x[:, n:]) & jnp.uint32(0xFFFF0000)) | (bits(x[:, :n]) >> 16)
    return lax.bitcast_convert_type(word, jnp.int32)


def _unpack_rows(w):
    u = lax.bitcast_convert_type(w, jnp.uint32)
    lo = lax.bitcast_convert_type(u << 16, F32)
    hi = lax.bitcast_convert_type(u & jnp.uint32(0xFFFF0000), F32)
    return jnp.concatenate([lo, hi], axis=1)


def _rms(x, g):
    return x * lax.rsqrt(jnp.mean(x * x, axis=-1, keepdims=True) + RMS_EPS) * g


def _params(*sem):
    return pltpu.CompilerParams(dimension_semantics=sem, vmem_limit_bytes=VMEM_LIMIT)


def _mod_kernel(c_ref, w_ref, b_ref, o_ref):
    c = c_ref[...]
    s1, s2, s3 = _split3(c * _sigmoid(c))
    w1, w2, w3 = _split3(w_ref[...])
    dot = functools.partial(jnp.dot, preferred_element_type=F32)
    acc = dot(s1, w1) + (dot(s1, w2) + dot(s2, w1)) + (dot(s2, w2) + dot(s1, w3) + dot(s3, w1))
    o_ref[...] = acc + b_ref[...]


def _mod_call(c_rows, ada_w, ada_b):
    m, d = c_rows.shape
    n = ada_w.shape[1]
    tn = 512
    return pl.pallas_call(
        _mod_kernel,
        grid=(n // tn,),
        in_specs=[pl.BlockSpec((m, d), lambda j: (0, 0)),
                  pl.BlockSpec((d, tn), lambda j: (0, j)),
                  pl.BlockSpec((1, tn), lambda j: (0, j))],
        out_specs=pl.BlockSpec((m, tn), lambda j: (0, j)),
        out_shape=jax.ShapeDtypeStruct((m, n), F32),
        compiler_params=_params("arbitrary"),
        name="mod",
    )(c_rows, ada_w, ada_b.reshape(1, n))


def _in_kernel(x_ref, mod_ref, g_ref, w_ref, cv_ref, rkv_ref, lo_ref):
    m = mod_ref[0]
    h = _rms(x_ref[...], g_ref[...]) * (1.0 + m[1:2]) + m[0:1]
    proj = jnp.dot(h.astype(BF16), w_ref[...], preferred_element_type=F32)
    cv_ref[...] = proj[:, :2 * CONV_CH]
    rkv_ref[...] = proj[:, 2 * CONV_CH:2 * CONV_CH + 3 * RWKV_DIM]
    lo_ref[...] = proj[:, 2 * CONV_CH + 3 * RWKV_DIM:]


def _mod_index(tile, seq_len, n_mod_rows):
    def index(i):
        return ((i * tile) // seq_len) % n_mod_rows, 0, 0
    return index


def _in_call(x, mod, g, w_all, seq_len):
    n, d = x.shape
    ncol = w_all.shape[1]
    return pl.pallas_call(
        _in_kernel,
        grid=(n // TM_IN,),
        in_specs=[pl.BlockSpec((TM_IN, d), lambda i: (i, 0)),
                  pl.BlockSpec((1, N_MOD, d), _mod_index(TM_IN, seq_len, mod.shape[0])),
                  pl.BlockSpec((1, d), lambda i: (0, 0)),
                  pl.BlockSpec((d, ncol), lambda i: (0, 0))],
        out_specs=[pl.BlockSpec((TM_IN, 2 * CONV_CH), lambda i: (i, 0)),
                   pl.BlockSpec((TM_IN, 3 * RWKV_DIM), lambda i: (i, 0)),
                   pl.BlockSpec((TM_IN, LORA_COLS), lambda i: (i, 0))],
        out_shape=[jax.ShapeDtypeStruct((n, 2 * CONV_CH), F32),
                   jax.ShapeDtypeStruct((n, 3 * RWKV_DIM), F32),
                   jax.ShapeDtypeStruct((n, LORA_COLS), F32)],
        compiler_params=_params("arbitrary"),
        name="in_proj",
    )(x, mod, g, w_all)


CONV_HALO = 16
CONV_ROWS = 32


def _conv_kernel(cv_ref, w_ref, b_ref, g_ref, be_ref, o_ref, pad_ref, acc_ref, *, seg_len):
    nseg = TM_CONV // seg_len
    stride = seg_len + 2 * CONV_HALO
    rows = nseg * stride
    cv = cv_ref[...]
    u = cv[:, :CONV_CH] * _sigmoid(cv[:, CONV_CH:])
    pad_ref[0] = jnp.zeros(pad_ref.shape[1:], F32)
    for s in range(nseg):
        pad_ref[0, s * stride + CONV_HALO:s * stride + CONV_HALO + seg_len, :] = (
            u[s * seg_len:(s + 1) * seg_len])
    for r in range(1, SUBLANES):
        pad_ref[r, 0:rows, :] = pad_ref[0, r:r + rows, :]
    first = CONV_HALO - CONV_WIDTH // 2
    chunks_per_seg = seg_len // CONV_ROWS

    def row_chunk(i, carry):
        pad0 = i * CONV_ROWS + (i // chunks_per_seg) * (2 * CONV_HALO)
        acc = jnp.zeros((CONV_ROWS, CONV_CH), F32)
        for t in range(CONV_WIDTH):
            q, r = divmod(first + t, SUBLANES)
            start = pl.multiple_of(pad0 + q * SUBLANES, SUBLANES)
            acc = acc + w_ref[t:t + 1, :] * pad_ref[r, pl.ds(start, CONV_ROWS), :]
        acc_ref[pl.ds(pl.multiple_of(i * CONV_ROWS, CONV_ROWS), CONV_ROWS), :] = acc
        return carry

    lax.fori_loop(0, TM_CONV // CONV_ROWS, row_chunk, 0)
    y = acc_ref[...] + b_ref[...]
    mu = jnp.mean(y, axis=-1, keepdims=True)
    yc = y - mu
    var = jnp.mean(yc * yc, axis=-1, keepdims=True)
    z = yc * lax.rsqrt(var + LN_EPS) * g_ref[...] + be_ref[...]
    o_ref[...] = z * _sigmoid(z)


def _conv_call(cv, conv_w, conv_b, ln_g, ln_b, seg_len):
    n = cv.shape[0]
    nseg = TM_CONV // seg_len
    row = lambda i: (i, 0)
    const = lambda i: (0, 0)
    return pl.pallas_call(
        functools.partial(_conv_kernel, seg_len=seg_len),
        grid=(n // TM_CONV,),
        in_specs=[pl.BlockSpec((TM_CONV, 2 * CONV_CH), row),
                  pl.BlockSpec((CONV_WIDTH, CONV_CH), const),
                  pl.BlockSpec((1, CONV_CH), const),
                  pl.BlockSpec((1, CONV_CH), const),
                  pl.BlockSpec((1, CONV_CH), const)],
        out_specs=pl.BlockSpec((TM_CONV, CONV_CH), row),
        out_shape=jax.ShapeDtypeStruct((n, CONV_CH), F32),
        scratch_shapes=[pltpu.VMEM((SUBLANES, nseg * (seg_len + 2 * CONV_HALO) + SUBLANES, CONV_CH),
                                   F32),
                        pltpu.VMEM((TM_CONV, CONV_CH), F32)],
        compiler_params=_params("arbitrary"),
        name="conv_module",
    )(cv, conv_w, conv_b, ln_g, ln_b)


def _head_sums(x, ones_bd):
    hi, lo = _split2(x)
    return jnp.dot(jnp.concatenate([hi, lo], axis=1), jnp.concatenate([ones_bd, ones_bd], axis=0),
                   preferred_element_type=F32)


def _wkv_masks():
    c = CHUNK
    r2 = lax.broadcasted_iota(jnp.int32, (2 * c, 2 * c), 0)
    c2 = lax.broadcasted_iota(jnp.int32, (2 * c, 2 * c), 1)
    same = (r2 // c) == (c2 // c)
    rr, cc = r2 % c, c2 % c
    r1 = lax.broadcasted_iota(jnp.int32, (c, 2 * c), 0)
    c1 = lax.broadcasted_iota(jnp.int32, (c, 2 * c), 1)
    cc1 = c1 % c
    rs = lax.broadcasted_iota(jnp.int32, (c, c), 0)
    cs = lax.broadcasted_iota(jnp.int32, (c, c), 1)
    masks = {"same": same, "eye": r2 == c2, "left": c1 < c,
             "ones_bd": same.astype(BF16)}
    for name, lt in (("f", lambda a, b: a < b), ("b", lambda a, b: a > b)):
        le = (lambda a, b, lt=lt: lt(a, b) | (a == b))
        masks["strict_" + name] = same & lt(cc, rr)
        masks["inc_" + name] = le(cc1, r1)
        masks["ak0_" + name] = (c1 >= c) & lt(cc1, r1)
        masks["ak1_" + name] = (c1 < c) & lt(cc1, r1)
        masks["tri_" + name] = le(cs, rs).astype(BF16)
    return masks


def _wkv_chains(chains, mk):
    c = CHUNK
    left = mk["left"]
    for ch in chains:
        zero = jnp.zeros_like(ch["at"])
        ch["at0"], ch["at1"] = jnp.where(left, ch["at"], zero), jnp.where(left, zero, ch["at"])
        rt0, rt1 = jnp.where(left, ch["rt"], zero), jnp.where(left, zero, ch["rt"])
        bt, kt = ch["bt"].astype(BF16), ch["kt"].astype(BF16)
        ch["g0"] = _mm_nt(jnp.concatenate([ch["at0"], rt0], axis=0), jnp.concatenate([bt, kt], axis=0))
        ch["g1"] = _mm_nt(jnp.concatenate([ch["at1"], rt1], axis=0), jnp.concatenate([kt, bt], axis=0))
    yield
    for ch in chains:
        g0, g1 = ch["g0"], ch["g1"]
        lmat = jnp.where(mk["strict_" + ch["d"]], jnp.concatenate([g0[:c], g1[:c]], axis=0), 0.0)
        ch["tinv"] = jnp.where(mk["eye"], 1.0, lmat)
        ch["lpow"] = _mm(lmat, lmat)
    for step in range(5):
        yield
        for ch in chains:
            lp = ch["lpow"].astype(BF16)
            if step < 4:
                prod = jnp.dot(lp, jnp.concatenate([lp, ch["tinv"].astype(BF16)], axis=1),
                               preferred_element_type=F32)
                ch["lpow"] = prod[:, :LANES]
                ch["tinv"] = ch["tinv"] + prod[:, LANES:]
            else:
                ch["tinv"] = ch["tinv"] + _mm(lp, ch["tinv"])
    yield
    for ch in chains:
        d, v = ch["d"], ch["v"]
        zv = jnp.zeros_like(v)
        ak0 = jnp.where(mk["ak0_" + d], ch["g0"][:c], 0.0)
        ak1 = jnp.where(mk["ak1_" + d], ch["g1"][:c], 0.0)
        ch["x0"] = _mm(ak0, jnp.concatenate([zv, v], axis=0))
        ch["x1"] = _mm(ak1, jnp.concatenate([v, zv], axis=0))
    yield
    for ch in chains:
        zero = jnp.zeros_like(ch["x0"])
        rhs = jnp.concatenate([
            jnp.concatenate([ch["at0"], jnp.where(left, ch["x0"], zero)], axis=1),
            jnp.concatenate([ch["at1"], jnp.where(left, zero, ch["x1"])], axis=1)], axis=0)
        ch["aw_blk"] = _mm(ch["tinv"], rhs)
    yield
    for ch in chains:
        d, v, aw_blk = ch["d"], ch["v"], ch["aw_blk"]
        zv = jnp.zeros_like(v)
        aw = aw_blk[:c] + aw_blk[c:]
        inc = mk["inc_" + d]
        g0, g1 = ch["g0"], ch["g1"]
        rb = jnp.where(inc, jnp.where(left, g0[c:], g1[c:]), 0.0)
        rk = jnp.where(inc, jnp.where(left, g1[c:], g0[c:]), 0.0)
        v0, v1 = jnp.where(left, v, zv), jnp.where(left, zv, v)
        vblk = jnp.concatenate([jnp.zeros((2 * c, LANES), F32),
                                jnp.concatenate([v1, v0], axis=0)], axis=1)
        qz = _mm(jnp.concatenate([rb, rk], axis=1), jnp.concatenate([aw_blk, vblk], axis=0))
        ch["q"] = ch["rt"] + qz[:, :LANES]
        ch["zz"] = qz[:, LANES:]
        rhs3 = jnp.concatenate([aw, jnp.concatenate([zv, v], axis=1)], axis=0)
        ch["mn"] = _mm_tn(jnp.concatenate([ch["bh"], ch["kh"]], axis=0), rhs3)
    yield
    same = mk["same"]
    for ch in chains:
        mn = ch["mn"]
        mt = jnp.where(same, mn[:, :LANES], 0.0) + jnp.where(mk["eye"], ch["ptot"], 0.0)
        nt = jnp.where(same, mn[:, LANES:], 0.0)
        both = _mm(jnp.concatenate([ch["q"], mt], axis=0), ch["st"])
        ch["y"] = both[:c] + ch["zz"]
        ch["st_new"] = both[c:] + nt


WKV_OPERANDS = ("at", "rt", "bt", "kt", "bh", "kh", "v")


def _wkv_prep(*args):
    for e in range(WKV_SEQS):
        yield from _wkv_prep_seq(e, *args)


def _wkv_prep_seq(e, rkv_refs, lo_refs, w2_ref, w0_ref, a2_ref, a0_ref, kk_ref, ka_ref, mk,
                  ops_ref, ptot_ref, slot):
    c = CHUNK
    for di, (d, rkv_ref, lo_ref) in enumerate(zip("fb", rkv_refs, lo_refs)):
        rkv = rkv_ref[e]
        lo = lo_ref[e]
        r = rkv[:, :RWKV_DIM]
        k = rkv[:, RWKV_DIM:2 * RWKV_DIM]
        v = rkv[:, 2 * RWKV_DIM:]
        w_raw = w0_ref[di:di + 1, :] + _mm(jnp.tanh(lo[:, :LANES]), w2_ref[di])
        lw = -EXP_M05 * _sigmoid(w_raw)
        yield
        ag = _sigmoid(a0_ref[di:di + 1, :] + _mm(lo[:, LANES:2 * LANES], a2_ref[di]))
        kd = k * (1.0 + (ag - 1.0) * ka_ref[...])
        kk_raw = k * kk_ref[...]
        h1, h2, h3 = _split3(lw)
        cs = jnp.dot(mk["tri_" + d], jnp.concatenate([h1, h2, h3], axis=1),
                     preferred_element_type=F32)
        logp = cs[:, :RWKV_DIM] + cs[:, RWKV_DIM:2 * RWKV_DIM] + cs[:, 2 * RWKV_DIM:]
        yield
        tot = logp[0:1] if d == "b" else logp[c - 1:c]
        e_inv = jnp.exp(-logp)
        e_hat = jnp.exp(tot - logp)
        rt = r * jnp.exp(logp)
        e_ex = jnp.exp(logp - lw)
        kt, kh = kd * e_inv, kd * e_hat
        ptot = jnp.exp(tot)
        for p in range(N_PAIRS):
            yield
            cols = slice(p * LANES, (p + 1) * LANES)
            kk = kk_raw[:, cols]
            kk = kk * lax.rsqrt(jnp.maximum(_head_sums(kk * kk, mk["ones_bd"]), 1e-24))
            kb = kk * ag[:, cols]
            tiles = {"at": -kk * e_ex[:, cols], "rt": rt[:, cols], "bt": kb * e_inv[:, cols],
                     "kt": kt[:, cols], "bh": kb * e_hat[:, cols], "kh": kh[:, cols],
                     "v": v[:, cols]}
            chain = (e * 2 + di) * N_PAIRS + p
            for j, name in enumerate(WKV_OPERANDS):
                ops_ref[slot, chain, j] = tiles[name]
            ptot_ref[slot, chain] = jnp.broadcast_to(ptot[:, cols], (SUBLANES, LANES))


def _wkv_kernel(rkv_f0_ref, lo_f0_ref, rkv_b0_ref, lo_b0_ref,
                rkv_f1_ref, lo_f1_ref, rkv_b1_ref, lo_b1_ref, s0f_ref, s0b_ref,
                w2_ref, w0_ref, a2_ref, a0_ref, kk_ref, ka_ref,
                yf_ref, yb_ref, sf_ref, sb_ref, st_ref, ops_ref, ptot_ref, *, nc):
    g = pl.program_id(0)
    ci = g % nc
    slot = g % 2
    mk = _wkv_masks()
    prm = (w2_ref, w0_ref, a2_ref, a0_ref, kk_ref, ka_ref, mk, ops_ref, ptot_ref)

    @pl.when(g == 0)
    def _():
        for _ in _wkv_prep((rkv_f0_ref, rkv_b0_ref), (lo_f0_ref, lo_b0_ref), *prm, 0):
            pass

    @pl.when(ci == 0)
    def _():
        for e in range(WKV_SEQS):
            st_ref[e, 0] = s0f_ref[e]
            st_ref[e, 1] = s0b_ref[e]

    chains = []
    for e in range(WKV_SEQS):
        for di, (d, y_ref) in enumerate((("f", yf_ref), ("b", yb_ref))):
            for p in range(N_PAIRS):
                chain = (e * 2 + di) * N_PAIRS + p
                ch = {"d": d, "e": e, "di": di, "p": p, "y_ref": y_ref, "st": st_ref[e, di, p],
                      "ptot": ptot_ref[slot, chain][0:1]}
                for j, name in enumerate(WKV_OPERANDS):
                    ch[name] = ops_ref[slot, chain, j]
                chains.append(ch)
    prep = _wkv_prep((rkv_f1_ref, rkv_b1_ref), (lo_f1_ref, lo_b1_ref), *prm, 1 - slot)
    for _ in _wkv_chains(chains, mk):
        for _ in range(WKV_SEQS):
            next(prep, None)
    for _ in prep:
        pass
    for ch in chains:
        ch["y_ref"][ch["e"], :, ch["p"] * LANES:(ch["p"] + 1) * LANES] = ch["y"]
        st_ref[ch["e"], ch["di"], ch["p"]] = ch["st_new"]

    @pl.when(ci == nc - 1)
    def _():
        for e in range(WKV_SEQS):
            for di, s_ref in enumerate((sf_ref, sb_ref)):
                for p in range(N_PAIRS):
                    st_t = st_ref[e, di, p].T
                    s_ref[e, 2 * p] = st_t[:HEAD_DIM, :HEAD_DIM]
                    s_ref[e, 2 * p + 1] = st_t[HEAD_DIM:, HEAD_DIM:]


def _wkv_call(rkv, lora, s0f, s0b, w2p, w0, a2p, a0, k_k, k_a, batch, seq_len):
    n = rkv.shape[0]
    nc = seq_len // CHUNK
    assert batch % WKV_SEQS == 0
    steps = (batch // WKV_SEQS) * nc
    rkv3 = rkv.reshape(batch, seq_len, 3 * RWKV_DIM)
    lora3 = lora.reshape(batch, seq_len, LORA_COLS)
    at = lambda g, back: (g // nc, (nc - 1 - g % nc) if back else g % nc, 0)
    fwd = lambda g: at(g, False)
    bwd = lambda g: at(g, True)
    fwd_first = lambda g: (0, 0, 0)
    bwd_first = lambda g: (0, nc - 1, 0)
    fwd_next = lambda g: at(jnp.minimum(g + 1, steps - 1), False)
    bwd_next = lambda g: at(jnp.minimum(g + 1, steps - 1), True)
    st = lambda g: (g // nc, 0, 0, 0)
    c2 = lambda g: (0, 0)
    c3 = lambda g: (0, 0, 0)
    st_shape = jax.ShapeDtypeStruct((batch, RWKV_HEADS, HEAD_DIM, HEAD_DIM), F32)
    y_shape = jax.ShapeDtypeStruct((batch, seq_len, RWKV_DIM), F32)
    n_chains = WKV_SEQS * 2 * N_PAIRS
    rkv_blk = (WKV_SEQS, CHUNK, 3 * RWKV_DIM)
    lora_blk = (WKV_SEQS, CHUNK, LORA_COLS)
    yf, yb, sf, sb = pl.pallas_call(
        functools.partial(_wkv_kernel, nc=nc),
        grid=(steps,),
        in_specs=[pl.BlockSpec(rkv_blk, fwd_first),
                  pl.BlockSpec(lora_blk, fwd_first),
                  pl.BlockSpec(rkv_blk, bwd_first),
                  pl.BlockSpec(lora_blk, bwd_first),
                  pl.BlockSpec(rkv_blk, fwd_next),
                  pl.BlockSpec(lora_blk, fwd_next),
                  pl.BlockSpec(rkv_blk, bwd_next),
                  pl.BlockSpec(lora_blk, bwd_next),
                  pl.BlockSpec((WKV_SEQS, N_PAIRS, LANES, LANES), st),
                  pl.BlockSpec((WKV_SEQS, N_PAIRS, LANES, LANES), st),
                  pl.BlockSpec((2, LANES, RWKV_DIM), c3),
                  pl.BlockSpec((2, RWKV_DIM), c2),
                  pl.BlockSpec((2, LANES, RWKV_DIM), c3),
                  pl.BlockSpec((2, RWKV_DIM), c2),
                  pl.BlockSpec((1, RWKV_DIM), c2),
                  pl.BlockSpec((1, RWKV_DIM), c2)],
        out_specs=[pl.BlockSpec((WKV_SEQS, CHUNK, RWKV_DIM), fwd),
                   pl.BlockSpec((WKV_SEQS, CHUNK, RWKV_DIM), bwd),
                   pl.BlockSpec((WKV_SEQS, RWKV_HEADS, HEAD_DIM, HEAD_DIM), st),
                   pl.BlockSpec((WKV_SEQS, RWKV_HEADS, HEAD_DIM, HEAD_DIM), st)],
        out_shape=[y_shape, y_shape, st_shape, st_shape],
        scratch_shapes=[pltpu.VMEM((WKV_SEQS, 2, N_PAIRS, LANES, LANES), F32),
                        pltpu.VMEM((2, n_chains, len(WKV_OPERANDS), CHUNK, LANES), F32),
                        pltpu.VMEM((2, n_chains, SUBLANES, LANES), F32)],
        compiler_params=_params("arbitrary"),
        name="wkv_chunked",
    )(rkv3, lora3, rkv3, lora3, rkv3, lora3, rkv3, lora3, s0f, s0b, w2p, w0, a2p, a0, k_k, k_a)
    return yf.reshape(n, RWKV_DIM), yb.reshape(n, RWKV_DIM), sf, sb


def _out_kernel(x_ref, mod_ref, conv_ref, yf_ref, yb_ref, rkv_ref, lo_ref,
                a2_ref, a0_ref, ka_ref, rk_ref, gng_ref, gnb_ref, g2_ref, wout_ref,
                postg_ref, preg_ref, rw_ref, rb_ref, x1_ref, h2_ref, route_ref, cnt_ref):
    @pl.when(pl.program_id(0) == 0)
    def _():
        cnt_ref[...] = jnp.zeros(cnt_ref.shape, F32)

    m = mod_ref[0]
    rkv = rkv_ref[...]
    lo = lo_ref[...]
    r = rkv[:, :RWKV_DIM]
    k = rkv[:, RWKV_DIM:2 * RWKV_DIM]
    v = rkv[:, 2 * RWKV_DIM:]
    ha = lo[:, LANES:2 * LANES]
    ag_f = _sigmoid(a0_ref[0:1, :] + _mm(ha, a2_ref[0]))
    ag_b = _sigmoid(a0_ref[1:2, :] + _mm(ha, a2_ref[1]))
    rkk = r * k * rk_ref[...] * (2.0 + (ag_f + ag_b - 2.0) * ka_ref[...])
    o = yf_ref[...] + yb_ref[...]
    gate = _mm(_sigmoid(lo[:, 2 * LANES:]), g2_ref[...])
    r2 = lax.broadcasted_iota(jnp.int32, (LANES, LANES), 0)
    c2 = lax.broadcasted_iota(jnp.int32, (LANES, LANES), 1)
    ones_bd = ((r2 // HEAD_DIM) == (c2 // HEAD_DIM)).astype(BF16)
    parts = []
    for p in range(N_PAIRS):
        cols = slice(p * LANES, (p + 1) * LANES)
        op = o[:, cols]
        mu = _head_sums(op, ones_bd) * (1.0 / HEAD_DIM)
        oc = op - mu
        var = _head_sums(oc * oc, ones_bd) * (1.0 / HEAD_DIM)
        on = oc * lax.rsqrt(var + GN_EPS) * gng_ref[:, cols] + gnb_ref[:, cols]
        bonus = _head_sums(rkk[:, cols], ones_bd) * v[:, cols]
        parts.append((on + bonus) * gate[:, cols])
    mix_in = jnp.concatenate([conv_ref[...]] + parts, axis=1)
    mix = jnp.dot(mix_in.astype(BF16), wout_ref[...], preferred_element_type=F32)
    x1 = x_ref[...] + m[2:3] * _rms(mix, postg_ref[...])
    x1_ref[...] = x1
    h2 = _rms(x1, preg_ref[...]) * (1.0 + m[4:5]) + m[3:4]
    h2_ref[...] = _pack_rows(h2)
    hh, hl = _split2(h2)
    dot = functools.partial(jnp.dot, preferred_element_type=F32)
    logits = dot(hh, rw_ref[0]) + (dot(hh, rw_ref[1]) + dot(hl, rw_ref[0])) + rb_ref[...]
    lane =lax.broadcasted_iota(jnp.int32, logits.shape, 1)
    work = logits
    picks, vals, idxs = [], [], []
    for _ in range(TOP_K):
        mx = jnp.max(work, axis=-1, keepdims=True)
        idx = jnp.min(jnp.where(work == mx, lane, LANES), axis=-1, keepdims=True)
        pick = lane == idx
        picks.append(pick)
        vals.append(mx)
        idxs.append(idx)
        work = jnp.where(pick, 2.0 * NEG_BIG, work)
    exps = [jnp.exp(val - vals[0]) for val in vals]
    den = exps[0] + exps[1] + exps[2] + exps[3]
    sel = jnp.zeros(logits.shape, F32)
    for pick in picks:
        sel = jnp.where(pick, 1.0, sel)
    tr = lax.broadcasted_iota(jnp.int32, (TM_OUT, TM_OUT), 0)
    tc = lax.broadcasted_iota(jnp.int32, (TM_OUT, TM_OUT), 1)
    before = (tc < tr).astype(BF16)
    rank = dot(before, sel.astype(BF16)) + cnt_ref[...]
    cnt_ref[...] = cnt_ref[...] + jnp.sum(sel, axis=0, keepdims=True)
    route = jnp.zeros(logits.shape, F32)
    for q, (pick, idx, e) in enumerate(zip(picks, idxs, exps)):
        rank_q = jnp.sum(jnp.where(pick, rank, 0.0), axis=-1, keepdims=True)
        route = jnp.where(lane == q, idx.astype(F32), route)
        route = jnp.where(lane == TOP_K + q, e / den, route)
        route = jnp.where(lane == 2 * TOP_K + q, rank_q, route)
    route_ref[...] = route


def _out_call(x, mod, conv_out, yf, yb, rkv, lora, wts, seq_len):
    n, d = x.shape
    row = lambda i: (i, 0)
    c2 = lambda i: (0, 0)
    c3 = lambda i: (0, 0, 0)
    full = lambda a: pl.BlockSpec(a.shape, c3 if a.ndim == 3 else c2)
    consts = [wts["a2p"], wts["a0"], wts["k_a"], wts["r_k"], wts["gn_g"], wts["gn_b"], wts["g2"],
              wts["w_out"], wts["post_mix_g"], wts["pre_ffn_g"], wts["router_w"], wts["router_b"]]
    return pl.pallas_call(
        _out_kernel,
        grid=(n // TM_OUT,),
        in_specs=[pl.BlockSpec((TM_OUT, d), row),
                  pl.BlockSpec((1, N_MOD, d), _mod_index(TM_OUT, seq_len, mod.shape[0])),
                  pl.BlockSpec((TM_OUT, CONV_CH), row),
                  pl.BlockSpec((TM_OUT, RWKV_DIM), row),
                  pl.BlockSpec((TM_OUT, RWKV_DIM), row),
                  pl.BlockSpec((TM_OUT, 3 * RWKV_DIM), row),
                  pl.BlockSpec((TM_OUT, LORA_COLS), row)] + [full(a) for a in consts],
        out_specs=[pl.BlockSpec((TM_OUT, d), row),
                   pl.BlockSpec((TM_OUT, d // 2), row),
                   pl.BlockSpec((TM_OUT, LANES), row),
                   pl.BlockSpec((1, LANES), c2)],
        out_shape=[jax.ShapeDtypeStruct((n, d), F32),
                   jax.ShapeDtypeStruct((n, d // 2), jnp.int32),
                   jax.ShapeDtypeStruct((n, LANES), F32),
                   jax.ShapeDtypeStruct((1, LANES), F32)],
        compiler_params=_params("arbitrary"),
        name="out_router",
    )(x, mod, conv_out, yf, yb, rkv, lora, *consts)


def _sc_mesh():
    return plsc.VectorSubcoreMesh(core_axis_name="c", subcore_axis_name="s",
                                  num_cores=SC_CORES, num_subcores=SC_SUBCORES)


def _sc_worker_id():
    return lax.axis_index("s") * SC_CORES + lax.axis_index("c")


def _sc_scatter_rows(src_a, src_b, idx, n_out):
    (n_a, d), n_b = src_a.shape, src_b.shape[0]
    n = n_a + n_b
    workers = SC_CORES * SC_SUBCORES
    per_worker = n // workers
    assert idx.shape[0] == TOP_K * n and per_worker * workers == n
    assert per_worker % SC_ROWS == 0 and n_a % per_worker == 0

    @functools.partial(
        pl.kernel, mesh=_sc_mesh(),
        out_type=jax.ShapeDtypeStruct((n_out, d), src_a.dtype),
        scratch_types=[pltpu.VMEM((SC_ROWS,), jnp.int32),
                       pltpu.VMEM((SC_ROWS, d), src_a.dtype),
                       pltpu.SemaphoreType.DMA],
        name="sc_scatter_rows")
    def scatter(a_hbm, b_hbm, idx_hbm, out_hbm, idx_v, rows_v, sem):
        base = _sc_worker_id() * per_worker

        @pl.loop(0, per_worker // SC_ROWS)
        def _(j):
            tok = base + j * SC_ROWS

            @pl.when(tok < n_a)
            def _():
                pltpu.sync_copy(a_hbm.at[pl.ds(tok, SC_ROWS)], rows_v)

            @pl.when(tok >= n_a)
            def _():
                pltpu.sync_copy(b_hbm.at[pl.ds(tok - n_a, SC_ROWS)], rows_v)

            for q in range(TOP_K):
                pltpu.sync_copy(idx_hbm.at[pl.ds(q * n + tok, SC_ROWS)], idx_v)
                pltpu.async_copy(rows_v, out_hbm.at[idx_v], sem).wait()

    return scatter(src_a, src_b, idx)


def _sc_gather_rows(src, idx):
    n_idx, d = idx.shape[0], src.shape[1]
    workers = SC_CORES * SC_SUBCORES
    per_worker = n_idx // workers
    assert per_worker * workers == n_idx and per_worker % SC_ROWS == 0

    @functools.partial(
        pl.kernel, mesh=_sc_mesh(),
        out_type=jax.ShapeDtypeStruct((n_idx, d), src.dtype),
        scratch_types=[pltpu.VMEM((SC_ROWS,), jnp.int32),
                       pltpu.VMEM((SC_ROWS, d), src.dtype),
                       pltpu.SemaphoreType.DMA],
        name="sc_gather_rows")
    def gather(src_hbm, idx_hbm, out_hbm, idx_v, rows_v, sem):
        base = _sc_worker_id() * per_worker

        @pl.loop(0, per_worker // SC_ROWS)
        def _(j):
            off = base + j * SC_ROWS
            pltpu.sync_copy(idx_hbm.at[pl.ds(off, SC_ROWS)], idx_v)
            pltpu.async_copy(src_hbm.at[idx_v], rows_v, sem).wait()
            pltpu.sync_copy(rows_v, out_hbm.at[pl.ds(off, SC_ROWS)])

    return gather(src, idx)


def _moe_kernel(blk_e_ref, blk_first_ref, blk_rows_ref, blk_next_ref,
                xs_ref, wgu_hbm, bgu_ref, wd_hbm, bd_ref,
                o_ref, wgu_f32_ref, wd_f32_ref, wgu_bf_ref, wd_bf_ref, sem):
    b = pl.program_id(0)
    n_rows = blk_rows_ref[b]

    def weight_copies(e):
        return (pltpu.make_async_copy(wgu_hbm.at[e], wgu_f32_ref, sem.at[0]),
                pltpu.make_async_copy(wd_hbm.at[e], wd_f32_ref, sem.at[1]))

    @pl.when(b == 0)
    def _():
        for copy in weight_copies(blk_e_ref[0]):
            copy.start()

    @pl.when(blk_first_ref[b] == 1)
    def _():
        for copy in weight_copies(blk_e_ref[b]):
            copy.wait()
        wgu_bf_ref[...] = wgu_f32_ref[...].astype(BF16)
        wd_bf_ref[...] = wd_f32_ref[...].astype(BF16)

        @pl.when(blk_next_ref[b] >= 0)
        def _():
            for copy in weight_copies(blk_next_ref[b]):
                copy.start()

    @pl.when(n_rows > 0)
    def _():
        live = lax.broadcasted_iota(jnp.int32, xs_ref.shape, 0) < n_rows
        xs = _unpack_rows(jnp.where(live, xs_ref[...], 0))
        gu = jnp.dot(xs.astype(BF16), wgu_bf_ref[...],
                     preferred_element_type=F32) + bgu_ref[0]
        x_glu = jnp.minimum(gu[:, :D_FF], SWIGLU_LIMIT)
        x_lin = jnp.clip(gu[:, D_FF:], -SWIGLU_LIMIT, SWIGLU_LIMIT)
        act = x_glu * _sigmoid(SWIGLU_ALPHA * x_glu) * (x_lin + 1.0)
        o_ref[...] = _pack_rows(jnp.dot(act.astype(BF16), wd_bf_ref[...],
                                        preferred_element_type=F32) + bd_ref[0])

    @pl.when(n_rows == 0)
    def _():
        o_ref[...] = jnp.zeros(o_ref.shape, jnp.int32)


def _moe_call(blk_e, blk_first, blk_rows, blk_next, xs, wgu, bgu, wd, bd):
    n_slots, half = xs.shape
    d = 2 * half
    ex = lambda b, be, bf, br, bn: (be[b], 0, 0)
    row = lambda b, be, bf, br, bn: (b, 0)
    grid_spec = pltpu.PrefetchScalarGridSpec(
        num_scalar_prefetch=4,
        grid=(n_slots // MOE_BLOCK,),
        in_specs=[pl.BlockSpec((MOE_BLOCK, half), row),
                  pl.BlockSpec(memory_space=pl.ANY),
                  pl.BlockSpec((1, 1, 2 * D_FF), ex),
                  pl.BlockSpec(memory_space=pl.ANY),
                  pl.BlockSpec((1, 1, d), ex)],
        out_specs=pl.BlockSpec((MOE_BLOCK, half), row),
        scratch_shapes=[pltpu.VMEM((d, 2 * D_FF), F32), pltpu.VMEM((D_FF, d), F32),
                        pltpu.VMEM((d, 2 * D_FF), BF16), pltpu.VMEM((D_FF, d), BF16),
                        pltpu.SemaphoreType.DMA((2,))])
    return pl.pallas_call(
        _moe_kernel,
        grid_spec=grid_spec,
        out_shape=jax.ShapeDtypeStruct((n_slots, half), jnp.int32),
        compiler_params=_params("arbitrary"),
        name="moe_experts",
    )(blk_e, blk_first, blk_rows, blk_next, xs, wgu, bgu, wd, bd)


def _final_kernel(x1_ref, y4_ref, route_ref, mod_ref, g_ref, o_ref):
    m = mod_ref[0]
    route = route_ref[...]
    y = jnp.zeros(x1_ref.shape, F32)
    for q in range(TOP_K):
        y = y + route[:, TOP_K + q:TOP_K + q + 1] * _unpack_rows(y4_ref[q])
    o_ref[...] = x1_ref[...] + m[5:6] * _rms(y, g_ref[...])


def _final_call(x1, y4, route, mod, g, seq_len, row_offset):
    n, d = x1.shape
    row = lambda i: (i, 0)
    shifted = lambda i: (0, i + row_offset // TM_FIN, 0)
    return pl.pallas_call(
        _final_kernel,
        grid=(n // TM_FIN,),
        in_specs=[pl.BlockSpec((TM_FIN, d), row),
                  pl.BlockSpec((TOP_K, TM_FIN, d // 2), shifted),
                  pl.BlockSpec((TM_FIN, LANES), row),
                  pl.BlockSpec((1, N_MOD, d), _mod_index(TM_FIN, seq_len, mod.shape[0])),
                  pl.BlockSpec((1, d), lambda i: (0, 0))],
        out_specs=pl.BlockSpec((TM_FIN, d), row),
        out_shape=jax.ShapeDtypeStruct((n, d), F32),
        compiler_params=_params("arbitrary"),
        name="final_residual",
    )(x1, y4, route, mod, g)


def _pack_state(s):
    b = s.shape[0]
    st = jnp.swapaxes(s.astype(F32), -1, -2).reshape(b, N_PAIRS, 2, HEAD_DIM, HEAD_DIM)
    z = jnp.zeros_like(st[:, :, 0])
    top = jnp.concatenate([st[:, :, 0], z], axis=-1)
    bot = jnp.concatenate([z, st[:, :, 1]], axis=-1)
    return jnp.concatenate([top, bot], axis=-2)


def _mix_sublayer(x, mod, s0f, s0b, seg_len, wts):
    batch, seq_len, d = x.shape
    x2 = x.reshape(batch * seq_len, d)
    cv, rkv, lora = _in_call(x2, mod, wts["pre_mix_g"], wts["w_all"], seq_len)
    conv_out = _conv_call(cv, wts["conv_w"], wts["conv_b"], wts["conv_ln_g"], wts["conv_ln_b"],
                          seg_len)
    yf, yb, sf, sb = _wkv_call(rkv, lora, s0f, s0b, wts["w2p"], wts["w0"], wts["a2p"], wts["a0"],
                               wts["k_k"], wts["k_a"], batch, seq_len)
    x1, h2, route, cnt = _out_call(x2, mod, conv_out, yf, yb, rkv, lora, wts, seq_len)
    return x1, h2, route, cnt, sf, sb


def _dispatch_plan(routes, cnts):
    experts = jnp.arange(N_EXPERTS, dtype=jnp.int32)
    counts = [c[0, :N_EXPERTS].astype(jnp.int32) for c in cnts]
    total = sum(counts)
    padded = (total + MOE_BLOCK - 1) // MOE_BLOCK * MOE_BLOCK
    pad_end = jnp.cumsum(padded)
    pad_start = pad_end - padded
    dests = []
    prior = jnp.zeros_like(total)
    for route, count in zip(routes, counts):
        idx = route[:, :TOP_K].astype(jnp.int32)
        rank = route[:, 2 * TOP_K:3 * TOP_K].astype(jnp.int32)
        first = jnp.sum(jnp.where(idx[..., None] == experts, pad_start + prior, 0), axis=-1)
        dests.append(first + rank)
        prior = prior + count
    dest = jnp.concatenate(dests, axis=0).T.reshape(-1)
    n_slots = dest.shape[0] + N_EXPERTS * MOE_BLOCK
    blk_row0 = jnp.arange(n_slots // MOE_BLOCK, dtype=jnp.int32) * MOE_BLOCK
    blk_e = jnp.sum((pad_end[None, :] <= blk_row0[:, None]).astype(jnp.int32), axis=-1)
    blk_e = jnp.minimum(blk_e, N_EXPERTS - 1)
    blk_first = jnp.concatenate([jnp.ones((1,), jnp.int32),
                                 (blk_e[1:] != blk_e[:-1]).astype(jnp.int32)])
    of_blk = lambda per_expert: jnp.sum(
        jnp.where(blk_e[:, None] == experts[None, :], per_expert[None, :], 0), axis=-1)
    blk_rows = jnp.clip(of_blk(pad_start + total) - blk_row0, 0, MOE_BLOCK).astype(jnp.int32)
    pos = jnp.arange(blk_e.shape[0], dtype=jnp.int32)
    later_first = (pos[None, :] > pos[:, None]) & (blk_first[None, :] == 1)
    next_pos = jnp.min(jnp.where(later_first, pos[None, :], pos.shape[0]), axis=-1)
    blk_next = jnp.where(next_pos < pos.shape[0],
                         blk_e[jnp.minimum(next_pos, pos.shape[0] - 1)], -1).astype(jnp.int32)
    return dest, n_slots, blk_e, blk_first, blk_rows, blk_next


def _routed_ffn(h2s, routes, cnts, wts):
    dest, n_slots, blk_e, blk_first, blk_rows, blk_next = _dispatch_plan(routes, cnts)
    xs = _sc_scatter_rows(h2s[0], h2s[1], dest, n_slots)
    out_sorted = _moe_call(blk_e, blk_first, blk_rows, blk_next, xs, wts["w_gu"], wts["b_gu"],
                           wts["w_down"], wts["b_down"])
    y4 = _sc_gather_rows(out_sorted, dest)
    return y4.reshape(TOP_K, dest.shape[0] // TOP_K, xs.shape[1])


def _prep_weights(l, pre_mix_g, post_mix_g, pre_ffn_g, post_ffn_g, w_in, w_out, conv_w, conv_b,
                  conv_ln_g, conv_ln_b, rw_w0, rw_w1, rw_w2, rw_a0, rw_a1, rw_a2, rw_g1, rw_g2,
                  rw_k_k, rw_k_a, rw_r_k, rw_gn_g, rw_gn_b, router_w, router_b, w_gu, b_gu,
                  w_down, b_down):
    row = lambda a: a.reshape(1, -1).astype(F32)
    zpad = jnp.zeros((HEAD_DIM, RWKV_DIM), F32)
    w_all = jnp.concatenate([w_in[l], rw_w1[l, 0], rw_w1[l, 1], rw_a1[l, 0], rw_a1[l, 1],
                             rw_g1[l]], axis=1).astype(BF16)
    w2p = jnp.stack([jnp.concatenate([rw_w2[l, 0], zpad], axis=0),
                     jnp.concatenate([zpad, rw_w2[l, 1]], axis=0)]).astype(BF16)
    a2p = jnp.stack([jnp.concatenate([rw_a2[l, 0], zpad], axis=0),
                     jnp.concatenate([zpad, rw_a2[l, 1]], axis=0)]).astype(BF16)
    rw_pad = jnp.pad(router_w[l].astype(F32), ((0, 0), (0, LANES - N_EXPERTS)))
    rw_hi = rw_pad.astype(BF16)
    rw_lo = (rw_pad - rw_hi.astype(F32)).astype(BF16)
    rb_pad = jnp.concatenate([router_b[l].astype(F32),
                              jnp.full((LANES - N_EXPERTS,), NEG_BIG, F32)]).reshape(1, LANES)
    return {
        "pre_mix_g": row(pre_mix_g[l]), "post_mix_g": row(post_mix_g[l]),
        "pre_ffn_g": row(pre_ffn_g[l]), "post_ffn_g": row(post_ffn_g[l]),
        "w_all": w_all, "w_out": w_out[l].astype(BF16),
        "conv_w": conv_w[l].astype(F32), "conv_b": row(conv_b[l]),
        "conv_ln_g": row(conv_ln_g[l]), "conv_ln_b": row(conv_ln_b[l]),
        "w2p": w2p, "w0": rw_w0[l].astype(F32), "a2p": a2p, "a0": rw_a0[l].astype(F32),
        "k_k": row(rw_k_k[l]), "k_a": row(rw_k_a[l]), "r_k": row(rw_r_k[l]),
        "gn_g": row(rw_gn_g[l]), "gn_b": row(rw_gn_b[l]), "g2": rw_g2[l].astype(BF16),
        "router_w": jnp.stack([rw_hi, rw_lo]), "router_b": rb_pad,
        "w_gu": w_gu[l].astype(F32), "b_gu": b_gu[l].reshape(N_EXPERTS, 1, -1).astype(F32),
        "w_down": w_down[l].astype(F32), "b_down": b_down[l].reshape(N_EXPERTS, 1, -1).astype(F32),
    }


def kernel(x_prompt, x_sample, state_wkv_fwd, state_wkv_bwd, c, c_ctx, ada_w, ada_b, pre_mix_g, post_mix_g, pre_ffn_g, post_ffn_g, w_in, w_out, conv_w, conv_b, conv_ln_g, conv_ln_b, rw_w0, rw_w1, rw_w2, rw_a0, rw_a1, rw_a2, rw_g1, rw_g2, rw_k_k, rw_k_a, rw_r_k, rw_gn_g, rw_gn_b, router_w, router_b, w_gu, b_gu, w_down, b_down):
    depth = ada_w.shape[0]
    dec_batch = c.shape[0]
    mod_rows = 16
    c_rows = jnp.concatenate([c, c_ctx[None, :],
                              jnp.zeros((mod_rows - dec_batch - 1, D_MODEL), F32)], axis=0)
    zero_state = jnp.zeros((x_prompt.shape[0], N_PAIRS, LANES, LANES), F32)
    y_prompt, y_sample = x_prompt, x_sample
    new_f, new_b = [], []
    for l in range(depth):
        wts = _prep_weights(l, pre_mix_g, post_mix_g, pre_ffn_g, post_ffn_g, w_in, w_out, conv_w,
                            conv_b, conv_ln_g, conv_ln_b, rw_w0, rw_w1, rw_w2, rw_a0, rw_a1, rw_a2,
                            rw_g1, rw_g2, rw_k_k, rw_k_a, rw_r_k, rw_gn_g, rw_gn_b, router_w,
                            router_b, w_gu, b_gu, w_down, b_down)
        mod = _mod_call(c_rows, ada_w[l], ada_b[l]).reshape(mod_rows, N_MOD, D_MODEL)
        mod_lat = mod[:dec_batch]
        mod_ctx = mod[dec_batch:dec_batch + 1]
        ctx = _mix_sublayer(y_prompt, mod_ctx, zero_state, zero_state, y_prompt.shape[1], wts)
        lat = _mix_sublayer(y_sample, mod_lat, _pack_state(state_wkv_fwd[:, l]),
                            _pack_state(state_wkv_bwd[:, l]), GRID_W, wts)
        new_f.append(ctx[4])
        new_b.append(ctx[5])
        y4 = _routed_ffn([ctx[1], lat[1]], [ctx[2], lat[2]], [ctx[3], lat[3]], wts)
        n_ctx = ctx[0].shape[0]
        y_prompt = _final_call(ctx[0], y4, ctx[2], mod_ctx, wts["post_ffn_g"], y_prompt.shape[1],
                               0).reshape(y_prompt.shape)
        y_sample = _final_call(lat[0], y4, lat[2], mod_lat, wts["post_ffn_g"], y_sample.shape[1],
                               n_ctx).reshape(y_sample.shape)
    return (y_prompt, y_sample, jnp.stack(new_f, axis=1), jnp.stack(new_b, axis=1))
```

```python
import functools
import math

import jax
import jax.numpy as jnp
from jax import lax
from jax.experimental import pallas as pl
from jax.experimental.pallas import tpu as pltpu
from jax.experimental.pallas import tpu_sc as plsc

F32 = jnp.float32
BF16 = jnp.bfloat16

D_MODEL = 1024
CONV_CH = 512
RWKV_DIM = 512
HEAD_DIM = 64
RWKV_HEADS = 8
N_PAIRS = RWKV_HEADS // 2
CONV_WIDTH = 31
N_EXPERTS = 32
TOP_K = 4
D_FF = 1024
SWIGLU_LIMIT = 7.0
SWIGLU_ALPHA = 1.702
RMS_EPS = 1e-6
LN_EPS = 1e-5
GN_EPS = 64e-5
N_MOD = 6
GRID_W = 64

LANES = 128
SUBLANES = 8
CHUNK = 64
WKV_SEQS = 4
LORA_COLS = 384
TM_IN = 512
TM_CONV = 256
TM_OUT = 256
TM_FIN = 512
MOE_BLOCK = 512
SC_CORES = 2
SC_SUBCORES = 16
SC_ROWS = 128
NEG_BIG = -1e30
EXP_M05 = math.exp(-0.5)
VMEM_LIMIT = 56 * 1024 * 1024


def _sigmoid(x):
    return 1.0 / (1.0 + jnp.exp(-x))


def _mm(a, b):
    return jnp.dot(a.astype(BF16), b.astype(BF16), preferred_element_type=F32)


def _mm_nt(a, b):
    return lax.dot_general(a.astype(BF16), b.astype(BF16), (((1,), (1,)), ((), ())),
                           preferred_element_type=F32)


def _mm_tn(a, b):
    return lax.dot_general(a.astype(BF16), b.astype(BF16), (((0,), (0,)), ((), ())),
                           preferred_element_type=F32)


def _split2(x):
    hi = x.astype(BF16)
    lo = (x - hi.astype(F32)).astype(BF16)
    return hi, lo


def _split3(x):
    h1 = x.astype(BF16)
    r1 = x - h1.astype(F32)
    h2 = r1.astype(BF16)
    h3 = (r1 - h2.astype(F32)).astype(BF16)
    return h1, h2, h3


def _pack_rows(x):
    n = x.shape[1] // 2
    bits = lambda v: lax.bitcast_convert_type(v.astype(BF16).astype(F32), jnp.uint32)
    word = (bits(x[:, n:]) & jnp.uint32(0xFFFF0000)) | (bits(x[:, :n]) >> 16)
    return lax.bitcast_convert_type(word, jnp.int32)


def _unpack_rows(w):
    u = lax.bitcast_convert_type(w, jnp.uint32)
    lo = lax.bitcast_convert_type(u << 16, F32)
    hi = lax.bitcast_convert_type(u & jnp.uint32(0xFFFF0000), F32)
    return jnp.concatenate([lo, hi], axis=1)


def _rms(x, g):
    return x * lax.rsqrt(jnp.mean(x * x, axis=-1, keepdims=True) + RMS_EPS) * g


def _params(*sem):
    return pltpu.CompilerParams(dimension_semantics=sem, vmem_limit_bytes=VMEM_LIMIT)


def _mod_kernel(c_ref, w_ref, b_ref, o_ref):
    c = c_ref[...]
    s1, s2 = _split2(c * _sigmoid(c))
    w1, w2 = _split2(w_ref[...])
    dot = functools.partial(jnp.dot, preferred_element_type=F32)
    o_ref[...] = dot(s1, w1) + (dot(s1, w2) + dot(s2, w1)) + b_ref[...]


def _mod_call(c_rows, ada_w, ada_b):
    m, d = c_rows.shape
    n = ada_w.shape[1]
    tn = 512
    return pl.pallas_call(
        _mod_kernel,
        grid=(n // tn,),
        in_specs=[pl.BlockSpec((m, d), lambda j: (0, 0)),
                  pl.BlockSpec((d, tn), lambda j: (0, j)),
                  pl.BlockSpec((1, tn), lambda j: (0, j))],
        out_specs=pl.BlockSpec((m, tn), lambda j: (0, j)),
        out_shape=jax.ShapeDtypeStruct((m, n), F32),
        compiler_params=_params("arbitrary"),
        name="mod",
    )(c_rows, ada_w, ada_b.reshape(1, n))


def _in_kernel(x_ref, mod_ref, g_ref, w_ref, cv_ref, rkv_ref, lo_ref):
    m = mod_ref[0]
    h = _rms(x_ref[...], g_ref[...]) * (1.0 + m[1:2]) + m[0:1]
    proj = jnp.dot(h.astype(BF16), w_ref[...], preferred_element_type=F32)
    cv_ref[...] = proj[:, :2 * CONV_CH]
    rkv_ref[...] = proj[:, 2 * CONV_CH:2 * CONV_CH + 3 * RWKV_DIM]
    lo_ref[...] = proj[:, 2 * CONV_CH + 3 * RWKV_DIM:]


def _mod_index(tile, seq_len, n_mod_rows):
    def index(i):
        return ((i * tile) // seq_len) % n_mod_rows, 0, 0
    return index


def _in_call(x, mod, g, w_all, seq_len):
    n, d = x.shape
    ncol = w_all.shape[1]
    return pl.pallas_call(
        _in_kernel,
        grid=(n // TM_IN,),
        in_specs=[pl.BlockSpec((TM_IN, d), lambda i: (i, 0)),
                  pl.BlockSpec((1, N_MOD, d), _mod_index(TM_IN, seq_len, mod.shape[0])),
                  pl.BlockSpec((1, d), lambda i: (0, 0)),
                  pl.BlockSpec((d, ncol), lambda i: (0, 0))],
        out_specs=[pl.BlockSpec((TM_IN, 2 * CONV_CH), lambda i: (i, 0)),
                   pl.BlockSpec((TM_IN, 3 * RWKV_DIM), lambda i: (i, 0)),
                   pl.BlockSpec((TM_IN, LORA_COLS), lambda i: (i, 0))],
        out_shape=[jax.ShapeDtypeStruct((n, 2 * CONV_CH), F32),
                   jax.ShapeDtypeStruct((n, 3 * RWKV_DIM), F32),
                   jax.ShapeDtypeStruct((n, LORA_COLS), F32)],
        compiler_params=_params("arbitrary"),
        name="in_proj",
    )(x, mod, g, w_all)


CONV_HALO = 16
CONV_ROWS = 32


def _conv_kernel(cv_ref, w_ref, b_ref, g_ref, be_ref, o_ref, pad_ref, acc_ref, *, seg_len):
    nseg = TM_CONV // seg_len
    stride = seg_len + 2 * CONV_HALO
    rows = nseg * stride
    cv = cv_ref[...]
    u = cv[:, :CONV_CH] * _sigmoid(cv[:, CONV_CH:])
    pad_ref[0] = jnp.zeros(pad_ref.shape[1:], F32)
    for s in range(nseg):
        pad_ref[0, s * stride + CONV_HALO:s * stride + CONV_HALO + seg_len, :] = (
            u[s * seg_len:(s + 1) * seg_len])
    for r in range(1, SUBLANES):
        pad_ref[r, 0:rows, :] = pad_ref[0, r:r + rows, :]
    first = CONV_HALO - CONV_WIDTH // 2
    chunks_per_seg = seg_len // CONV_ROWS

    def row_chunk(i, carry):
        pad0 = i * CONV_ROWS + (i // chunks_per_seg) * (2 * CONV_HALO)
        acc = jnp.zeros((CONV_ROWS, CONV_CH), F32)
        for t in range(CONV_WIDTH):
            q, r = divmod(first + t, SUBLANES)
            start = pl.multiple_of(pad0 + q * SUBLANES, SUBLANES)
            acc = acc + w_ref[t:t + 1, :] * pad_ref[r, pl.ds(start, CONV_ROWS), :]
        acc_ref[pl.ds(pl.multiple_of(i * CONV_ROWS, CONV_ROWS), CONV_ROWS), :] = acc
        return carry

    lax.fori_loop(0, TM_CONV // CONV_ROWS, row_chunk, 0)
    y = acc_ref[...] + b_ref[...]
    mu = jnp.mean(y, axis=-1, keepdims=True)
    yc = y - mu
    var = jnp.mean(yc * yc, axis=-1, keepdims=True)
    z = yc * lax.rsqrt(var + LN_EPS) * g_ref[...] + be_ref[...]
    o_ref[...] = z * _sigmoid(z)


def _conv_call(cv, conv_w, conv_b, ln_g, ln_b, seg_len):
    n = cv.shape[0]
    nseg = TM_CONV // seg_len
    row = lambda i: (i, 0)
    const = lambda i: (0, 0)
    return pl.pallas_call(
        functools.partial(_conv_kernel, seg_len=seg_len),
        grid=(n // TM_CONV,),
        in_specs=[pl.BlockSpec((TM_CONV, 2 * CONV_CH), row),
                  pl.BlockSpec((CONV_WIDTH, CONV_CH), const),
                  pl.BlockSpec((1, CONV_CH), const),
                  pl.BlockSpec((1, CONV_CH), const),
                  pl.BlockSpec((1, CONV_CH), const)],
        out_specs=pl.BlockSpec((TM_CONV, CONV_CH), row),
        out_shape=jax.ShapeDtypeStruct((n, CONV_CH), F32),
        scratch_shapes=[pltpu.VMEM((SUBLANES, nseg * (seg_len + 2 * CONV_HALO) + SUBLANES, CONV_CH),
                                   F32),
                        pltpu.VMEM((TM_CONV, CONV_CH), F32)],
        compiler_params=_params("arbitrary"),
        name="conv_module",
    )(cv, conv_w, conv_b, ln_g, ln_b)


def _head_sums(x, ones_bd):
    hi, lo = _split2(x)
    return jnp.dot(jnp.concatenate([hi, lo], axis=1), jnp.concatenate([ones_bd, ones_bd], axis=0),
                   preferred_element_type=F32)


def _wkv_masks():
    c = CHUNK
    r2 = lax.broadcasted_iota(jnp.int32, (2 * c, 2 * c), 0)
    c2 = lax.broadcasted_iota(jnp.int32, (2 * c, 2 * c), 1)
    same = (r2 // c) == (c2 // c)
    rr, cc = r2 % c, c2 % c
    r1 = lax.broadcasted_iota(jnp.int32, (c, 2 * c), 0)
    c1 = lax.broadcasted_iota(jnp.int32, (c, 2 * c), 1)
    cc1 = c1 % c
    rs = lax.broadcasted_iota(jnp.int32, (c, c), 0)
    cs = lax.broadcasted_iota(jnp.int32, (c, c), 1)
    masks = {"same": same, "eye": r2 == c2, "left": c1 < c,
             "ones_bd": same.astype(BF16)}
    for name, lt in (("f", lambda a, b: a < b), ("b", lambda a, b: a > b)):
        le = (lambda a, b, lt=lt: lt(a, b) | (a == b))
        masks["strict_" + name] = same & lt(cc, rr)
        masks["inc_" + name] = le(cc1, r1)
        masks["ak0_" + name] = (c1 >= c) & lt(cc1, r1)
        masks["ak1_" + name] = (c1 < c) & lt(cc1, r1)
        masks["tri_" + name] = le(cs, rs).astype(BF16)
    return masks


def _wkv_chains(chains, mk):
    c = CHUNK
    left = mk["left"]
    for ch in chains:
        zero = jnp.zeros_like(ch["at"])
        ch["at0"], ch["at1"] = jnp.where(left, ch["at"], zero), jnp.where(left, zero, ch["at"])
        rt0, rt1 = jnp.where(left, ch["rt"], zero), jnp.where(left, zero, ch["rt"])
        bt, kt = ch["bt"].astype(BF16), ch["kt"].astype(BF16)
        ch["g0"] = _mm_nt(jnp.concatenate([ch["at0"], rt0], axis=0), jnp.concatenate([bt, kt], axis=0))
        ch["g1"] = _mm_nt(jnp.concatenate([ch["at1"], rt1], axis=0), jnp.concatenate([kt, bt], axis=0))
    yield
    for ch in chains:
        g0, g1 = ch["g0"], ch["g1"]
        lmat = jnp.where(mk["strict_" + ch["d"]], jnp.concatenate([g0[:c], g1[:c]], axis=0), 0.0)
        ch["tinv"] = jnp.where(mk["eye"], 1.0, lmat)
        ch["lpow"] = _mm(lmat, lmat)
    for step in range(5):
        yield
        for ch in chains:
            lp = ch["lpow"].astype(BF16)
            if step < 4:
                prod = jnp.dot(lp, jnp.concatenate([lp, ch["tinv"].astype(BF16)], axis=1),
                               preferred_element_type=F32)
                ch["lpow"] = prod[:, :LANES]
                ch["tinv"] = ch["tinv"] + prod[:, LANES:]
            else:
                ch["tinv"] = ch["tinv"] + _mm(lp, ch["tinv"])
    yield
    for ch in chains:
        d, v = ch["d"], ch["v"]
        zv = jnp.zeros_like(v)
        ak0 = jnp.where(mk["ak0_" + d], ch["g0"][:c], 0.0)
        ak1 = jnp.where(mk["ak1_" + d], ch["g1"][:c], 0.0)
        ch["x0"] = _mm(ak0, jnp.concatenate([zv, v], axis=0))
        ch["x1"] = _mm(ak1, jnp.concatenate([v, zv], axis=0))
    yield
    for ch in chains:
        zero = jnp.zeros_like(ch["x0"])
        rhs = jnp.concatenate([
            jnp.concatenate([ch["at0"], jnp.where(left, ch["x0"], zero)], axis=1),
            jnp.concatenate([ch["at1"], jnp.where(left, zero, ch["x1"])], axis=1)], axis=0)
        ch["aw_blk"] = _mm(ch["tinv"], rhs)
    yield
    for ch in chains:
        d, v, aw_blk = ch["d"], ch["v"], ch["aw_blk"]
        zv = jnp.zeros_like(v)
        aw = aw_blk[:c] + aw_blk[c:]
        inc = mk["inc_" + d]
        g0, g1 = ch["g0"], ch["g1"]
        rb = jnp.where(inc, jnp.where(left, g0[c:], g1[c:]), 0.0)
        rk = jnp.where(inc, jnp.where(left, g1[c:], g0[c:]), 0.0)
        v0, v1 = jnp.where(left, v, zv), jnp.where(left, zv, v)
        vblk = jnp.concatenate([jnp.zeros((2 * c, LANES), F32),
                                jnp.concatenate([v1, v0], axis=0)], axis=1)
        qz = _mm(jnp.concatenate([rb, rk], axis=1), jnp.concatenate([aw_blk, vblk], axis=0))
        ch["q"] = ch["rt"] + qz[:, :LANES]
        ch["zz"] = qz[:, LANES:]
        rhs3 = jnp.concatenate([aw, jnp.concatenate([zv, v], axis=1)], axis=0)
        ch["mn"] = _mm_tn(jnp.concatenate([ch["bh"], ch["kh"]], axis=0), rhs3)
    yield
    same = mk["same"]
    for ch in chains:
        mn = ch["mn"]
        mt = jnp.where(same, mn[:, :LANES], 0.0) + jnp.where(mk["eye"], ch["ptot"], 0.0)
        nt = jnp.where(same, mn[:, LANES:], 0.0)
        both = _mm(jnp.concatenate([ch["q"], mt], axis=0), ch["st"])
        ch["y"] = both[:c] + ch["zz"]
        ch["st_new"] = both[c:] + nt


WKV_OPERANDS = ("at", "rt", "bt", "kt", "bh", "kh", "v")


def _wkv_prep(*args):
    for e in range(WKV_SEQS):
        yield from _wkv_prep_seq(e, *args)


def _wkv_prep_seq(e, rkv_refs, lo_refs, w2_ref, w0_ref, a2_ref, a0_ref, kk_ref, ka_ref, mk,
                  ops_ref, ptot_ref, slot):
    c = CHUNK
    for di, (d, rkv_ref, lo_ref) in enumerate(zip("fb", rkv_refs, lo_refs)):
        rkv = rkv_ref[e]
        lo = lo_ref[e]
        r = rkv[:, :RWKV_DIM]
        k = rkv[:, RWKV_DIM:2 * RWKV_DIM]
        v = rkv[:, 2 * RWKV_DIM:]
        w_raw = w0_ref[di:di + 1, :] + _mm(jnp.tanh(lo[:, :LANES]), w2_ref[di])
        lw = -EXP_M05 * _sigmoid(w_raw)
        yield
        ag = _sigmoid(a0_ref[di:di + 1, :] + _mm(lo[:, LANES:2 * LANES], a2_ref[di]))
        kd = k * (1.0 + (ag - 1.0) * ka_ref[...])
        kk_raw = k * kk_ref[...]
        h1, h2, h3 = _split3(lw)
        cs = jnp.dot(mk["tri_" + d], jnp.concatenate([h1, h2, h3], axis=1),
                     preferred_element_type=F32)
        logp = cs[:, :RWKV_DIM] + cs[:, RWKV_DIM:2 * RWKV_DIM] + cs[:, 2 * RWKV_DIM:]
        yield
        tot = logp[0:1] if d == "b" else logp[c - 1:c]
        e_inv = jnp.exp(-logp)
        e_hat = jnp.exp(tot - logp)
        rt = r * jnp.exp(logp)
        e_ex = jnp.exp(logp - lw)
        kt, kh = kd * e_inv, kd * e_hat
        ptot = jnp.exp(tot)
        for p in range(N_PAIRS):
            yield
            cols = slice(p * LANES, (p + 1) * LANES)
            kk = kk_raw[:, cols]
            kk = kk * lax.rsqrt(jnp.maximum(_head_sums(kk * kk, mk["ones_bd"]), 1e-24))
            kb = kk * ag[:, cols]
            tiles = {"at": -kk * e_ex[:, cols], "rt": rt[:, cols], "bt": kb * e_inv[:, cols],
                     "kt": kt[:, cols], "bh": kb * e_hat[:, cols], "kh": kh[:, cols],
                     "v": v[:, cols]}
            chain = (e * 2 + di) * N_PAIRS + p
            for j, name in enumerate(WKV_OPERANDS):
                ops_ref[slot, chain, j] = tiles[name]
            ptot_ref[slot, chain] = jnp.broadcast_to(ptot[:, cols], (SUBLANES, LANES))


def _wkv_kernel(rkv_f0_ref, lo_f0_ref, rkv_b0_ref, lo_b0_ref,
                rkv_f1_ref, lo_f1_ref, rkv_b1_ref, lo_b1_ref, s0f_ref, s0b_ref,
                w2_ref, w0_ref, a2_ref, a0_ref, kk_ref, ka_ref,
                yf_ref, yb_ref, sf_ref, sb_ref, st_ref, ops_ref, ptot_ref, *, nc):
    g = pl.program_id(0)
    ci = g % nc
    slot = g % 2
    mk = _wkv_masks()
    prm = (w2_ref, w0_ref, a2_ref, a0_ref, kk_ref, ka_ref, mk, ops_ref, ptot_ref)

    @pl.when(g == 0)
    def _():
        for _ in _wkv_prep((rkv_f0_ref, rkv_b0_ref), (lo_f0_ref, lo_b0_ref), *prm, 0):
            pass

    @pl.when(ci == 0)
    def _():
        for e in range(WKV_SEQS):
            st_ref[e, 0] = s0f_ref[e]
            st_ref[e, 1] = s0b_ref[e]

    chains = []
    for e in range(WKV_SEQS):
        for di, (d, y_ref) in enumerate((("f", yf_ref), ("b", yb_ref))):
            for p in range(N_PAIRS):
                chain = (e * 2 + di) * N_PAIRS + p
                ch = {"d": d, "e": e, "di": di, "p": p, "y_ref": y_ref, "st": st_ref[e, di, p],
                      "ptot": ptot_ref[slot, chain][0:1]}
                for j, name in enumerate(WKV_OPERANDS):
                    ch[name] = ops_ref[slot, chain, j]
                chains.append(ch)
    prep = _wkv_prep((rkv_f1_ref, rkv_b1_ref), (lo_f1_ref, lo_b1_ref), *prm, 1 - slot)
    for _ in _wkv_chains(chains, mk):
        for _ in range(WKV_SEQS):
            next(prep, None)
    for _ in prep:
        pass
    for ch in chains:
        ch["y_ref"][ch["e"], :, ch["p"] * LANES:(ch["p"] + 1) * LANES] = ch["y"]
        st_ref[ch["e"], ch["di"], ch["p"]] = ch["st_new"]

    @pl.when(ci == nc - 1)
    def _():
        for e in range(WKV_SEQS):
            for di, s_ref in enumerate((sf_ref, sb_ref)):
                for p in range(N_PAIRS):
                    st_t = st_ref[e, di, p].T
                    s_ref[e, 2 * p] = st_t[:HEAD_DIM, :HEAD_DIM]
                    s_ref[e, 2 * p + 1] = st_t[HEAD_DIM:, HEAD_DIM:]


def _wkv_call(rkv, lora, s0f, s0b, w2p, w0, a2p, a0, k_k, k_a, batch, seq_len):
    n = rkv.shape[0]
    nc = seq_len // CHUNK
    assert batch % WKV_SEQS == 0
    steps = (batch // WKV_SEQS) * nc
    rkv3 = rkv.reshape(batch, seq_len, 3 * RWKV_DIM)
    lora3 = lora.reshape(batch, seq_len, LORA_COLS)
    at = lambda g, back: (g // nc, (nc - 1 - g % nc) if back else g % nc, 0)
    fwd = lambda g: at(g, False)
    bwd = lambda g: at(g, True)
    fwd_first = lambda g: (0, 0, 0)
    bwd_first = lambda g: (0, nc - 1, 0)
    fwd_next = lambda g: at(jnp.minimum(g + 1, steps - 1), False)
    bwd_next = lambda g: at(jnp.minimum(g + 1, steps - 1), True)
    st = lambda g: (g // nc, 0, 0, 0)
    c2 = lambda g: (0, 0)
    c3 = lambda g: (0, 0, 0)
    st_shape = jax.ShapeDtypeStruct((batch, RWKV_HEADS, HEAD_DIM, HEAD_DIM), F32)
    y_shape = jax.ShapeDtypeStruct((batch, seq_len, RWKV_DIM), F32)
    n_chains = WKV_SEQS * 2 * N_PAIRS
    rkv_blk = (WKV_SEQS, CHUNK, 3 * RWKV_DIM)
    lora_blk = (WKV_SEQS, CHUNK, LORA_COLS)
    yf, yb, sf, sb = pl.pallas_call(
        functools.partial(_wkv_kernel, nc=nc),
        grid=(steps,),
        in_specs=[pl.BlockSpec(rkv_blk, fwd_first),
                  pl.BlockSpec(lora_blk, fwd_first),
                  pl.BlockSpec(rkv_blk, bwd_first),
                  pl.BlockSpec(lora_blk, bwd_first),
                  pl.BlockSpec(rkv_blk, fwd_next),
                  pl.BlockSpec(lora_blk, fwd_next),
                  pl.BlockSpec(rkv_blk, bwd_next),
                  pl.BlockSpec(lora_blk, bwd_next),
                  pl.BlockSpec((WKV_SEQS, N_PAIRS, LANES, LANES), st),
                  pl.BlockSpec((WKV_SEQS, N_PAIRS, LANES, LANES), st),
                  pl.BlockSpec((2, LANES, RWKV_DIM), c3),
                  pl.BlockSpec((2, RWKV_DIM), c2),
                  pl.BlockSpec((2, LANES, RWKV_DIM), c3),
                  pl.BlockSpec((2, RWKV_DIM), c2),
                  pl.BlockSpec((1, RWKV_DIM), c2),
                  pl.BlockSpec((1, RWKV_DIM), c2)],
        out_specs=[pl.BlockSpec((WKV_SEQS, CHUNK, RWKV_DIM), fwd),
                   pl.BlockSpec((WKV_SEQS, CHUNK, RWKV_DIM), bwd),
                   pl.BlockSpec((WKV_SEQS, RWKV_HEADS, HEAD_DIM, HEAD_DIM), st),
                   pl.BlockSpec((WKV_SEQS, RWKV_HEADS, HEAD_DIM, HEAD_DIM), st)],
        out_shape=[y_shape, y_shape, st_shape, st_shape],
        scratch_shapes=[pltpu.VMEM((WKV_SEQS, 2, N_PAIRS, LANES, LANES), F32),
                        pltpu.VMEM((2, n_chains, len(WKV_OPERANDS), CHUNK, LANES), F32),
                        pltpu.VMEM((2, n_chains, SUBLANES, LANES), F32)],
        compiler_params=_params("arbitrary"),
        name="wkv_chunked",
    )(rkv3, lora3, rkv3, lora3, rkv3, lora3, rkv3, lora3, s0f, s0b, w2p, w0, a2p, a0, k_k, k_a)
    return yf.reshape(n, RWKV_DIM), yb.reshape(n, RWKV_DIM), sf, sb


def _out_kernel(x_ref, mod_ref, conv_ref, yf_ref, yb_ref, rkv_ref, lo_ref,
                a2_ref, a0_ref, ka_ref, rk_ref, gng_ref, gnb_ref, g2_ref, wout_ref,
                postg_ref, preg_ref, rw_ref, rb_ref, x1_ref, h2_ref, route_ref, cnt_ref):
    @pl.when(pl.program_id(0) == 0)
    def _():
        cnt_ref[...] = jnp.zeros(cnt_ref.shape, F32)

    m = mod_ref[0]
    rkv = rkv_ref[...]
    lo = lo_ref[...]
    r = rkv[:, :RWKV_DIM]
    k = rkv[:, RWKV_DIM:2 * RWKV_DIM]
    v = rkv[:, 2 * RWKV_DIM:]
    ha = lo[:, LANES:2 * LANES]
    ag_f = _sigmoid(a0_ref[0:1, :] + _mm(ha, a2_ref[0]))
    ag_b = _sigmoid(a0_ref[1:2, :] + _mm(ha, a2_ref[1]))
    rkk = r * k * rk_ref[...] * (2.0 + (ag_f + ag_b - 2.0) * ka_ref[...])
    o = yf_ref[...] + yb_ref[...]
    gate = _mm(_sigmoid(lo[:, 2 * LANES:]), g2_ref[...])
    r2 = lax.broadcasted_iota(jnp.int32, (LANES, LANES), 0)
    c2 = lax.broadcasted_iota(jnp.int32, (LANES, LANES), 1)
    ones_bd = ((r2 // HEAD_DIM) == (c2 // HEAD_DIM)).astype(BF16)
    parts = []
    for p in range(N_PAIRS):
        cols = slice(p * LANES, (p + 1) * LANES)
        op = o[:, cols]
        mu = _head_sums(op, ones_bd) * (1.0 / HEAD_DIM)
        oc = op - mu
        var = _head_sums(oc * oc, ones_bd) * (1.0 / HEAD_DIM)
        on = oc * lax.rsqrt(var + GN_EPS) * gng_ref[:, cols] + gnb_ref[:, cols]
        bonus = _head_sums(rkk[:, cols], ones_bd) * v[:, cols]
        parts.append((on + bonus) * gate[:, cols])
    mix_in = jnp.concatenate([conv_ref[...]] + parts, axis=1)
    mix = jnp.dot(mix_in.astype(BF16), wout_ref[...], preferred_element_type=F32)
    x1 = x_ref[...] + m[2:3] * _rms(mix, postg_ref[...])
    x1_ref[...] = x1
    h2 = _rms(x1, preg_ref[...]) * (1.0 + m[4:5]) + m[3:4]
    h2_ref[...] = _pack_rows(h2)
    hh, hl = _split2(h2)
    dot = functools.partial(jnp.dot, preferred_element_type=F32)
    logits = dot(hh, rw_ref[0]) + (dot(hh, rw_ref[1]) + dot(hl, rw_ref[0])) + rb_ref[...]
    lane =lax.broadcasted_iota(jnp.int32, logits.shape, 1)
    work = logits
    picks, vals, idxs = [], [], []
    for _ in range(TOP_K):
        mx = jnp.max(work, axis=-1, keepdims=True)
        idx = jnp.min(jnp.where(work == mx, lane, LANES), axis=-1, keepdims=True)
        pick = lane == idx
        picks.append(pick)
        vals.append(mx)
        idxs.append(idx)
        work = jnp.where(pick, 2.0 * NEG_BIG, work)
    exps = [jnp.exp(val - vals[0]) for val in vals]
    den = exps[0] + exps[1] + exps[2] + exps[3]
    sel = jnp.zeros(logits.shape, F32)
    for pick in picks:
        sel = jnp.where(pick, 1.0, sel)
    tr = lax.broadcasted_iota(jnp.int32, (TM_OUT, TM_OUT), 0)
    tc = lax.broadcasted_iota(jnp.int32, (TM_OUT, TM_OUT), 1)
    before = (tc < tr).astype(BF16)
    rank = dot(before, sel.astype(BF16)) + cnt_ref[...]
    cnt_ref[...] = cnt_ref[...] + jnp.sum(sel, axis=0, keepdims=True)
    route = jnp.zeros(logits.shape, F32)
    for q, (pick, idx, e) in enumerate(zip(picks, idxs, exps)):
        rank_q = jnp.sum(jnp.where(pick, rank, 0.0), axis=-1, keepdims=True)
        route = jnp.where(lane == q, idx.astype(F32), route)
        route = jnp.where(lane == TOP_K + q, e / den, route)
        route = jnp.where(lane == 2 * TOP_K + q, rank_q, route)
    route_ref[...] = route


def _out_call(x, mod, conv_out, yf, yb, rkv, lora, wts, seq_len):
    n, d = x.shape
    row = lambda i: (i, 0)
    c2 = lambda i: (0, 0)
    c3 = lambda i: (0, 0, 0)
    full = lambda a: pl.BlockSpec(a.shape, c3 if a.ndim == 3 else c2)
    consts = [wts["a2p"], wts["a0"], wts["k_a"], wts["r_k"], wts["gn_g"], wts["gn_b"], wts["g2"],
              wts["w_out"], wts["post_mix_g"], wts["pre_ffn_g"], wts["router_w"], wts["router_b"]]
    return pl.pallas_call(
        _out_kernel,
        grid=(n // TM_OUT,),
        in_specs=[pl.BlockSpec((TM_OUT, d), row),
                  pl.BlockSpec((1, N_MOD, d), _mod_index(TM_OUT, seq_len, mod.shape[0])),
                  pl.BlockSpec((TM_OUT, CONV_CH), row),
                  pl.BlockSpec((TM_OUT, RWKV_DIM), row),
                  pl.BlockSpec((TM_OUT, RWKV_DIM), row),
                  pl.BlockSpec((TM_OUT, 3 * RWKV_DIM), row),
                  pl.BlockSpec((TM_OUT, LORA_COLS), row)] + [full(a) for a in consts],
        out_specs=[pl.BlockSpec((TM_OUT, d), row),
                   pl.BlockSpec((TM_OUT, d // 2), row),
                   pl.BlockSpec((TM_OUT, LANES), row),
                   pl.BlockSpec((1, LANES), c2)],
        out_shape=[jax.ShapeDtypeStruct((n, d), F32),
                   jax.ShapeDtypeStruct((n, d // 2), jnp.int32),
                   jax.ShapeDtypeStruct((n, LANES), F32),
                   jax.ShapeDtypeStruct((1, LANES), F32)],
        compiler_params=_params("arbitrary"),
        name="out_router",
    )(x, mod, conv_out, yf, yb, rkv, lora, *consts)


def _sc_mesh():
    return plsc.VectorSubcoreMesh(core_axis_name="c", subcore_axis_name="s",
                                  num_cores=SC_CORES, num_subcores=SC_SUBCORES)


def _sc_worker_id():
    return lax.axis_index("s") * SC_CORES + lax.axis_index("c")


def _sc_scatter_rows(src_a, src_b, idx, n_out):
    (n_a, d), n_b = src_a.shape, src_b.shape[0]
    n = n_a + n_b
    workers = SC_CORES * SC_SUBCORES
    per_worker = n // workers
    assert idx.shape[0] == TOP_K * n and per_worker * workers == n
    assert per_worker % SC_ROWS == 0 and n_a % per_worker == 0

    @functools.partial(
        pl.kernel, mesh=_sc_mesh(),
        out_type=jax.ShapeDtypeStruct((n_out, d), src_a.dtype),
        scratch_types=[pltpu.VMEM((SC_ROWS,), jnp.int32),
                       pltpu.VMEM((SC_ROWS, d), src_a.dtype),
                       pltpu.SemaphoreType.DMA],
        name="sc_scatter_rows")
    def scatter(a_hbm, b_hbm, idx_hbm, out_hbm, idx_v, rows_v, sem):
        base = _sc_worker_id() * per_worker

        @pl.loop(0, per_worker // SC_ROWS)
        def _(j):
            tok = base + j * SC_ROWS

            @pl.when(tok < n_a)
            def _():
                pltpu.sync_copy(a_hbm.at[pl.ds(tok, SC_ROWS)], rows_v)

            @pl.when(tok >= n_a)
            def _():
                pltpu.sync_copy(b_hbm.at[pl.ds(tok - n_a, SC_ROWS)], rows_v)

            for q in range(TOP_K):
                pltpu.sync_copy(idx_hbm.at[pl.ds(q * n + tok, SC_ROWS)], idx_v)
                pltpu.async_copy(rows_v, out_hbm.at[idx_v], sem).wait()

    return scatter(src_a, src_b, idx)


def _sc_gather_rows(src, idx):
    n_idx, d = idx.shape[0], src.shape[1]
    workers = SC_CORES * SC_SUBCORES
    per_worker = n_idx // workers
    assert per_worker * workers == n_idx and per_worker % SC_ROWS == 0

    @functools.partial(
        pl.kernel, mesh=_sc_mesh(),
        out_type=jax.ShapeDtypeStruct((n_idx, d), src.dtype),
        scratch_types=[pltpu.VMEM((SC_ROWS,), jnp.int32),
                       pltpu.VMEM((SC_ROWS, d), src.dtype),
                       pltpu.SemaphoreType.DMA],
        name="sc_gather_rows")
    def gather(src_hbm, idx_hbm, out_hbm, idx_v, rows_v, sem):
        base = _sc_worker_id() * per_worker

        @pl.loop(0, per_worker // SC_ROWS)
        def _(j):
            off = base + j * SC_ROWS
            pltpu.sync_copy(idx_hbm.at[pl.ds(off, SC_ROWS)], idx_v)
            pltpu.async_copy(src_hbm.at[idx_v], rows_v, sem).wait()
            pltpu.sync_copy(rows_v, out_hbm.at[pl.ds(off, SC_ROWS)])

    return gather(src, idx)


def _moe_kernel(blk_e_ref, blk_first_ref, blk_rows_ref, blk_next_ref,
                xs_ref, wgu_hbm, bgu_ref, wd_hbm, bd_ref,
                o_ref, wgu_f32_ref, wd_f32_ref, wgu_bf_ref, wd_bf_ref, sem):
    b = pl.program_id(0)
    n_rows = blk_rows_ref[b]

    def weight_copies(e):
        return (pltpu.make_async_copy(wgu_hbm.at[e], wgu_f32_ref, sem.at[0]),
                pltpu.make_async_copy(wd_hbm.at[e], wd_f32_ref, sem.at[1]))

    @pl.when(b == 0)
    def _():
        for copy in weight_copies(blk_e_ref[0]):
            copy.start()

    @pl.when(blk_first_ref[b] == 1)
    def _():
        for copy in weight_copies(blk_e_ref[b]):
            copy.wait()
        wgu_bf_ref[...] = wgu_f32_ref[...].astype(BF16)
        wd_bf_ref[...] = wd_f32_ref[...].astype(BF16)

        @pl.when(blk_next_ref[b] >= 0)
        def _():
            for copy in weight_copies(blk_next_ref[b]):
                copy.start()

    @pl.when(n_rows > 0)
    def _():
        live = lax.broadcasted_iota(jnp.int32, xs_ref.shape, 0) < n_rows
        xs = _unpack_rows(jnp.where(live, xs_ref[...], 0))
        gu = jnp.dot(xs.astype(BF16), wgu_bf_ref[...],
                     preferred_element_type=F32) + bgu_ref[0]
        x_glu = jnp.minimum(gu[:, :D_FF], SWIGLU_LIMIT)
        x_lin = jnp.clip(gu[:, D_FF:], -SWIGLU_LIMIT, SWIGLU_LIMIT)
        act = x_glu * _sigmoid(SWIGLU_ALPHA * x_glu) * (x_lin + 1.0)
        o_ref[...] = _pack_rows(jnp.dot(act.astype(BF16), wd_bf_ref[...],
                                        preferred_element_type=F32) + bd_ref[0])

    @pl.when(n_rows == 0)
    def _():
        o_ref[...] = jnp.zeros(o_ref.shape, jnp.int32)


def _moe_call(blk_e, blk_first, blk_rows, blk_next, xs, wgu, bgu, wd, bd):
    n_slots, half = xs.shape
    d = 2 * half
    ex = lambda b, be, bf, br, bn: (be[b], 0, 0)
    row = lambda b, be, bf, br, bn: (b, 0)
    grid_spec = pltpu.PrefetchScalarGridSpec(
        num_scalar_prefetch=4,
        grid=(n_slots // MOE_BLOCK,),
        in_specs=[pl.BlockSpec((MOE_BLOCK, half), row),
                  pl.BlockSpec(memory_space=pl.ANY),
                  pl.BlockSpec((1, 1, 2 * D_FF), ex),
                  pl.BlockSpec(memory_space=pl.ANY),
                  pl.BlockSpec((1, 1, d), ex)],
        out_specs=pl.BlockSpec((MOE_BLOCK, half), row),
        scratch_shapes=[pltpu.VMEM((d, 2 * D_FF), F32), pltpu.VMEM((D_FF, d), F32),
                        pltpu.VMEM((d, 2 * D_FF), BF16), pltpu.VMEM((D_FF, d), BF16),
                        pltpu.SemaphoreType.DMA((2,))])
    return pl.pallas_call(
        _moe_kernel,
        grid_spec=grid_spec,
        out_shape=jax.ShapeDtypeStruct((n_slots, half), jnp.int32),
        compiler_params=_params("arbitrary"),
        name="moe_experts",
    )(blk_e, blk_first, blk_rows, blk_next, xs, wgu, bgu, wd, bd)


def _final_kernel(x1_ref, y4_ref, route_ref, mod_ref, g_ref, o_ref):
    m = mod_ref[0]
    route = route_ref[...]
    y = jnp.zeros(x1_ref.shape, F32)
    for q in range(TOP_K):
        y = y + route[:, TOP_K + q:TOP_K + q + 1] * _unpack_rows(y4_ref[q])
    o_ref[...] = x1_ref[...] + m[5:6] * _rms(y, g_ref[...])


def _final_call(x1, y4, route, mod, g, seq_len, row_offset):
    n, d = x1.shape
    row = lambda i: (i, 0)
    shifted = lambda i: (0, i + row_offset // TM_FIN, 0)
    return pl.pallas_call(
        _final_kernel,
        grid=(n // TM_FIN,),
        in_specs=[pl.BlockSpec((TM_FIN, d), row),
                  pl.BlockSpec((TOP_K, TM_FIN, d // 2), shifted),
                  pl.BlockSpec((TM_FIN, LANES), row),
                  pl.BlockSpec((1, N_MOD, d), _mod_index(TM_FIN, seq_len, mod.shape[0])),
                  pl.BlockSpec((1, d), lambda i: (0, 0))],
        out_specs=pl.BlockSpec((TM_FIN, d), row),
        out_shape=jax.ShapeDtypeStruct((n, d), F32),
        compiler_params=_params("arbitrary"),
        name="final_residual",
    )(x1, y4, route, mod, g)


def _pack_state(s):
    b = s.shape[0]
    st = jnp.swapaxes(s.astype(F32), -1, -2).reshape(b, N_PAIRS, 2, HEAD_DIM, HEAD_DIM)
    z = jnp.zeros_like(st[:, :, 0])
    top = jnp.concatenate([st[:, :, 0], z], axis=-1)
    bot = jnp.concatenate([z, st[:, :, 1]], axis=-1)
    return jnp.concatenate([top, bot], axis=-2)


def _mix_sublayer(x, mod, s0f, s0b, seg_len, wts):
    batch, seq_len, d = x.shape
    x2 = x.reshape(batch * seq_len, d)
    cv, rkv, lora = _in_call(x2, mod, wts["pre_mix_g"], wts["w_all"], seq_len)
    conv_out = _conv_call(cv, wts["conv_w"], wts["conv_b"], wts["conv_ln_g"], wts["conv_ln_b"],
                          seg_len)
    yf, yb, sf, sb = _wkv_call(rkv, lora, s0f, s0b, wts["w2p"], wts["w0"], wts["a2p"], wts["a0"],
                               wts["k_k"], wts["k_a"], batch, seq_len)
    x1, h2, route, cnt = _out_call(x2, mod, conv_out, yf, yb, rkv, lora, wts, seq_len)
    return x1, h2, route, cnt, sf, sb


def _dispatch_plan(routes, cnts):
    experts = jnp.arange(N_EXPERTS, dtype=jnp.int32)
    counts = [c[0, :N_EXPERTS].astype(jnp.int32) for c in cnts]
    total = sum(counts)
    padded = (total + MOE_BLOCK - 1) // MOE_BLOCK * MOE_BLOCK
    pad_end = jnp.cumsum(padded)
    pad_start = pad_end - padded
    dests = []
    prior = jnp.zeros_like(total)
    for route, count in zip(routes, counts):
        idx = route[:, :TOP_K].astype(jnp.int32)
        rank = route[:, 2 * TOP_K:3 * TOP_K].astype(jnp.int32)
        first = jnp.sum(jnp.where(idx[..., None] == experts, pad_start + prior, 0), axis=-1)
        dests.append(first + rank)
        prior = prior + count
    dest = jnp.concatenate(dests, axis=0).T.reshape(-1)
    n_slots = dest.shape[0] + N_EXPERTS * MOE_BLOCK
    blk_row0 = jnp.arange(n_slots // MOE_BLOCK, dtype=jnp.int32) * MOE_BLOCK
    blk_e = jnp.sum((pad_end[None, :] <= blk_row0[:, None]).astype(jnp.int32), axis=-1)
    blk_e = jnp.minimum(blk_e, N_EXPERTS - 1)
    blk_first = jnp.concatenate([jnp.ones((1,), jnp.int32),
                                 (blk_e[1:] != blk_e[:-1]).astype(jnp.int32)])
    of_blk = lambda per_expert: jnp.sum(
        jnp.where(blk_e[:, None] == experts[None, :], per_expert[None, :], 0), axis=-1)
    blk_rows = jnp.clip(of_blk(pad_start + total) - blk_row0, 0, MOE_BLOCK).astype(jnp.int32)
    pos = jnp.arange(blk_e.shape[0], dtype=jnp.int32)
    later_first = (pos[None, :] > pos[:, None]) & (blk_first[None, :] == 1)
    next_pos = jnp.min(jnp.where(later_first, pos[None, :], pos.shape[0]), axis=-1)
    blk_next = jnp.where(next_pos < pos.shape[0],
                         blk_e[jnp.minimum(next_pos, pos.shape[0] - 1)], -1).astype(jnp.int32)
    return dest, n_slots, blk_e, blk_first, blk_rows, blk_next


def _routed_ffn(h2s, routes, cnts, wts):
    dest, n_slots, blk_e, blk_first, blk_rows, blk_next = _dispatch_plan(routes, cnts)
    xs = _sc_scatter_rows(h2s[0], h2s[1], dest, n_slots)
    out_sorted = _moe_call(blk_e, blk_first, blk_rows, blk_next, xs, wts["w_gu"], wts["b_gu"],
                           wts["w_down"], wts["b_down"])
    y4 = _sc_gather_rows(out_sorted, dest)
    return y4.reshape(TOP_K, dest.shape[0] // TOP_K, xs.shape[1])


def _prep_weights(l, pre_mix_g, post_mix_g, pre_ffn_g, post_ffn_g, w_in, w_out, conv_w, conv_b,
                  conv_ln_g, conv_ln_b, rw_w0, rw_w1, rw_w2, rw_a0, rw_a1, rw_a2, rw_g1, rw_g2,
                  rw_k_k, rw_k_a, rw_r_k, rw_gn_g, rw_gn_b, router_w, router_b, w_gu, b_gu,
                  w_down, b_down):
    row = lambda a: a.reshape(1, -1).astype(F32)
    zpad = jnp.zeros((HEAD_DIM, RWKV_DIM), F32)
    w_all = jnp.concatenate([w_in[l], rw_w1[l, 0], rw_w1[l, 1], rw_a1[l, 0], rw_a1[l, 1],
                             rw_g1[l]], axis=1).astype(BF16)
    w2p = jnp.stack([jnp.concatenate([rw_w2[l, 0], zpad], axis=0),
                     jnp.concatenate([zpad, rw_w2[l, 1]], axis=0)]).astype(BF16)
    a2p = jnp.stack([jnp.concatenate([rw_a2[l, 0], zpad], axis=0),
                     jnp.concatenate([zpad, rw_a2[l, 1]], axis=0)]).astype(BF16)
    rw_pad = jnp.pad(router_w[l].astype(F32), ((0, 0), (0, LANES - N_EXPERTS)))
    rw_hi = rw_pad.astype(BF16)
    rw_lo = (rw_pad - rw_hi.astype(F32)).astype(BF16)
    rb_pad = jnp.concatenate([router_b[l].astype(F32),
                              jnp.full((LANES - N_EXPERTS,), NEG_BIG, F32)]).reshape(1, LANES)
    return {
        "pre_mix_g": row(pre_mix_g[l]), "post_mix_g": row(post_mix_g[l]),
        "pre_ffn_g": row(pre_ffn_g[l]), "post_ffn_g": row(post_ffn_g[l]),
        "w_all": w_all, "w_out": w_out[l].astype(BF16),
        "conv_w": conv_w[l].astype(F32), "conv_b": row(conv_b[l]),
        "conv_ln_g": row(conv_ln_g[l]), "conv_ln_b": row(conv_ln_b[l]),
        "w2p": w2p, "w0": rw_w0[l].astype(F32), "a2p": a2p, "a0": rw_a0[l].astype(F32),
        "k_k": row(rw_k_k[l]), "k_a": row(rw_k_a[l]), "r_k": row(rw_r_k[l]),
        "gn_g": row(rw_gn_g[l]), "gn_b": row(rw_gn_b[l]), "g2": rw_g2[l].astype(BF16),
        "router_w": jnp.stack([rw_hi, rw_lo]), "router_b": rb_pad,
        "w_gu": w_gu[l].astype(F32), "b_gu": b_gu[l].reshape(N_EXPERTS, 1, -1).astype(F32),
        "w_down": w_down[l].astype(F32), "b_down": b_down[l].reshape(N_EXPERTS, 1, -1).astype(F32),
    }


def kernel(x_prompt, x_sample, state_wkv_fwd, state_wkv_bwd, c, c_ctx, ada_w, ada_b, pre_mix_g, post_mix_g, pre_ffn_g, post_ffn_g, w_in, w_out, conv_w, conv_b, conv_ln_g, conv_ln_b, rw_w0, rw_w1, rw_w2, rw_a0, rw_a1, rw_a2, rw_g1, rw_g2, rw_k_k, rw_k_a, rw_r_k, rw_gn_g, rw_gn_b, router_w, router_b, w_gu, b_gu, w_down, b_down):
    depth = ada_w.shape[0]
    dec_batch = c.shape[0]
    mod_rows = 16
    c_rows = jnp.concatenate([c, c_ctx[None, :],
                              jnp.zeros((mod_rows - dec_batch - 1, D_MODEL), F32)], axis=0)
    zero_state = jnp.zeros((x_prompt.shape[0], N_PAIRS, LANES, LANES), F32)
    y_prompt, y_sample = x_prompt, x_sample
    new_f, new_b = [], []
    for l in range(depth):
        wts = _prep_weights(l, pre_mix_g, post_mix_g, pre_ffn_g, post_ffn_g, w_in, w_out, conv_w,
                            conv_b, conv_ln_g, conv_ln_b, rw_w0, rw_w1, rw_w2, rw_a0, rw_a1, rw_a2,
                            rw_g1, rw_g2, rw_k_k, rw_k_a, rw_r_k, rw_gn_g, rw_gn_b, router_w,
                            router_b, w_gu, b_gu, w_down, b_down)
        mod = _mod_call(c_rows, ada_w[l], ada_b[l]).reshape(mod_rows, N_MOD, D_MODEL)
        mod_lat = mod[:dec_batch]
        mod_ctx = mod[dec_batch:dec_batch + 1]
        ctx = _mix_sublayer(y_prompt, mod_ctx, zero_state, zero_state, y_prompt.shape[1], wts)
        lat = _mix_sublayer(y_sample, mod_lat, _pack_state(state_wkv_fwd[:, l]),
                            _pack_state(state_wkv_bwd[:, l]), GRID_W, wts)
        new_f.append(ctx[4])
        new_b.append(ctx[5])
        y4 = _routed_ffn([ctx[1], lat[1]], [ctx[2], lat[2]], [ctx[3], lat[3]], wts)
        n_ctx = ctx[0].shape[0]
        y_prompt = _final_call(ctx[0], y4, ctx[2], mod_ctx, wts["post_ffn_g"], y_prompt.shape[1],
                               0).reshape(y_prompt.shape)
        y_sample = _final_call(lat[0], y4, lat[2], mod_lat, wts["post_ffn_g"], y_sample.shape[1],
                               n_ctx).reshape(y_sample.shape)
    return (y_prompt, y_sample, jnp.stack(new_f, axis=1), jnp.stack(new_b, axis=1))
```

```python
import functools
import math

import jax
import jax.numpy as jnp
from jax import lax
from jax.experimental import pallas as pl
from jax.experimental.pallas import tpu as pltpu
from jax.experimental.pallas import tpu_sc as plsc

F32 = jnp.float32
BF16 = jnp.bfloat16

D_MODEL = 1024
CONV_CH = 512
RWKV_DIM = 512
HEAD_DIM = 64
RWKV_HEADS = 8
N_PAIRS = RWKV_HEADS // 2
CONV_WIDTH = 31
N_EXPERTS = 32
TOP_K = 4
D_FF = 1024
SWIGLU_LIMIT = 7.0
SWIGLU_ALPHA = 1.702
RMS_EPS = 1e-6
LN_EPS = 1e-5
GN_EPS = 64e-5
N_MOD = 6
GRID_W = 64

LANES = 128
SUBLANES = 8
CHUNK = 64
WKV_SEQS = 4
LORA_COLS = 384
TM_IN = 1024
TM_CONV = 256
TM_OUT = 512
TM_FIN = 512
MOE_BLOCK = 512
SC_CORES = 2
SC_SUBCORES = 16
SC_ROWS = 128
NEG_BIG = -1e30
EXP_M05 = math.exp(-0.5)
VMEM_LIMIT = 56 * 1024 * 1024


def _sigmoid(x):
    return 1.0 / (1.0 + jnp.exp(-x))


def _mm(a, b):
    return jnp.dot(a.astype(BF16), b.astype(BF16), preferred_element_type=F32)


def _mm_nt(a, b):
    return lax.dot_general(a.astype(BF16), b.astype(BF16), (((1,), (1,)), ((), ())),
                           preferred_element_type=F32)


def _mm_tn(a, b):
    return lax.dot_general(a.astype(BF16), b.astype(BF16), (((0,), (0,)), ((), ())),
                           preferred_element_type=F32)


def _split2(x):
    hi = x.astype(BF16)
    lo = (x - hi.astype(F32)).astype(BF16)
    return hi, lo


def _split3(x):
    h1 = x.astype(BF16)
    r1 = x - h1.astype(F32)
    h2 = r1.astype(BF16)
    h3 = (r1 - h2.astype(F32)).astype(BF16)
    return h1, h2, h3


def _pack_rows(x):
    n = x.shape[1] // 2
    bits = lambda v: lax.bitcast_convert_type(v.astype(BF16).astype(F32), jnp.uint32)
    word = (bits(x[:, n:]) & jnp.uint32(0xFFFF0000)) | (bits(x[:, :n]) >> 16)
    return lax.bitcast_convert_type(word, jnp.int32)


def _unpack_rows(w):
    u = lax.bitcast_convert_type(w, jnp.uint32)
    lo = lax.bitcast_convert_type(u << 16, F32)
    hi = lax.bitcast_convert_type(u & jnp.uint32(0xFFFF0000), F32)
    return jnp.concatenate([lo, hi], axis=1)


def _rms(x, g):
    return x * lax.rsqrt(jnp.mean(x * x, axis=-1, keepdims=True) + RMS_EPS) * g


def _params(*sem):
    return pltpu.CompilerParams(dimension_semantics=sem, vmem_limit_bytes=VMEM_LIMIT)


def _mod_kernel(c_ref, w_ref, b_ref, o_ref):
    c = c_ref[...]
    s1, s2 = _split2(c * _sigmoid(c))
    w1, w2 = _split2(w_ref[...])
    dot = functools.partial(jnp.dot, preferred_element_type=F32)
    o_ref[...] = dot(s1, w1) + (dot(s1, w2) + dot(s2, w1)) + b_ref[...]


def _mod_call(c_rows, ada_w, ada_b):
    m, d = c_rows.shape
    n = ada_w.shape[1]
    tn = 512
    return pl.pallas_call(
        _mod_kernel,
        grid=(n // tn,),
        in_specs=[pl.BlockSpec((m, d), lambda j: (0, 0)),
                  pl.BlockSpec((d, tn), lambda j: (0, j)),
                  pl.BlockSpec((1, tn), lambda j: (0, j))],
        out_specs=pl.BlockSpec((m, tn), lambda j: (0, j)),
        out_shape=jax.ShapeDtypeStruct((m, n), F32),
        compiler_params=_params("arbitrary"),
        name="mod",
    )(c_rows, ada_w, ada_b.reshape(1, n))


def _in_kernel(x_ref, mod_ref, g_ref, w_ref, cv_ref, rkv_ref, lo_ref):
    m = mod_ref[0]
    h = _rms(x_ref[...], g_ref[...]) * (1.0 + m[1:2]) + m[0:1]
    proj = jnp.dot(h.astype(BF16), w_ref[...], preferred_element_type=F32)
    cv_ref[...] = proj[:, :2 * CONV_CH]
    rkv_ref[...] = proj[:, 2 * CONV_CH:2 * CONV_CH + 3 * RWKV_DIM]
    lo_ref[...] = proj[:, 2 * CONV_CH + 3 * RWKV_DIM:]


def _mod_index(tile, seq_len, n_mod_rows):
    def index(i):
        return ((i * tile) // seq_len) % n_mod_rows, 0, 0
    return index


def _in_call(x, mod, g, w_all, seq_len):
    n, d = x.shape
    ncol = w_all.shape[1]
    return pl.pallas_call(
        _in_kernel,
        grid=(n // TM_IN,),
        in_specs=[pl.BlockSpec((TM_IN, d), lambda i: (i, 0)),
                  pl.BlockSpec((1, N_MOD, d), _mod_index(TM_IN, seq_len, mod.shape[0])),
                  pl.BlockSpec((1, d), lambda i: (0, 0)),
                  pl.BlockSpec((d, ncol), lambda i: (0, 0))],
        out_specs=[pl.BlockSpec((TM_IN, 2 * CONV_CH), lambda i: (i, 0)),
                   pl.BlockSpec((TM_IN, 3 * RWKV_DIM), lambda i: (i, 0)),
                   pl.BlockSpec((TM_IN, LORA_COLS), lambda i: (i, 0))],
        out_shape=[jax.ShapeDtypeStruct((n, 2 * CONV_CH), F32),
                   jax.ShapeDtypeStruct((n, 3 * RWKV_DIM), F32),
                   jax.ShapeDtypeStruct((n, LORA_COLS), F32)],
        compiler_params=_params("arbitrary"),
        name="in_proj",
    )(x, mod, g, w_all)


CONV_HALO = 16
CONV_ROWS = 32


def _conv_kernel(cv_ref, w_ref, b_ref, g_ref, be_ref, o_ref, pad_ref, acc_ref, *, seg_len):
    nseg = TM_CONV // seg_len
    stride = seg_len + 2 * CONV_HALO
    rows = nseg * stride
    cv = cv_ref[...]
    u = cv[:, :CONV_CH] * _sigmoid(cv[:, CONV_CH:])
    pad_ref[0] = jnp.zeros(pad_ref.shape[1:], F32)
    for s in range(nseg):
        pad_ref[0, s * stride + CONV_HALO:s * stride + CONV_HALO + seg_len, :] = (
            u[s * seg_len:(s + 1) * seg_len])
    for r in range(1, SUBLANES):
        pad_ref[r, 0:rows, :] = pad_ref[0, r:r + rows, :]
    first = CONV_HALO - CONV_WIDTH // 2
    chunks_per_seg = seg_len // CONV_ROWS

    def row_chunk(i, carry):
        pad0 = i * CONV_ROWS + (i // chunks_per_seg) * (2 * CONV_HALO)
        acc = jnp.zeros((CONV_ROWS, CONV_CH), F32)
        for t in range(CONV_WIDTH):
            q, r = divmod(first + t, SUBLANES)
            start = pl.multiple_of(pad0 + q * SUBLANES, SUBLANES)
            acc = acc + w_ref[t:t + 1, :] * pad_ref[r, pl.ds(start, CONV_ROWS), :]
        acc_ref[pl.ds(pl.multiple_of(i * CONV_ROWS, CONV_ROWS), CONV_ROWS), :] = acc
        return carry

    lax.fori_loop(0, TM_CONV // CONV_ROWS, row_chunk, 0)
    y = acc_ref[...] + b_ref[...]
    mu = jnp.mean(y, axis=-1, keepdims=True)
    yc = y - mu
    var = jnp.mean(yc * yc, axis=-1, keepdims=True)
    z = yc * lax.rsqrt(var + LN_EPS) * g_ref[...] + be_ref[...]
    o_ref[...] = z * _sigmoid(z)


def _conv_call(cv, conv_w, conv_b, ln_g, ln_b, seg_len):
    n = cv.shape[0]
    nseg = TM_CONV // seg_len
    row = lambda i: (i, 0)
    const = lambda i: (0, 0)
    return pl.pallas_call(
        functools.partial(_conv_kernel, seg_len=seg_len),
        grid=(n // TM_CONV,),
        in_specs=[pl.BlockSpec((TM_CONV, 2 * CONV_CH), row),
                  pl.BlockSpec((CONV_WIDTH, CONV_CH), const),
                  pl.BlockSpec((1, CONV_CH), const),
                  pl.BlockSpec((1, CONV_CH), const),
                  pl.BlockSpec((1, CONV_CH), const)],
        out_specs=pl.BlockSpec((TM_CONV, CONV_CH), row),
        out_shape=jax.ShapeDtypeStruct((n, CONV_CH), F32),
        scratch_shapes=[pltpu.VMEM((SUBLANES, nseg * (seg_len + 2 * CONV_HALO) + SUBLANES, CONV_CH),
                                   F32),
                        pltpu.VMEM((TM_CONV, CONV_CH), F32)],
        compiler_params=_params("arbitrary"),
        name="conv_module",
    )(cv, conv_w, conv_b, ln_g, ln_b)


def _head_sums(x, ones_bd):
    hi, lo = _split2(x)
    return jnp.dot(jnp.concatenate([hi, lo], axis=1), jnp.concatenate([ones_bd, ones_bd], axis=0),
                   preferred_element_type=F32)


def _wkv_masks():
    c = CHUNK
    r2 = lax.broadcasted_iota(jnp.int32, (2 * c, 2 * c), 0)
    c2 = lax.broadcasted_iota(jnp.int32, (2 * c, 2 * c), 1)
    same = (r2 // c) == (c2 // c)
    rr, cc = r2 % c, c2 % c
    r1 = lax.broadcasted_iota(jnp.int32, (c, 2 * c), 0)
    c1 = lax.broadcasted_iota(jnp.int32, (c, 2 * c), 1)
    cc1 = c1 % c
    rs = lax.broadcasted_iota(jnp.int32, (c, c), 0)
    cs = lax.broadcasted_iota(jnp.int32, (c, c), 1)
    masks = {"same": same, "eye": r2 == c2, "left": c1 < c,
             "ones_bd": same.astype(BF16)}
    for name, lt in (("f", lambda a, b: a < b), ("b", lambda a, b: a > b)):
        le = (lambda a, b, lt=lt: lt(a, b) | (a == b))
        masks["strict_" + name] = same & lt(cc, rr)
        masks["inc_" + name] = le(cc1, r1)
        masks["ak0_" + name] = (c1 >= c) & lt(cc1, r1)
        masks["ak1_" + name] = (c1 < c) & lt(cc1, r1)
        masks["tri_" + name] = le(cs, rs).astype(BF16)
    return masks


def _wkv_chains(chains, mk):
    c = CHUNK
    left = mk["left"]
    for ch in chains:
        zero = jnp.zeros_like(ch["at"])
        ch["at0"], ch["at1"] = jnp.where(left, ch["at"], zero), jnp.where(left, zero, ch["at"])
        rt0, rt1 = jnp.where(left, ch["rt"], zero), jnp.where(left, zero, ch["rt"])
        bt, kt = ch["bt"].astype(BF16), ch["kt"].astype(BF16)
        ch["g0"] = _mm_nt(jnp.concatenate([ch["at0"], rt0], axis=0), jnp.concatenate([bt, kt], axis=0))
        ch["g1"] = _mm_nt(jnp.concatenate([ch["at1"], rt1], axis=0), jnp.concatenate([kt, bt], axis=0))
    yield
    for ch in chains:
        g0, g1 = ch["g0"], ch["g1"]
        lmat = jnp.where(mk["strict_" + ch["d"]], jnp.concatenate([g0[:c], g1[:c]], axis=0), 0.0)
        ch["tinv"] = jnp.where(mk["eye"], 1.0, lmat)
        ch["lpow"] = _mm(lmat, lmat)
    for step in range(5):
        yield
        for ch in chains:
            lp = ch["lpow"].astype(BF16)
            if step < 4:
                prod = jnp.dot(lp, jnp.concatenate([lp, ch["tinv"].astype(BF16)], axis=1),
                               preferred_element_type=F32)
                ch["lpow"] = prod[:, :LANES]
                ch["tinv"] = ch["tinv"] + prod[:, LANES:]
            else:
                ch["tinv"] = ch["tinv"] + _mm(lp, ch["tinv"])
    yield
    for ch in chains:
        d, v = ch["d"], ch["v"]
        zv = jnp.zeros_like(v)
        ak0 = jnp.where(mk["ak0_" + d], ch["g0"][:c], 0.0)
        ak1 = jnp.where(mk["ak1_" + d], ch["g1"][:c], 0.0)
        ch["x0"] = _mm(ak0, jnp.concatenate([zv, v], axis=0))
        ch["x1"] = _mm(ak1, jnp.concatenate([v, zv], axis=0))
    yield
    for ch in chains:
        zero = jnp.zeros_like(ch["x0"])
        rhs = jnp.concatenate([
            jnp.concatenate([ch["at0"], jnp.where(left, ch["x0"], zero)], axis=1),
            jnp.concatenate([ch["at1"], jnp.where(left, zero, ch["x1"])], axis=1)], axis=0)
        ch["aw_blk"] = _mm(ch["tinv"], rhs)
    yield
    for ch in chains:
        d, v, aw_blk = ch["d"], ch["v"], ch["aw_blk"]
        zv = jnp.zeros_like(v)
        aw = aw_blk[:c] + aw_blk[c:]
        inc = mk["inc_" + d]
        g0, g1 = ch["g0"], ch["g1"]
        rb = jnp.where(inc, jnp.where(left, g0[c:], g1[c:]), 0.0)
        rk = jnp.where(inc, jnp.where(left, g1[c:], g0[c:]), 0.0)
        v0, v1 = jnp.where(left, v, zv), jnp.where(left, zv, v)
        vblk = jnp.concatenate([jnp.zeros((2 * c, LANES), F32),
                                jnp.concatenate([v1, v0], axis=0)], axis=1)
        qz = _mm(jnp.concatenate([rb, rk], axis=1), jnp.concatenate([aw_blk, vblk], axis=0))
        ch["q"] = ch["rt"] + qz[:, :LANES]
        ch["zz"] = qz[:, LANES:]
        rhs3 = jnp.concatenate([aw, jnp.concatenate([zv, v], axis=1)], axis=0)
        ch["mn"] = _mm_tn(jnp.concatenate([ch["bh"], ch["kh"]], axis=0), rhs3)
    yield
    same = mk["same"]
    for ch in chains:
        mn = ch["mn"]
        mt = jnp.where(same, mn[:, :LANES], 0.0) + jnp.where(mk["eye"], ch["ptot"], 0.0)
        nt = jnp.where(same, mn[:, LANES:], 0.0)
        both = _mm(jnp.concatenate([ch["q"], mt], axis=0), ch["st"])
        ch["y"] = both[:c] + ch["zz"]
        ch["st_new"] = both[c:] + nt


WKV_OPERANDS = ("at", "rt", "bt", "kt", "bh", "kh", "v")


def _wkv_prep(*args):
    for e in range(WKV_SEQS):
        yield from _wkv_prep_seq(e, *args)


def _wkv_prep_seq(e, rkv_refs, lo_refs, w2_ref, w0_ref, a2_ref, a0_ref, kk_ref, ka_ref, mk,
                  ops_ref, ptot_ref, slot):
    c = CHUNK
    for di, (d, rkv_ref, lo_ref) in enumerate(zip("fb", rkv_refs, lo_refs)):
        rkv = rkv_ref[e]
        lo = lo_ref[e]
        r = rkv[:, :RWKV_DIM]
        k = rkv[:, RWKV_DIM:2 * RWKV_DIM]
        v = rkv[:, 2 * RWKV_DIM:]
        w_raw = w0_ref[di:di + 1, :] + _mm(jnp.tanh(lo[:, :LANES]), w2_ref[di])
        lw = -EXP_M05 * _sigmoid(w_raw)
        yield
        ag = _sigmoid(a0_ref[di:di + 1, :] + _mm(lo[:, LANES:2 * LANES], a2_ref[di]))
        kd = k * (1.0 + (ag - 1.0) * ka_ref[...])
        kk_raw = k * kk_ref[...]
        h1, h2, h3 = _split3(lw)
        cs = jnp.dot(mk["tri_" + d], jnp.concatenate([h1, h2, h3], axis=1),
                     preferred_element_type=F32)
        logp = cs[:, :RWKV_DIM] + cs[:, RWKV_DIM:2 * RWKV_DIM] + cs[:, 2 * RWKV_DIM:]
        yield
        tot = logp[0:1] if d == "b" else logp[c - 1:c]
        e_inv = jnp.exp(-logp)
        e_hat = jnp.exp(tot - logp)
        rt = r * jnp.exp(logp)
        e_ex = jnp.exp(logp - lw)
        kt, kh = kd * e_inv, kd * e_hat
        ptot = jnp.exp(tot)
        for p in range(N_PAIRS):
            yield
            cols = slice(p * LANES, (p + 1) * LANES)
            kk = kk_raw[:, cols]
            kk = kk * lax.rsqrt(jnp.maximum(_head_sums(kk * kk, mk["ones_bd"]), 1e-24))
            kb = kk * ag[:, cols]
            tiles = {"at": -kk * e_ex[:, cols], "rt": rt[:, cols], "bt": kb * e_inv[:, cols],
                     "kt": kt[:, cols], "bh": kb * e_hat[:, cols], "kh": kh[:, cols],
                     "v": v[:, cols]}
            chain = (e * 2 + di) * N_PAIRS + p
            for j, name in enumerate(WKV_OPERANDS):
                ops_ref[slot, chain, j] = tiles[name]
            ptot_ref[slot, chain] = jnp.broadcast_to(ptot[:, cols], (SUBLANES, LANES))


def _wkv_kernel(rkv_f0_ref, lo_f0_ref, rkv_b0_ref, lo_b0_ref,
                rkv_f1_ref, lo_f1_ref, rkv_b1_ref, lo_b1_ref, s0f_ref, s0b_ref,
                w2_ref, w0_ref, a2_ref, a0_ref, kk_ref, ka_ref,
                yf_ref, yb_ref, sf_ref, sb_ref, st_ref, ops_ref, ptot_ref, *, nc):
    g = pl.program_id(0)
    ci = g % nc
    slot = g % 2
    mk = _wkv_masks()
    prm = (w2_ref, w0_ref, a2_ref, a0_ref, kk_ref, ka_ref, mk, ops_ref, ptot_ref)

    @pl.when(g == 0)
    def _():
        for _ in _wkv_prep((rkv_f0_ref, rkv_b0_ref), (lo_f0_ref, lo_b0_ref), *prm, 0):
            pass

    @pl.when(ci == 0)
    def _():
        for e in range(WKV_SEQS):
            st_ref[e, 0] = s0f_ref[e]
            st_ref[e, 1] = s0b_ref[e]

    chains = []
    for e in range(WKV_SEQS):
        for di, (d, y_ref) in enumerate((("f", yf_ref), ("b", yb_ref))):
            for p in range(N_PAIRS):
                chain = (e * 2 + di) * N_PAIRS + p
                ch = {"d": d, "e": e, "di": di, "p": p, "y_ref": y_ref, "st": st_ref[e, di, p],
                      "ptot": ptot_ref[slot, chain][0:1]}
                for j, name in enumerate(WKV_OPERANDS):
                    ch[name] = ops_ref[slot, chain, j]
                chains.append(ch)
    prep = _wkv_prep((rkv_f1_ref, rkv_b1_ref), (lo_f1_ref, lo_b1_ref), *prm, 1 - slot)
    for _ in _wkv_chains(chains, mk):
        for _ in range(WKV_SEQS):
            next(prep, None)
    for _ in prep:
        pass
    for ch in chains:
        ch["y_ref"][ch["e"], :, ch["p"] * LANES:(ch["p"] + 1) * LANES] = ch["y"]
        st_ref[ch["e"], ch["di"], ch["p"]] = ch["st_new"]

    @pl.when(ci == nc - 1)
    def _():
        for e in range(WKV_SEQS):
            for di, s_ref in enumerate((sf_ref, sb_ref)):
                for p in range(N_PAIRS):
                    st_t = st_ref[e, di, p].T
                    s_ref[e, 2 * p] = st_t[:HEAD_DIM, :HEAD_DIM]
                    s_ref[e, 2 * p + 1] = st_t[HEAD_DIM:, HEAD_DIM:]


def _wkv_call(rkv, lora, s0f, s0b, w2p, w0, a2p, a0, k_k, k_a, batch, seq_len):
    n = rkv.shape[0]
    nc = seq_len // CHUNK
    assert batch % WKV_SEQS == 0
    steps = (batch // WKV_SEQS) * nc
    rkv3 = rkv.reshape(batch, seq_len, 3 * RWKV_DIM)
    lora3 = lora.reshape(batch, seq_len, LORA_COLS)
    at = lambda g, back: (g // nc, (nc - 1 - g % nc) if back else g % nc, 0)
    fwd = lambda g: at(g, False)
    bwd = lambda g: at(g, True)
    fwd_first = lambda g: (0, 0, 0)
    bwd_first = lambda g: (0, nc - 1, 0)
    fwd_next = lambda g: at(jnp.minimum(g + 1, steps - 1), False)
    bwd_next = lambda g: at(jnp.minimum(g + 1, steps - 1), True)
    st = lambda g: (g // nc, 0, 0, 0)
    c2 = lambda g: (0, 0)
    c3 = lambda g: (0, 0, 0)
    st_shape = jax.ShapeDtypeStruct((batch, RWKV_HEADS, HEAD_DIM, HEAD_DIM), F32)
    y_shape = jax.ShapeDtypeStruct((batch, seq_len, RWKV_DIM), F32)
    n_chains = WKV_SEQS * 2 * N_PAIRS
    rkv_blk = (WKV_SEQS, CHUNK, 3 * RWKV_DIM)
    lora_blk = (WKV_SEQS, CHUNK, LORA_COLS)
    yf, yb, sf, sb = pl.pallas_call(
        functools.partial(_wkv_kernel, nc=nc),
        grid=(steps,),
        in_specs=[pl.BlockSpec(rkv_blk, fwd_first),
                  pl.BlockSpec(lora_blk, fwd_first),
                  pl.BlockSpec(rkv_blk, bwd_first),
                  pl.BlockSpec(lora_blk, bwd_first),
                  pl.BlockSpec(rkv_blk, fwd_next),
                  pl.BlockSpec(lora_blk, fwd_next),
                  pl.BlockSpec(rkv_blk, bwd_next),
                  pl.BlockSpec(lora_blk, bwd_next),
                  pl.BlockSpec((WKV_SEQS, N_PAIRS, LANES, LANES), st),
                  pl.BlockSpec((WKV_SEQS, N_PAIRS, LANES, LANES), st),
                  pl.BlockSpec((2, LANES, RWKV_DIM), c3),
                  pl.BlockSpec((2, RWKV_DIM), c2),
                  pl.BlockSpec((2, LANES, RWKV_DIM), c3),
                  pl.BlockSpec((2, RWKV_DIM), c2),
                  pl.BlockSpec((1, RWKV_DIM), c2),
                  pl.BlockSpec((1, RWKV_DIM), c2)],
        out_specs=[pl.BlockSpec((WKV_SEQS, CHUNK, RWKV_DIM), fwd),
                   pl.BlockSpec((WKV_SEQS, CHUNK, RWKV_DIM), bwd),
                   pl.BlockSpec((WKV_SEQS, RWKV_HEADS, HEAD_DIM, HEAD_DIM), st),
                   pl.BlockSpec((WKV_SEQS, RWKV_HEADS, HEAD_DIM, HEAD_DIM), st)],
        out_shape=[y_shape, y_shape, st_shape, st_shape],
        scratch_shapes=[pltpu.VMEM((WKV_SEQS, 2, N_PAIRS, LANES, LANES), F32),
                        pltpu.VMEM((2, n_chains, len(WKV_OPERANDS), CHUNK, LANES), F32),
                        pltpu.VMEM((2, n_chains, SUBLANES, LANES), F32)],
        compiler_params=_params("arbitrary"),
        name="wkv_chunked",
    )(rkv3, lora3, rkv3, lora3, rkv3, lora3, rkv3, lora3, s0f, s0b, w2p, w0, a2p, a0, k_k, k_a)
    return yf.reshape(n, RWKV_DIM), yb.reshape(n, RWKV_DIM), sf, sb


def _out_kernel(x_ref, mod_ref, conv_ref, yf_ref, yb_ref, rkv_ref, lo_ref,
                a2_ref, a0_ref, ka_ref, rk_ref, gng_ref, gnb_ref, g2_ref, wout_ref,
                postg_ref, preg_ref, rw_ref, rb_ref, x1_ref, h2_ref, route_ref, cnt_ref):
    @pl.when(pl.program_id(0) == 0)
    def _():
        cnt_ref[...] = jnp.zeros(cnt_ref.shape, F32)

    m = mod_ref[0]
    rkv = rkv_ref[...]
    lo = lo_ref[...]
    r = rkv[:, :RWKV_DIM]
    k = rkv[:, RWKV_DIM:2 * RWKV_DIM]
    v = rkv[:, 2 * RWKV_DIM:]
    ha = lo[:, LANES:2 * LANES]
    ag_f = _sigmoid(a0_ref[0:1, :] + _mm(ha, a2_ref[0]))
    ag_b = _sigmoid(a0_ref[1:2, :] + _mm(ha, a2_ref[1]))
    rkk = r * k * rk_ref[...] * (2.0 + (ag_f + ag_b - 2.0) * ka_ref[...])
    o = yf_ref[...] + yb_ref[...]
    gate = _mm(_sigmoid(lo[:, 2 * LANES:]), g2_ref[...])
    r2 = lax.broadcasted_iota(jnp.int32, (LANES, LANES), 0)
    c2 = lax.broadcasted_iota(jnp.int32, (LANES, LANES), 1)
    ones_bd = ((r2 // HEAD_DIM) == (c2 // HEAD_DIM)).astype(BF16)
    parts = []
    for p in range(N_PAIRS):
        cols = slice(p * LANES, (p + 1) * LANES)
        op = o[:, cols]
        mu = _head_sums(op, ones_bd) * (1.0 / HEAD_DIM)
        oc = op - mu
        var = _head_sums(oc * oc, ones_bd) * (1.0 / HEAD_DIM)
        on = oc * lax.rsqrt(var + GN_EPS) * gng_ref[:, cols] + gnb_ref[:, cols]
        bonus = _head_sums(rkk[:, cols], ones_bd) * v[:, cols]
        parts.append((on + bonus) * gate[:, cols])
    mix_in = jnp.concatenate([conv_ref[...]] + parts, axis=1)
    mix = jnp.dot(mix_in.astype(BF16), wout_ref[...], preferred_element_type=F32)
    x1 = x_ref[...] + m[2:3] * _rms(mix, postg_ref[...])
    x1_ref[...] = x1
    h2 = _rms(x1, preg_ref[...]) * (1.0 + m[4:5]) + m[3:4]
    h2_ref[...] = _pack_rows(h2)
    hh, hl = _split2(h2)
    dot = functools.partial(jnp.dot, preferred_element_type=F32)
    logits = dot(hh, rw_ref[0]) + (dot(hh, rw_ref[1]) + dot(hl, rw_ref[0])) + rb_ref[...]
    lane =lax.broadcasted_iota(jnp.int32, logits.shape, 1)
    work = logits
    picks, vals, idxs = [], [], []
    for _ in range(TOP_K):
        mx = jnp.max(work, axis=-1, keepdims=True)
        idx = jnp.min(jnp.where(work == mx, lane, LANES), axis=-1, keepdims=True)
        pick = lane == idx
        picks.append(pick)
        vals.append(mx)
        idxs.append(idx)
        work = jnp.where(pick, 2.0 * NEG_BIG, work)
    exps = [jnp.exp(val - vals[0]) for val in vals]
    den = exps[0] + exps[1] + exps[2] + exps[3]
    sel = jnp.zeros(logits.shape, F32)
    for pick in picks:
        sel = jnp.where(pick, 1.0, sel)
    tr = lax.broadcasted_iota(jnp.int32, (TM_OUT, TM_OUT), 0)
    tc = lax.broadcasted_iota(jnp.int32, (TM_OUT, TM_OUT), 1)
    before = (tc < tr).astype(BF16)
    rank = dot(before, sel.astype(BF16)) + cnt_ref[...]
    cnt_ref[...] = cnt_ref[...] + jnp.sum(sel, axis=0, keepdims=True)
    route = jnp.zeros(logits.shape, F32)
    for q, (pick, idx, e) in enumerate(zip(picks, idxs, exps)):
        rank_q = jnp.sum(jnp.where(pick, rank, 0.0), axis=-1, keepdims=True)
        route = jnp.where(lane == q, idx.astype(F32), route)
        route = jnp.where(lane == TOP_K + q, e / den, route)
        route = jnp.where(lane == 2 * TOP_K + q, rank_q, route)
    route_ref[...] = route


def _out_call(x, mod, conv_out, yf, yb, rkv, lora, wts, seq_len):
    n, d = x.shape
    row = lambda i: (i, 0)
    c2 = lambda i: (0, 0)
    c3 = lambda i: (0, 0, 0)
    full = lambda a: pl.BlockSpec(a.shape, c3 if a.ndim == 3 else c2)
    consts = [wts["a2p"], wts["a0"], wts["k_a"], wts["r_k"], wts["gn_g"], wts["gn_b"], wts["g2"],
              wts["w_out"], wts["post_mix_g"], wts["pre_ffn_g"], wts["router_w"], wts["router_b"]]
    return pl.pallas_call(
        _out_kernel,
        grid=(n // TM_OUT,),
        in_specs=[pl.BlockSpec((TM_OUT, d), row),
                  pl.BlockSpec((1, N_MOD, d), _mod_index(TM_OUT, seq_len, mod.shape[0])),
                  pl.BlockSpec((TM_OUT, CONV_CH), row),
                  pl.BlockSpec((TM_OUT, RWKV_DIM), row),
                  pl.BlockSpec((TM_OUT, RWKV_DIM), row),
                  pl.BlockSpec((TM_OUT, 3 * RWKV_DIM), row),
                  pl.BlockSpec((TM_OUT, LORA_COLS), row)] + [full(a) for a in consts],
        out_specs=[pl.BlockSpec((TM_OUT, d), row),
                   pl.BlockSpec((TM_OUT, d // 2), row),
                   pl.BlockSpec((TM_OUT, LANES), row),
                   pl.BlockSpec((1, LANES), c2)],
        out_shape=[jax.ShapeDtypeStruct((n, d), F32),
                   jax.ShapeDtypeStruct((n, d // 2), jnp.int32),
                   jax.ShapeDtypeStruct((n, LANES), F32),
                   jax.ShapeDtypeStruct((1, LANES), F32)],
        compiler_params=_params("arbitrary"),
        name="out_router",
    )(x, mod, conv_out, yf, yb, rkv, lora, *consts)


def _sc_mesh():
    return plsc.VectorSubcoreMesh(core_axis_name="c", subcore_axis_name="s",
                                  num_cores=SC_CORES, num_subcores=SC_SUBCORES)


def _sc_worker_id():
    return lax.axis_index("s") * SC_CORES + lax.axis_index("c")


def _sc_scatter_rows(src_a, src_b, idx, n_out):
    (n_a, d), n_b = src_a.shape, src_b.shape[0]
    n = n_a + n_b
    workers = SC_CORES * SC_SUBCORES
    per_worker = n // workers
    assert idx.shape[0] == TOP_K * n and per_worker * workers == n
    assert per_worker % SC_ROWS == 0 and n_a % per_worker == 0

    @functools.partial(
        pl.kernel, mesh=_sc_mesh(),
        out_type=jax.ShapeDtypeStruct((n_out, d), src_a.dtype),
        scratch_types=[pltpu.VMEM((SC_ROWS,), jnp.int32),
                       pltpu.VMEM((SC_ROWS, d), src_a.dtype),
                       pltpu.SemaphoreType.DMA],
        name="sc_scatter_rows")
    def scatter(a_hbm, b_hbm, idx_hbm, out_hbm, idx_v, rows_v, sem):
        base = _sc_worker_id() * per_worker

        @pl.loop(0, per_worker // SC_ROWS)
        def _(j):
            tok = base + j * SC_ROWS

            @pl.when(tok < n_a)
            def _():
                pltpu.sync_copy(a_hbm.at[pl.ds(tok, SC_ROWS)], rows_v)

            @pl.when(tok >= n_a)
            def _():
                pltpu.sync_copy(b_hbm.at[pl.ds(tok - n_a, SC_ROWS)], rows_v)

            for q in range(TOP_K):
                pltpu.sync_copy(idx_hbm.at[pl.ds(q * n + tok, SC_ROWS)], idx_v)
                pltpu.async_copy(rows_v, out_hbm.at[idx_v], sem).wait()

    return scatter(src_a, src_b, idx)


def _sc_gather_rows(src, idx):
    n_idx, d = idx.shape[0], src.shape[1]
    workers = SC_CORES * SC_SUBCORES
    per_worker = n_idx // workers
    assert per_worker * workers == n_idx and per_worker % SC_ROWS == 0

    @functools.partial(
        pl.kernel, mesh=_sc_mesh(),
        out_type=jax.ShapeDtypeStruct((n_idx, d), src.dtype),
        scratch_types=[pltpu.VMEM((SC_ROWS,), jnp.int32),
                       pltpu.VMEM((SC_ROWS, d), src.dtype),
                       pltpu.SemaphoreType.DMA],
        name="sc_gather_rows")
    def gather(src_hbm, idx_hbm, out_hbm, idx_v, rows_v, sem):
        base = _sc_worker_id() * per_worker

        @pl.loop(0, per_worker // SC_ROWS)
        def _(j):
            off = base + j * SC_ROWS
            pltpu.sync_copy(idx_hbm.at[pl.ds(off, SC_ROWS)], idx_v)
            pltpu.async_copy(src_hbm.at[idx_v], rows_v, sem).wait()
            pltpu.sync_copy(rows_v, out_hbm.at[pl.ds(off, SC_ROWS)])

    return gather(src, idx)


def _moe_kernel(blk_e_ref, blk_first_ref, blk_rows_ref, blk_next_ref,
                xs_ref, wgu_hbm, bgu_ref, wd_hbm, bd_ref,
                o_ref, wgu_f32_ref, wd_f32_ref, wgu_bf_ref, wd_bf_ref, sem):
    b = pl.program_id(0)
    n_rows = blk_rows_ref[b]

    def weight_copies(e):
        return (pltpu.make_async_copy(wgu_hbm.at[e], wgu_f32_ref, sem.at[0]),
                pltpu.make_async_copy(wd_hbm.at[e], wd_f32_ref, sem.at[1]))

    @pl.when(b == 0)
    def _():
        for copy in weight_copies(blk_e_ref[0]):
            copy.start()

    @pl.when(blk_first_ref[b] == 1)
    def _():
        for copy in weight_copies(blk_e_ref[b]):
            copy.wait()
        wgu_bf_ref[...] = wgu_f32_ref[...].astype(BF16)
        wd_bf_ref[...] = wd_f32_ref[...].astype(BF16)

        @pl.when(blk_next_ref[b] >= 0)
        def _():
            for copy in weight_copies(blk_next_ref[b]):
                copy.start()

    @pl.when(n_rows > 0)
    def _():
        live = lax.broadcasted_iota(jnp.int32, xs_ref.shape, 0) < n_rows
        xs = _unpack_rows(jnp.where(live, xs_ref[...], 0))
        gu = jnp.dot(xs.astype(BF16), wgu_bf_ref[...],
                     preferred_element_type=F32) + bgu_ref[0]
        x_glu = jnp.minimum(gu[:, :D_FF], SWIGLU_LIMIT)
        x_lin = jnp.clip(gu[:, D_FF:], -SWIGLU_LIMIT, SWIGLU_LIMIT)
        act = x_glu * _sigmoid(SWIGLU_ALPHA * x_glu) * (x_lin + 1.0)
        o_ref[...] = _pack_rows(jnp.dot(act.astype(BF16), wd_bf_ref[...],
                                        preferred_element_type=F32) + bd_ref[0])

    @pl.when(n_rows == 0)
    def _():
        o_ref[...] = jnp.zeros(o_ref.shape, jnp.int32)


def _moe_call(blk_e, blk_first, blk_rows, blk_next, xs, wgu, bgu, wd, bd):
    n_slots, half = xs.shape
    d = 2 * half
    ex = lambda b, be, bf, br, bn: (be[b], 0, 0)
    row = lambda b, be, bf, br, bn: (b, 0)
    grid_spec = pltpu.PrefetchScalarGridSpec(
        num_scalar_prefetch=4,
        grid=(n_slots // MOE_BLOCK,),
        in_specs=[pl.BlockSpec((MOE_BLOCK, half), row),
                  pl.BlockSpec(memory_space=pl.ANY),
                  pl.BlockSpec((1, 1, 2 * D_FF), ex),
                  pl.BlockSpec(memory_space=pl.ANY),
                  pl.BlockSpec((1, 1, d), ex)],
        out_specs=pl.BlockSpec((MOE_BLOCK, half), row),
        scratch_shapes=[pltpu.VMEM((d, 2 * D_FF), F32), pltpu.VMEM((D_FF, d), F32),
                        pltpu.VMEM((d, 2 * D_FF), BF16), pltpu.VMEM((D_FF, d), BF16),
                        pltpu.SemaphoreType.DMA((2,))])
    return pl.pallas_call(
        _moe_kernel,
        grid_spec=grid_spec,
        out_shape=jax.ShapeDtypeStruct((n_slots, half), jnp.int32),
        compiler_params=_params("arbitrary"),
        name="moe_experts",
    )(blk_e, blk_first, blk_rows, blk_next, xs, wgu, bgu, wd, bd)


def _final_kernel(x1_ref, y4_ref, route_ref, mod_ref, g_ref, o_ref):
    m = mod_ref[0]
    route = route_ref[...]
    y = jnp.zeros(x1_ref.shape, F32)
    for q in range(TOP_K):
        y = y + route[:, TOP_K + q:TOP_K + q + 1] * _unpack_rows(y4_ref[q])
    o_ref[...] = x1_ref[...] + m[5:6] * _rms(y, g_ref[...])


def _final_call(x1, y4, route, mod, g, seq_len):
    n, d = x1.shape
    row = lambda i: (i, 0)
    return pl.pallas_call(
        _final_kernel,
        grid=(n // TM_FIN,),
        in_specs=[pl.BlockSpec((TM_FIN, d), row),
                  pl.BlockSpec((TOP_K, TM_FIN, d // 2), lambda i: (0, i, 0)),
                  pl.BlockSpec((TM_FIN, LANES), row),
                  pl.BlockSpec((1, N_MOD, d), _mod_index(TM_FIN, seq_len, mod.shape[0])),
                  pl.BlockSpec((1, d), lambda i: (0, 0))],
        out_specs=pl.BlockSpec((TM_FIN, d), row),
        out_shape=jax.ShapeDtypeStruct((n, d), F32),
        compiler_params=_params("arbitrary"),
        name="final_residual",
    )(x1, y4, route, mod, g)


def _pack_state(s):
    b = s.shape[0]
    st = jnp.swapaxes(s.astype(F32), -1, -2).reshape(b, N_PAIRS, 2, HEAD_DIM, HEAD_DIM)
    z = jnp.zeros_like(st[:, :, 0])
    top = jnp.concatenate([st[:, :, 0], z], axis=-1)
    bot = jnp.concatenate([z, st[:, :, 1]], axis=-1)
    return jnp.concatenate([top, bot], axis=-2)


def _mix_sublayer(x, mod, s0f, s0b, seg_len, wts):
    batch, seq_len, d = x.shape
    x2 = x.reshape(batch * seq_len, d)
    cv, rkv, lora = _in_call(x2, mod, wts["pre_mix_g"], wts["w_all"], seq_len)
    conv_out = _conv_call(cv, wts["conv_w"], wts["conv_b"], wts["conv_ln_g"], wts["conv_ln_b"],
                          seg_len)
    yf, yb, sf, sb = _wkv_call(rkv, lora, s0f, s0b, wts["w2p"], wts["w0"], wts["a2p"], wts["a0"],
                               wts["k_k"], wts["k_a"], batch, seq_len)
    x1, h2, route, cnt = _out_call(x2, mod, conv_out, yf, yb, rkv, lora, wts, seq_len)
    return x1, h2, route, cnt, sf, sb


def _dispatch_plan(routes, cnts):
    experts = jnp.arange(N_EXPERTS, dtype=jnp.int32)
    counts = [c[0, :N_EXPERTS].astype(jnp.int32) for c in cnts]
    total = sum(counts)
    padded = (total + MOE_BLOCK - 1) // MOE_BLOCK * MOE_BLOCK
    pad_end = jnp.cumsum(padded)
    pad_start = pad_end - padded
    dests = []
    prior = jnp.zeros_like(total)
    for route, count in zip(routes, counts):
        idx = route[:, :TOP_K].astype(jnp.int32)
        rank = route[:, 2 * TOP_K:3 * TOP_K].astype(jnp.int32)
        first = jnp.sum(jnp.where(idx[..., None] == experts, pad_start + prior, 0), axis=-1)
        dests.append(first + rank)
        prior = prior + count
    dest = jnp.concatenate(dests, axis=0).T.reshape(-1)
    n_slots = dest.shape[0] + N_EXPERTS * MOE_BLOCK
    blk_row0 = jnp.arange(n_slots // MOE_BLOCK, dtype=jnp.int32) * MOE_BLOCK
    blk_e = jnp.sum((pad_end[None, :] <= blk_row0[:, None]).astype(jnp.int32), axis=-1)
    blk_e = jnp.minimum(blk_e, N_EXPERTS - 1)
    blk_first = jnp.concatenate([jnp.ones((1,), jnp.int32),
                                 (blk_e[1:] != blk_e[:-1]).astype(jnp.int32)])
    of_blk = lambda per_expert: jnp.sum(
        jnp.where(blk_e[:, None] == experts[None, :], per_expert[None, :], 0), axis=-1)
    blk_rows = jnp.clip(of_blk(pad_start + total) - blk_row0, 0, MOE_BLOCK).astype(jnp.int32)
    pos = jnp.arange(blk_e.shape[0], dtype=jnp.int32)
    later_first = (pos[None, :] > pos[:, None]) & (blk_first[None, :] == 1)
    next_pos = jnp.min(jnp.where(later_first, pos[None, :], pos.shape[0]), axis=-1)
    blk_next = jnp.where(next_pos < pos.shape[0],
                         blk_e[jnp.minimum(next_pos, pos.shape[0] - 1)], -1).astype(jnp.int32)
    return dest, n_slots, blk_e, blk_first, blk_rows, blk_next


def _routed_ffn(h2s, routes, cnts, wts):
    dest, n_slots, blk_e, blk_first, blk_rows, blk_next = _dispatch_plan(routes, cnts)
    xs = _sc_scatter_rows(h2s[0], h2s[1], dest, n_slots)
    out_sorted = _moe_call(blk_e, blk_first, blk_rows, blk_next, xs, wts["w_gu"], wts["b_gu"],
                           wts["w_down"], wts["b_down"])
    dest_qt = dest.reshape(TOP_K, -1)
    y4s, row0 = [], 0
    for h2 in h2s:
        n_path = h2.shape[0]
        y4 = _sc_gather_rows(out_sorted, dest_qt[:, row0:row0 + n_path].reshape(-1))
        y4s.append(y4.reshape(TOP_K, n_path, xs.shape[1]))
        row0 += n_path
    return y4s


def _prep_weights(l, pre_mix_g, post_mix_g, pre_ffn_g, post_ffn_g, w_in, w_out, conv_w, conv_b,
                  conv_ln_g, conv_ln_b, rw_w0, rw_w1, rw_w2, rw_a0, rw_a1, rw_a2, rw_g1, rw_g2,
                  rw_k_k, rw_k_a, rw_r_k, rw_gn_g, rw_gn_b, router_w, router_b, w_gu, b_gu,
                  w_down, b_down):
    row = lambda a: a.reshape(1, -1).astype(F32)
    zpad = jnp.zeros((HEAD_DIM, RWKV_DIM), F32)
    w_all = jnp.concatenate([w_in[l], rw_w1[l, 0], rw_w1[l, 1], rw_a1[l, 0], rw_a1[l, 1],
                             rw_g1[l]], axis=1).astype(BF16)
    w2p = jnp.stack([jnp.concatenate([rw_w2[l, 0], zpad], axis=0),
                     jnp.concatenate([zpad, rw_w2[l, 1]], axis=0)]).astype(BF16)
    a2p = jnp.stack([jnp.concatenate([rw_a2[l, 0], zpad], axis=0),
                     jnp.concatenate([zpad, rw_a2[l, 1]], axis=0)]).astype(BF16)
    rw_pad = jnp.pad(router_w[l].astype(F32), ((0, 0), (0, LANES - N_EXPERTS)))
    rw_hi = rw_pad.astype(BF16)
    rw_lo = (rw_pad - rw_hi.astype(F32)).astype(BF16)
    rb_pad = jnp.concatenate([router_b[l].astype(F32),
                              jnp.full((LANES - N_EXPERTS,), NEG_BIG, F32)]).reshape(1, LANES)
    return {
        "pre_mix_g": row(pre_mix_g[l]), "post_mix_g": row(post_mix_g[l]),
        "pre_ffn_g": row(pre_ffn_g[l]), "post_ffn_g": row(post_ffn_g[l]),
        "w_all": w_all, "w_out": w_out[l].astype(BF16),
        "conv_w": conv_w[l].astype(F32), "conv_b": row(conv_b[l]),
        "conv_ln_g": row(conv_ln_g[l]), "conv_ln_b": row(conv_ln_b[l]),
        "w2p": w2p, "w0": rw_w0[l].astype(F32), "a2p": a2p, "a0": rw_a0[l].astype(F32),
        "k_k": row(rw_k_k[l]), "k_a": row(rw_k_a[l]), "r_k": row(rw_r_k[l]),
        "gn_g": row(rw_gn_g[l]), "gn_b": row(rw_gn_b[l]), "g2": rw_g2[l].astype(BF16),
        "router_w": jnp.stack([rw_hi, rw_lo]), "router_b": rb_pad,
        "w_gu": w_gu[l].astype(F32), "b_gu": b_gu[l].reshape(N_EXPERTS, 1, -1).astype(F32),
        "w_down": w_down[l].astype(F32), "b_down": b_down[l].reshape(N_EXPERTS, 1, -1).astype(F32),
    }


def kernel(x_prompt, x_sample, state_wkv_fwd, state_wkv_bwd, c, c_ctx, ada_w, ada_b, pre_mix_g, post_mix_g, pre_ffn_g, post_ffn_g, w_in, w_out, conv_w, conv_b, conv_ln_g, conv_ln_b, rw_w0, rw_w1, rw_w2, rw_a0, rw_a1, rw_a2, rw_g1, rw_g2, rw_k_k, rw_k_a, rw_r_k, rw_gn_g, rw_gn_b, router_w, router_b, w_gu, b_gu, w_down, b_down):
    depth = ada_w.shape[0]
    dec_batch = c.shape[0]
    mod_rows = 16
    c_rows = jnp.concatenate([c, c_ctx[None, :],
                              jnp.zeros((mod_rows - dec_batch - 1, D_MODEL), F32)], axis=0)
    zero_state = jnp.zeros((x_prompt.shape[0], N_PAIRS, LANES, LANES), F32)
    y_prompt, y_sample = x_prompt, x_sample
    new_f, new_b = [], []
    for l in range(depth):
        wts = _prep_weights(l, pre_mix_g, post_mix_g, pre_ffn_g, post_ffn_g, w_in, w_out, conv_w,
                            conv_b, conv_ln_g, conv_ln_b, rw_w0, rw_w1, rw_w2, rw_a0, rw_a1, rw_a2,
                            rw_g1, rw_g2, rw_k_k, rw_k_a, rw_r_k, rw_gn_g, rw_gn_b, router_w,
                            router_b, w_gu, b_gu, w_down, b_down)
        mod = _mod_call(c_rows, ada_w[l], ada_b[l]).reshape(mod_rows, N_MOD, D_MODEL)
        mod_lat = mod[:dec_batch]
        mod_ctx = mod[dec_batch:dec_batch + 1]
        ctx = _mix_sublayer(y_prompt, mod_ctx, zero_state, zero_state, y_prompt.shape[1], wts)
        lat = _mix_sublayer(y_sample, mod_lat, _pack_state(state_wkv_fwd[:, l]),
                            _pack_state(state_wkv_bwd[:, l]), GRID_W, wts)
        new_f.append(ctx[4])
        new_b.append(ctx[5])
        y4_ctx, y4_lat = _routed_ffn([ctx[1], lat[1]], [ctx[2], lat[2]], [ctx[3], lat[3]], wts)
        y_prompt = _final_call(ctx[0], y4_ctx, ctx[2], mod_ctx, wts["post_ffn_g"],
                               y_prompt.shape[1]).reshape(y_prompt.shape)
        y_sample = _final_call(lat[0], y4_lat, lat[2], mod_lat, wts["post_ffn_g"],
                               y_sample.shape[1]).reshape(y_sample.shape)
    return (y_prompt, y_sample, jnp.stack(new_f, axis=1), jnp.stack(new_b, axis=1))
```

```python
import functools
import math

import jax
import jax.numpy as jnp
from jax import lax
from jax.experimental import pallas as pl
from jax.experimental.pallas import tpu as pltpu
from jax.experimental.pallas import tpu_sc as plsc

F32 = jnp.float32
BF16 = jnp.bfloat16

D_MODEL = 1024
CONV_CH = 512
RWKV_DIM = 512
HEAD_DIM = 64
RWKV_HEADS = 8
N_PAIRS = RWKV_HEADS // 2
CONV_WIDTH = 31
N_EXPERTS = 32
TOP_K = 4
D_FF = 1024
SWIGLU_LIMIT = 7.0
SWIGLU_ALPHA = 1.702
RMS_EPS = 1e-6
LN_EPS = 1e-5
GN_EPS = 64e-5
N_MOD = 6
GRID_W = 64

LANES = 128
SUBLANES = 8
CHUNK = 64
WKV_SEQS = 4
LORA_COLS = 384
TM_IN = 1024
TM_CONV = 256
TM_OUT = 512
TM_FIN = 512
MOE_BLOCK = 512
SC_CORES = 2
SC_SUBCORES = 16
SC_ROWS = 128
NEG_BIG = -1e30
EXP_M05 = math.exp(-0.5)
VMEM_LIMIT = 56 * 1024 * 1024


def _sigmoid(x):
    return 1.0 / (1.0 + jnp.exp(-x))


def _mm(a, b):
    return jnp.dot(a.astype(BF16), b.astype(BF16), preferred_element_type=F32)


def _mm_nt(a, b):
    return lax.dot_general(a.astype(BF16), b.astype(BF16), (((1,), (1,)), ((), ())),
                           preferred_element_type=F32)


def _mm_tn(a, b):
    return lax.dot_general(a.astype(BF16), b.astype(BF16), (((0,), (0,)), ((), ())),
                           preferred_element_type=F32)


def _split2(x):
    hi = x.astype(BF16)
    lo = (x - hi.astype(F32)).astype(BF16)
    return hi, lo


def _split3(x):
    h1 = x.astype(BF16)
    r1 = x - h1.astype(F32)
    h2 = r1.astype(BF16)
    h3 = (r1 - h2.astype(F32)).astype(BF16)
    return h1, h2, h3


def _pack_rows(x):
    n = x.shape[1] // 2
    bits = lambda v: lax.bitcast_convert_type(v.astype(BF16).astype(F32), jnp.uint32)
    word = (bits(x[:, n:]) & jnp.uint32(0xFFFF0000)) | (bits(x[:, :n]) >> 16)
    return lax.bitcast_convert_type(word, jnp.int32)


def _unpack_rows(w):
    u = lax.bitcast_convert_type(w, jnp.uint32)
    lo = lax.bitcast_convert_type(u << 16, F32)
    hi = lax.bitcast_convert_type(u & jnp.uint32(0xFFFF0000), F32)
    return jnp.concatenate([lo, hi], axis=1)


def _rms(x, g):
    return x * lax.rsqrt(jnp.mean(x * x, axis=-1, keepdims=True) + RMS_EPS) * g


def _params(*sem):
    return pltpu.CompilerParams(dimension_semantics=sem, vmem_limit_bytes=VMEM_LIMIT)


def _mod_kernel(c_ref, w_ref, b_ref, o_ref):
    c = c_ref[...]
    s1, s2 = _split2(c * _sigmoid(c))
    w1, w2 = _split2(w_ref[...])
    dot = functools.partial(jnp.dot, preferred_element_type=F32)
    o_ref[...] = dot(s1, w1) + (dot(s1, w2) + dot(s2, w1)) + b_ref[...]


def _mod_call(c_rows, ada_w, ada_b):
    m, d = c_rows.shape
    n = ada_w.shape[1]
    tn = 512
    return pl.pallas_call(
        _mod_kernel,
        grid=(n // tn,),
        in_specs=[pl.BlockSpec((m, d), lambda j: (0, 0)),
                  pl.BlockSpec((d, tn), lambda j: (0, j)),
                  pl.BlockSpec((1, tn), lambda j: (0, j))],
        out_specs=pl.BlockSpec((m, tn), lambda j: (0, j)),
        out_shape=jax.ShapeDtypeStruct((m, n), F32),
        compiler_params=_params("arbitrary"),
        name="mod",
    )(c_rows, ada_w, ada_b.reshape(1, n))


def _in_kernel(x_ref, mod_ref, g_ref, w_ref, cv_ref, rkv_ref, lo_ref):
    m = mod_ref[0]
    h = _rms(x_ref[...], g_ref[...]) * (1.0 + m[1:2]) + m[0:1]
    proj = jnp.dot(h.astype(BF16), w_ref[...], preferred_element_type=F32)
    cv_ref[...] = proj[:, :CONV_CH] * _sigmoid(proj[:, CONV_CH:2 * CONV_CH])
    rkv_ref[...] = proj[:, 2 * CONV_CH:2 * CONV_CH + 3 * RWKV_DIM]
    lo_ref[...] = proj[:, 2 * CONV_CH + 3 * RWKV_DIM:]


def _mod_index(tile, seq_len, n_mod_rows):
    def index(i):
        return ((i * tile) // seq_len) % n_mod_rows, 0, 0
    return index


def _in_call(x, mod, g, w_all, seq_len):
    n, d = x.shape
    ncol = w_all.shape[1]
    return pl.pallas_call(
        _in_kernel,
        grid=(n // TM_IN,),
        in_specs=[pl.BlockSpec((TM_IN, d), lambda i: (i, 0)),
                  pl.BlockSpec((1, N_MOD, d), _mod_index(TM_IN, seq_len, mod.shape[0])),
                  pl.BlockSpec((1, d), lambda i: (0, 0)),
                  pl.BlockSpec((d, ncol), lambda i: (0, 0))],
        out_specs=[pl.BlockSpec((TM_IN, CONV_CH), lambda i: (i, 0)),
                   pl.BlockSpec((TM_IN, 3 * RWKV_DIM), lambda i: (i, 0)),
                   pl.BlockSpec((TM_IN, LORA_COLS), lambda i: (i, 0))],
        out_shape=[jax.ShapeDtypeStruct((n, CONV_CH), F32),
                   jax.ShapeDtypeStruct((n, 3 * RWKV_DIM), F32),
                   jax.ShapeDtypeStruct((n, LORA_COLS), F32)],
        compiler_params=_params("arbitrary"),
        name="in_proj",
    )(x, mod, g, w_all)


CONV_HALO = 16
CONV_ROWS = 32


def _conv_kernel(cv_ref, w_ref, b_ref, g_ref, be_ref, o_ref, pad_ref, acc_ref, *, seg_len):
    nseg = TM_CONV // seg_len
    stride = seg_len + 2 * CONV_HALO
    rows = nseg * stride
    u = cv_ref[...]
    pad_ref[0] = jnp.zeros(pad_ref.shape[1:], F32)
    for s in range(nseg):
        pad_ref[0, s * stride + CONV_HALO:s * stride + CONV_HALO + seg_len, :] = (
            u[s * seg_len:(s + 1) * seg_len])
    for r in range(1, SUBLANES):
        pad_ref[r, 0:rows, :] = pad_ref[0, r:r + rows, :]
    first = CONV_HALO - CONV_WIDTH // 2
    chunks_per_seg = seg_len // CONV_ROWS

    def row_chunk(i, carry):
        pad0 = i * CONV_ROWS + (i // chunks_per_seg) * (2 * CONV_HALO)
        acc = jnp.zeros((CONV_ROWS, CONV_CH), F32)
        for t in range(CONV_WIDTH):
            q, r = divmod(first + t, SUBLANES)
            start = pl.multiple_of(pad0 + q * SUBLANES, SUBLANES)
            acc = acc + w_ref[t:t + 1, :] * pad_ref[r, pl.ds(start, CONV_ROWS), :]
        acc_ref[pl.ds(pl.multiple_of(i * CONV_ROWS, CONV_ROWS), CONV_ROWS), :] = acc
        return carry

    lax.fori_loop(0, TM_CONV // CONV_ROWS, row_chunk, 0)
    y = acc_ref[...] + b_ref[...]
    mu = jnp.mean(y, axis=-1, keepdims=True)
    yc = y - mu
    var = jnp.mean(yc * yc, axis=-1, keepdims=True)
    z = yc * lax.rsqrt(var + LN_EPS) * g_ref[...] + be_ref[...]
    o_ref[...] = z * _sigmoid(z)


def _conv_call(cv, conv_w, conv_b, ln_g, ln_b, seg_len):
    n = cv.shape[0]
    nseg = TM_CONV // seg_len
    row = lambda i: (i, 0)
    const = lambda i: (0, 0)
    return pl.pallas_call(
        functools.partial(_conv_kernel, seg_len=seg_len),
        grid=(n // TM_CONV,),
        in_specs=[pl.BlockSpec((TM_CONV, CONV_CH), row),
                  pl.BlockSpec((CONV_WIDTH, CONV_CH), const),
                  pl.BlockSpec((1, CONV_CH), const),
                  pl.BlockSpec((1, CONV_CH), const),
                  pl.BlockSpec((1, CONV_CH), const)],
        out_specs=pl.BlockSpec((TM_CONV, CONV_CH), row),
        out_shape=jax.ShapeDtypeStruct((n, CONV_CH), F32),
        scratch_shapes=[pltpu.VMEM((SUBLANES, nseg * (seg_len + 2 * CONV_HALO) + SUBLANES, CONV_CH),
                                   F32),
                        pltpu.VMEM((TM_CONV, CONV_CH), F32)],
        compiler_params=_params("arbitrary"),
        name="conv_module",
    )(cv, conv_w, conv_b, ln_g, ln_b)


def _head_sums(x, ones_bd):
    hi, lo = _split2(x)
    return jnp.dot(jnp.concatenate([hi, lo], axis=1), jnp.concatenate([ones_bd, ones_bd], axis=0),
                   preferred_element_type=F32)


def _wkv_masks():
    c = CHUNK
    r2 = lax.broadcasted_iota(jnp.int32, (2 * c, 2 * c), 0)
    c2 = lax.broadcasted_iota(jnp.int32, (2 * c, 2 * c), 1)
    same = (r2 // c) == (c2 // c)
    rr, cc = r2 % c, c2 % c
    r1 = lax.broadcasted_iota(jnp.int32, (c, 2 * c), 0)
    c1 = lax.broadcasted_iota(jnp.int32, (c, 2 * c), 1)
    cc1 = c1 % c
    rs = lax.broadcasted_iota(jnp.int32, (c, c), 0)
    cs = lax.broadcasted_iota(jnp.int32, (c, c), 1)
    masks = {"same": same, "eye": r2 == c2, "left": c1 < c,
             "ones_bd": same.astype(BF16)}
    for name, lt in (("f", lambda a, b: a < b), ("b", lambda a, b: a > b)):
        le = (lambda a, b, lt=lt: lt(a, b) | (a == b))
        masks["strict_" + name] = same & lt(cc, rr)
        masks["inc_" + name] = le(cc1, r1)
        masks["ak0_" + name] = (c1 >= c) & lt(cc1, r1)
        masks["ak1_" + name] = (c1 < c) & lt(cc1, r1)
        masks["tri_" + name] = le(cs, rs).astype(BF16)
    return masks


def _wkv_chains(chains, mk):
    c = CHUNK
    left = mk["left"]
    for ch in chains:
        zero = jnp.zeros_like(ch["at"])
        ch["at0"], ch["at1"] = jnp.where(left, ch["at"], zero), jnp.where(left, zero, ch["at"])
        rt0, rt1 = jnp.where(left, ch["rt"], zero), jnp.where(left, zero, ch["rt"])
        bt, kt = ch["bt"].astype(BF16), ch["kt"].astype(BF16)
        ch["g0"] = _mm_nt(jnp.concatenate([ch["at0"], rt0], axis=0), jnp.concatenate([bt, kt], axis=0))
        ch["g1"] = _mm_nt(jnp.concatenate([ch["at1"], rt1], axis=0), jnp.concatenate([kt, bt], axis=0))
    yield
    for ch in chains:
        g0, g1 = ch["g0"], ch["g1"]
        lmat = jnp.where(mk["strict_" + ch["d"]], jnp.concatenate([g0[:c], g1[:c]], axis=0), 0.0)
        ch["tinv"] = jnp.where(mk["eye"], 1.0, lmat)
        ch["lpow"] = _mm(lmat, lmat)
    for step in range(5):
        yield
        for ch in chains:
            lp = ch["lpow"].astype(BF16)
            if step < 4:
                prod = jnp.dot(lp, jnp.concatenate([lp, ch["tinv"].astype(BF16)], axis=1),
                               preferred_element_type=F32)
                ch["lpow"] = prod[:, :LANES]
                ch["tinv"] = ch["tinv"] + prod[:, LANES:]
            else:
                ch["tinv"] = ch["tinv"] + _mm(lp, ch["tinv"])
    yield
    for ch in chains:
        d, v = ch["d"], ch["v"]
        zv = jnp.zeros_like(v)
        ak0 = jnp.where(mk["ak0_" + d], ch["g0"][:c], 0.0)
        ak1 = jnp.where(mk["ak1_" + d], ch["g1"][:c], 0.0)
        ch["x0"] = _mm(ak0, jnp.concatenate([zv, v], axis=0))
        ch["x1"] = _mm(ak1, jnp.concatenate([v, zv], axis=0))
    yield
    for ch in chains:
        zero = jnp.zeros_like(ch["x0"])
        rhs = jnp.concatenate([
            jnp.concatenate([ch["at0"], jnp.where(left, ch["x0"], zero)], axis=1),
            jnp.concatenate([ch["at1"], jnp.where(left, zero, ch["x1"])], axis=1)], axis=0)
        ch["aw_blk"] = _mm(ch["tinv"], rhs)
    yield
    for ch in chains:
        d, v, aw_blk = ch["d"], ch["v"], ch["aw_blk"]
        zv = jnp.zeros_like(v)
        aw = aw_blk[:c] + aw_blk[c:]
        inc = mk["inc_" + d]
        g0, g1 = ch["g0"], ch["g1"]
        rb = jnp.where(inc, jnp.where(left, g0[c:], g1[c:]), 0.0)
        rk = jnp.where(inc, jnp.where(left, g1[c:], g0[c:]), 0.0)
        v0, v1 = jnp.where(left, v, zv), jnp.where(left, zv, v)
        vblk = jnp.concatenate([jnp.zeros((2 * c, LANES), F32),
                                jnp.concatenate([v1, v0], axis=0)], axis=1)
        qz = _mm(jnp.concatenate([rb, rk], axis=1), jnp.concatenate([aw_blk, vblk], axis=0))
        ch["q"] = ch["rt"] + qz[:, :LANES]
        ch["zz"] = qz[:, LANES:]
        rhs3 = jnp.concatenate([aw, jnp.concatenate([zv, v], axis=1)], axis=0)
        ch["mn"] = _mm_tn(jnp.concatenate([ch["bh"], ch["kh"]], axis=0), rhs3)
    yield
    same = mk["same"]
    for ch in chains:
        mn = ch["mn"]
        mt = jnp.where(same, mn[:, :LANES], 0.0) + jnp.where(mk["eye"], ch["ptot"], 0.0)
        nt = jnp.where(same, mn[:, LANES:], 0.0)
        both = _mm(jnp.concatenate([ch["q"], mt], axis=0), ch["st"])
        ch["y"] = both[:c] + ch["zz"]
        ch["st_new"] = both[c:] + nt


WKV_OPERANDS = ("at", "rt", "bt", "kt", "bh", "kh", "v")


def _wkv_prep(*args):
    for e in range(WKV_SEQS):
        yield from _wkv_prep_seq(e, *args)


def _wkv_prep_seq(e, rkv_refs, lo_refs, w2_ref, w0_ref, a2_ref, a0_ref, kk_ref, ka_ref, mk,
                  ops_ref, ptot_ref, slot):
    c = CHUNK
    for di, (d, rkv_ref, lo_ref) in enumerate(zip("fb", rkv_refs, lo_refs)):
        rkv = rkv_ref[e]
        lo = lo_ref[e]
        r = rkv[:, :RWKV_DIM]
        k = rkv[:, RWKV_DIM:2 * RWKV_DIM]
        v = rkv[:, 2 * RWKV_DIM:]
        w_raw = w0_ref[di:di + 1, :] + _mm(jnp.tanh(lo[:, :LANES]), w2_ref[di])
        lw = -EXP_M05 * _sigmoid(w_raw)
        yield
        ag = _sigmoid(a0_ref[di:di + 1, :] + _mm(lo[:, LANES:2 * LANES], a2_ref[di]))
        kd = k * (1.0 + (ag - 1.0) * ka_ref[...])
        kk_raw = k * kk_ref[...]
        h1, h2, h3 = _split3(lw)
        cs = jnp.dot(mk["tri_" + d], jnp.concatenate([h1, h2, h3], axis=1),
                     preferred_element_type=F32)
        logp = cs[:, :RWKV_DIM] + cs[:, RWKV_DIM:2 * RWKV_DIM] + cs[:, 2 * RWKV_DIM:]
        yield
        tot = logp[0:1] if d == "b" else logp[c - 1:c]
        e_inv = jnp.exp(-logp)
        e_hat = jnp.exp(tot - logp)
        rt = r * jnp.exp(logp)
        e_ex = jnp.exp(logp - lw)
        kt, kh = kd * e_inv, kd * e_hat
        ptot = jnp.exp(tot)
        for p in range(N_PAIRS):
            yield
            cols = slice(p * LANES, (p + 1) * LANES)
            kk = kk_raw[:, cols]
            kk = kk * lax.rsqrt(jnp.maximum(_head_sums(kk * kk, mk["ones_bd"]), 1e-24))
            kb = kk * ag[:, cols]
            tiles = {"at": -kk * e_ex[:, cols], "rt": rt[:, cols], "bt": kb * e_inv[:, cols],
                     "kt": kt[:, cols], "bh": kb * e_hat[:, cols], "kh": kh[:, cols],
                     "v": v[:, cols]}
            chain = (e * 2 + di) * N_PAIRS + p
            for j, name in enumerate(WKV_OPERANDS):
                ops_ref[slot, chain, j] = tiles[name]
            ptot_ref[slot, chain] = jnp.broadcast_to(ptot[:, cols], (SUBLANES, LANES))


def _wkv_kernel(rkv_f0_ref, lo_f0_ref, rkv_b0_ref, lo_b0_ref,
                rkv_f1_ref, lo_f1_ref, rkv_b1_ref, lo_b1_ref, s0f_ref, s0b_ref,
                w2_ref, w0_ref, a2_ref, a0_ref, kk_ref, ka_ref,
                yf_ref, yb_ref, sf_ref, sb_ref, st_ref, ops_ref, ptot_ref, *, nc):
    g = pl.program_id(0)
    ci = g % nc
    slot = g % 2
    mk = _wkv_masks()
    prm = (w2_ref, w0_ref, a2_ref, a0_ref, kk_ref, ka_ref, mk, ops_ref, ptot_ref)

    @pl.when(g == 0)
    def _():
        for _ in _wkv_prep((rkv_f0_ref, rkv_b0_ref), (lo_f0_ref, lo_b0_ref), *prm, 0):
            pass

    @pl.when(ci == 0)
    def _():
        for e in range(WKV_SEQS):
            st_ref[e, 0] = s0f_ref[e]
            st_ref[e, 1] = s0b_ref[e]

    chains = []
    for e in range(WKV_SEQS):
        for di, (d, y_ref) in enumerate((("f", yf_ref), ("b", yb_ref))):
            for p in range(N_PAIRS):
                chain = (e * 2 + di) * N_PAIRS + p
                ch = {"d": d, "e": e, "di": di, "p": p, "y_ref": y_ref, "st": st_ref[e, di, p],
                      "ptot": ptot_ref[slot, chain][0:1]}
                for j, name in enumerate(WKV_OPERANDS):
                    ch[name] = ops_ref[slot, chain, j]
                chains.append(ch)
    prep = _wkv_prep((rkv_f1_ref, rkv_b1_ref), (lo_f1_ref, lo_b1_ref), *prm, 1 - slot)
    for _ in _wkv_chains(chains, mk):
        for _ in range(WKV_SEQS):
            next(prep, None)
    for _ in prep:
        pass
    for ch in chains:
        ch["y_ref"][ch["e"], :, ch["p"] * LANES:(ch["p"] + 1) * LANES] = ch["y"]
        st_ref[ch["e"], ch["di"], ch["p"]] = ch["st_new"]

    @pl.when(ci == nc - 1)
    def _():
        for e in range(WKV_SEQS):
            for di, s_ref in enumerate((sf_ref, sb_ref)):
                for p in range(N_PAIRS):
                    st_t = st_ref[e, di, p].T
                    s_ref[e, 2 * p] = st_t[:HEAD_DIM, :HEAD_DIM]
                    s_ref[e, 2 * p + 1] = st_t[HEAD_DIM:, HEAD_DIM:]


def _wkv_call(rkv, lora, s0f, s0b, w2p, w0, a2p, a0, k_k, k_a, batch, seq_len):
    n = rkv.shape[0]
    nc = seq_len // CHUNK
    assert batch % WKV_SEQS == 0
    steps = (batch // WKV_SEQS) * nc
    rkv3 = rkv.reshape(batch, seq_len, 3 * RWKV_DIM)
    lora3 = lora.reshape(batch, seq_len, LORA_COLS)
    at = lambda g, back: (g // nc, (nc - 1 - g % nc) if back else g % nc, 0)
    fwd = lambda g: at(g, False)
    bwd = lambda g: at(g, True)
    fwd_first = lambda g: (0, 0, 0)
    bwd_first = lambda g: (0, nc - 1, 0)
    fwd_next = lambda g: at(jnp.minimum(g + 1, steps - 1), False)
    bwd_next = lambda g: at(jnp.minimum(g + 1, steps - 1), True)
    st = lambda g: (g // nc, 0, 0, 0)
    c2 = lambda g: (0, 0)
    c3 = lambda g: (0, 0, 0)
    st_shape = jax.ShapeDtypeStruct((batch, RWKV_HEADS, HEAD_DIM, HEAD_DIM), F32)
    y_shape = jax.ShapeDtypeStruct((batch, seq_len, RWKV_DIM), F32)
    n_chains = WKV_SEQS * 2 * N_PAIRS
    rkv_blk = (WKV_SEQS, CHUNK, 3 * RWKV_DIM)
    lora_blk = (WKV_SEQS, CHUNK, LORA_COLS)
    yf, yb, sf, sb = pl.pallas_call(
        functools.partial(_wkv_kernel, nc=nc),
        grid=(steps,),
        in_specs=[pl.BlockSpec(rkv_blk, fwd_first),
                  pl.BlockSpec(lora_blk, fwd_first),
                  pl.BlockSpec(rkv_blk, bwd_first),
                  pl.BlockSpec(lora_blk, bwd_first),
                  pl.BlockSpec(rkv_blk, fwd_next),
                  pl.BlockSpec(lora_blk, fwd_next),
                  pl.BlockSpec(rkv_blk, bwd_next),
                  pl.BlockSpec(lora_blk, bwd_next),
                  pl.BlockSpec((WKV_SEQS, N_PAIRS, LANES, LANES), st),
                  pl.BlockSpec((WKV_SEQS, N_PAIRS, LANES, LANES), st),
                  pl.BlockSpec((2, LANES, RWKV_DIM), c3),
                  pl.BlockSpec((2, RWKV_DIM), c2),
                  pl.BlockSpec((2, LANES, RWKV_DIM), c3),
                  pl.BlockSpec((2, RWKV_DIM), c2),
                  pl.BlockSpec((1, RWKV_DIM), c2),
                  pl.BlockSpec((1, RWKV_DIM), c2)],
        out_specs=[pl.BlockSpec((WKV_SEQS, CHUNK, RWKV_DIM), fwd),
                   pl.BlockSpec((WKV_SEQS, CHUNK, RWKV_DIM), bwd),
                   pl.BlockSpec((WKV_SEQS, RWKV_HEADS, HEAD_DIM, HEAD_DIM), st),
                   pl.BlockSpec((WKV_SEQS, RWKV_HEADS, HEAD_DIM, HEAD_DIM), st)],
        out_shape=[y_shape, y_shape, st_shape, st_shape],
        scratch_shapes=[pltpu.VMEM((WKV_SEQS, 2, N_PAIRS, LANES, LANES), F32),
                        pltpu.VMEM((2, n_chains, len(WKV_OPERANDS), CHUNK, LANES), F32),
                        pltpu.VMEM((2, n_chains, SUBLANES, LANES), F32)],
        compiler_params=_params("arbitrary"),
        name="wkv_chunked",
    )(rkv3, lora3, rkv3, lora3, rkv3, lora3, rkv3, lora3, s0f, s0b, w2p, w0, a2p, a0, k_k, k_a)
    return yf.reshape(n, RWKV_DIM), yb.reshape(n, RWKV_DIM), sf, sb


def _out_kernel(x_ref, mod_ref, conv_ref, yf_ref, yb_ref, rkv_ref, lo_ref,
                a2_ref, a0_ref, ka_ref, rk_ref, gng_ref, gnb_ref, g2_ref, wout_ref,
                postg_ref, preg_ref, rw_ref, rb_ref, x1_ref, h2_ref, route_ref, cnt_ref):
    @pl.when(pl.program_id(0) == 0)
    def _():
        cnt_ref[...] = jnp.zeros(cnt_ref.shape, F32)

    m = mod_ref[0]
    rkv = rkv_ref[...]
    lo = lo_ref[...]
    r = rkv[:, :RWKV_DIM]
    k = rkv[:, RWKV_DIM:2 * RWKV_DIM]
    v = rkv[:, 2 * RWKV_DIM:]
    ha = lo[:, LANES:2 * LANES]
    ag_f = _sigmoid(a0_ref[0:1, :] + _mm(ha, a2_ref[0]))
    ag_b = _sigmoid(a0_ref[1:2, :] + _mm(ha, a2_ref[1]))
    rkk = r * k * rk_ref[...] * (2.0 + (ag_f + ag_b - 2.0) * ka_ref[...])
    o = yf_ref[...] + yb_ref[...]
    gate = _mm(_sigmoid(lo[:, 2 * LANES:]), g2_ref[...])
    r2 = lax.broadcasted_iota(jnp.int32, (LANES, LANES), 0)
    c2 = lax.broadcasted_iota(jnp.int32, (LANES, LANES), 1)
    ones_bd = ((r2 // HEAD_DIM) == (c2 // HEAD_DIM)).astype(BF16)
    parts = []
    for p in range(N_PAIRS):
        cols = slice(p * LANES, (p + 1) * LANES)
        op = o[:, cols]
        mu = _head_sums(op, ones_bd) * (1.0 / HEAD_DIM)
        oc = op - mu
        var = _head_sums(oc * oc, ones_bd) * (1.0 / HEAD_DIM)
        on = oc * lax.rsqrt(var + GN_EPS) * gng_ref[:, cols] + gnb_ref[:, cols]
        bonus = _head_sums(rkk[:, cols], ones_bd) * v[:, cols]
        parts.append((on + bonus) * gate[:, cols])
    mix_in = jnp.concatenate([conv_ref[...]] + parts, axis=1)
    mix = jnp.dot(mix_in.astype(BF16), wout_ref[...], preferred_element_type=F32)
    x1 = x_ref[...] + m[2:3] * _rms(mix, postg_ref[...])
    x1_ref[...] = x1
    h2 = _rms(x1, preg_ref[...]) * (1.0 + m[4:5]) + m[3:4]
    h2_ref[...] = _pack_rows(h2)
    hh, hl = _split2(h2)
    dot = functools.partial(jnp.dot, preferred_element_type=F32)
    logits = dot(hh, rw_ref[0]) + (dot(hh, rw_ref[1]) + dot(hl, rw_ref[0])) + rb_ref[...]
    lane =lax.broadcasted_iota(jnp.int32, logits.shape, 1)
    work = logits
    picks, vals, idxs = [], [], []
    for _ in range(TOP_K):
        mx = jnp.max(work, axis=-1, keepdims=True)
        idx = jnp.min(jnp.where(work == mx, lane, LANES), axis=-1, keepdims=True)
        pick = lane == idx
        picks.append(pick)
        vals.append(mx)
        idxs.append(idx)
        work = jnp.where(pick, 2.0 * NEG_BIG, work)
    exps = [jnp.exp(val - vals[0]) for val in vals]
    den = exps[0] + exps[1] + exps[2] + exps[3]
    sel = jnp.zeros(logits.shape, F32)
    for pick in picks:
        sel = jnp.where(pick, 1.0, sel)
    tr = lax.broadcasted_iota(jnp.int32, (TM_OUT, TM_OUT), 0)
    tc = lax.broadcasted_iota(jnp.int32, (TM_OUT, TM_OUT), 1)
    before = (tc < tr).astype(BF16)
    rank = dot(before, sel.astype(BF16)) + cnt_ref[...]
    cnt_ref[...] = cnt_ref[...] + jnp.sum(sel, axis=0, keepdims=True)
    route = jnp.zeros(logits.shape, F32)
    for q, (pick, idx, e) in enumerate(zip(picks, idxs, exps)):
        rank_q = jnp.sum(jnp.where(pick, rank, 0.0), axis=-1, keepdims=True)
        route = jnp.where(lane == q, idx.astype(F32), route)
        route = jnp.where(lane == TOP_K + q, e / den, route)
        route = jnp.where(lane == 2 * TOP_K + q, rank_q, route)
    route_ref[...] = route


def _out_call(x, mod, conv_out, yf, yb, rkv, lora, wts, seq_len):
    n, d = x.shape
    row = lambda i: (i, 0)
    c2 = lambda i: (0, 0)
    c3 = lambda i: (0, 0, 0)
    full = lambda a: pl.BlockSpec(a.shape, c3 if a.ndim == 3 else c2)
    consts = [wts["a2p"], wts["a0"], wts["k_a"], wts["r_k"], wts["gn_g"], wts["gn_b"], wts["g2"],
              wts["w_out"], wts["post_mix_g"], wts["pre_ffn_g"], wts["router_w"], wts["router_b"]]
    return pl.pallas_call(
        _out_kernel,
        grid=(n // TM_OUT,),
        in_specs=[pl.BlockSpec((TM_OUT, d), row),
                  pl.BlockSpec((1, N_MOD, d), _mod_index(TM_OUT, seq_len, mod.shape[0])),
                  pl.BlockSpec((TM_OUT, CONV_CH), row),
                  pl.BlockSpec((TM_OUT, RWKV_DIM), row),
                  pl.BlockSpec((TM_OUT, RWKV_DIM), row),
                  pl.BlockSpec((TM_OUT, 3 * RWKV_DIM), row),
                  pl.BlockSpec((TM_OUT, LORA_COLS), row)] + [full(a) for a in consts],
        out_specs=[pl.BlockSpec((TM_OUT, d), row),
                   pl.BlockSpec((TM_OUT, d // 2), row),
                   pl.BlockSpec((TM_OUT, LANES), row),
                   pl.BlockSpec((1, LANES), c2)],
        out_shape=[jax.ShapeDtypeStruct((n, d), F32),
                   jax.ShapeDtypeStruct((n, d // 2), jnp.int32),
                   jax.ShapeDtypeStruct((n, LANES), F32),
                   jax.ShapeDtypeStruct((1, LANES), F32)],
        compiler_params=_params("arbitrary"),
        name="out_router",
    )(x, mod, conv_out, yf, yb, rkv, lora, *consts)


def _sc_mesh():
    return plsc.VectorSubcoreMesh(core_axis_name="c", subcore_axis_name="s",
                                  num_cores=SC_CORES, num_subcores=SC_SUBCORES)


def _sc_worker_id():
    return lax.axis_index("s") * SC_CORES + lax.axis_index("c")


def _sc_scatter_rows(src_a, src_b, idx, n_out):
    (n_a, d), n_b = src_a.shape, src_b.shape[0]
    n = n_a + n_b
    workers = SC_CORES * SC_SUBCORES
    per_worker = n // workers
    assert idx.shape[0] == TOP_K * n and per_worker * workers == n
    assert per_worker % SC_ROWS == 0 and n_a % per_worker == 0

    @functools.partial(
        pl.kernel, mesh=_sc_mesh(),
        out_type=jax.ShapeDtypeStruct((n_out, d), src_a.dtype),
        scratch_types=[pltpu.VMEM((SC_ROWS,), jnp.int32),
                       pltpu.VMEM((SC_ROWS, d), src_a.dtype),
                       pltpu.SemaphoreType.DMA],
        name="sc_scatter_rows")
    def scatter(a_hbm, b_hbm, idx_hbm, out_hbm, idx_v, rows_v, sem):
        base = _sc_worker_id() * per_worker

        @pl.loop(0, per_worker // SC_ROWS)
        def _(j):
            tok = base + j * SC_ROWS

            @pl.when(tok < n_a)
            def _():
                pltpu.sync_copy(a_hbm.at[pl.ds(tok, SC_ROWS)], rows_v)

            @pl.when(tok >= n_a)
            def _():
                pltpu.sync_copy(b_hbm.at[pl.ds(tok - n_a, SC_ROWS)], rows_v)

            for q in range(TOP_K):
                pltpu.sync_copy(idx_hbm.at[pl.ds(q * n + tok, SC_ROWS)], idx_v)
                pltpu.async_copy(rows_v, out_hbm.at[idx_v], sem).wait()

    return scatter(src_a, src_b, idx)


def _sc_gather_rows(src, idx):
    n_idx, d = idx.shape[0], src.shape[1]
    workers = SC_CORES * SC_SUBCORES
    per_worker = n_idx // workers
    assert per_worker * workers == n_idx and per_worker % SC_ROWS == 0

    @functools.partial(
        pl.kernel, mesh=_sc_mesh(),
        out_type=jax.ShapeDtypeStruct((n_idx, d), src.dtype),
        scratch_types=[pltpu.VMEM((SC_ROWS,), jnp.int32),
                       pltpu.VMEM((SC_ROWS, d), src.dtype),
                       pltpu.SemaphoreType.DMA],
        name="sc_gather_rows")
    def gather(src_hbm, idx_hbm, out_hbm, idx_v, rows_v, sem):
        base = _sc_worker_id() * per_worker

        @pl.loop(0, per_worker // SC_ROWS)
        def _(j):
            off = base + j * SC_ROWS
            pltpu.sync_copy(idx_hbm.at[pl.ds(off, SC_ROWS)], idx_v)
            pltpu.async_copy(src_hbm.at[idx_v], rows_v, sem).wait()
            pltpu.sync_copy(rows_v, out_hbm.at[pl.ds(off, SC_ROWS)])

    return gather(src, idx)


def _moe_kernel(blk_e_ref, blk_first_ref, blk_rows_ref, blk_next_ref,
                xs_ref, wgu_hbm, bgu_ref, wd_hbm, bd_ref,
                o_ref, wgu_f32_ref, wd_f32_ref, wgu_bf_ref, wd_bf_ref, sem):
    b = pl.program_id(0)
    n_rows = blk_rows_ref[b]

    def weight_copies(e):
        return (pltpu.make_async_copy(wgu_hbm.at[e], wgu_f32_ref, sem.at[0]),
                pltpu.make_async_copy(wd_hbm.at[e], wd_f32_ref, sem.at[1]))

    @pl.when(b == 0)
    def _():
        for copy in weight_copies(blk_e_ref[0]):
            copy.start()

    @pl.when(blk_first_ref[b] == 1)
    def _():
        for copy in weight_copies(blk_e_ref[b]):
            copy.wait()
        wgu_bf_ref[...] = wgu_f32_ref[...].astype(BF16)
        wd_bf_ref[...] = wd_f32_ref[...].astype(BF16)

        @pl.when(blk_next_ref[b] >= 0)
        def _():
            for copy in weight_copies(blk_next_ref[b]):
                copy.start()

    @pl.when(n_rows > 0)
    def _():
        live = lax.broadcasted_iota(jnp.int32, xs_ref.shape, 0) < n_rows
        xs = _unpack_rows(jnp.where(live, xs_ref[...], 0))
        gu = jnp.dot(xs.astype(BF16), wgu_bf_ref[...],
                     preferred_element_type=F32) + bgu_ref[0]
        x_glu = jnp.minimum(gu[:, :D_FF], SWIGLU_LIMIT)
        x_lin = jnp.clip(gu[:, D_FF:], -SWIGLU_LIMIT, SWIGLU_LIMIT)
        act = x_glu * _sigmoid(SWIGLU_ALPHA * x_glu) * (x_lin + 1.0)
        o_ref[...] = _pack_rows(jnp.dot(act.astype(BF16), wd_bf_ref[...],
                                        preferred_element_type=F32) + bd_ref[0])

    @pl.when(n_rows == 0)
    def _():
        o_ref[...] = jnp.zeros(o_ref.shape, jnp.int32)


def _moe_call(blk_e, blk_first, blk_rows, blk_next, xs, wgu, bgu, wd, bd):
    n_slots, half = xs.shape
    d = 2 * half
    ex = lambda b, be, bf, br, bn: (be[b], 0, 0)
    row = lambda b, be, bf, br, bn: (b, 0)
    grid_spec = pltpu.PrefetchScalarGridSpec(
        num_scalar_prefetch=4,
        grid=(n_slots // MOE_BLOCK,),
        in_specs=[pl.BlockSpec((MOE_BLOCK, half), row),
                  pl.BlockSpec(memory_space=pl.ANY),
                  pl.BlockSpec((1, 1, 2 * D_FF), ex),
                  pl.BlockSpec(memory_space=pl.ANY),
                  pl.BlockSpec((1, 1, d), ex)],
        out_specs=pl.BlockSpec((MOE_BLOCK, half), row),
        scratch_shapes=[pltpu.VMEM((d, 2 * D_FF), F32), pltpu.VMEM((D_FF, d), F32),
                        pltpu.VMEM((d, 2 * D_FF), BF16), pltpu.VMEM((D_FF, d), BF16),
                        pltpu.SemaphoreType.DMA((2,))])
    return pl.pallas_call(
        _moe_kernel,
        grid_spec=grid_spec,
        out_shape=jax.ShapeDtypeStruct((n_slots, half), jnp.int32),
        compiler_params=_params("arbitrary"),
        name="moe_experts",
    )(blk_e, blk_first, blk_rows, blk_next, xs, wgu, bgu, wd, bd)


def _final_kernel(x1_ref, y4_ref, route_ref, mod_ref, g_ref, o_ref):
    m = mod_ref[0]
    route = route_ref[...]
    y = jnp.zeros(x1_ref.shape, F32)
    for q in range(TOP_K):
        y = y + route[:, TOP_K + q:TOP_K + q + 1] * _unpack_rows(y4_ref[q])
    o_ref[...] = x1_ref[...] + m[5:6] * _rms(y, g_ref[...])


def _final_call(x1, y4, route, mod, g, seq_len):
    n, d = x1.shape
    row = lambda i: (i, 0)
    return pl.pallas_call(
        _final_kernel,
        grid=(n // TM_FIN,),
        in_specs=[pl.BlockSpec((TM_FIN, d), row),
                  pl.BlockSpec((TOP_K, TM_FIN, d // 2), lambda i: (0, i, 0)),
                  pl.BlockSpec((TM_FIN, LANES), row),
                  pl.BlockSpec((1, N_MOD, d), _mod_index(TM_FIN, seq_len, mod.shape[0])),
                  pl.BlockSpec((1, d), lambda i: (0, 0))],
        out_specs=pl.BlockSpec((TM_FIN, d), row),
        out_shape=jax.ShapeDtypeStruct((n, d), F32),
        compiler_params=_params("arbitrary"),
        name="final_residual",
    )(x1, y4, route, mod, g)


def _pack_state(s):
    b = s.shape[0]
    st = jnp.swapaxes(s.astype(F32), -1, -2).reshape(b, N_PAIRS, 2, HEAD_DIM, HEAD_DIM)
    z = jnp.zeros_like(st[:, :, 0])
    top = jnp.concatenate([st[:, :, 0], z], axis=-1)
    bot = jnp.concatenate([z, st[:, :, 1]], axis=-1)
    return jnp.concatenate([top, bot], axis=-2)


def _mix_sublayer(x, mod, s0f, s0b, seg_len, wts):
    batch, seq_len, d = x.shape
    x2 = x.reshape(batch * seq_len, d)
    cv, rkv, lora = _in_call(x2, mod, wts["pre_mix_g"], wts["w_all"], seq_len)
    conv_out = _conv_call(cv, wts["conv_w"], wts["conv_b"], wts["conv_ln_g"], wts["conv_ln_b"],
                          seg_len)
    yf, yb, sf, sb = _wkv_call(rkv, lora, s0f, s0b, wts["w2p"], wts["w0"], wts["a2p"], wts["a0"],
                               wts["k_k"], wts["k_a"], batch, seq_len)
    x1, h2, route, cnt = _out_call(x2, mod, conv_out, yf, yb, rkv, lora, wts, seq_len)
    return x1, h2, route, cnt, sf, sb


def _dispatch_plan(routes, cnts):
    experts = jnp.arange(N_EXPERTS, dtype=jnp.int32)
    counts = [c[0, :N_EXPERTS].astype(jnp.int32) for c in cnts]
    total = sum(counts)
    padded = (total + MOE_BLOCK - 1) // MOE_BLOCK * MOE_BLOCK
    pad_end = jnp.cumsum(padded)
    pad_start = pad_end - padded
    dests = []
    prior = jnp.zeros_like(total)
    for route, count in zip(routes, counts):
        idx = route[:, :TOP_K].astype(jnp.int32)
        rank = route[:, 2 * TOP_K:3 * TOP_K].astype(jnp.int32)
        first = jnp.sum(jnp.where(idx[..., None] == experts, pad_start + prior, 0), axis=-1)
        dests.append(first + rank)
        prior = prior + count
    dest = jnp.concatenate(dests, axis=0).T.reshape(-1)
    n_slots = dest.shape[0] + N_EXPERTS * MOE_BLOCK
    blk_row0 = jnp.arange(n_slots // MOE_BLOCK, dtype=jnp.int32) * MOE_BLOCK
    blk_e = jnp.sum((pad_end[None, :] <= blk_row0[:, None]).astype(jnp.int32), axis=-1)
    blk_e = jnp.minimum(blk_e, N_EXPERTS - 1)
    blk_first = jnp.concatenate([jnp.ones((1,), jnp.int32),
                                 (blk_e[1:] != blk_e[:-1]).astype(jnp.int32)])
    of_blk = lambda per_expert: jnp.sum(
        jnp.where(blk_e[:, None] == experts[None, :], per_expert[None, :], 0), axis=-1)
    blk_rows = jnp.clip(of_blk(pad_start + total) - blk_row0, 0, MOE_BLOCK).astype(jnp.int32)
    pos = jnp.arange(blk_e.shape[0], dtype=jnp.int32)
    later_first = (pos[None, :] > pos[:, None]) & (blk_first[None, :] == 1)
    next_pos = jnp.min(jnp.where(later_first, pos[None, :], pos.shape[0]), axis=-1)
    blk_next = jnp.where(next_pos < pos.shape[0],
                         blk_e[jnp.minimum(next_pos, pos.shape[0] - 1)], -1).astype(jnp.int32)
    return dest, n_slots, blk_e, blk_first, blk_rows, blk_next


def _routed_ffn(h2s, routes, cnts, wts):
    dest, n_slots, blk_e, blk_first, blk_rows, blk_next = _dispatch_plan(routes, cnts)
    xs = _sc_scatter_rows(h2s[0], h2s[1], dest, n_slots)
    out_sorted = _moe_call(blk_e, blk_first, blk_rows, blk_next, xs, wts["w_gu"], wts["b_gu"],
                           wts["w_down"], wts["b_down"])
    dest_qt = dest.reshape(TOP_K, -1)
    y4s, row0 = [], 0
    for h2 in h2s:
        n_path = h2.shape[0]
        y4 = _sc_gather_rows(out_sorted, dest_qt[:, row0:row0 + n_path].reshape(-1))
        y4s.append(y4.reshape(TOP_K, n_path, xs.shape[1]))
        row0 += n_path
    return y4s


def _prep_weights(l, pre_mix_g, post_mix_g, pre_ffn_g, post_ffn_g, w_in, w_out, conv_w, conv_b,
                  conv_ln_g, conv_ln_b, rw_w0, rw_w1, rw_w2, rw_a0, rw_a1, rw_a2, rw_g1, rw_g2,
                  rw_k_k, rw_k_a, rw_r_k, rw_gn_g, rw_gn_b, router_w, router_b, w_gu, b_gu,
                  w_down, b_down):
    row = lambda a: a.reshape(1, -1).astype(F32)
    zpad = jnp.zeros((HEAD_DIM, RWKV_DIM), F32)
    w_all = jnp.concatenate([w_in[l], rw_w1[l, 0], rw_w1[l, 1], rw_a1[l, 0], rw_a1[l, 1],
                             rw_g1[l]], axis=1).astype(BF16)
    w2p = jnp.stack([jnp.concatenate([rw_w2[l, 0], zpad], axis=0),
                     jnp.concatenate([zpad, rw_w2[l, 1]], axis=0)]).astype(BF16)
    a2p = jnp.stack([jnp.concatenate([rw_a2[l, 0], zpad], axis=0),
                     jnp.concatenate([zpad, rw_a2[l, 1]], axis=0)]).astype(BF16)
    rw_pad = jnp.pad(router_w[l].astype(F32), ((0, 0), (0, LANES - N_EXPERTS)))
    rw_hi = rw_pad.astype(BF16)
    rw_lo = (rw_pad - rw_hi.astype(F32)).astype(BF16)
    rb_pad = jnp.concatenate([router_b[l].astype(F32),
                              jnp.full((LANES - N_EXPERTS,), NEG_BIG, F32)]).reshape(1, LANES)
    return {
        "pre_mix_g": row(pre_mix_g[l]), "post_mix_g": row(post_mix_g[l]),
        "pre_ffn_g": row(pre_ffn_g[l]), "post_ffn_g": row(post_ffn_g[l]),
        "w_all": w_all, "w_out": w_out[l].astype(BF16),
        "conv_w": conv_w[l].astype(F32), "conv_b": row(conv_b[l]),
        "conv_ln_g": row(conv_ln_g[l]), "conv_ln_b": row(conv_ln_b[l]),
        "w2p": w2p, "w0": rw_w0[l].astype(F32), "a2p": a2p, "a0": rw_a0[l].astype(F32),
        "k_k": row(rw_k_k[l]), "k_a": row(rw_k_a[l]), "r_k": row(rw_r_k[l]),
        "gn_g": row(rw_gn_g[l]), "gn_b": row(rw_gn_b[l]), "g2": rw_g2[l].astype(BF16),
        "router_w": jnp.stack([rw_hi, rw_lo]), "router_b": rb_pad,
        "w_gu": w_gu[l].astype(F32), "b_gu": b_gu[l].reshape(N_EXPERTS, 1, -1).astype(F32),
        "w_down": w_down[l].astype(F32), "b_down": b_down[l].reshape(N_EXPERTS, 1, -1).astype(F32),
    }


def kernel(x_prompt, x_sample, state_wkv_fwd, state_wkv_bwd, c, c_ctx, ada_w, ada_b, pre_mix_g, post_mix_g, pre_ffn_g, post_ffn_g, w_in, w_out, conv_w, conv_b, conv_ln_g, conv_ln_b, rw_w0, rw_w1, rw_w2, rw_a0, rw_a1, rw_a2, rw_g1, rw_g2, rw_k_k, rw_k_a, rw_r_k, rw_gn_g, rw_gn_b, router_w, router_b, w_gu, b_gu, w_down, b_down):
    depth = ada_w.shape[0]
    dec_batch = c.shape[0]
    mod_rows = 16
    c_rows = jnp.concatenate([c, c_ctx[None, :],
                              jnp.zeros((mod_rows - dec_batch - 1, D_MODEL), F32)], axis=0)
    zero_state = jnp.zeros((x_prompt.shape[0], N_PAIRS, LANES, LANES), F32)
    y_prompt, y_sample = x_prompt, x_sample
    new_f, new_b = [], []
    for l in range(depth):
        wts = _prep_weights(l, pre_mix_g, post_mix_g, pre_ffn_g, post_ffn_g, w_in, w_out, conv_w,
                            conv_b, conv_ln_g, conv_ln_b, rw_w0, rw_w1, rw_w2, rw_a0, rw_a1, rw_a2,
                            rw_g1, rw_g2, rw_k_k, rw_k_a, rw_r_k, rw_gn_g, rw_gn_b, router_w,
                            router_b, w_gu, b_gu, w_down, b_down)
        mod = _mod_call(c_rows, ada_w[l], ada_b[l]).reshape(mod_rows, N_MOD, D_MODEL)
        mod_lat = mod[:dec_batch]
        mod_ctx = mod[dec_batch:dec_batch + 1]
        ctx = _mix_sublayer(y_prompt, mod_ctx, zero_state, zero_state, y_prompt.shape[1], wts)
        lat = _mix_sublayer(y_sample, mod_lat, _pack_state(state_wkv_fwd[:, l]),
                            _pack_state(state_wkv_bwd[:, l]), GRID_W, wts)
        new_f.append(ctx[4])
        new_b.append(ctx[5])
        y4_ctx, y4_lat = _routed_ffn([ctx[1], lat[1]], [ctx[2], lat[2]], [ctx[3], lat[3]], wts)
        y_prompt = _final_call(ctx[0], y4_ctx, ctx[2], mod_ctx, wts["post_ffn_g"],
                               y_prompt.shape[1]).reshape(y_prompt.shape)
        y_sample = _final_call(lat[0], y4_lat, lat[2], mod_lat, wts["post_ffn_g"],
                               y_sample.shape[1]).reshape(y_sample.shape)
    return (y_prompt, y_sample, jnp.stack(new_f, axis=1), jnp.stack(new_b, axis=1))
```

```python
import functools
import math

import jax
import jax.numpy as jnp
from jax import lax
from jax.experimental import pallas as pl
from jax.experimental.pallas import tpu as pltpu
from jax.experimental.pallas import tpu_sc as plsc

F32 = jnp.float32
BF16 = jnp.bfloat16

D_MODEL = 1024
CONV_CH = 512
RWKV_DIM = 512
HEAD_DIM = 64
RWKV_HEADS = 8
N_PAIRS = RWKV_HEADS // 2
CONV_WIDTH = 31
N_EXPERTS = 32
TOP_K = 4
D_FF = 1024
SWIGLU_LIMIT = 7.0
SWIGLU_ALPHA = 1.702
RMS_EPS = 1e-6
LN_EPS = 1e-5
GN_EPS = 64e-5
N_MOD = 6
GRID_W = 64

LANES = 128
SUBLANES = 8
CHUNK = 64
WKV_SEQS = 4
LORA_COLS = 384
TM_IN = 1024
TM_CONV = 256
TM_OUT = 512
TM_FIN = 512
MOE_BLOCK = 512
SC_CORES = 2
SC_SUBCORES = 16
SC_ROWS = 128
NEG_BIG = -1e30
EXP_M05 = math.exp(-0.5)
VMEM_LIMIT = 56 * 1024 * 1024


def _sigmoid(x):
    return 1.0 / (1.0 + jnp.exp(-x))


def _mm(a, b):
    return jnp.dot(a.astype(BF16), b.astype(BF16), preferred_element_type=F32)


def _mm_nt(a, b):
    return lax.dot_general(a.astype(BF16), b.astype(BF16), (((1,), (1,)), ((), ())),
                           preferred_element_type=F32)


def _mm_tn(a, b):
    return lax.dot_general(a.astype(BF16), b.astype(BF16), (((0,), (0,)), ((), ())),
                           preferred_element_type=F32)


def _split2(x):
    hi = x.astype(BF16)
    lo = (x - hi.astype(F32)).astype(BF16)
    return hi, lo


def _split3(x):
    h1 = x.astype(BF16)
    r1 = x - h1.astype(F32)
    h2 = r1.astype(BF16)
    h3 = (r1 - h2.astype(F32)).astype(BF16)
    return h1, h2, h3


def _pack_rows(x):
    n = x.shape[1] // 2
    bits = lambda v: lax.bitcast_convert_type(v.astype(BF16).astype(F32), jnp.uint32)
    word = (bits(x[:, n:]) & jnp.uint32(0xFFFF0000)) | (bits(x[:, :n]) >> 16)
    return lax.bitcast_convert_type(word, jnp.int32)


def _unpack_rows(w):
    u = lax.bitcast_convert_type(w, jnp.uint32)
    lo = lax.bitcast_convert_type(u << 16, F32)
    hi = lax.bitcast_convert_type(u & jnp.uint32(0xFFFF0000), F32)
    return jnp.concatenate([lo, hi], axis=1)


def _rms(x, g):
    return x * lax.rsqrt(jnp.mean(x * x, axis=-1, keepdims=True) + RMS_EPS) * g


def _params(*sem):
    return pltpu.CompilerParams(dimension_semantics=sem, vmem_limit_bytes=VMEM_LIMIT)


def _mod_kernel(c_ref, w_ref, b_ref, o_ref):
    c = c_ref[...]
    s1, s2 = _split2(c * _sigmoid(c))
    w1, w2 = _split2(w_ref[...])
    dot = functools.partial(jnp.dot, preferred_element_type=F32)
    o_ref[...] = dot(s1, w1) + (dot(s1, w2) + dot(s2, w1)) + b_ref[...]


def _mod_call(c_rows, ada_w, ada_b):
    m, d = c_rows.shape
    n = ada_w.shape[1]
    tn = 512
    return pl.pallas_call(
        _mod_kernel,
        grid=(n // tn,),
        in_specs=[pl.BlockSpec((m, d), lambda j: (0, 0)),
                  pl.BlockSpec((d, tn), lambda j: (0, j)),
                  pl.BlockSpec((1, tn), lambda j: (0, j))],
        out_specs=pl.BlockSpec((m, tn), lambda j: (0, j)),
        out_shape=jax.ShapeDtypeStruct((m, n), F32),
        compiler_params=_params("arbitrary"),
        name="mod",
    )(c_rows, ada_w, ada_b.reshape(1, n))


def _in_kernel(x_ref, mod_ref, g_ref, w_ref, cv_ref, rkv_ref, lo_ref):
    m = mod_ref[0]
    h = _rms(x_ref[...], g_ref[...]) * (1.0 + m[1:2]) + m[0:1]
    proj = jnp.dot(h.astype(BF16), w_ref[...], preferred_element_type=F32)
    cv_ref[...] = proj[:, :CONV_CH] * _sigmoid(proj[:, CONV_CH:2 * CONV_CH])
    rkv_ref[...] = proj[:, 2 * CONV_CH:2 * CONV_CH + 3 * RWKV_DIM]
    lo_ref[...] = proj[:, 2 * CONV_CH + 3 * RWKV_DIM:]


def _mod_index(tile, seq_len, n_mod_rows):
    def index(i):
        return ((i * tile) // seq_len) % n_mod_rows, 0, 0
    return index


def _in_call(x, mod, g, w_all, seq_len):
    n, d = x.shape
    ncol = w_all.shape[1]
    return pl.pallas_call(
        _in_kernel,
        grid=(n // TM_IN,),
        in_specs=[pl.BlockSpec((TM_IN, d), lambda i: (i, 0)),
                  pl.BlockSpec((1, N_MOD, d), _mod_index(TM_IN, seq_len, mod.shape[0])),
                  pl.BlockSpec((1, d), lambda i: (0, 0)),
                  pl.BlockSpec((d, ncol), lambda i: (0, 0))],
        out_specs=[pl.BlockSpec((TM_IN, CONV_CH), lambda i: (i, 0)),
                   pl.BlockSpec((TM_IN, 3 * RWKV_DIM), lambda i: (i, 0)),
                   pl.BlockSpec((TM_IN, LORA_COLS), lambda i: (i, 0))],
        out_shape=[jax.ShapeDtypeStruct((n, CONV_CH), F32),
                   jax.ShapeDtypeStruct((n, 3 * RWKV_DIM), F32),
                   jax.ShapeDtypeStruct((n, LORA_COLS), F32)],
        compiler_params=_params("arbitrary"),
        name="in_proj",
    )(x, mod, g, w_all)


CONV_HALO = 16
CONV_ROWS = 32


def _conv_kernel(cv_ref, w_ref, b_ref, g_ref, be_ref, o_ref, pad_ref, acc_ref, *, seg_len):
    nseg = TM_CONV // seg_len
    stride = seg_len + 2 * CONV_HALO
    rows = nseg * stride
    u = cv_ref[...]
    pad_ref[0] = jnp.zeros(pad_ref.shape[1:], F32)
    for s in range(nseg):
        pad_ref[0, s * stride + CONV_HALO:s * stride + CONV_HALO + seg_len, :] = (
            u[s * seg_len:(s + 1) * seg_len])
    for r in range(1, SUBLANES):
        pad_ref[r, 0:rows, :] = pad_ref[0, r:r + rows, :]
    first = CONV_HALO - CONV_WIDTH // 2
    chunks_per_seg = seg_len // CONV_ROWS

    def row_chunk(i, carry):
        pad0 = i * CONV_ROWS + (i // chunks_per_seg) * (2 * CONV_HALO)
        accs = [jnp.zeros((SUBLANES, CONV_CH), F32) for _ in range(CONV_ROWS // SUBLANES)]
        for t in range(CONV_WIDTH):
            q, r = divmod(first + t, SUBLANES)
            w_t = w_ref[t]
            for j in range(len(accs)):
                start = pl.multiple_of(pad0 + (q + j) * SUBLANES, SUBLANES)
                accs[j] = accs[j] + w_t * pad_ref[r, pl.ds(start, SUBLANES), :]
        for j, acc in enumerate(accs):
            row0 = pl.multiple_of(i * CONV_ROWS + j * SUBLANES, SUBLANES)
            acc_ref[pl.ds(row0, SUBLANES), :] = acc
        return carry

    lax.fori_loop(0, TM_CONV // CONV_ROWS, row_chunk, 0)
    y = acc_ref[...] + b_ref[...]
    mu = jnp.mean(y, axis=-1, keepdims=True)
    yc = y - mu
    var = jnp.mean(yc * yc, axis=-1, keepdims=True)
    z = yc * lax.rsqrt(var + LN_EPS) * g_ref[...] + be_ref[...]
    o_ref[...] = z * _sigmoid(z)


def _conv_call(cv, conv_w, conv_b, ln_g, ln_b, seg_len):
    n = cv.shape[0]
    nseg = TM_CONV // seg_len
    row = lambda i: (i, 0)
    const = lambda i: (0, 0)
    return pl.pallas_call(
        functools.partial(_conv_kernel, seg_len=seg_len),
        grid=(n // TM_CONV,),
        in_specs=[pl.BlockSpec((TM_CONV, CONV_CH), row),
                  pl.BlockSpec((CONV_WIDTH, SUBLANES, CONV_CH), lambda i: (0, 0, 0)),
                  pl.BlockSpec((1, CONV_CH), const),
                  pl.BlockSpec((1, CONV_CH), const),
                  pl.BlockSpec((1, CONV_CH), const)],
        out_specs=pl.BlockSpec((TM_CONV, CONV_CH), row),
        out_shape=jax.ShapeDtypeStruct((n, CONV_CH), F32),
        scratch_shapes=[pltpu.VMEM((SUBLANES, nseg * (seg_len + 2 * CONV_HALO) + SUBLANES, CONV_CH),
                                   F32),
                        pltpu.VMEM((TM_CONV, CONV_CH), F32)],
        compiler_params=_params("arbitrary"),
        name="conv_module",
    )(cv, conv_w, conv_b, ln_g, ln_b)


def _head_sums(x, ones_bd):
    hi, lo = _split2(x)
    return jnp.dot(jnp.concatenate([hi, lo], axis=1), jnp.concatenate([ones_bd, ones_bd], axis=0),
                   preferred_element_type=F32)


def _wkv_masks():
    c = CHUNK
    r2 = lax.broadcasted_iota(jnp.int32, (2 * c, 2 * c), 0)
    c2 = lax.broadcasted_iota(jnp.int32, (2 * c, 2 * c), 1)
    same = (r2 // c) == (c2 // c)
    rr, cc = r2 % c, c2 % c
    r1 = lax.broadcasted_iota(jnp.int32, (c, 2 * c), 0)
    c1 = lax.broadcasted_iota(jnp.int32, (c, 2 * c), 1)
    cc1 = c1 % c
    rs = lax.broadcasted_iota(jnp.int32, (c, c), 0)
    cs = lax.broadcasted_iota(jnp.int32, (c, c), 1)
    masks = {"same": same, "eye": r2 == c2, "left": c1 < c,
             "ones_bd": same.astype(BF16)}
    for name, lt in (("f", lambda a, b: a < b), ("b", lambda a, b: a > b)):
        le = (lambda a, b, lt=lt: lt(a, b) | (a == b))
        masks["strict_" + name] = same & lt(cc, rr)
        masks["inc_" + name] = le(cc1, r1)
        masks["ak0_" + name] = (c1 >= c) & lt(cc1, r1)
        masks["ak1_" + name] = (c1 < c) & lt(cc1, r1)
        masks["tri_" + name] = le(cs, rs).astype(BF16)
    return masks


def _wkv_chains(chains, mk):
    c = CHUNK
    left = mk["left"]
    for ch in chains:
        zero = jnp.zeros_like(ch["at"])
        ch["at0"], ch["at1"] = jnp.where(left, ch["at"], zero), jnp.where(left, zero, ch["at"])
        rt0, rt1 = jnp.where(left, ch["rt"], zero), jnp.where(left, zero, ch["rt"])
        bt, kt = ch["bt"].astype(BF16), ch["kt"].astype(BF16)
        ch["g0"] = _mm_nt(jnp.concatenate([ch["at0"], rt0], axis=0), jnp.concatenate([bt, kt], axis=0))
        ch["g1"] = _mm_nt(jnp.concatenate([ch["at1"], rt1], axis=0), jnp.concatenate([kt, bt], axis=0))
    yield
    for ch in chains:
        g0, g1 = ch["g0"], ch["g1"]
        lmat = jnp.where(mk["strict_" + ch["d"]], jnp.concatenate([g0[:c], g1[:c]], axis=0), 0.0)
        ch["tinv"] = jnp.where(mk["eye"], 1.0, lmat)
        ch["lpow"] = _mm(lmat, lmat)
    for step in range(5):
        yield
        for ch in chains:
            lp = ch["lpow"].astype(BF16)
            if step < 4:
                prod = jnp.dot(lp, jnp.concatenate([lp, ch["tinv"].astype(BF16)], axis=1),
                               preferred_element_type=F32)
                ch["lpow"] = prod[:, :LANES]
                ch["tinv"] = ch["tinv"] + prod[:, LANES:]
            else:
                ch["tinv"] = ch["tinv"] + _mm(lp, ch["tinv"])
    yield
    for ch in chains:
        d, v = ch["d"], ch["v"]
        zv = jnp.zeros_like(v)
        ak0 = jnp.where(mk["ak0_" + d], ch["g0"][:c], 0.0)
        ak1 = jnp.where(mk["ak1_" + d], ch["g1"][:c], 0.0)
        ch["x0"] = _mm(ak0, jnp.concatenate([zv, v], axis=0))
        ch["x1"] = _mm(ak1, jnp.concatenate([v, zv], axis=0))
    yield
    for ch in chains:
        zero = jnp.zeros_like(ch["x0"])
        rhs = jnp.concatenate([
            jnp.concatenate([ch["at0"], jnp.where(left, ch["x0"], zero)], axis=1),
            jnp.concatenate([ch["at1"], jnp.where(left, zero, ch["x1"])], axis=1)], axis=0)
        ch["aw_blk"] = _mm(ch["tinv"], rhs)
    yield
    for ch in chains:
        d, v, aw_blk = ch["d"], ch["v"], ch["aw_blk"]
        zv = jnp.zeros_like(v)
        aw = aw_blk[:c] + aw_blk[c:]
        inc = mk["inc_" + d]
        g0, g1 = ch["g0"], ch["g1"]
        rb = jnp.where(inc, jnp.where(left, g0[c:], g1[c:]), 0.0)
        rk = jnp.where(inc, jnp.where(left, g1[c:], g0[c:]), 0.0)
        v0, v1 = jnp.where(left, v, zv), jnp.where(left, zv, v)
        vblk = jnp.concatenate([jnp.zeros((2 * c, LANES), F32),
                                jnp.concatenate([v1, v0], axis=0)], axis=1)
        qz = _mm(jnp.concatenate([rb, rk], axis=1), jnp.concatenate([aw_blk, vblk], axis=0))
        ch["q"] = ch["rt"] + qz[:, :LANES]
        ch["zz"] = qz[:, LANES:]
        rhs3 = jnp.concatenate([aw, jnp.concatenate([zv, v], axis=1)], axis=0)
        ch["mn"] = _mm_tn(jnp.concatenate([ch["bh"], ch["kh"]], axis=0), rhs3)
    yield
    same = mk["same"]
    for ch in chains:
        mn = ch["mn"]
        mt = jnp.where(same, mn[:, :LANES], 0.0) + jnp.where(mk["eye"], ch["ptot"], 0.0)
        nt = jnp.where(same, mn[:, LANES:], 0.0)
        both = _mm(jnp.concatenate([ch["q"], mt], axis=0), ch["st"])
        ch["y"] = both[:c] + ch["zz"]
        ch["st_new"] = both[c:] + nt


WKV_OPERANDS = ("at", "rt", "bt", "kt", "bh", "kh", "v")


def _wkv_prep(*args):
    for e in range(WKV_SEQS):
        yield from _wkv_prep_seq(e, *args)


def _wkv_prep_seq(e, rkv_refs, lo_refs, w2_ref, w0_ref, a2_ref, a0_ref, kk_ref, ka_ref, mk,
                  ops_ref, ptot_ref, slot):
    c = CHUNK
    for di, (d, rkv_ref, lo_ref) in enumerate(zip("fb", rkv_refs, lo_refs)):
        rkv = rkv_ref[e]
        lo = lo_ref[e]
        r = rkv[:, :RWKV_DIM]
        k = rkv[:, RWKV_DIM:2 * RWKV_DIM]
        v = rkv[:, 2 * RWKV_DIM:]
        w_raw = w0_ref[di:di + 1, :] + _mm(jnp.tanh(lo[:, :LANES]), w2_ref[di])
        lw = -EXP_M05 * _sigmoid(w_raw)
        yield
        ag = _sigmoid(a0_ref[di:di + 1, :] + _mm(lo[:, LANES:2 * LANES], a2_ref[di]))
        kd = k * (1.0 + (ag - 1.0) * ka_ref[...])
        kk_raw = k * kk_ref[...]
        h1, h2, h3 = _split3(lw)
        cs = jnp.dot(mk["tri_" + d], jnp.concatenate([h1, h2, h3], axis=1),
                     preferred_element_type=F32)
        logp = cs[:, :RWKV_DIM] + cs[:, RWKV_DIM:2 * RWKV_DIM] + cs[:, 2 * RWKV_DIM:]
        yield
        tot = logp[0:1] if d == "b" else logp[c - 1:c]
        e_inv = jnp.exp(-logp)
        e_hat = jnp.exp(tot - logp)
        rt = r * jnp.exp(logp)
        e_ex = jnp.exp(logp - lw)
        kt, kh = kd * e_inv, kd * e_hat
        ptot = jnp.exp(tot)
        for p in range(N_PAIRS):
            yield
            cols = slice(p * LANES, (p + 1) * LANES)
            kk = kk_raw[:, cols]
            kk = kk * lax.rsqrt(jnp.maximum(_head_sums(kk * kk, mk["ones_bd"]), 1e-24))
            kb = kk * ag[:, cols]
            tiles = {"at": -kk * e_ex[:, cols], "rt": rt[:, cols], "bt": kb * e_inv[:, cols],
                     "kt": kt[:, cols], "bh": kb * e_hat[:, cols], "kh": kh[:, cols],
                     "v": v[:, cols]}
            chain = (e * 2 + di) * N_PAIRS + p
            for j, name in enumerate(WKV_OPERANDS):
                ops_ref[slot, chain, j] = tiles[name]
            ptot_ref[slot, chain] = jnp.broadcast_to(ptot[:, cols], (SUBLANES, LANES))


def _wkv_kernel(rkv_f0_ref, lo_f0_ref, rkv_b0_ref, lo_b0_ref,
                rkv_f1_ref, lo_f1_ref, rkv_b1_ref, lo_b1_ref, s0f_ref, s0b_ref,
                w2_ref, w0_ref, a2_ref, a0_ref, kk_ref, ka_ref,
                yf_ref, yb_ref, sf_ref, sb_ref, st_ref, ops_ref, ptot_ref, *, nc):
    g = pl.program_id(0)
    ci = g % nc
    slot = g % 2
    mk = _wkv_masks()
    prm = (w2_ref, w0_ref, a2_ref, a0_ref, kk_ref, ka_ref, mk, ops_ref, ptot_ref)

    @pl.when(g == 0)
    def _():
        for _ in _wkv_prep((rkv_f0_ref, rkv_b0_ref), (lo_f0_ref, lo_b0_ref), *prm, 0):
            pass

    @pl.when(ci == 0)
    def _():
        for e in range(WKV_SEQS):
            st_ref[e, 0] = s0f_ref[e]
            st_ref[e, 1] = s0b_ref[e]

    chains = []
    for e in range(WKV_SEQS):
        for di, (d, y_ref) in enumerate((("f", yf_ref), ("b", yb_ref))):
            for p in range(N_PAIRS):
                chain = (e * 2 + di) * N_PAIRS + p
                ch = {"d": d, "e": e, "di": di, "p": p, "y_ref": y_ref, "st": st_ref[e, di, p],
                      "ptot": ptot_ref[slot, chain][0:1]}
                for j, name in enumerate(WKV_OPERANDS):
                    ch[name] = ops_ref[slot, chain, j]
                chains.append(ch)
    prep = _wkv_prep((rkv_f1_ref, rkv_b1_ref), (lo_f1_ref, lo_b1_ref), *prm, 1 - slot)
    for _ in _wkv_chains(chains, mk):
        for _ in range(WKV_SEQS):
            next(prep, None)
    for _ in prep:
        pass
    for ch in chains:
        ch["y_ref"][ch["e"], :, ch["p"] * LANES:(ch["p"] + 1) * LANES] = ch["y"]
        st_ref[ch["e"], ch["di"], ch["p"]] = ch["st_new"]

    @pl.when(ci == nc - 1)
    def _():
        for e in range(WKV_SEQS):
            for di, s_ref in enumerate((sf_ref, sb_ref)):
                for p in range(N_PAIRS):
                    st_t = st_ref[e, di, p].T
                    s_ref[e, 2 * p] = st_t[:HEAD_DIM, :HEAD_DIM]
                    s_ref[e, 2 * p + 1] = st_t[HEAD_DIM:, HEAD_DIM:]


def _wkv_call(rkv, lora, s0f, s0b, w2p, w0, a2p, a0, k_k, k_a, batch, seq_len):
    n = rkv.shape[0]
    nc = seq_len // CHUNK
    assert batch % WKV_SEQS == 0
    steps = (batch // WKV_SEQS) * nc
    rkv3 = rkv.reshape(batch, seq_len, 3 * RWKV_DIM)
    lora3 = lora.reshape(batch, seq_len, LORA_COLS)
    at = lambda g, back: (g // nc, (nc - 1 - g % nc) if back else g % nc, 0)
    fwd = lambda g: at(g, False)
    bwd = lambda g: at(g, True)
    fwd_first = lambda g: (0, 0, 0)
    bwd_first = lambda g: (0, nc - 1, 0)
    fwd_next = lambda g: at(jnp.minimum(g + 1, steps - 1), False)
    bwd_next = lambda g: at(jnp.minimum(g + 1, steps - 1), True)
    st = lambda g: (g // nc, 0, 0, 0)
    st_in = lambda g: ((g // nc) % (s0f.shape[0] // WKV_SEQS), 0, 0, 0)
    c2 = lambda g: (0, 0)
    c3 = lambda g: (0, 0, 0)
    st_shape = jax.ShapeDtypeStruct((batch, RWKV_HEADS, HEAD_DIM, HEAD_DIM), F32)
    y_shape = jax.ShapeDtypeStruct((batch, seq_len, RWKV_DIM), F32)
    n_chains = WKV_SEQS * 2 * N_PAIRS
    rkv_blk = (WKV_SEQS, CHUNK, 3 * RWKV_DIM)
    lora_blk = (WKV_SEQS, CHUNK, LORA_COLS)
    yf, yb, sf, sb = pl.pallas_call(
        functools.partial(_wkv_kernel, nc=nc),
        grid=(steps,),
        in_specs=[pl.BlockSpec(rkv_blk, fwd_first),
                  pl.BlockSpec(lora_blk, fwd_first),
                  pl.BlockSpec(rkv_blk, bwd_first),
                  pl.BlockSpec(lora_blk, bwd_first),
                  pl.BlockSpec(rkv_blk, fwd_next),
                  pl.BlockSpec(lora_blk, fwd_next),
                  pl.BlockSpec(rkv_blk, bwd_next),
                  pl.BlockSpec(lora_blk, bwd_next),
                  pl.BlockSpec((WKV_SEQS, N_PAIRS, LANES, LANES), st_in),
                  pl.BlockSpec((WKV_SEQS, N_PAIRS, LANES, LANES), st_in),
                  pl.BlockSpec((2, LANES, RWKV_DIM), c3),
                  pl.BlockSpec((2, RWKV_DIM), c2),
                  pl.BlockSpec((2, LANES, RWKV_DIM), c3),
                  pl.BlockSpec((2, RWKV_DIM), c2),
                  pl.BlockSpec((1, RWKV_DIM), c2),
                  pl.BlockSpec((1, RWKV_DIM), c2)],
        out_specs=[pl.BlockSpec((WKV_SEQS, CHUNK, RWKV_DIM), fwd),
                   pl.BlockSpec((WKV_SEQS, CHUNK, RWKV_DIM), bwd),
                   pl.BlockSpec((WKV_SEQS, RWKV_HEADS, HEAD_DIM, HEAD_DIM), st),
                   pl.BlockSpec((WKV_SEQS, RWKV_HEADS, HEAD_DIM, HEAD_DIM), st)],
        out_shape=[y_shape, y_shape, st_shape, st_shape],
        scratch_shapes=[pltpu.VMEM((WKV_SEQS, 2, N_PAIRS, LANES, LANES), F32),
                        pltpu.VMEM((2, n_chains, len(WKV_OPERANDS), CHUNK, LANES), F32),
                        pltpu.VMEM((2, n_chains, SUBLANES, LANES), F32)],
        compiler_params=_params("arbitrary"),
        name="wkv_chunked",
    )(rkv3, lora3, rkv3, lora3, rkv3, lora3, rkv3, lora3, s0f, s0b, w2p, w0, a2p, a0, k_k, k_a)
    return yf.reshape(n, RWKV_DIM), yb.reshape(n, RWKV_DIM), sf, sb


def _out_kernel(x_ref, mod_ref, conv_ref, yf_ref, yb_ref, rkv_ref, lo_ref,
                a2_ref, a0_ref, ka_ref, rk_ref, gng_ref, gnb_ref, g2_ref, wout_ref,
                postg_ref, preg_ref, rw_ref, rb_ref, x1_ref, h2_ref, route_ref, cnt_ref):
    @pl.when(pl.program_id(0) == 0)
    def _():
        cnt_ref[...] = jnp.zeros(cnt_ref.shape, F32)

    m = mod_ref[0]
    rkv = rkv_ref[...]
    lo = lo_ref[...]
    r = rkv[:, :RWKV_DIM]
    k = rkv[:, RWKV_DIM:2 * RWKV_DIM]
    v = rkv[:, 2 * RWKV_DIM:]
    ha = lo[:, LANES:2 * LANES]
    ag_f = _sigmoid(a0_ref[0:1, :] + _mm(ha, a2_ref[0]))
    ag_b = _sigmoid(a0_ref[1:2, :] + _mm(ha, a2_ref[1]))
    rkk = r * k * rk_ref[...] * (2.0 + (ag_f + ag_b - 2.0) * ka_ref[...])
    o = yf_ref[...] + yb_ref[...]
    gate = _mm(_sigmoid(lo[:, 2 * LANES:]), g2_ref[...])
    r2 = lax.broadcasted_iota(jnp.int32, (LANES, LANES), 0)
    c2 = lax.broadcasted_iota(jnp.int32, (LANES, LANES), 1)
    ones_bd = ((r2 // HEAD_DIM) == (c2 // HEAD_DIM)).astype(BF16)
    parts = []
    for p in range(N_PAIRS):
        cols = slice(p * LANES, (p + 1) * LANES)
        op = o[:, cols]
        mu = _head_sums(op, ones_bd) * (1.0 / HEAD_DIM)
        oc = op - mu
        var = _head_sums(oc * oc, ones_bd) * (1.0 / HEAD_DIM)
        on = oc * lax.rsqrt(var + GN_EPS) * gng_ref[:, cols] + gnb_ref[:, cols]
        bonus = _head_sums(rkk[:, cols], ones_bd) * v[:, cols]
        parts.append((on + bonus) * gate[:, cols])
    mix_in = jnp.concatenate([conv_ref[...]] + parts, axis=1)
    mix = jnp.dot(mix_in.astype(BF16), wout_ref[...], preferred_element_type=F32)
    x1 = x_ref[...] + m[2:3] * _rms(mix, postg_ref[...])
    x1_ref[...] = x1
    h2 = _rms(x1, preg_ref[...]) * (1.0 + m[4:5]) + m[3:4]
    h2_ref[...] = _pack_rows(h2)
    hh, hl = _split2(h2)
    dot = functools.partial(jnp.dot, preferred_element_type=F32)
    logits = dot(hh, rw_ref[0]) + (dot(hh, rw_ref[1]) + dot(hl, rw_ref[0])) + rb_ref[...]
    lane =lax.broadcasted_iota(jnp.int32, logits.shape, 1)
    work = logits
    picks, vals, idxs = [], [], []
    for _ in range(TOP_K):
        mx = jnp.max(work, axis=-1, keepdims=True)
        idx = jnp.min(jnp.where(work == mx, lane, LANES), axis=-1, keepdims=True)
        pick = lane == idx
        picks.append(pick)
        vals.append(mx)
        idxs.append(idx)
        work = jnp.where(pick, 2.0 * NEG_BIG, work)
    exps = [jnp.exp(val - vals[0]) for val in vals]
    den = exps[0] + exps[1] + exps[2] + exps[3]
    sel = jnp.zeros(logits.shape, F32)
    for pick in picks:
        sel = jnp.where(pick, 1.0, sel)
    tr = lax.broadcasted_iota(jnp.int32, (TM_OUT, TM_OUT), 0)
    tc = lax.broadcasted_iota(jnp.int32, (TM_OUT, TM_OUT), 1)
    before = (tc < tr).astype(BF16)
    rank = dot(before, sel.astype(BF16)) + cnt_ref[...]
    cnt_ref[...] = cnt_ref[...] + jnp.sum(sel, axis=0, keepdims=True)
    route = jnp.zeros(logits.shape, F32)
    for q, (pick, idx, e) in enumerate(zip(picks, idxs, exps)):
        rank_q = jnp.sum(jnp.where(pick, rank, 0.0), axis=-1, keepdims=True)
        route = jnp.where(lane == q, idx.astype(F32), route)
        route = jnp.where(lane == TOP_K + q, e / den, route)
        route = jnp.where(lane == 2 * TOP_K + q, rank_q, route)
    route_ref[...] = route


def _out_call(x, mod, conv_out, yf, yb, rkv, lora, wts, seq_len):
    n, d = x.shape
    row = lambda i: (i, 0)
    c2 = lambda i: (0, 0)
    c3 = lambda i: (0, 0, 0)
    full = lambda a: pl.BlockSpec(a.shape, c3 if a.ndim == 3 else c2)
    consts = [wts["a2p"], wts["a0"], wts["k_a"], wts["r_k"], wts["gn_g"], wts["gn_b"], wts["g2"],
              wts["w_out"], wts["post_mix_g"], wts["pre_ffn_g"], wts["router_w"], wts["router_b"]]
    return pl.pallas_call(
        _out_kernel,
        grid=(n // TM_OUT,),
        in_specs=[pl.BlockSpec((TM_OUT, d), row),
                  pl.BlockSpec((1, N_MOD, d), _mod_index(TM_OUT, seq_len, mod.shape[0])),
                  pl.BlockSpec((TM_OUT, CONV_CH), row),
                  pl.BlockSpec((TM_OUT, RWKV_DIM), row),
                  pl.BlockSpec((TM_OUT, RWKV_DIM), row),
                  pl.BlockSpec((TM_OUT, 3 * RWKV_DIM), row),
                  pl.BlockSpec((TM_OUT, LORA_COLS), row)] + [full(a) for a in consts],
        out_specs=[pl.BlockSpec((TM_OUT, d), row),
                   pl.BlockSpec((TM_OUT, d // 2), row),
                   pl.BlockSpec((TM_OUT, LANES), row),
                   pl.BlockSpec((1, LANES), c2)],
        out_shape=[jax.ShapeDtypeStruct((n, d), F32),
                   jax.ShapeDtypeStruct((n, d // 2), jnp.int32),
                   jax.ShapeDtypeStruct((n, LANES), F32),
                   jax.ShapeDtypeStruct((1, LANES), F32)],
        compiler_params=_params("arbitrary"),
        name="out_router",
    )(x, mod, conv_out, yf, yb, rkv, lora, *consts)


def _sc_mesh():
    return plsc.VectorSubcoreMesh(core_axis_name="c", subcore_axis_name="s",
                                  num_cores=SC_CORES, num_subcores=SC_SUBCORES)


def _sc_worker_id():
    return lax.axis_index("s") * SC_CORES + lax.axis_index("c")


def _sc_scatter_rows(src_a, src_b, idx, n_out):
    (n_a, d), n_b = src_a.shape, src_b.shape[0]
    n = n_a + n_b
    workers = SC_CORES * SC_SUBCORES
    per_worker = n // workers
    assert idx.shape[0] == TOP_K * n and per_worker * workers == n
    assert per_worker % SC_ROWS == 0 and n_a % per_worker == 0

    @functools.partial(
        pl.kernel, mesh=_sc_mesh(),
        out_type=jax.ShapeDtypeStruct((n_out, d), src_a.dtype),
        scratch_types=[pltpu.VMEM((SC_ROWS,), jnp.int32),
                       pltpu.VMEM((SC_ROWS, d), src_a.dtype),
                       pltpu.SemaphoreType.DMA],
        name="sc_scatter_rows")
    def scatter(a_hbm, b_hbm, idx_hbm, out_hbm, idx_v, rows_v, sem):
        base = _sc_worker_id() * per_worker

        @pl.loop(0, per_worker // SC_ROWS)
        def _(j):
            tok = base + j * SC_ROWS

            @pl.when(tok < n_a)
            def _():
                pltpu.sync_copy(a_hbm.at[pl.ds(tok, SC_ROWS)], rows_v)

            @pl.when(tok >= n_a)
            def _():
                pltpu.sync_copy(b_hbm.at[pl.ds(tok - n_a, SC_ROWS)], rows_v)

            for q in range(TOP_K):
                pltpu.sync_copy(idx_hbm.at[pl.ds(q * n + tok, SC_ROWS)], idx_v)
                pltpu.async_copy(rows_v, out_hbm.at[idx_v], sem).wait()

    return scatter(src_a, src_b, idx)


def _sc_gather_rows(src, idx):
    n_idx, d = idx.shape[0], src.shape[1]
    workers = SC_CORES * SC_SUBCORES
    per_worker = n_idx // workers
    assert per_worker * workers == n_idx and per_worker % SC_ROWS == 0

    @functools.partial(
        pl.kernel, mesh=_sc_mesh(),
        out_type=jax.ShapeDtypeStruct((n_idx, d), src.dtype),
        scratch_types=[pltpu.VMEM((SC_ROWS,), jnp.int32),
                       pltpu.VMEM((SC_ROWS, d), src.dtype),
                       pltpu.SemaphoreType.DMA],
        name="sc_gather_rows")
    def gather(src_hbm, idx_hbm, out_hbm, idx_v, rows_v, sem):
        base = _sc_worker_id() * per_worker

        @pl.loop(0, per_worker // SC_ROWS)
        def _(j):
            off = base + j * SC_ROWS
            pltpu.sync_copy(idx_hbm.at[pl.ds(off, SC_ROWS)], idx_v)
            pltpu.async_copy(src_hbm.at[idx_v], rows_v, sem).wait()
            pltpu.sync_copy(rows_v, out_hbm.at[pl.ds(off, SC_ROWS)])

    return gather(src, idx)


def _moe_kernel(blk_e_ref, blk_first_ref, blk_rows_ref, blk_next_ref,
                xs_ref, wgu_hbm, bgu_ref, wd_hbm, bd_ref,
                o_ref, wgu_f32_ref, wd_f32_ref, wgu_bf_ref, wd_bf_ref, sem):
    b = pl.program_id(0)
    n_rows = blk_rows_ref[b]

    def weight_copies(e):
        return (pltpu.make_async_copy(wgu_hbm.at[e], wgu_f32_ref, sem.at[0]),
                pltpu.make_async_copy(wd_hbm.at[e], wd_f32_ref, sem.at[1]))

    @pl.when(b == 0)
    def _():
        for copy in weight_copies(blk_e_ref[0]):
            copy.start()

    @pl.when(blk_first_ref[b] == 1)
    def _():
        for copy in weight_copies(blk_e_ref[b]):
            copy.wait()
        wgu_bf_ref[...] = wgu_f32_ref[...].astype(BF16)
        wd_bf_ref[...] = wd_f32_ref[...].astype(BF16)

        @pl.when(blk_next_ref[b] >= 0)
        def _():
            for copy in weight_copies(blk_next_ref[b]):
                copy.start()

    @pl.when(n_rows > 0)
    def _():
        live = lax.broadcasted_iota(jnp.int32, xs_ref.shape, 0) < n_rows
        xs = _unpack_rows(jnp.where(live, xs_ref[...], 0))
        gu = jnp.dot(xs.astype(BF16), wgu_bf_ref[...],
                     preferred_element_type=F32) + bgu_ref[0]
        x_glu = jnp.minimum(gu[:, :D_FF], SWIGLU_LIMIT)
        x_lin = jnp.clip(gu[:, D_FF:], -SWIGLU_LIMIT, SWIGLU_LIMIT)
        act = x_glu * _sigmoid(SWIGLU_ALPHA * x_glu) * (x_lin + 1.0)
        o_ref[...] = _pack_rows(jnp.dot(act.astype(BF16), wd_bf_ref[...],
                                        preferred_element_type=F32) + bd_ref[0])

    @pl.when(n_rows == 0)
    def _():
        o_ref[...] = jnp.zeros(o_ref.shape, jnp.int32)


def _moe_call(blk_e, blk_first, blk_rows, blk_next, xs, wgu, bgu, wd, bd):
    n_slots, half = xs.shape
    d = 2 * half
    ex = lambda b, be, bf, br, bn: (be[b], 0, 0)
    row = lambda b, be, bf, br, bn: (b, 0)
    grid_spec = pltpu.PrefetchScalarGridSpec(
        num_scalar_prefetch=4,
        grid=(n_slots // MOE_BLOCK,),
        in_specs=[pl.BlockSpec((MOE_BLOCK, half), row),
                  pl.BlockSpec(memory_space=pl.ANY),
                  pl.BlockSpec((1, 1, 2 * D_FF), ex),
                  pl.BlockSpec(memory_space=pl.ANY),
                  pl.BlockSpec((1, 1, d), ex)],
        out_specs=pl.BlockSpec((MOE_BLOCK, half), row),
        scratch_shapes=[pltpu.VMEM((d, 2 * D_FF), F32), pltpu.VMEM((D_FF, d), F32),
                        pltpu.VMEM((d, 2 * D_FF), BF16), pltpu.VMEM((D_FF, d), BF16),
                        pltpu.SemaphoreType.DMA((2,))])
    return pl.pallas_call(
        _moe_kernel,
        grid_spec=grid_spec,
        out_shape=jax.ShapeDtypeStruct((n_slots, half), jnp.int32),
        compiler_params=_params("arbitrary"),
        name="moe_experts",
    )(blk_e, blk_first, blk_rows, blk_next, xs, wgu, bgu, wd, bd)


def _final_kernel(x1_ref, y4_ref, route_ref, mod_ref, g_ref, o_ref):
    m = mod_ref[0]
    route = route_ref[...]
    y = jnp.zeros(x1_ref.shape, F32)
    for q in range(TOP_K):
        y = y + route[:, TOP_K + q:TOP_K + q + 1] * _unpack_rows(y4_ref[q])
    o_ref[...] = x1_ref[...] + m[5:6] * _rms(y, g_ref[...])


def _final_call(x1, y4, route, mod, g, seq_len):
    n, d = x1.shape
    row = lambda i: (i, 0)
    return pl.pallas_call(
        _final_kernel,
        grid=(n // TM_FIN,),
        in_specs=[pl.BlockSpec((TM_FIN, d), row),
                  pl.BlockSpec((TOP_K, TM_FIN, d // 2), lambda i: (0, i, 0)),
                  pl.BlockSpec((TM_FIN, LANES), row),
                  pl.BlockSpec((1, N_MOD, d), _mod_index(TM_FIN, seq_len, mod.shape[0])),
                  pl.BlockSpec((1, d), lambda i: (0, 0))],
        out_specs=pl.BlockSpec((TM_FIN, d), row),
        out_shape=jax.ShapeDtypeStruct((n, d), F32),
        compiler_params=_params("arbitrary"),
        name="final_residual",
    )(x1, y4, route, mod, g)


def _pack_state(s):
    b = s.shape[0]
    st = jnp.swapaxes(s.astype(F32), -1, -2).reshape(b, N_PAIRS, 2, HEAD_DIM, HEAD_DIM)
    z = jnp.zeros_like(st[:, :, 0])
    top = jnp.concatenate([st[:, :, 0], z], axis=-1)
    bot = jnp.concatenate([z, st[:, :, 1]], axis=-1)
    return jnp.concatenate([top, bot], axis=-2)


def _mix_sublayer(x, mod, s0f, s0b, seg_len, wts):
    batch, seq_len, d = x.shape
    x2 = x.reshape(batch * seq_len, d)
    cv, rkv, lora = _in_call(x2, mod, wts["pre_mix_g"], wts["w_all"], seq_len)
    conv_out = _conv_call(cv, wts["conv_w"], wts["conv_b"], wts["conv_ln_g"], wts["conv_ln_b"],
                          seg_len)
    yf, yb, sf, sb = _wkv_call(rkv, lora, s0f, s0b, wts["w2p"], wts["w0"], wts["a2p"], wts["a0"],
                               wts["k_k"], wts["k_a"], batch, seq_len)
    x1, h2, route, cnt = _out_call(x2, mod, conv_out, yf, yb, rkv, lora, wts, seq_len)
    return x1, h2, route, cnt, sf, sb


def _dispatch_plan(routes, cnts):
    experts = jnp.arange(N_EXPERTS, dtype=jnp.int32)
    counts = [c[0, :N_EXPERTS].astype(jnp.int32) for c in cnts]
    total = sum(counts)
    padded = (total + MOE_BLOCK - 1) // MOE_BLOCK * MOE_BLOCK
    pad_end = jnp.cumsum(padded)
    pad_start = pad_end - padded
    dests = []
    prior = jnp.zeros_like(total)
    for route, count in zip(routes, counts):
        idx = route[:, :TOP_K].astype(jnp.int32)
        rank = route[:, 2 * TOP_K:3 * TOP_K].astype(jnp.int32)
        first = jnp.sum(jnp.where(idx[..., None] == experts, pad_start + prior, 0), axis=-1)
        dests.append(first + rank)
        prior = prior + count
    dest = jnp.concatenate(dests, axis=0).T.reshape(-1)
    n_slots = dest.shape[0] + N_EXPERTS * MOE_BLOCK
    blk_row0 = jnp.arange(n_slots // MOE_BLOCK, dtype=jnp.int32) * MOE_BLOCK
    blk_e = jnp.sum((pad_end[None, :] <= blk_row0[:, None]).astype(jnp.int32), axis=-1)
    blk_e = jnp.minimum(blk_e, N_EXPERTS - 1)
    blk_first = jnp.concatenate([jnp.ones((1,), jnp.int32),
                                 (blk_e[1:] != blk_e[:-1]).astype(jnp.int32)])
    of_blk = lambda per_expert: jnp.sum(
        jnp.where(blk_e[:, None] == experts[None, :], per_expert[None, :], 0), axis=-1)
    blk_rows = jnp.clip(of_blk(pad_start + total) - blk_row0, 0, MOE_BLOCK).astype(jnp.int32)
    pos = jnp.arange(blk_e.shape[0], dtype=jnp.int32)
    later_first = (pos[None, :] > pos[:, None]) & (blk_first[None, :] == 1)
    next_pos = jnp.min(jnp.where(later_first, pos[None, :], pos.shape[0]), axis=-1)
    blk_next = jnp.where(next_pos < pos.shape[0],
                         blk_e[jnp.minimum(next_pos, pos.shape[0] - 1)], -1).astype(jnp.int32)
    return dest, n_slots, blk_e, blk_first, blk_rows, blk_next


def _routed_ffn(h2s, routes, cnts, wts):
    dest, n_slots, blk_e, blk_first, blk_rows, blk_next = _dispatch_plan(routes, cnts)
    xs = _sc_scatter_rows(h2s[0], h2s[1], dest, n_slots)
    out_sorted = _moe_call(blk_e, blk_first, blk_rows, blk_next, xs, wts["w_gu"], wts["b_gu"],
                           wts["w_down"], wts["b_down"])
    dest_qt = dest.reshape(TOP_K, -1)
    y4s, row0 = [], 0
    for h2 in h2s:
        n_path = h2.shape[0]
        y4 = _sc_gather_rows(out_sorted, dest_qt[:, row0:row0 + n_path].reshape(-1))
        y4s.append(y4.reshape(TOP_K, n_path, xs.shape[1]))
        row0 += n_path
    return y4s


def _prep_weights(l, pre_mix_g, post_mix_g, pre_ffn_g, post_ffn_g, w_in, w_out, conv_w, conv_b,
                  conv_ln_g, conv_ln_b, rw_w0, rw_w1, rw_w2, rw_a0, rw_a1, rw_a2, rw_g1, rw_g2,
                  rw_k_k, rw_k_a, rw_r_k, rw_gn_g, rw_gn_b, router_w, router_b, w_gu, b_gu,
                  w_down, b_down):
    row = lambda a: a.reshape(1, -1).astype(F32)
    zpad = jnp.zeros((HEAD_DIM, RWKV_DIM), F32)
    w_all = jnp.concatenate([w_in[l], rw_w1[l, 0], rw_w1[l, 1], rw_a1[l, 0], rw_a1[l, 1],
                             rw_g1[l]], axis=1).astype(BF16)
    w2p = jnp.stack([jnp.concatenate([rw_w2[l, 0], zpad], axis=0),
                     jnp.concatenate([zpad, rw_w2[l, 1]], axis=0)]).astype(BF16)
    a2p = jnp.stack([jnp.concatenate([rw_a2[l, 0], zpad], axis=0),
                     jnp.concatenate([zpad, rw_a2[l, 1]], axis=0)]).astype(BF16)
    rw_pad = jnp.pad(router_w[l].astype(F32), ((0, 0), (0, LANES - N_EXPERTS)))
    rw_hi = rw_pad.astype(BF16)
    rw_lo = (rw_pad - rw_hi.astype(F32)).astype(BF16)
    rb_pad = jnp.concatenate([router_b[l].astype(F32),
                              jnp.full((LANES - N_EXPERTS,), NEG_BIG, F32)]).reshape(1, LANES)
    return {
        "pre_mix_g": row(pre_mix_g[l]), "post_mix_g": row(post_mix_g[l]),
        "pre_ffn_g": row(pre_ffn_g[l]), "post_ffn_g": row(post_ffn_g[l]),
        "w_all": w_all, "w_out": w_out[l].astype(BF16),
        "conv_w": jnp.broadcast_to(conv_w[l].astype(F32)[:, None, :],
                                   (CONV_WIDTH, SUBLANES, CONV_CH)),
        "conv_b": row(conv_b[l]),
        "conv_ln_g": row(conv_ln_g[l]), "conv_ln_b": row(conv_ln_b[l]),
        "w2p": w2p, "w0": rw_w0[l].astype(F32), "a2p": a2p, "a0": rw_a0[l].astype(F32),
        "k_k": row(rw_k_k[l]), "k_a": row(rw_k_a[l]), "r_k": row(rw_r_k[l]),
        "gn_g": row(rw_gn_g[l]), "gn_b": row(rw_gn_b[l]), "g2": rw_g2[l].astype(BF16),
        "router_w": jnp.stack([rw_hi, rw_lo]), "router_b": rb_pad,
        "w_gu": w_gu[l].astype(F32), "b_gu": b_gu[l].reshape(N_EXPERTS, 1, -1).astype(F32),
        "w_down": w_down[l].astype(F32), "b_down": b_down[l].reshape(N_EXPERTS, 1, -1).astype(F32),
    }


def kernel(x_prompt, x_sample, state_wkv_fwd, state_wkv_bwd, c, c_ctx, ada_w, ada_b, pre_mix_g, post_mix_g, pre_ffn_g, post_ffn_g, w_in, w_out, conv_w, conv_b, conv_ln_g, conv_ln_b, rw_w0, rw_w1, rw_w2, rw_a0, rw_a1, rw_a2, rw_g1, rw_g2, rw_k_k, rw_k_a, rw_r_k, rw_gn_g, rw_gn_b, router_w, router_b, w_gu, b_gu, w_down, b_down):
    depth = ada_w.shape[0]
    dec_batch = c.shape[0]
    mod_rows = 16
    c_rows = jnp.concatenate([c, c_ctx[None, :],
                              jnp.zeros((mod_rows - dec_batch - 1, D_MODEL), F32)], axis=0)
    zero_state = jnp.zeros((WKV_SEQS, N_PAIRS, LANES, LANES), F32)
    y_prompt, y_sample = x_prompt, x_sample
    new_f, new_b = [], []
    for l in range(depth):
        wts = _prep_weights(l, pre_mix_g, post_mix_g, pre_ffn_g, post_ffn_g, w_in, w_out, conv_w,
                            conv_b, conv_ln_g, conv_ln_b, rw_w0, rw_w1, rw_w2, rw_a0, rw_a1, rw_a2,
                            rw_g1, rw_g2, rw_k_k, rw_k_a, rw_r_k, rw_gn_g, rw_gn_b, router_w,
                            router_b, w_gu, b_gu, w_down, b_down)
        mod = _mod_call(c_rows, ada_w[l], ada_b[l]).reshape(mod_rows, N_MOD, D_MODEL)
        mod_lat = mod[:dec_batch]
        mod_ctx = mod[dec_batch:dec_batch + 1]
        ctx = _mix_sublayer(y_prompt, mod_ctx, zero_state, zero_state, y_prompt.shape[1], wts)
        lat = _mix_sublayer(y_sample, mod_lat, _pack_state(state_wkv_fwd[:, l]),
                            _pack_state(state_wkv_bwd[:, l]), GRID_W, wts)
        new_f.append(ctx[4])
        new_b.append(ctx[5])
        y4_ctx, y4_lat = _routed_ffn([ctx[1], lat[1]], [ctx[2], lat[2]], [ctx[3], lat[3]], wts)
        y_prompt = _final_call(ctx[0], y4_ctx, ctx[2], mod_ctx, wts["post_ffn_g"],
                               y_prompt.shape[1]).reshape(y_prompt.shape)
        y_sample = _final_call(lat[0], y4_lat, lat[2], mod_lat, wts["post_ffn_g"],
                               y_sample.shape[1]).reshape(y_sample.shape)
    return (y_prompt, y_sample, jnp.stack(new_f, axis=1), jnp.stack(new_b, axis=1))
```

```python
import functools
import math

import jax
import jax.numpy as jnp
from jax import lax
from jax.experimental import pallas as pl
from jax.experimental.pallas import tpu as pltpu
from jax.experimental.pallas import tpu_sc as plsc

F32 = jnp.float32
BF16 = jnp.bfloat16

D_MODEL = 1024
CONV_CH = 512
RWKV_DIM = 512
HEAD_DIM = 64
RWKV_HEADS = 8
N_PAIRS = RWKV_HEADS // 2
CONV_WIDTH = 31
N_EXPERTS = 32
TOP_K = 4
D_FF = 1024
SWIGLU_LIMIT = 7.0
SWIGLU_ALPHA = 1.702
RMS_EPS = 1e-6
LN_EPS = 1e-5
GN_EPS = 64e-5
N_MOD = 6
GRID_W = 64

LANES = 128
SUBLANES = 8
CHUNK = 64
WKV_SEQS = 4
LORA_COLS = 384
TM_IN = 1024
TM_CONV = 512
TM_OUT = 512
TM_FIN = 1024
MOE_BLOCK = 512
SC_CORES = 2
SC_SUBCORES = 16
SC_ROWS = 128
NEG_BIG = -1e30
EXP_M05 = math.exp(-0.5)
VMEM_LIMIT = 56 * 1024 * 1024


def _sigmoid(x):
    return 1.0 / (1.0 + jnp.exp(-x))


def _mm(a, b):
    return jnp.dot(a.astype(BF16), b.astype(BF16), preferred_element_type=F32)


def _mm_nt(a, b):
    return lax.dot_general(a.astype(BF16), b.astype(BF16), (((1,), (1,)), ((), ())),
                           preferred_element_type=F32)


def _mm_tn(a, b):
    return lax.dot_general(a.astype(BF16), b.astype(BF16), (((0,), (0,)), ((), ())),
                           preferred_element_type=F32)


def _split2(x):
    hi = x.astype(BF16)
    lo = (x - hi.astype(F32)).astype(BF16)
    return hi, lo


def _split3(x):
    h1 = x.astype(BF16)
    r1 = x - h1.astype(F32)
    h2 = r1.astype(BF16)
    h3 = (r1 - h2.astype(F32)).astype(BF16)
    return h1, h2, h3


def _pack_rows(x):
    n = x.shape[1] // 2
    bits = lambda v: lax.bitcast_convert_type(v.astype(BF16).astype(F32), jnp.uint32)
    word = (bits(x[:, n:]) & jnp.uint32(0xFFFF0000)) | (bits(x[:, :n]) >> 16)
    return lax.bitcast_convert_type(word, jnp.int32)


def _unpack_rows(w):
    u = lax.bitcast_convert_type(w, jnp.uint32)
    lo = lax.bitcast_convert_type(u << 16, F32)
    hi = lax.bitcast_convert_type(u & jnp.uint32(0xFFFF0000), F32)
    return jnp.concatenate([lo, hi], axis=1)


def _rms(x, g):
    return x * lax.rsqrt(jnp.mean(x * x, axis=-1, keepdims=True) + RMS_EPS) * g


def _params(*sem):
    return pltpu.CompilerParams(dimension_semantics=sem, vmem_limit_bytes=VMEM_LIMIT)


def _mod_kernel(c_ref, w_ref, b_ref, o_ref):
    c = c_ref[...]
    s1, s2 = _split2(c * _sigmoid(c))
    w1, w2 = _split2(w_ref[...])
    dot = functools.partial(jnp.dot, preferred_element_type=F32)
    o_ref[...] = dot(s1, w1) + (dot(s1, w2) + dot(s2, w1)) + b_ref[...]


def _mod_call(c_rows, ada_w, ada_b):
    m, d = c_rows.shape
    n = ada_w.shape[1]
    tn = 512
    return pl.pallas_call(
        _mod_kernel,
        grid=(n // tn,),
        in_specs=[pl.BlockSpec((m, d), lambda j: (0, 0)),
                  pl.BlockSpec((d, tn), lambda j: (0, j)),
                  pl.BlockSpec((1, tn), lambda j: (0, j))],
        out_specs=pl.BlockSpec((m, tn), lambda j: (0, j)),
        out_shape=jax.ShapeDtypeStruct((m, n), F32),
        compiler_params=_params("arbitrary"),
        name="mod",
    )(c_rows, ada_w, ada_b.reshape(1, n))


def _in_kernel(x_ref, mod_ref, g_ref, w_ref, cv_ref, rkv_ref, lo_ref):
    m = mod_ref[0]
    h = _rms(x_ref[...], g_ref[...]) * (1.0 + m[1:2]) + m[0:1]
    proj = jnp.dot(h.astype(BF16), w_ref[...], preferred_element_type=F32)
    cv_ref[...] = proj[:, :CONV_CH] * _sigmoid(proj[:, CONV_CH:2 * CONV_CH])
    rkv_ref[...] = proj[:, 2 * CONV_CH:2 * CONV_CH + 3 * RWKV_DIM]
    lo_ref[...] = proj[:, 2 * CONV_CH + 3 * RWKV_DIM:]


def _mod_index(tile, seq_len, n_mod_rows):
    def index(i):
        return ((i * tile) // seq_len) % n_mod_rows, 0, 0
    return index


def _in_call(x, mod, g, w_all, seq_len):
    n, d = x.shape
    ncol = w_all.shape[1]
    return pl.pallas_call(
        _in_kernel,
        grid=(n // TM_IN,),
        in_specs=[pl.BlockSpec((TM_IN, d), lambda i: (i, 0)),
                  pl.BlockSpec((1, N_MOD, d), _mod_index(TM_IN, seq_len, mod.shape[0])),
                  pl.BlockSpec((1, d), lambda i: (0, 0)),
                  pl.BlockSpec((d, ncol), lambda i: (0, 0))],
        out_specs=[pl.BlockSpec((TM_IN, CONV_CH), lambda i: (i, 0)),
                   pl.BlockSpec((TM_IN, 3 * RWKV_DIM), lambda i: (i, 0)),
                   pl.BlockSpec((TM_IN, LORA_COLS), lambda i: (i, 0))],
        out_shape=[jax.ShapeDtypeStruct((n, CONV_CH), F32),
                   jax.ShapeDtypeStruct((n, 3 * RWKV_DIM), F32),
                   jax.ShapeDtypeStruct((n, LORA_COLS), F32)],
        compiler_params=_params("arbitrary"),
        name="in_proj",
    )(x, mod, g, w_all)


CONV_HALO = 16
CONV_ROWS = 32


def _conv_kernel(cv_ref, w_ref, b_ref, g_ref, be_ref, o_ref, pad_ref, acc_ref, *, seg_len):
    nseg = TM_CONV // seg_len
    stride = seg_len + 2 * CONV_HALO
    rows = nseg * stride
    u = cv_ref[...]
    pad_ref[0] = jnp.zeros(pad_ref.shape[1:], F32)
    for s in range(nseg):
        pad_ref[0, s * stride + CONV_HALO:s * stride + CONV_HALO + seg_len, :] = (
            u[s * seg_len:(s + 1) * seg_len])
    for r in range(1, SUBLANES):
        pad_ref[r, 0:rows, :] = pad_ref[0, r:r + rows, :]
    first = CONV_HALO - CONV_WIDTH // 2
    chunks_per_seg = seg_len // CONV_ROWS

    def row_chunk(i, carry):
        pad0 = i * CONV_ROWS + (i // chunks_per_seg) * (2 * CONV_HALO)
        accs = [jnp.zeros((SUBLANES, CONV_CH), F32) for _ in range(CONV_ROWS // SUBLANES)]
        for t in range(CONV_WIDTH):
            q, r = divmod(first + t, SUBLANES)
            w_t = w_ref[t]
            for j in range(len(accs)):
                start = pl.multiple_of(pad0 + (q + j) * SUBLANES, SUBLANES)
                accs[j] = accs[j] + w_t * pad_ref[r, pl.ds(start, SUBLANES), :]
        for j, acc in enumerate(accs):
            row0 = pl.multiple_of(i * CONV_ROWS + j * SUBLANES, SUBLANES)
            acc_ref[pl.ds(row0, SUBLANES), :] = acc
        return carry

    lax.fori_loop(0, TM_CONV // CONV_ROWS, row_chunk, 0)
    y = acc_ref[...] + b_ref[...]
    mu = jnp.mean(y, axis=-1, keepdims=True)
    yc = y - mu
    var = jnp.mean(yc * yc, axis=-1, keepdims=True)
    z = yc * lax.rsqrt(var + LN_EPS) * g_ref[...] + be_ref[...]
    o_ref[...] = z * _sigmoid(z)


def _conv_call(cv, conv_w, conv_b, ln_g, ln_b, seg_len):
    n = cv.shape[0]
    nseg = TM_CONV // seg_len
    row = lambda i: (i, 0)
    const = lambda i: (0, 0)
    return pl.pallas_call(
        functools.partial(_conv_kernel, seg_len=seg_len),
        grid=(n // TM_CONV,),
        in_specs=[pl.BlockSpec((TM_CONV, CONV_CH), row),
                  pl.BlockSpec((CONV_WIDTH, SUBLANES, CONV_CH), lambda i: (0, 0, 0)),
                  pl.BlockSpec((1, CONV_CH), const),
                  pl.BlockSpec((1, CONV_CH), const),
                  pl.BlockSpec((1, CONV_CH), const)],
        out_specs=pl.BlockSpec((TM_CONV, CONV_CH), row),
        out_shape=jax.ShapeDtypeStruct((n, CONV_CH), F32),
        scratch_shapes=[pltpu.VMEM((SUBLANES, nseg * (seg_len + 2 * CONV_HALO) + SUBLANES, CONV_CH),
                                   F32),
                        pltpu.VMEM((TM_CONV, CONV_CH), F32)],
        compiler_params=_params("arbitrary"),
        name="conv_module",
    )(cv, conv_w, conv_b, ln_g, ln_b)


def _head_sums(x, ones_bd):
    hi, lo = _split2(x)
    return jnp.dot(jnp.concatenate([hi, lo], axis=1), jnp.concatenate([ones_bd, ones_bd], axis=0),
                   preferred_element_type=F32)


def _wkv_masks():
    c = CHUNK
    r2 = lax.broadcasted_iota(jnp.int32, (2 * c, 2 * c), 0)
    c2 = lax.broadcasted_iota(jnp.int32, (2 * c, 2 * c), 1)
    same = (r2 // c) == (c2 // c)
    rr, cc = r2 % c, c2 % c
    r1 = lax.broadcasted_iota(jnp.int32, (c, 2 * c), 0)
    c1 = lax.broadcasted_iota(jnp.int32, (c, 2 * c), 1)
    cc1 = c1 % c
    rs = lax.broadcasted_iota(jnp.int32, (c, c), 0)
    cs = lax.broadcasted_iota(jnp.int32, (c, c), 1)
    masks = {"same": same, "eye": r2 == c2, "left": c1 < c,
             "ones_bd": same.astype(BF16)}
    for name, lt in (("f", lambda a, b: a < b), ("b", lambda a, b: a > b)):
        le = (lambda a, b, lt=lt: lt(a, b) | (a == b))
        masks["strict_" + name] = same & lt(cc, rr)
        masks["inc_" + name] = le(cc1, r1)
        masks["ak0_" + name] = (c1 >= c) & lt(cc1, r1)
        masks["ak1_" + name] = (c1 < c) & lt(cc1, r1)
        masks["tri_" + name] = le(cs, rs).astype(BF16)
    return masks


def _wkv_chains(chains, mk):
    c = CHUNK
    left = mk["left"]
    for ch in chains:
        zero = jnp.zeros_like(ch["at"])
        ch["at0"], ch["at1"] = jnp.where(left, ch["at"], zero), jnp.where(left, zero, ch["at"])
        rt0, rt1 = jnp.where(left, ch["rt"], zero), jnp.where(left, zero, ch["rt"])
        bt, kt = ch["bt"].astype(BF16), ch["kt"].astype(BF16)
        ch["g0"] = _mm_nt(jnp.concatenate([ch["at0"], rt0], axis=0), jnp.concatenate([bt, kt], axis=0))
        ch["g1"] = _mm_nt(jnp.concatenate([ch["at1"], rt1], axis=0), jnp.concatenate([kt, bt], axis=0))
    yield
    for ch in chains:
        g0, g1 = ch["g0"], ch["g1"]
        lmat = jnp.where(mk["strict_" + ch["d"]], jnp.concatenate([g0[:c], g1[:c]], axis=0), 0.0)
        ch["tinv"] = jnp.where(mk["eye"], 1.0, lmat)
        ch["lpow"] = _mm(lmat, lmat)
    for step in range(5):
        yield
        for ch in chains:
            lp = ch["lpow"].astype(BF16)
            if step < 4:
                prod = jnp.dot(lp, jnp.concatenate([lp, ch["tinv"].astype(BF16)], axis=1),
                               preferred_element_type=F32)
                ch["lpow"] = prod[:, :LANES]
                ch["tinv"] = ch["tinv"] + prod[:, LANES:]
            else:
                ch["tinv"] = ch["tinv"] + _mm(lp, ch["tinv"])
    yield
    for ch in chains:
        d, v = ch["d"], ch["v"]
        zv = jnp.zeros_like(v)
        ak0 = jnp.where(mk["ak0_" + d], ch["g0"][:c], 0.0)
        ak1 = jnp.where(mk["ak1_" + d], ch["g1"][:c], 0.0)
        ch["x0"] = _mm(ak0, jnp.concatenate([zv, v], axis=0))
        ch["x1"] = _mm(ak1, jnp.concatenate([v, zv], axis=0))
    yield
    for ch in chains:
        zero = jnp.zeros_like(ch["x0"])
        rhs = jnp.concatenate([
            jnp.concatenate([ch["at0"], jnp.where(left, ch["x0"], zero)], axis=1),
            jnp.concatenate([ch["at1"], jnp.where(left, zero, ch["x1"])], axis=1)], axis=0)
        ch["aw_blk"] = _mm(ch["tinv"], rhs)
    yield
    for ch in chains:
        d, v, aw_blk = ch["d"], ch["v"], ch["aw_blk"]
        zv = jnp.zeros_like(v)
        aw = aw_blk[:c] + aw_blk[c:]
        inc = mk["inc_" + d]
        g0, g1 = ch["g0"], ch["g1"]
        rb = jnp.where(inc, jnp.where(left, g0[c:], g1[c:]), 0.0)
        rk = jnp.where(inc, jnp.where(left, g1[c:], g0[c:]), 0.0)
        v0, v1 = jnp.where(left, v, zv), jnp.where(left, zv, v)
        vblk = jnp.concatenate([jnp.zeros((2 * c, LANES), F32),
                                jnp.concatenate([v1, v0], axis=0)], axis=1)
        qz = _mm(jnp.concatenate([rb, rk], axis=1), jnp.concatenate([aw_blk, vblk], axis=0))
        ch["q"] = ch["rt"] + qz[:, :LANES]
        ch["zz"] = qz[:, LANES:]
        rhs3 = jnp.concatenate([aw, jnp.concatenate([zv, v], axis=1)], axis=0)
        ch["mn"] = _mm_tn(jnp.concatenate([ch["bh"], ch["kh"]], axis=0), rhs3)
    yield
    same = mk["same"]
    for ch in chains:
        mn = ch["mn"]
        mt = jnp.where(same, mn[:, :LANES], 0.0) + jnp.where(mk["eye"], ch["ptot"], 0.0)
        nt = jnp.where(same, mn[:, LANES:], 0.0)
        both = _mm(jnp.concatenate([ch["q"], mt], axis=0), ch["st"])
        ch["y"] = both[:c] + ch["zz"]
        ch["st_new"] = both[c:] + nt


WKV_OPERANDS = ("at", "rt", "bt", "kt", "bh", "kh", "v")


def _wkv_prep(*args):
    for e in range(WKV_SEQS):
        yield from _wkv_prep_seq(e, *args)


def _wkv_prep_seq(e, rkv_refs, lo_refs, w2_ref, w0_ref, a2_ref, a0_ref, kk_ref, ka_ref, mk,
                  ops_ref, ptot_ref, slot):
    c = CHUNK
    for di, (d, rkv_ref, lo_ref) in enumerate(zip("fb", rkv_refs, lo_refs)):
        rkv = rkv_ref[e]
        lo = lo_ref[e]
        r = rkv[:, :RWKV_DIM]
        k = rkv[:, RWKV_DIM:2 * RWKV_DIM]
        v = rkv[:, 2 * RWKV_DIM:]
        w_raw = w0_ref[di:di + 1, :] + _mm(jnp.tanh(lo[:, :LANES]), w2_ref[di])
        lw = -EXP_M05 * _sigmoid(w_raw)
        yield
        ag = _sigmoid(a0_ref[di:di + 1, :] + _mm(lo[:, LANES:2 * LANES], a2_ref[di]))
        kd = k * (1.0 + (ag - 1.0) * ka_ref[...])
        kk_raw = k * kk_ref[...]
        h1, h2, h3 = _split3(lw)
        cs = jnp.dot(mk["tri_" + d], jnp.concatenate([h1, h2, h3], axis=1),
                     preferred_element_type=F32)
        logp = cs[:, :RWKV_DIM] + cs[:, RWKV_DIM:2 * RWKV_DIM] + cs[:, 2 * RWKV_DIM:]
        yield
        tot = logp[0:1] if d == "b" else logp[c - 1:c]
        e_inv = jnp.exp(-logp)
        e_hat = jnp.exp(tot - logp)
        rt = r * jnp.exp(logp)
        e_ex = jnp.exp(logp - lw)
        kt, kh = kd * e_inv, kd * e_hat
        ptot = jnp.exp(tot)
        for p in range(N_PAIRS):
            yield
            cols = slice(p * LANES, (p + 1) * LANES)
            kk = kk_raw[:, cols]
            kk = kk * lax.rsqrt(jnp.maximum(_head_sums(kk * kk, mk["ones_bd"]), 1e-24))
            kb = kk * ag[:, cols]
            tiles = {"at": -kk * e_ex[:, cols], "rt": rt[:, cols], "bt": kb * e_inv[:, cols],
                     "kt": kt[:, cols], "bh": kb * e_hat[:, cols], "kh": kh[:, cols],
                     "v": v[:, cols]}
            chain = (e * 2 + di) * N_PAIRS + p
            for j, name in enumerate(WKV_OPERANDS):
                ops_ref[slot, chain, j] = tiles[name]
            ptot_ref[slot, chain] = jnp.broadcast_to(ptot[:, cols], (SUBLANES, LANES))


def _wkv_kernel(rkv_f0_ref, lo_f0_ref, rkv_b0_ref, lo_b0_ref,
                rkv_f1_ref, lo_f1_ref, rkv_b1_ref, lo_b1_ref, s0f_ref, s0b_ref,
                w2_ref, w0_ref, a2_ref, a0_ref, kk_ref, ka_ref,
                yf_ref, yb_ref, sf_ref, sb_ref, st_ref, ops_ref, ptot_ref, *, nc):
    g = pl.program_id(0)
    ci = g % nc
    slot = g % 2
    mk = _wkv_masks()
    prm = (w2_ref, w0_ref, a2_ref, a0_ref, kk_ref, ka_ref, mk, ops_ref, ptot_ref)

    @pl.when(g == 0)
    def _():
        for _ in _wkv_prep((rkv_f0_ref, rkv_b0_ref), (lo_f0_ref, lo_b0_ref), *prm, 0):
            pass

    @pl.when(ci == 0)
    def _():
        for e in range(WKV_SEQS):
            st_ref[e, 0] = s0f_ref[e]
            st_ref[e, 1] = s0b_ref[e]

    chains = []
    for e in range(WKV_SEQS):
        for di, (d, y_ref) in enumerate((("f", yf_ref), ("b", yb_ref))):
            for p in range(N_PAIRS):
                chain = (e * 2 + di) * N_PAIRS + p
                ch = {"d": d, "e": e, "di": di, "p": p, "y_ref": y_ref, "st": st_ref[e, di, p],
                      "ptot": ptot_ref[slot, chain][0:1]}
                for j, name in enumerate(WKV_OPERANDS):
                    ch[name] = ops_ref[slot, chain, j]
                chains.append(ch)
    prep = _wkv_prep((rkv_f1_ref, rkv_b1_ref), (lo_f1_ref, lo_b1_ref), *prm, 1 - slot)
    for _ in _wkv_chains(chains, mk):
        for _ in range(WKV_SEQS):
            next(prep, None)
    for _ in prep:
        pass
    for ch in chains:
        ch["y_ref"][ch["e"], :, ch["p"] * LANES:(ch["p"] + 1) * LANES] = ch["y"]
        st_ref[ch["e"], ch["di"], ch["p"]] = ch["st_new"]

    @pl.when(ci == nc - 1)
    def _():
        for e in range(WKV_SEQS):
            for di, s_ref in enumerate((sf_ref, sb_ref)):
                for p in range(N_PAIRS):
                    st_t = st_ref[e, di, p].T
                    s_ref[e, 2 * p] = st_t[:HEAD_DIM, :HEAD_DIM]
                    s_ref[e, 2 * p + 1] = st_t[HEAD_DIM:, HEAD_DIM:]


def _wkv_call(rkv, lora, s0f, s0b, w2p, w0, a2p, a0, k_k, k_a, batch, seq_len):
    n = rkv.shape[0]
    nc = seq_len // CHUNK
    assert batch % WKV_SEQS == 0
    steps = (batch // WKV_SEQS) * nc
    rkv3 = rkv.reshape(batch, seq_len, 3 * RWKV_DIM)
    lora3 = lora.reshape(batch, seq_len, LORA_COLS)
    at = lambda g, back: (g // nc, (nc - 1 - g % nc) if back else g % nc, 0)
    fwd = lambda g: at(g, False)
    bwd = lambda g: at(g, True)
    fwd_first = lambda g: (0, 0, 0)
    bwd_first = lambda g: (0, nc - 1, 0)
    fwd_next = lambda g: at(jnp.minimum(g + 1, steps - 1), False)
    bwd_next = lambda g: at(jnp.minimum(g + 1, steps - 1), True)
    st = lambda g: (g // nc, 0, 0, 0)
    st_in = lambda g: ((g // nc) % (s0f.shape[0] // WKV_SEQS), 0, 0, 0)
    c2 = lambda g: (0, 0)
    c3 = lambda g: (0, 0, 0)
    st_shape = jax.ShapeDtypeStruct((batch, RWKV_HEADS, HEAD_DIM, HEAD_DIM), F32)
    y_shape = jax.ShapeDtypeStruct((batch, seq_len, RWKV_DIM), F32)
    n_chains = WKV_SEQS * 2 * N_PAIRS
    rkv_blk = (WKV_SEQS, CHUNK, 3 * RWKV_DIM)
    lora_blk = (WKV_SEQS, CHUNK, LORA_COLS)
    yf, yb, sf, sb = pl.pallas_call(
        functools.partial(_wkv_kernel, nc=nc),
        grid=(steps,),
        in_specs=[pl.BlockSpec(rkv_blk, fwd_first),
                  pl.BlockSpec(lora_blk, fwd_first),
                  pl.BlockSpec(rkv_blk, bwd_first),
                  pl.BlockSpec(lora_blk, bwd_first),
                  pl.BlockSpec(rkv_blk, fwd_next),
                  pl.BlockSpec(lora_blk, fwd_next),
                  pl.BlockSpec(rkv_blk, bwd_next),
                  pl.BlockSpec(lora_blk, bwd_next),
                  pl.BlockSpec((WKV_SEQS, N_PAIRS, LANES, LANES), st_in),
                  pl.BlockSpec((WKV_SEQS, N_PAIRS, LANES, LANES), st_in),
                  pl.BlockSpec((2, LANES, RWKV_DIM), c3),
                  pl.BlockSpec((2, RWKV_DIM), c2),
                  pl.BlockSpec((2, LANES, RWKV_DIM), c3),
                  pl.BlockSpec((2, RWKV_DIM), c2),
                  pl.BlockSpec((1, RWKV_DIM), c2),
                  pl.BlockSpec((1, RWKV_DIM), c2)],
        out_specs=[pl.BlockSpec((WKV_SEQS, CHUNK, RWKV_DIM), fwd),
                   pl.BlockSpec((WKV_SEQS, CHUNK, RWKV_DIM), bwd),
                   pl.BlockSpec((WKV_SEQS, RWKV_HEADS, HEAD_DIM, HEAD_DIM), st),
                   pl.BlockSpec((WKV_SEQS, RWKV_HEADS, HEAD_DIM, HEAD_DIM), st)],
        out_shape=[y_shape, y_shape, st_shape, st_shape],
        scratch_shapes=[pltpu.VMEM((WKV_SEQS, 2, N_PAIRS, LANES, LANES), F32),
                        pltpu.VMEM((2, n_chains, len(WKV_OPERANDS), CHUNK, LANES), F32),
                        pltpu.VMEM((2, n_chains, SUBLANES, LANES), F32)],
        compiler_params=_params("arbitrary"),
        name="wkv_chunked",
    )(rkv3, lora3, rkv3, lora3, rkv3, lora3, rkv3, lora3, s0f, s0b, w2p, w0, a2p, a0, k_k, k_a)
    return yf.reshape(n, RWKV_DIM), yb.reshape(n, RWKV_DIM), sf, sb


def _out_kernel(x_ref, mod_ref, conv_ref, yf_ref, yb_ref, rkv_ref, lo_ref,
                a2_ref, a0_ref, ka_ref, rk_ref, gng_ref, gnb_ref, g2_ref, wout_ref,
                postg_ref, preg_ref, rw_ref, rb_ref, x1_ref, h2_ref, route_ref, cnt_ref):
    @pl.when(pl.program_id(0) == 0)
    def _():
        cnt_ref[...] = jnp.zeros(cnt_ref.shape, F32)

    m = mod_ref[0]
    rkv = rkv_ref[...]
    lo = lo_ref[...]
    r = rkv[:, :RWKV_DIM]
    k = rkv[:, RWKV_DIM:2 * RWKV_DIM]
    v = rkv[:, 2 * RWKV_DIM:]
    ha = lo[:, LANES:2 * LANES]
    ag_f = _sigmoid(a0_ref[0:1, :] + _mm(ha, a2_ref[0]))
    ag_b = _sigmoid(a0_ref[1:2, :] + _mm(ha, a2_ref[1]))
    rkk = r * k * rk_ref[...] * (2.0 + (ag_f + ag_b - 2.0) * ka_ref[...])
    o = yf_ref[...] + yb_ref[...]
    gate = _mm(_sigmoid(lo[:, 2 * LANES:]), g2_ref[...])
    r2 = lax.broadcasted_iota(jnp.int32, (LANES, LANES), 0)
    c2 = lax.broadcasted_iota(jnp.int32, (LANES, LANES), 1)
    ones_bd = ((r2 // HEAD_DIM) == (c2 // HEAD_DIM)).astype(BF16)
    parts = []
    for p in range(N_PAIRS):
        cols = slice(p * LANES, (p + 1) * LANES)
        op = o[:, cols]
        mu = _head_sums(op, ones_bd) * (1.0 / HEAD_DIM)
        oc = op - mu
        var = _head_sums(oc * oc, ones_bd) * (1.0 / HEAD_DIM)
        on = oc * lax.rsqrt(var + GN_EPS) * gng_ref[:, cols] + gnb_ref[:, cols]
        bonus = _head_sums(rkk[:, cols], ones_bd) * v[:, cols]
        parts.append((on + bonus) * gate[:, cols])
    mix_in = jnp.concatenate([conv_ref[...]] + parts, axis=1)
    mix = jnp.dot(mix_in.astype(BF16), wout_ref[...], preferred_element_type=F32)
    x1 = x_ref[...] + m[2:3] * _rms(mix, postg_ref[...])
    x1_ref[...] = x1
    h2 = _rms(x1, preg_ref[...]) * (1.0 + m[4:5]) + m[3:4]
    h2_ref[...] = _pack_rows(h2)
    hh, hl = _split2(h2)
    dot = functools.partial(jnp.dot, preferred_element_type=F32)
    logits = dot(hh, rw_ref[0]) + (dot(hh, rw_ref[1]) + dot(hl, rw_ref[0])) + rb_ref[...]
    lane =lax.broadcasted_iota(jnp.int32, logits.shape, 1)
    work = logits
    picks, vals, idxs = [], [], []
    for _ in range(TOP_K):
        mx = jnp.max(work, axis=-1, keepdims=True)
        idx = jnp.min(jnp.where(work == mx, lane, LANES), axis=-1, keepdims=True)
        pick = lane == idx
        picks.append(pick)
        vals.append(mx)
        idxs.append(idx)
        work = jnp.where(pick, 2.0 * NEG_BIG, work)
    exps = [jnp.exp(val - vals[0]) for val in vals]
    den = exps[0] + exps[1] + exps[2] + exps[3]
    sel = jnp.zeros(logits.shape, F32)
    for pick in picks:
        sel = jnp.where(pick, 1.0, sel)
    tr = lax.broadcasted_iota(jnp.int32, (TM_OUT, TM_OUT), 0)
    tc = lax.broadcasted_iota(jnp.int32, (TM_OUT, TM_OUT), 1)
    before = (tc < tr).astype(BF16)
    rank = dot(before, sel.astype(BF16)) + cnt_ref[...]
    cnt_ref[...] = cnt_ref[...] + jnp.sum(sel, axis=0, keepdims=True)
    route = jnp.zeros(logits.shape, F32)
    for q, (pick, idx, e) in enumerate(zip(picks, idxs, exps)):
        rank_q = jnp.sum(jnp.where(pick, rank, 0.0), axis=-1, keepdims=True)
        route = jnp.where(lane == q, idx.astype(F32), route)
        route = jnp.where(lane == TOP_K + q, e / den, route)
        route = jnp.where(lane == 2 * TOP_K + q, rank_q, route)
    route_ref[...] = route


def _out_call(x, mod, conv_out, yf, yb, rkv, lora, wts, seq_len):
    n, d = x.shape
    row = lambda i: (i, 0)
    c2 = lambda i: (0, 0)
    c3 = lambda i: (0, 0, 0)
    full = lambda a: pl.BlockSpec(a.shape, c3 if a.ndim == 3 else c2)
    consts = [wts["a2p"], wts["a0"], wts["k_a"], wts["r_k"], wts["gn_g"], wts["gn_b"], wts["g2"],
              wts["w_out"], wts["post_mix_g"], wts["pre_ffn_g"], wts["router_w"], wts["router_b"]]
    return pl.pallas_call(
        _out_kernel,
        grid=(n // TM_OUT,),
        in_specs=[pl.BlockSpec((TM_OUT, d), row),
                  pl.BlockSpec((1, N_MOD, d), _mod_index(TM_OUT, seq_len, mod.shape[0])),
                  pl.BlockSpec((TM_OUT, CONV_CH), row),
                  pl.BlockSpec((TM_OUT, RWKV_DIM), row),
                  pl.BlockSpec((TM_OUT, RWKV_DIM), row),
                  pl.BlockSpec((TM_OUT, 3 * RWKV_DIM), row),
                  pl.BlockSpec((TM_OUT, LORA_COLS), row)] + [full(a) for a in consts],
        out_specs=[pl.BlockSpec((TM_OUT, d), row),
                   pl.BlockSpec((TM_OUT, d // 2), row),
                   pl.BlockSpec((TM_OUT, LANES), row),
                   pl.BlockSpec((1, LANES), c2)],
        out_shape=[jax.ShapeDtypeStruct((n, d), F32),
                   jax.ShapeDtypeStruct((n, d // 2), jnp.int32),
                   jax.ShapeDtypeStruct((n, LANES), F32),
                   jax.ShapeDtypeStruct((1, LANES), F32)],
        compiler_params=_params("arbitrary"),
        name="out_router",
    )(x, mod, conv_out, yf, yb, rkv, lora, *consts)


def _sc_mesh():
    return plsc.VectorSubcoreMesh(core_axis_name="c", subcore_axis_name="s",
                                  num_cores=SC_CORES, num_subcores=SC_SUBCORES)


def _sc_worker_id():
    return lax.axis_index("s") * SC_CORES + lax.axis_index("c")


def _sc_scatter_rows(src_a, src_b, idx, n_out):
    (n_a, d), n_b = src_a.shape, src_b.shape[0]
    n = n_a + n_b
    workers = SC_CORES * SC_SUBCORES
    per_worker = n // workers
    assert idx.shape[0] == TOP_K * n and per_worker * workers == n
    assert per_worker % SC_ROWS == 0 and n_a % per_worker == 0

    @functools.partial(
        pl.kernel, mesh=_sc_mesh(),
        out_type=jax.ShapeDtypeStruct((n_out, d), src_a.dtype),
        scratch_types=[pltpu.VMEM((SC_ROWS,), jnp.int32),
                       pltpu.VMEM((SC_ROWS, d), src_a.dtype),
                       pltpu.SemaphoreType.DMA],
        name="sc_scatter_rows")
    def scatter(a_hbm, b_hbm, idx_hbm, out_hbm, idx_v, rows_v, sem):
        base = _sc_worker_id() * per_worker

        @pl.loop(0, per_worker // SC_ROWS)
        def _(j):
            tok = base + j * SC_ROWS

            @pl.when(tok < n_a)
            def _():
                pltpu.sync_copy(a_hbm.at[pl.ds(tok, SC_ROWS)], rows_v)

            @pl.when(tok >= n_a)
            def _():
                pltpu.sync_copy(b_hbm.at[pl.ds(tok - n_a, SC_ROWS)], rows_v)

            for q in range(TOP_K):
                pltpu.sync_copy(idx_hbm.at[pl.ds(q * n + tok, SC_ROWS)], idx_v)
                pltpu.async_copy(rows_v, out_hbm.at[idx_v], sem).wait()

    return scatter(src_a, src_b, idx)


def _sc_gather_rows(src, idx):
    n_idx, d = idx.shape[0], src.shape[1]
    workers = SC_CORES * SC_SUBCORES
    per_worker = n_idx // workers
    assert per_worker * workers == n_idx and per_worker % SC_ROWS == 0

    @functools.partial(
        pl.kernel, mesh=_sc_mesh(),
        out_type=jax.ShapeDtypeStruct((n_idx, d), src.dtype),
        scratch_types=[pltpu.VMEM((SC_ROWS,), jnp.int32),
                       pltpu.VMEM((SC_ROWS, d), src.dtype),
                       pltpu.SemaphoreType.DMA],
        name="sc_gather_rows")
    def gather(src_hbm, idx_hbm, out_hbm, idx_v, rows_v, sem):
        base = _sc_worker_id() * per_worker

        @pl.loop(0, per_worker // SC_ROWS)
        def _(j):
            off = base + j * SC_ROWS
            pltpu.sync_copy(idx_hbm.at[pl.ds(off, SC_ROWS)], idx_v)
            pltpu.async_copy(src_hbm.at[idx_v], rows_v, sem).wait()
            pltpu.sync_copy(rows_v, out_hbm.at[pl.ds(off, SC_ROWS)])

    return gather(src, idx)


def _moe_kernel(blk_e_ref, blk_first_ref, blk_rows_ref, blk_next_ref,
                xs_ref, wgu_hbm, bgu_ref, wd_hbm, bd_ref,
                o_ref, wgu_f32_ref, wd_f32_ref, wgu_bf_ref, wd_bf_ref, sem):
    b = pl.program_id(0)
    n_rows = blk_rows_ref[b]

    def weight_copies(e):
        return (pltpu.make_async_copy(wgu_hbm.at[e], wgu_f32_ref, sem.at[0]),
                pltpu.make_async_copy(wd_hbm.at[e], wd_f32_ref, sem.at[1]))

    @pl.when(b == 0)
    def _():
        for copy in weight_copies(blk_e_ref[0]):
            copy.start()

    @pl.when(blk_first_ref[b] == 1)
    def _():
        for copy in weight_copies(blk_e_ref[b]):
            copy.wait()
        wgu_bf_ref[...] = wgu_f32_ref[...].astype(BF16)
        wd_bf_ref[...] = wd_f32_ref[...].astype(BF16)

        @pl.when(blk_next_ref[b] >= 0)
        def _():
            for copy in weight_copies(blk_next_ref[b]):
                copy.start()

    @pl.when(n_rows > 0)
    def _():
        live = lax.broadcasted_iota(jnp.int32, xs_ref.shape, 0) < n_rows
        xs = _unpack_rows(jnp.where(live, xs_ref[...], 0))
        gu = jnp.dot(xs.astype(BF16), wgu_bf_ref[...],
                     preferred_element_type=F32) + bgu_ref[0]
        x_glu = jnp.minimum(gu[:, :D_FF], SWIGLU_LIMIT)
        x_lin = jnp.clip(gu[:, D_FF:], -SWIGLU_LIMIT, SWIGLU_LIMIT)
        act = x_glu * _sigmoid(SWIGLU_ALPHA * x_glu) * (x_lin + 1.0)
        o_ref[...] = _pack_rows(jnp.dot(act.astype(BF16), wd_bf_ref[...],
                                        preferred_element_type=F32) + bd_ref[0])

    @pl.when(n_rows == 0)
    def _():
        o_ref[...] = jnp.zeros(o_ref.shape, jnp.int32)


def _moe_call(blk_e, blk_first, blk_rows, blk_next, xs, wgu, bgu, wd, bd):
    n_slots, half = xs.shape
    d = 2 * half
    ex = lambda b, be, bf, br, bn: (be[b], 0, 0)
    row = lambda b, be, bf, br, bn: (b, 0)
    grid_spec = pltpu.PrefetchScalarGridSpec(
        num_scalar_prefetch=4,
        grid=(n_slots // MOE_BLOCK,),
        in_specs=[pl.BlockSpec((MOE_BLOCK, half), row),
                  pl.BlockSpec(memory_space=pl.ANY),
                  pl.BlockSpec((1, 1, 2 * D_FF), ex),
                  pl.BlockSpec(memory_space=pl.ANY),
                  pl.BlockSpec((1, 1, d), ex)],
        out_specs=pl.BlockSpec((MOE_BLOCK, half), row),
        scratch_shapes=[pltpu.VMEM((d, 2 * D_FF), F32), pltpu.VMEM((D_FF, d), F32),
                        pltpu.VMEM((d, 2 * D_FF), BF16), pltpu.VMEM((D_FF, d), BF16),
                        pltpu.SemaphoreType.DMA((2,))])
    return pl.pallas_call(
        _moe_kernel,
        grid_spec=grid_spec,
        out_shape=jax.ShapeDtypeStruct((n_slots, half), jnp.int32),
        compiler_params=_params("arbitrary"),
        name="moe_experts",
    )(blk_e, blk_first, blk_rows, blk_next, xs, wgu, bgu, wd, bd)


def _final_kernel(x1_ref, y4_ref, route_ref, mod_ref, g_ref, o_ref):
    m = mod_ref[0]
    route = route_ref[...]
    y = jnp.zeros(x1_ref.shape, F32)
    for q in range(TOP_K):
        y = y + route[:, TOP_K + q:TOP_K + q + 1] * _unpack_rows(y4_ref[q])
    o_ref[...] = x1_ref[...] + m[5:6] * _rms(y, g_ref[...])


def _final_call(x1, y4, route, mod, g, seq_len):
    n, d = x1.shape
    row = lambda i: (i, 0)
    return pl.pallas_call(
        _final_kernel,
        grid=(n // TM_FIN,),
        in_specs=[pl.BlockSpec((TM_FIN, d), row),
                  pl.BlockSpec((TOP_K, TM_FIN, d // 2), lambda i: (0, i, 0)),
                  pl.BlockSpec((TM_FIN, LANES), row),
                  pl.BlockSpec((1, N_MOD, d), _mod_index(TM_FIN, seq_len, mod.shape[0])),
                  pl.BlockSpec((1, d), lambda i: (0, 0))],
        out_specs=pl.BlockSpec((TM_FIN, d), row),
        out_shape=jax.ShapeDtypeStruct((n, d), F32),
        compiler_params=_params("arbitrary"),
        name="final_residual",
    )(x1, y4, route, mod, g)


def _pack_state(s):
    b = s.shape[0]
    st = jnp.swapaxes(s.astype(F32), -1, -2).reshape(b, N_PAIRS, 2, HEAD_DIM, HEAD_DIM)
    z = jnp.zeros_like(st[:, :, 0])
    top = jnp.concatenate([st[:, :, 0], z], axis=-1)
    bot = jnp.concatenate([z, st[:, :, 1]], axis=-1)
    return jnp.concatenate([top, bot], axis=-2)


def _mix_sublayer(x, mod, s0f, s0b, seg_len, wts):
    batch, seq_len, d = x.shape
    x2 = x.reshape(batch * seq_len, d)
    cv, rkv, lora = _in_call(x2, mod, wts["pre_mix_g"], wts["w_all"], seq_len)
    conv_out = _conv_call(cv, wts["conv_w"], wts["conv_b"], wts["conv_ln_g"], wts["conv_ln_b"],
                          seg_len)
    yf, yb, sf, sb = _wkv_call(rkv, lora, s0f, s0b, wts["w2p"], wts["w0"], wts["a2p"], wts["a0"],
                               wts["k_k"], wts["k_a"], batch, seq_len)
    x1, h2, route, cnt = _out_call(x2, mod, conv_out, yf, yb, rkv, lora, wts, seq_len)
    return x1, h2, route, cnt, sf, sb


def _dispatch_plan(routes, cnts):
    experts = jnp.arange(N_EXPERTS, dtype=jnp.int32)
    counts = [c[0, :N_EXPERTS].astype(jnp.int32) for c in cnts]
    total = sum(counts)
    padded = (total + MOE_BLOCK - 1) // MOE_BLOCK * MOE_BLOCK
    pad_end = jnp.cumsum(padded)
    pad_start = pad_end - padded
    dests = []
    prior = jnp.zeros_like(total)
    for route, count in zip(routes, counts):
        idx = route[:, :TOP_K].astype(jnp.int32)
        rank = route[:, 2 * TOP_K:3 * TOP_K].astype(jnp.int32)
        first = jnp.sum(jnp.where(idx[..., None] == experts, pad_start + prior, 0), axis=-1)
        dests.append(first + rank)
        prior = prior + count
    dest = jnp.concatenate(dests, axis=0).T.reshape(-1)
    n_slots = dest.shape[0] + N_EXPERTS * MOE_BLOCK
    blk_row0 = jnp.arange(n_slots // MOE_BLOCK, dtype=jnp.int32) * MOE_BLOCK
    blk_e = jnp.sum((pad_end[None, :] <= blk_row0[:, None]).astype(jnp.int32), axis=-1)
    blk_e = jnp.minimum(blk_e, N_EXPERTS - 1)
    blk_first = jnp.concatenate([jnp.ones((1,), jnp.int32),
                                 (blk_e[1:] != blk_e[:-1]).astype(jnp.int32)])
    of_blk = lambda per_expert: jnp.sum(
        jnp.where(blk_e[:, None] == experts[None, :], per_expert[None, :], 0), axis=-1)
    blk_rows = jnp.clip(of_blk(pad_start + total) - blk_row0, 0, MOE_BLOCK).astype(jnp.int32)
    pos = jnp.arange(blk_e.shape[0], dtype=jnp.int32)
    later_first = (pos[None, :] > pos[:, None]) & (blk_first[None, :] == 1)
    next_pos = jnp.min(jnp.where(later_first, pos[None, :], pos.shape[0]), axis=-1)
    blk_next = jnp.where(next_pos < pos.shape[0],
                         blk_e[jnp.minimum(next_pos, pos.shape[0] - 1)], -1).astype(jnp.int32)
    return dest, n_slots, blk_e, blk_first, blk_rows, blk_next


def _routed_ffn(h2s, routes, cnts, wts):
    dest, n_slots, blk_e, blk_first, blk_rows, blk_next = _dispatch_plan(routes, cnts)
    xs = _sc_scatter_rows(h2s[0], h2s[1], dest, n_slots)
    out_sorted = _moe_call(blk_e, blk_first, blk_rows, blk_next, xs, wts["w_gu"], wts["b_gu"],
                           wts["w_down"], wts["b_down"])
    dest_qt = dest.reshape(TOP_K, -1)
    y4s, row0 = [], 0
    for h2 in h2s:
        n_path = h2.shape[0]
        y4 = _sc_gather_rows(out_sorted, dest_qt[:, row0:row0 + n_path].reshape(-1))
        y4s.append(y4.reshape(TOP_K, n_path, xs.shape[1]))
        row0 += n_path
    return y4s


def _prep_weights(l, pre_mix_g, post_mix_g, pre_ffn_g, post_ffn_g, w_in, w_out, conv_w, conv_b,
                  conv_ln_g, conv_ln_b, rw_w0, rw_w1, rw_w2, rw_a0, rw_a1, rw_a2, rw_g1, rw_g2,
                  rw_k_k, rw_k_a, rw_r_k, rw_gn_g, rw_gn_b, router_w, router_b, w_gu, b_gu,
                  w_down, b_down):
    row = lambda a: a.reshape(1, -1).astype(F32)
    zpad = jnp.zeros((HEAD_DIM, RWKV_DIM), F32)
    w_all = jnp.concatenate([w_in[l], rw_w1[l, 0], rw_w1[l, 1], rw_a1[l, 0], rw_a1[l, 1],
                             rw_g1[l]], axis=1).astype(BF16)
    w2p = jnp.stack([jnp.concatenate([rw_w2[l, 0], zpad], axis=0),
                     jnp.concatenate([zpad, rw_w2[l, 1]], axis=0)]).astype(BF16)
    a2p = jnp.stack([jnp.concatenate([rw_a2[l, 0], zpad], axis=0),
                     jnp.concatenate([zpad, rw_a2[l, 1]], axis=0)]).astype(BF16)
    rw_pad = jnp.pad(router_w[l].astype(F32), ((0, 0), (0, LANES - N_EXPERTS)))
    rw_hi = rw_pad.astype(BF16)
    rw_lo = (rw_pad - rw_hi.astype(F32)).astype(BF16)
    rb_pad = jnp.concatenate([router_b[l].astype(F32),
                              jnp.full((LANES - N_EXPERTS,), NEG_BIG, F32)]).reshape(1, LANES)
    return {
        "pre_mix_g": row(pre_mix_g[l]), "post_mix_g": row(post_mix_g[l]),
        "pre_ffn_g": row(pre_ffn_g[l]), "post_ffn_g": row(post_ffn_g[l]),
        "w_all": w_all, "w_out": w_out[l].astype(BF16),
        "conv_w": jnp.broadcast_to(conv_w[l].astype(F32)[:, None, :],
                                   (CONV_WIDTH, SUBLANES, CONV_CH)),
        "conv_b": row(conv_b[l]),
        "conv_ln_g": row(conv_ln_g[l]), "conv_ln_b": row(conv_ln_b[l]),
        "w2p": w2p, "w0": rw_w0[l].astype(F32), "a2p": a2p, "a0": rw_a0[l].astype(F32),
        "k_k": row(rw_k_k[l]), "k_a": row(rw_k_a[l]), "r_k": row(rw_r_k[l]),
        "gn_g": row(rw_gn_g[l]), "gn_b": row(rw_gn_b[l]), "g2": rw_g2[l].astype(BF16),
        "router_w": jnp.stack([rw_hi, rw_lo]), "router_b": rb_pad,
        "w_gu": w_gu[l].astype(F32), "b_gu": b_gu[l].reshape(N_EXPERTS, 1, -1).astype(F32),
        "w_down": w_down[l].astype(F32), "b_down": b_down[l].reshape(N_EXPERTS, 1, -1).astype(F32),
    }


def kernel(x_prompt, x_sample, state_wkv_fwd, state_wkv_bwd, c, c_ctx, ada_w, ada_b, pre_mix_g, post_mix_g, pre_ffn_g, post_ffn_g, w_in, w_out, conv_w, conv_b, conv_ln_g, conv_ln_b, rw_w0, rw_w1, rw_w2, rw_a0, rw_a1, rw_a2, rw_g1, rw_g2, rw_k_k, rw_k_a, rw_r_k, rw_gn_g, rw_gn_b, router_w, router_b, w_gu, b_gu, w_down, b_down):
    depth = ada_w.shape[0]
    dec_batch = c.shape[0]
    mod_rows = 16
    c_rows = jnp.concatenate([c, c_ctx[None, :],
                              jnp.zeros((mod_rows - dec_batch - 1, D_MODEL), F32)], axis=0)
    zero_state = jnp.zeros((WKV_SEQS, N_PAIRS, LANES, LANES), F32)
    y_prompt, y_sample = x_prompt, x_sample
    new_f, new_b = [], []
    for l in range(depth):
        wts = _prep_weights(l, pre_mix_g, post_mix_g, pre_ffn_g, post_ffn_g, w_in, w_out, conv_w,
                            conv_b, conv_ln_g, conv_ln_b, rw_w0, rw_w1, rw_w2, rw_a0, rw_a1, rw_a2,
                            rw_g1, rw_g2, rw_k_k, rw_k_a, rw_r_k, rw_gn_g, rw_gn_b, router_w,
                            router_b, w_gu, b_gu, w_down, b_down)
        mod = _mod_call(c_rows, ada_w[l], ada_b[l]).reshape(mod_rows, N_MOD, D_MODEL)
        mod_lat = mod[:dec_batch]
        mod_ctx = mod[dec_batch:dec_batch + 1]
        ctx = _mix_sublayer(y_prompt, mod_ctx, zero_state, zero_state, y_prompt.shape[1], wts)
        lat = _mix_sublayer(y_sample, mod_lat, _pack_state(state_wkv_fwd[:, l]),
                            _pack_state(state_wkv_bwd[:, l]), GRID_W, wts)
        new_f.append(ctx[4])
        new_b.append(ctx[5])
        y4_ctx, y4_lat = _routed_ffn([ctx[1], lat[1]], [ctx[2], lat[2]], [ctx[3], lat[3]], wts)
        y_prompt = _final_call(ctx[0], y4_ctx, ctx[2], mod_ctx, wts["post_ffn_g"],
                               y_prompt.shape[1]).reshape(y_prompt.shape)
        y_sample = _final_call(lat[0], y4_lat, lat[2], mod_lat, wts["post_ffn_g"],
                               y_sample.shape[1]).reshape(y_sample.shape)
    return (y_prompt, y_sample, jnp.stack(new_f, axis=1), jnp.stack(new_b, axis=1))
```

```python
import functools
import math

import jax
import jax.numpy as jnp
from jax import lax
from jax.experimental import pallas as pl
from jax.experimental.pallas import tpu as pltpu
from jax.experimental.pallas import tpu_sc as plsc

F32 = jnp.float32
BF16 = jnp.bfloat16

D_MODEL = 1024
CONV_CH = 512
RWKV_DIM = 512
HEAD_DIM = 64
RWKV_HEADS = 8
N_PAIRS = RWKV_HEADS // 2
CONV_WIDTH = 31
N_EXPERTS = 32
TOP_K = 4
D_FF = 1024
SWIGLU_LIMIT = 7.0
SWIGLU_ALPHA = 1.702
RMS_EPS = 1e-6
LN_EPS = 1e-5
GN_EPS = 64e-5
N_MOD = 6
GRID_W = 64

LANES = 128
SUBLANES = 8
CHUNK = 64
WKV_SEQS = 4
LORA_COLS = 384
TM_IN = 1024
TM_CONV = 512
TM_OUT = 512
TM_FIN = 1024
ROUTE_ROWS = 16
MOE_BLOCK = 512
SC_CORES = 2
SC_SUBCORES = 16
SC_ROWS = 128
NEG_BIG = -1e30
EXP_M05 = math.exp(-0.5)
VMEM_LIMIT = 56 * 1024 * 1024


def _sigmoid(x):
    return 1.0 / (1.0 + jnp.exp(-x))


def _mm(a, b):
    return jnp.dot(a.astype(BF16), b.astype(BF16), preferred_element_type=F32)


def _mm_nt(a, b):
    return lax.dot_general(a.astype(BF16), b.astype(BF16), (((1,), (1,)), ((), ())),
                           preferred_element_type=F32)


def _mm_tn(a, b):
    return lax.dot_general(a.astype(BF16), b.astype(BF16), (((0,), (0,)), ((), ())),
                           preferred_element_type=F32)


def _split2(x):
    hi = x.astype(BF16)
    lo = (x - hi.astype(F32)).astype(BF16)
    return hi, lo


def _split3(x):
    h1 = x.astype(BF16)
    r1 = x - h1.astype(F32)
    h2 = r1.astype(BF16)
    h3 = (r1 - h2.astype(F32)).astype(BF16)
    return h1, h2, h3


def _pack_rows(x):
    n = x.shape[1] // 2
    bits = lambda v: lax.bitcast_convert_type(v.astype(BF16).astype(F32), jnp.uint32)
    word = (bits(x[:, n:]) & jnp.uint32(0xFFFF0000)) | (bits(x[:, :n]) >> 16)
    return lax.bitcast_convert_type(word, jnp.int32)


def _unpack_rows(w):
    u = lax.bitcast_convert_type(w, jnp.uint32)
    lo = lax.bitcast_convert_type(u << 16, F32)
    hi = lax.bitcast_convert_type(u & jnp.uint32(0xFFFF0000), F32)
    return jnp.concatenate([lo, hi], axis=1)


def _rms(x, g):
    return x * lax.rsqrt(jnp.mean(x * x, axis=-1, keepdims=True) + RMS_EPS) * g


def _params(*sem):
    return pltpu.CompilerParams(dimension_semantics=sem, vmem_limit_bytes=VMEM_LIMIT)


def _mod_kernel(c_ref, w_ref, b_ref, o_ref):
    c = c_ref[...]
    s1, s2 = _split2(c * _sigmoid(c))
    w1, w2 = _split2(w_ref[...])
    dot = functools.partial(jnp.dot, preferred_element_type=F32)
    o_ref[...] = dot(s1, w1) + (dot(s1, w2) + dot(s2, w1)) + b_ref[...]


def _mod_call(c_rows, ada_w, ada_b):
    m, d = c_rows.shape
    n = ada_w.shape[1]
    tn = 512
    return pl.pallas_call(
        _mod_kernel,
        grid=(n // tn,),
        in_specs=[pl.BlockSpec((m, d), lambda j: (0, 0)),
                  pl.BlockSpec((d, tn), lambda j: (0, j)),
                  pl.BlockSpec((1, tn), lambda j: (0, j))],
        out_specs=pl.BlockSpec((m, tn), lambda j: (0, j)),
        out_shape=jax.ShapeDtypeStruct((m, n), F32),
        compiler_params=_params("arbitrary"),
        name="mod",
    )(c_rows, ada_w, ada_b.reshape(1, n))


def _in_kernel(x_ref, mod_ref, g_ref, w_ref, cv_ref, rkv_ref, lo_ref):
    m = mod_ref[0]
    h = _rms(x_ref[...], g_ref[...]) * (1.0 + m[1:2]) + m[0:1]
    proj = jnp.dot(h.astype(BF16), w_ref[...], preferred_element_type=F32)
    cv_ref[...] = proj[:, :CONV_CH] * _sigmoid(proj[:, CONV_CH:2 * CONV_CH])
    rkv_ref[...] = proj[:, 2 * CONV_CH:2 * CONV_CH + 3 * RWKV_DIM]
    lo_ref[...] = proj[:, 2 * CONV_CH + 3 * RWKV_DIM:]


def _mod_index(tile, seq_len, n_mod_rows):
    def index(i):
        return ((i * tile) // seq_len) % n_mod_rows, 0, 0
    return index


def _in_call(x, mod, g, w_all, seq_len):
    n, d = x.shape
    ncol = w_all.shape[1]
    return pl.pallas_call(
        _in_kernel,
        grid=(n // TM_IN,),
        in_specs=[pl.BlockSpec((TM_IN, d), lambda i: (i, 0)),
                  pl.BlockSpec((1, N_MOD, d), _mod_index(TM_IN, seq_len, mod.shape[0])),
                  pl.BlockSpec((1, d), lambda i: (0, 0)),
                  pl.BlockSpec((d, ncol), lambda i: (0, 0))],
        out_specs=[pl.BlockSpec((TM_IN, CONV_CH), lambda i: (i, 0)),
                   pl.BlockSpec((TM_IN, 3 * RWKV_DIM), lambda i: (i, 0)),
                   pl.BlockSpec((TM_IN, LORA_COLS), lambda i: (i, 0))],
        out_shape=[jax.ShapeDtypeStruct((n, CONV_CH), F32),
                   jax.ShapeDtypeStruct((n, 3 * RWKV_DIM), F32),
                   jax.ShapeDtypeStruct((n, LORA_COLS), F32)],
        compiler_params=_params("arbitrary"),
        name="in_proj",
    )(x, mod, g, w_all)


CONV_HALO = 16
CONV_ROWS = 32


def _conv_kernel(cv_ref, w_ref, b_ref, g_ref, be_ref, o_ref, pad_ref, acc_ref, *, seg_len):
    nseg = TM_CONV // seg_len
    stride = seg_len + 2 * CONV_HALO
    rows = nseg * stride
    u = cv_ref[...]
    pad_ref[0] = jnp.zeros(pad_ref.shape[1:], F32)
    for s in range(nseg):
        pad_ref[0, s * stride + CONV_HALO:s * stride + CONV_HALO + seg_len, :] = (
            u[s * seg_len:(s + 1) * seg_len])
    for r in range(1, SUBLANES):
        pad_ref[r, 0:rows, :] = pad_ref[0, r:r + rows, :]
    first = CONV_HALO - CONV_WIDTH // 2
    chunks_per_seg = seg_len // CONV_ROWS

    def row_chunk(i, carry):
        pad0 = i * CONV_ROWS + (i // chunks_per_seg) * (2 * CONV_HALO)
        accs = [jnp.zeros((SUBLANES, CONV_CH), F32) for _ in range(CONV_ROWS // SUBLANES)]
        for t in range(CONV_WIDTH):
            q, r = divmod(first + t, SUBLANES)
            w_t = w_ref[t]
            for j in range(len(accs)):
                start = pl.multiple_of(pad0 + (q + j) * SUBLANES, SUBLANES)
                accs[j] = accs[j] + w_t * pad_ref[r, pl.ds(start, SUBLANES), :]
        for j, acc in enumerate(accs):
            row0 = pl.multiple_of(i * CONV_ROWS + j * SUBLANES, SUBLANES)
            acc_ref[pl.ds(row0, SUBLANES), :] = acc
        return carry

    lax.fori_loop(0, TM_CONV // CONV_ROWS, row_chunk, 0)
    y = acc_ref[...] + b_ref[...]
    mu = jnp.mean(y, axis=-1, keepdims=True)
    yc = y - mu
    var = jnp.mean(yc * yc, axis=-1, keepdims=True)
    z = yc * lax.rsqrt(var + LN_EPS) * g_ref[...] + be_ref[...]
    o_ref[...] = z * _sigmoid(z)


def _conv_call(cv, conv_w, conv_b, ln_g, ln_b, seg_len):
    n = cv.shape[0]
    nseg = TM_CONV // seg_len
    row = lambda i: (i, 0)
    const = lambda i: (0, 0)
    return pl.pallas_call(
        functools.partial(_conv_kernel, seg_len=seg_len),
        grid=(n // TM_CONV,),
        in_specs=[pl.BlockSpec((TM_CONV, CONV_CH), row),
                  pl.BlockSpec((CONV_WIDTH, SUBLANES, CONV_CH), lambda i: (0, 0, 0)),
                  pl.BlockSpec((1, CONV_CH), const),
                  pl.BlockSpec((1, CONV_CH), const),
                  pl.BlockSpec((1, CONV_CH), const)],
        out_specs=pl.BlockSpec((TM_CONV, CONV_CH), row),
        out_shape=jax.ShapeDtypeStruct((n, CONV_CH), F32),
        scratch_shapes=[pltpu.VMEM((SUBLANES, nseg * (seg_len + 2 * CONV_HALO) + SUBLANES, CONV_CH),
                                   F32),
                        pltpu.VMEM((TM_CONV, CONV_CH), F32)],
        compiler_params=_params("arbitrary"),
        name="conv_module",
    )(cv, conv_w, conv_b, ln_g, ln_b)


def _head_sums(x, ones_bd):
    hi, lo = _split2(x)
    return jnp.dot(jnp.concatenate([hi, lo], axis=1), jnp.concatenate([ones_bd, ones_bd], axis=0),
                   preferred_element_type=F32)


def _wkv_masks():
    c = CHUNK
    r2 = lax.broadcasted_iota(jnp.int32, (2 * c, 2 * c), 0)
    c2 = lax.broadcasted_iota(jnp.int32, (2 * c, 2 * c), 1)
    same = (r2 // c) == (c2 // c)
    rr, cc = r2 % c, c2 % c
    r1 = lax.broadcasted_iota(jnp.int32, (c, 2 * c), 0)
    c1 = lax.broadcasted_iota(jnp.int32, (c, 2 * c), 1)
    cc1 = c1 % c
    rs = lax.broadcasted_iota(jnp.int32, (c, c), 0)
    cs = lax.broadcasted_iota(jnp.int32, (c, c), 1)
    masks = {"same": same, "eye": r2 == c2, "left": c1 < c,
             "ones_bd": same.astype(BF16)}
    for name, lt in (("f", lambda a, b: a < b), ("b", lambda a, b: a > b)):
        le = (lambda a, b, lt=lt: lt(a, b) | (a == b))
        masks["strict_" + name] = same & lt(cc, rr)
        masks["inc_" + name] = le(cc1, r1)
        masks["ak0_" + name] = (c1 >= c) & lt(cc1, r1)
        masks["ak1_" + name] = (c1 < c) & lt(cc1, r1)
        masks["tri_" + name] = le(cs, rs).astype(BF16)
    return masks


def _wkv_chains(chains, mk):
    c = CHUNK
    left = mk["left"]
    for ch in chains:
        zero = jnp.zeros_like(ch["at"])
        ch["at0"], ch["at1"] = jnp.where(left, ch["at"], zero), jnp.where(left, zero, ch["at"])
        rt0, rt1 = jnp.where(left, ch["rt"], zero), jnp.where(left, zero, ch["rt"])
        bt, kt = ch["bt"].astype(BF16), ch["kt"].astype(BF16)
        ch["g0"] = _mm_nt(jnp.concatenate([ch["at0"], rt0], axis=0), jnp.concatenate([bt, kt], axis=0))
        ch["g1"] = _mm_nt(jnp.concatenate([ch["at1"], rt1], axis=0), jnp.concatenate([kt, bt], axis=0))
    yield
    for ch in chains:
        g0, g1 = ch["g0"], ch["g1"]
        lmat = jnp.where(mk["strict_" + ch["d"]], jnp.concatenate([g0[:c], g1[:c]], axis=0), 0.0)
        ch["tinv"] = jnp.where(mk["eye"], 1.0, lmat)
        ch["lpow"] = _mm(lmat, lmat)
    for step in range(5):
        yield
        for ch in chains:
            lp = ch["lpow"].astype(BF16)
            if step < 4:
                prod = jnp.dot(lp, jnp.concatenate([lp, ch["tinv"].astype(BF16)], axis=1),
                               preferred_element_type=F32)
                ch["lpow"] = prod[:, :LANES]
                ch["tinv"] = ch["tinv"] + prod[:, LANES:]
            else:
                ch["tinv"] = ch["tinv"] + _mm(lp, ch["tinv"])
    yield
    for ch in chains:
        d, v = ch["d"], ch["v"]
        zv = jnp.zeros_like(v)
        ak0 = jnp.where(mk["ak0_" + d], ch["g0"][:c], 0.0)
        ak1 = jnp.where(mk["ak1_" + d], ch["g1"][:c], 0.0)
        ch["x0"] = _mm(ak0, jnp.concatenate([zv, v], axis=0))
        ch["x1"] = _mm(ak1, jnp.concatenate([v, zv], axis=0))
    yield
    for ch in chains:
        zero = jnp.zeros_like(ch["x0"])
        rhs = jnp.concatenate([
            jnp.concatenate([ch["at0"], jnp.where(left, ch["x0"], zero)], axis=1),
            jnp.concatenate([ch["at1"], jnp.where(left, zero, ch["x1"])], axis=1)], axis=0)
        ch["aw_blk"] = _mm(ch["tinv"], rhs)
    yield
    for ch in chains:
        d, v, aw_blk = ch["d"], ch["v"], ch["aw_blk"]
        zv = jnp.zeros_like(v)
        aw = aw_blk[:c] + aw_blk[c:]
        inc = mk["inc_" + d]
        g0, g1 = ch["g0"], ch["g1"]
        rb = jnp.where(inc, jnp.where(left, g0[c:], g1[c:]), 0.0)
        rk = jnp.where(inc, jnp.where(left, g1[c:], g0[c:]), 0.0)
        v0, v1 = jnp.where(left, v, zv), jnp.where(left, zv, v)
        vblk = jnp.concatenate([jnp.zeros((2 * c, LANES), F32),
                                jnp.concatenate([v1, v0], axis=0)], axis=1)
        qz = _mm(jnp.concatenate([rb, rk], axis=1), jnp.concatenate([aw_blk, vblk], axis=0))
        ch["q"] = ch["rt"] + qz[:, :LANES]
        ch["zz"] = qz[:, LANES:]
        rhs3 = jnp.concatenate([aw, jnp.concatenate([zv, v], axis=1)], axis=0)
        ch["mn"] = _mm_tn(jnp.concatenate([ch["bh"], ch["kh"]], axis=0), rhs3)
    yield
    same = mk["same"]
    for ch in chains:
        mn = ch["mn"]
        mt = jnp.where(same, mn[:, :LANES], 0.0) + jnp.where(mk["eye"], ch["ptot"], 0.0)
        nt = jnp.where(same, mn[:, LANES:], 0.0)
        both = _mm(jnp.concatenate([ch["q"], mt], axis=0), ch["st"])
        ch["y"] = both[:c] + ch["zz"]
        ch["st_new"] = both[c:] + nt


WKV_OPERANDS = ("at", "rt", "bt", "kt", "bh", "kh", "v")


def _wkv_prep(*args):
    for e in range(WKV_SEQS):
        yield from _wkv_prep_seq(e, *args)


def _wkv_prep_seq(e, rkv_refs, lo_refs, w2_ref, w0_ref, a2_ref, a0_ref, kk_ref, ka_ref, mk,
                  ops_ref, ptot_ref, slot):
    c = CHUNK
    for di, (d, rkv_ref, lo_ref) in enumerate(zip("fb", rkv_refs, lo_refs)):
        rkv = rkv_ref[e]
        lo = lo_ref[e]
        r = rkv[:, :RWKV_DIM]
        k = rkv[:, RWKV_DIM:2 * RWKV_DIM]
        v = rkv[:, 2 * RWKV_DIM:]
        w_raw = w0_ref[di:di + 1, :] + _mm(jnp.tanh(lo[:, :LANES]), w2_ref[di])
        lw = -EXP_M05 * _sigmoid(w_raw)
        yield
        ag = _sigmoid(a0_ref[di:di + 1, :] + _mm(lo[:, LANES:2 * LANES], a2_ref[di]))
        kd = k * (1.0 + (ag - 1.0) * ka_ref[...])
        kk_raw = k * kk_ref[...]
        h1, h2, h3 = _split3(lw)
        cs = jnp.dot(mk["tri_" + d], jnp.concatenate([h1, h2, h3], axis=1),
                     preferred_element_type=F32)
        logp = cs[:, :RWKV_DIM] + cs[:, RWKV_DIM:2 * RWKV_DIM] + cs[:, 2 * RWKV_DIM:]
        yield
        tot = logp[0:1] if d == "b" else logp[c - 1:c]
        e_inv = jnp.exp(-logp)
        e_hat = jnp.exp(tot - logp)
        rt = r * jnp.exp(logp)
        e_ex = jnp.exp(logp - lw)
        kt, kh = kd * e_inv, kd * e_hat
        ptot = jnp.exp(tot)
        for p in range(N_PAIRS):
            yield
            cols = slice(p * LANES, (p + 1) * LANES)
            kk = kk_raw[:, cols]
            kk = kk * lax.rsqrt(jnp.maximum(_head_sums(kk * kk, mk["ones_bd"]), 1e-24))
            kb = kk * ag[:, cols]
            tiles = {"at": -kk * e_ex[:, cols], "rt": rt[:, cols], "bt": kb * e_inv[:, cols],
                     "kt": kt[:, cols], "bh": kb * e_hat[:, cols], "kh": kh[:, cols],
                     "v": v[:, cols]}
            chain = (e * 2 + di) * N_PAIRS + p
            for j, name in enumerate(WKV_OPERANDS):
                ops_ref[slot, chain, j] = tiles[name]
            ptot_ref[slot, chain] = jnp.broadcast_to(ptot[:, cols], (SUBLANES, LANES))


def _wkv_kernel(rkv_f0_ref, lo_f0_ref, rkv_b0_ref, lo_b0_ref,
                rkv_f1_ref, lo_f1_ref, rkv_b1_ref, lo_b1_ref, s0f_ref, s0b_ref,
                w2_ref, w0_ref, a2_ref, a0_ref, kk_ref, ka_ref,
                yf_ref, yb_ref, sf_ref, sb_ref, st_ref, ops_ref, ptot_ref, *, nc):
    g = pl.program_id(0)
    ci = g % nc
    slot = g % 2
    mk = _wkv_masks()
    prm = (w2_ref, w0_ref, a2_ref, a0_ref, kk_ref, ka_ref, mk, ops_ref, ptot_ref)

    @pl.when(g == 0)
    def _():
        for _ in _wkv_prep((rkv_f0_ref, rkv_b0_ref), (lo_f0_ref, lo_b0_ref), *prm, 0):
            pass

    @pl.when(ci == 0)
    def _():
        for e in range(WKV_SEQS):
            st_ref[e, 0] = s0f_ref[e]
            st_ref[e, 1] = s0b_ref[e]

    chains = []
    for e in range(WKV_SEQS):
        for di, (d, y_ref) in enumerate((("f", yf_ref), ("b", yb_ref))):
            for p in range(N_PAIRS):
                chain = (e * 2 + di) * N_PAIRS + p
                ch = {"d": d, "e": e, "di": di, "p": p, "y_ref": y_ref, "st": st_ref[e, di, p],
                      "ptot": ptot_ref[slot, chain][0:1]}
                for j, name in enumerate(WKV_OPERANDS):
                    ch[name] = ops_ref[slot, chain, j]
                chains.append(ch)
    prep = _wkv_prep((rkv_f1_ref, rkv_b1_ref), (lo_f1_ref, lo_b1_ref), *prm, 1 - slot)
    for _ in _wkv_chains(chains, mk):
        for _ in range(WKV_SEQS):
            next(prep, None)
    for _ in prep:
        pass
    for ch in chains:
        ch["y_ref"][ch["e"], :, ch["p"] * LANES:(ch["p"] + 1) * LANES] = ch["y"]
        st_ref[ch["e"], ch["di"], ch["p"]] = ch["st_new"]

    @pl.when(ci == nc - 1)
    def _():
        for e in range(WKV_SEQS):
            for di, s_ref in enumerate((sf_ref, sb_ref)):
                for p in range(N_PAIRS):
                    st_t = st_ref[e, di, p].T
                    s_ref[e, 2 * p] = st_t[:HEAD_DIM, :HEAD_DIM]
                    s_ref[e, 2 * p + 1] = st_t[HEAD_DIM:, HEAD_DIM:]


def _wkv_call(rkv, lora, s0f, s0b, w2p, w0, a2p, a0, k_k, k_a, batch, seq_len):
    n = rkv.shape[0]
    nc = seq_len // CHUNK
    assert batch % WKV_SEQS == 0
    steps = (batch // WKV_SEQS) * nc
    rkv3 = rkv.reshape(batch, seq_len, 3 * RWKV_DIM)
    lora3 = lora.reshape(batch, seq_len, LORA_COLS)
    at = lambda g, back: (g // nc, (nc - 1 - g % nc) if back else g % nc, 0)
    fwd = lambda g: at(g, False)
    bwd = lambda g: at(g, True)
    fwd_first = lambda g: (0, 0, 0)
    bwd_first = lambda g: (0, nc - 1, 0)
    fwd_next = lambda g: at(jnp.minimum(g + 1, steps - 1), False)
    bwd_next = lambda g: at(jnp.minimum(g + 1, steps - 1), True)
    st = lambda g: (g // nc, 0, 0, 0)
    st_in = lambda g: ((g // nc) % (s0f.shape[0] // WKV_SEQS), 0, 0, 0)
    c2 = lambda g: (0, 0)
    c3 = lambda g: (0, 0, 0)
    st_shape = jax.ShapeDtypeStruct((batch, RWKV_HEADS, HEAD_DIM, HEAD_DIM), F32)
    y_shape = jax.ShapeDtypeStruct((batch, seq_len, RWKV_DIM), F32)
    n_chains = WKV_SEQS * 2 * N_PAIRS
    rkv_blk = (WKV_SEQS, CHUNK, 3 * RWKV_DIM)
    lora_blk = (WKV_SEQS, CHUNK, LORA_COLS)
    yf, yb, sf, sb = pl.pallas_call(
        functools.partial(_wkv_kernel, nc=nc),
        grid=(steps,),
        in_specs=[pl.BlockSpec(rkv_blk, fwd_first),
                  pl.BlockSpec(lora_blk, fwd_first),
                  pl.BlockSpec(rkv_blk, bwd_first),
                  pl.BlockSpec(lora_blk, bwd_first),
                  pl.BlockSpec(rkv_blk, fwd_next),
                  pl.BlockSpec(lora_blk, fwd_next),
                  pl.BlockSpec(rkv_blk, bwd_next),
                  pl.BlockSpec(lora_blk, bwd_next),
                  pl.BlockSpec((WKV_SEQS, N_PAIRS, LANES, LANES), st_in),
                  pl.BlockSpec((WKV_SEQS, N_PAIRS, LANES, LANES), st_in),
                  pl.BlockSpec((2, LANES, RWKV_DIM), c3),
                  pl.BlockSpec((2, RWKV_DIM), c2),
                  pl.BlockSpec((2, LANES, RWKV_DIM), c3),
                  pl.BlockSpec((2, RWKV_DIM), c2),
                  pl.BlockSpec((1, RWKV_DIM), c2),
                  pl.BlockSpec((1, RWKV_DIM), c2)],
        out_specs=[pl.BlockSpec((WKV_SEQS, CHUNK, RWKV_DIM), fwd),
                   pl.BlockSpec((WKV_SEQS, CHUNK, RWKV_DIM), bwd),
                   pl.BlockSpec((WKV_SEQS, RWKV_HEADS, HEAD_DIM, HEAD_DIM), st),
                   pl.BlockSpec((WKV_SEQS, RWKV_HEADS, HEAD_DIM, HEAD_DIM), st)],
        out_shape=[y_shape, y_shape, st_shape, st_shape],
        scratch_shapes=[pltpu.VMEM((WKV_SEQS, 2, N_PAIRS, LANES, LANES), F32),
                        pltpu.VMEM((2, n_chains, len(WKV_OPERANDS), CHUNK, LANES), F32),
                        pltpu.VMEM((2, n_chains, SUBLANES, LANES), F32)],
        compiler_params=_params("arbitrary"),
        name="wkv_chunked",
    )(rkv3, lora3, rkv3, lora3, rkv3, lora3, rkv3, lora3, s0f, s0b, w2p, w0, a2p, a0, k_k, k_a)
    return yf.reshape(n, RWKV_DIM), yb.reshape(n, RWKV_DIM), sf, sb


def _out_kernel(x_ref, mod_ref, conv_ref, yf_ref, yb_ref, rkv_ref, lo_ref,
                a2_ref, a0_ref, ka_ref, rk_ref, gng_ref, gnb_ref, g2_ref, wout_ref,
                postg_ref, preg_ref, rw_ref, rb_ref,
                x1_ref, h2_ref, route_ref, route_t_ref, cnt_ref):
    @pl.when(pl.program_id(0) == 0)
    def _():
        cnt_ref[...] = jnp.zeros(cnt_ref.shape, F32)

    m = mod_ref[0]
    rkv = rkv_ref[...]
    lo = lo_ref[...]
    r = rkv[:, :RWKV_DIM]
    k = rkv[:, RWKV_DIM:2 * RWKV_DIM]
    v = rkv[:, 2 * RWKV_DIM:]
    ha = lo[:, LANES:2 * LANES]
    ag_f = _sigmoid(a0_ref[0:1, :] + _mm(ha, a2_ref[0]))
    ag_b = _sigmoid(a0_ref[1:2, :] + _mm(ha, a2_ref[1]))
    rkk = r * k * rk_ref[...] * (2.0 + (ag_f + ag_b - 2.0) * ka_ref[...])
    o = yf_ref[...] + yb_ref[...]
    gate = _mm(_sigmoid(lo[:, 2 * LANES:]), g2_ref[...])
    r2 = lax.broadcasted_iota(jnp.int32, (LANES, LANES), 0)
    c2 = lax.broadcasted_iota(jnp.int32, (LANES, LANES), 1)
    ones_bd = ((r2 // HEAD_DIM) == (c2 // HEAD_DIM)).astype(BF16)
    parts = []
    for p in range(N_PAIRS):
        cols = slice(p * LANES, (p + 1) * LANES)
        op = o[:, cols]
        mu = _head_sums(op, ones_bd) * (1.0 / HEAD_DIM)
        oc = op - mu
        var = _head_sums(oc * oc, ones_bd) * (1.0 / HEAD_DIM)
        on = oc * lax.rsqrt(var + GN_EPS) * gng_ref[:, cols] + gnb_ref[:, cols]
        bonus = _head_sums(rkk[:, cols], ones_bd) * v[:, cols]
        parts.append((on + bonus) * gate[:, cols])
    mix_in = jnp.concatenate([conv_ref[...]] + parts, axis=1)
    mix = jnp.dot(mix_in.astype(BF16), wout_ref[...], preferred_element_type=F32)
    x1 = x_ref[...] + m[2:3] * _rms(mix, postg_ref[...])
    x1_ref[...] = x1
    h2 = _rms(x1, preg_ref[...]) * (1.0 + m[4:5]) + m[3:4]
    h2_ref[...] = _pack_rows(h2)
    hh, hl = _split2(h2)
    dot = functools.partial(jnp.dot, preferred_element_type=F32)
    logits = dot(hh, rw_ref[0]) + (dot(hh, rw_ref[1]) + dot(hl, rw_ref[0])) + rb_ref[...]
    lane =lax.broadcasted_iota(jnp.int32, logits.shape, 1)
    work = logits
    picks, vals, idxs = [], [], []
    for _ in range(TOP_K):
        mx = jnp.max(work, axis=-1, keepdims=True)
        idx = jnp.min(jnp.where(work == mx, lane, LANES), axis=-1, keepdims=True)
        pick = lane == idx
        picks.append(pick)
        vals.append(mx)
        idxs.append(idx)
        work = jnp.where(pick, 2.0 * NEG_BIG, work)
    exps = [jnp.exp(val - vals[0]) for val in vals]
    den = exps[0] + exps[1] + exps[2] + exps[3]
    sel = jnp.zeros(logits.shape, F32)
    for pick in picks:
        sel = jnp.where(pick, 1.0, sel)
    tr = lax.broadcasted_iota(jnp.int32, (TM_OUT, TM_OUT), 0)
    tc = lax.broadcasted_iota(jnp.int32, (TM_OUT, TM_OUT), 1)
    before = (tc < tr).astype(BF16)
    rank = dot(before, sel.astype(BF16)) + cnt_ref[...]
    cnt_ref[...] = cnt_ref[...] + jnp.sum(sel, axis=0, keepdims=True)
    route = jnp.zeros(logits.shape, F32)
    for q, (pick, idx, e) in enumerate(zip(picks, idxs, exps)):
        rank_q = jnp.sum(jnp.where(pick, rank, 0.0), axis=-1, keepdims=True)
        route = jnp.where(lane == q, idx.astype(F32), route)
        route = jnp.where(lane == TOP_K + q, e / den, route)
        route = jnp.where(lane == 2 * TOP_K + q, rank_q, route)
    route_ref[...] = route
    route_t_ref[...] = route.T[:ROUTE_ROWS, :]


def _out_call(x, mod, conv_out, yf, yb, rkv, lora, wts, seq_len):
    n, d = x.shape
    row = lambda i: (i, 0)
    c2 = lambda i: (0, 0)
    c3 = lambda i: (0, 0, 0)
    full = lambda a: pl.BlockSpec(a.shape, c3 if a.ndim == 3 else c2)
    consts = [wts["a2p"], wts["a0"], wts["k_a"], wts["r_k"], wts["gn_g"], wts["gn_b"], wts["g2"],
              wts["w_out"], wts["post_mix_g"], wts["pre_ffn_g"], wts["router_w"], wts["router_b"]]
    return pl.pallas_call(
        _out_kernel,
        grid=(n // TM_OUT,),
        in_specs=[pl.BlockSpec((TM_OUT, d), row),
                  pl.BlockSpec((1, N_MOD, d), _mod_index(TM_OUT, seq_len, mod.shape[0])),
                  pl.BlockSpec((TM_OUT, CONV_CH), row),
                  pl.BlockSpec((TM_OUT, RWKV_DIM), row),
                  pl.BlockSpec((TM_OUT, RWKV_DIM), row),
                  pl.BlockSpec((TM_OUT, 3 * RWKV_DIM), row),
                  pl.BlockSpec((TM_OUT, LORA_COLS), row)] + [full(a) for a in consts],
        out_specs=[pl.BlockSpec((TM_OUT, d), row),
                   pl.BlockSpec((TM_OUT, d // 2), row),
                   pl.BlockSpec((TM_OUT, LANES), row),
                   pl.BlockSpec((ROUTE_ROWS, TM_OUT), lambda i: (0, i)),
                   pl.BlockSpec((1, LANES), c2)],
        out_shape=[jax.ShapeDtypeStruct((n, d), F32),
                   jax.ShapeDtypeStruct((n, d // 2), jnp.int32),
                   jax.ShapeDtypeStruct((n, LANES), F32),
                   jax.ShapeDtypeStruct((ROUTE_ROWS, n), F32),
                   jax.ShapeDtypeStruct((1, LANES), F32)],
        compiler_params=_params("arbitrary"),
        name="out_router",
    )(x, mod, conv_out, yf, yb, rkv, lora, *consts)


def _sc_mesh():
    return plsc.VectorSubcoreMesh(core_axis_name="c", subcore_axis_name="s",
                                  num_cores=SC_CORES, num_subcores=SC_SUBCORES)


def _sc_worker_id():
    return lax.axis_index("s") * SC_CORES + lax.axis_index("c")


def _sc_scatter_rows(src_a, src_b, idx, n_out):
    (n_a, d), n_b = src_a.shape, src_b.shape[0]
    n = n_a + n_b
    workers = SC_CORES * SC_SUBCORES
    per_worker = n // workers
    assert idx.shape[0] == TOP_K * n and per_worker * workers == n
    assert per_worker % SC_ROWS == 0 and n_a % per_worker == 0

    @functools.partial(
        pl.kernel, mesh=_sc_mesh(),
        out_type=jax.ShapeDtypeStruct((n_out, d), src_a.dtype),
        scratch_types=[pltpu.VMEM((SC_ROWS,), jnp.int32),
                       pltpu.VMEM((SC_ROWS, d), src_a.dtype),
                       pltpu.SemaphoreType.DMA],
        name="sc_scatter_rows")
    def scatter(a_hbm, b_hbm, idx_hbm, out_hbm, idx_v, rows_v, sem):
        base = _sc_worker_id() * per_worker

        @pl.loop(0, per_worker // SC_ROWS)
        def _(j):
            tok = base + j * SC_ROWS

            @pl.when(tok < n_a)
            def _():
                pltpu.sync_copy(a_hbm.at[pl.ds(tok, SC_ROWS)], rows_v)

            @pl.when(tok >= n_a)
            def _():
                pltpu.sync_copy(b_hbm.at[pl.ds(tok - n_a, SC_ROWS)], rows_v)

            for q in range(TOP_K):
                pltpu.sync_copy(idx_hbm.at[pl.ds(q * n + tok, SC_ROWS)], idx_v)
                pltpu.async_copy(rows_v, out_hbm.at[idx_v], sem).wait()

    return scatter(src_a, src_b, idx)


def _sc_gather_rows(src, idx):
    n_idx, d = idx.shape[0], src.shape[1]
    workers = SC_CORES * SC_SUBCORES
    per_worker = n_idx // workers
    assert per_worker * workers == n_idx and per_worker % SC_ROWS == 0

    @functools.partial(
        pl.kernel, mesh=_sc_mesh(),
        out_type=jax.ShapeDtypeStruct((n_idx, d), src.dtype),
        scratch_types=[pltpu.VMEM((SC_ROWS,), jnp.int32),
                       pltpu.VMEM((SC_ROWS, d), src.dtype),
                       pltpu.SemaphoreType.DMA],
        name="sc_gather_rows")
    def gather(src_hbm, idx_hbm, out_hbm, idx_v, rows_v, sem):
        base = _sc_worker_id() * per_worker

        @pl.loop(0, per_worker // SC_ROWS)
        def _(j):
            off = base + j * SC_ROWS
            pltpu.sync_copy(idx_hbm.at[pl.ds(off, SC_ROWS)], idx_v)
            pltpu.async_copy(src_hbm.at[idx_v], rows_v, sem).wait()
            pltpu.sync_copy(rows_v, out_hbm.at[pl.ds(off, SC_ROWS)])

    return gather(src, idx)


def _moe_kernel(blk_e_ref, blk_first_ref, blk_rows_ref, blk_next_ref,
                xs_ref, wgu_hbm, bgu_ref, wd_hbm, bd_ref,
                o_ref, wgu_f32_ref, wd_f32_ref, wgu_bf_ref, wd_bf_ref, sem):
    b = pl.program_id(0)
    n_rows = blk_rows_ref[b]

    def weight_copies(e):
        return (pltpu.make_async_copy(wgu_hbm.at[e], wgu_f32_ref, sem.at[0]),
                pltpu.make_async_copy(wd_hbm.at[e], wd_f32_ref, sem.at[1]))

    @pl.when(b == 0)
    def _():
        for copy in weight_copies(blk_e_ref[0]):
            copy.start()

    @pl.when(blk_first_ref[b] == 1)
    def _():
        for copy in weight_copies(blk_e_ref[b]):
            copy.wait()
        wgu_bf_ref[...] = wgu_f32_ref[...].astype(BF16)
        wd_bf_ref[...] = wd_f32_ref[...].astype(BF16)

        @pl.when(blk_next_ref[b] >= 0)
        def _():
            for copy in weight_copies(blk_next_ref[b]):
                copy.start()

    @pl.when(n_rows > 0)
    def _():
        live = lax.broadcasted_iota(jnp.int32, xs_ref.shape, 0) < n_rows
        xs = _unpack_rows(jnp.where(live, xs_ref[...], 0))
        gu = jnp.dot(xs.astype(BF16), wgu_bf_ref[...],
                     preferred_element_type=F32) + bgu_ref[0]
        x_glu = jnp.minimum(gu[:, :D_FF], SWIGLU_LIMIT)
        x_lin = jnp.clip(gu[:, D_FF:], -SWIGLU_LIMIT, SWIGLU_LIMIT)
        act = x_glu * _sigmoid(SWIGLU_ALPHA * x_glu) * (x_lin + 1.0)
        o_ref[...] = _pack_rows(jnp.dot(act.astype(BF16), wd_bf_ref[...],
                                        preferred_element_type=F32) + bd_ref[0])

    @pl.when(n_rows == 0)
    def _():
        o_ref[...] = jnp.zeros(o_ref.shape, jnp.int32)


def _moe_call(blk_e, blk_first, blk_rows, blk_next, xs, wgu, bgu, wd, bd):
    n_slots, half = xs.shape
    d = 2 * half
    ex = lambda b, be, bf, br, bn: (be[b], 0, 0)
    row = lambda b, be, bf, br, bn: (b, 0)
    grid_spec = pltpu.PrefetchScalarGridSpec(
        num_scalar_prefetch=4,
        grid=(n_slots // MOE_BLOCK,),
        in_specs=[pl.BlockSpec((MOE_BLOCK, half), row),
                  pl.BlockSpec(memory_space=pl.ANY),
                  pl.BlockSpec((1, 1, 2 * D_FF), ex),
                  pl.BlockSpec(memory_space=pl.ANY),
                  pl.BlockSpec((1, 1, d), ex)],
        out_specs=pl.BlockSpec((MOE_BLOCK, half), row),
        scratch_shapes=[pltpu.VMEM((d, 2 * D_FF), F32), pltpu.VMEM((D_FF, d), F32),
                        pltpu.VMEM((d, 2 * D_FF), BF16), pltpu.VMEM((D_FF, d), BF16),
                        pltpu.SemaphoreType.DMA((2,))])
    return pl.pallas_call(
        _moe_kernel,
        grid_spec=grid_spec,
        out_shape=jax.ShapeDtypeStruct((n_slots, half), jnp.int32),
        compiler_params=_params("arbitrary"),
        name="moe_experts",
    )(blk_e, blk_first, blk_rows, blk_next, xs, wgu, bgu, wd, bd)


def _final_kernel(x1_ref, y4_ref, route_ref, mod_ref, g_ref, o_ref):
    m = mod_ref[0]
    route = route_ref[...]
    y = jnp.zeros(x1_ref.shape, F32)
    for q in range(TOP_K):
        y = y + route[:, TOP_K + q:TOP_K + q + 1] * _unpack_rows(y4_ref[q])
    o_ref[...] = x1_ref[...] + m[5:6] * _rms(y, g_ref[...])


def _final_call(x1, y4, route, mod, g, seq_len):
    n, d = x1.shape
    row = lambda i: (i, 0)
    return pl.pallas_call(
        _final_kernel,
        grid=(n // TM_FIN,),
        in_specs=[pl.BlockSpec((TM_FIN, d), row),
                  pl.BlockSpec((TOP_K, TM_FIN, d // 2), lambda i: (0, i, 0)),
                  pl.BlockSpec((TM_FIN, LANES), row),
                  pl.BlockSpec((1, N_MOD, d), _mod_index(TM_FIN, seq_len, mod.shape[0])),
                  pl.BlockSpec((1, d), lambda i: (0, 0))],
        out_specs=pl.BlockSpec((TM_FIN, d), row),
        out_shape=jax.ShapeDtypeStruct((n, d), F32),
        compiler_params=_params("arbitrary"),
        name="final_residual",
    )(x1, y4, route, mod, g)


def _pack_state(s):
    b = s.shape[0]
    st = jnp.swapaxes(s.astype(F32), -1, -2).reshape(b, N_PAIRS, 2, HEAD_DIM, HEAD_DIM)
    z = jnp.zeros_like(st[:, :, 0])
    top = jnp.concatenate([st[:, :, 0], z], axis=-1)
    bot = jnp.concatenate([z, st[:, :, 1]], axis=-1)
    return jnp.concatenate([top, bot], axis=-2)


def _mix_sublayer(x, mod, s0f, s0b, seg_len, wts):
    batch, seq_len, d = x.shape
    x2 = x.reshape(batch * seq_len, d)
    cv, rkv, lora = _in_call(x2, mod, wts["pre_mix_g"], wts["w_all"], seq_len)
    conv_out = _conv_call(cv, wts["conv_w"], wts["conv_b"], wts["conv_ln_g"], wts["conv_ln_b"],
                          seg_len)
    yf, yb, sf, sb = _wkv_call(rkv, lora, s0f, s0b, wts["w2p"], wts["w0"], wts["a2p"], wts["a0"],
                               wts["k_k"], wts["k_a"], batch, seq_len)
    x1, h2, route, route_t, cnt = _out_call(x2, mod, conv_out, yf, yb, rkv, lora, wts, seq_len)
    return x1, h2, route, cnt, sf, sb, route_t


def _dispatch_plan(routes, cnts):
    experts = jnp.arange(N_EXPERTS, dtype=jnp.int32)
    counts = [c[0, :N_EXPERTS].astype(jnp.int32) for c in cnts]
    total = sum(counts)
    padded = (total + MOE_BLOCK - 1) // MOE_BLOCK * MOE_BLOCK
    pad_end = jnp.cumsum(padded)
    pad_start = pad_end - padded
    dests = []
    prior = jnp.zeros_like(total)
    for route, count in zip(routes, counts):
        idx = route[:TOP_K].astype(jnp.int32)
        rank = route[2 * TOP_K:3 * TOP_K].astype(jnp.int32)
        first = jnp.sum(jnp.where(idx[..., None] == experts, pad_start + prior, 0), axis=-1)
        dests.append(first + rank)
        prior = prior + count
    dest = jnp.concatenate(dests, axis=1).reshape(-1)
    n_slots = dest.shape[0] + N_EXPERTS * MOE_BLOCK
    blk_row0 = jnp.arange(n_slots // MOE_BLOCK, dtype=jnp.int32) * MOE_BLOCK
    blk_e = jnp.sum((pad_end[None, :] <= blk_row0[:, None]).astype(jnp.int32), axis=-1)
    blk_e = jnp.minimum(blk_e, N_EXPERTS - 1)
    blk_first = jnp.concatenate([jnp.ones((1,), jnp.int32),
                                 (blk_e[1:] != blk_e[:-1]).astype(jnp.int32)])
    of_blk = lambda per_expert: jnp.sum(
        jnp.where(blk_e[:, None] == experts[None, :], per_expert[None, :], 0), axis=-1)
    blk_rows = jnp.clip(of_blk(pad_start + total) - blk_row0, 0, MOE_BLOCK).astype(jnp.int32)
    pos = jnp.arange(blk_e.shape[0], dtype=jnp.int32)
    later_first = (pos[None, :] > pos[:, None]) & (blk_first[None, :] == 1)
    next_pos = jnp.min(jnp.where(later_first, pos[None, :], pos.shape[0]), axis=-1)
    blk_next = jnp.where(next_pos < pos.shape[0],
                         blk_e[jnp.minimum(next_pos, pos.shape[0] - 1)], -1).astype(jnp.int32)
    return dest, n_slots, blk_e, blk_first, blk_rows, blk_next


def _routed_ffn(h2s, routes, cnts, wts):
    dest, n_slots, blk_e, blk_first, blk_rows, blk_next = _dispatch_plan(routes, cnts)
    xs = _sc_scatter_rows(h2s[0], h2s[1], dest, n_slots)
    out_sorted = _moe_call(blk_e, blk_first, blk_rows, blk_next, xs, wts["w_gu"], wts["b_gu"],
                           wts["w_down"], wts["b_down"])
    dest_qt = dest.reshape(TOP_K, -1)
    y4s, row0 = [], 0
    for h2 in h2s:
        n_path = h2.shape[0]
        y4 = _sc_gather_rows(out_sorted, dest_qt[:, row0:row0 + n_path].reshape(-1))
        y4s.append(y4.reshape(TOP_K, n_path, xs.shape[1]))
        row0 += n_path
    return y4s


def _prep_weights(l, pre_mix_g, post_mix_g, pre_ffn_g, post_ffn_g, w_in, w_out, conv_w, conv_b,
                  conv_ln_g, conv_ln_b, rw_w0, rw_w1, rw_w2, rw_a0, rw_a1, rw_a2, rw_g1, rw_g2,
                  rw_k_k, rw_k_a, rw_r_k, rw_gn_g, rw_gn_b, router_w, router_b, w_gu, b_gu,
                  w_down, b_down):
    row = lambda a: a.reshape(1, -1).astype(F32)
    zpad = jnp.zeros((HEAD_DIM, RWKV_DIM), F32)
    w_all = jnp.concatenate([w_in[l], rw_w1[l, 0], rw_w1[l, 1], rw_a1[l, 0], rw_a1[l, 1],
                             rw_g1[l]], axis=1).astype(BF16)
    w2p = jnp.stack([jnp.concatenate([rw_w2[l, 0], zpad], axis=0),
                     jnp.concatenate([zpad, rw_w2[l, 1]], axis=0)]).astype(BF16)
    a2p = jnp.stack([jnp.concatenate([rw_a2[l, 0], zpad], axis=0),
                     jnp.concatenate([zpad, rw_a2[l, 1]], axis=0)]).astype(BF16)
    rw_pad = jnp.pad(router_w[l].astype(F32), ((0, 0), (0, LANES - N_EXPERTS)))
    rw_hi = rw_pad.astype(BF16)
    rw_lo = (rw_pad - rw_hi.astype(F32)).astype(BF16)
    rb_pad = jnp.concatenate([router_b[l].astype(F32),
                              jnp.full((LANES - N_EXPERTS,), NEG_BIG, F32)]).reshape(1, LANES)
    return {
        "pre_mix_g": row(pre_mix_g[l]), "post_mix_g": row(post_mix_g[l]),
        "pre_ffn_g": row(pre_ffn_g[l]), "post_ffn_g": row(post_ffn_g[l]),
        "w_all": w_all, "w_out": w_out[l].astype(BF16),
        "conv_w": jnp.broadcast_to(conv_w[l].astype(F32)[:, None, :],
                                   (CONV_WIDTH, SUBLANES, CONV_CH)),
        "conv_b": row(conv_b[l]),
        "conv_ln_g": row(conv_ln_g[l]), "conv_ln_b": row(conv_ln_b[l]),
        "w2p": w2p, "w0": rw_w0[l].astype(F32), "a2p": a2p, "a0": rw_a0[l].astype(F32),
        "k_k": row(rw_k_k[l]), "k_a": row(rw_k_a[l]), "r_k": row(rw_r_k[l]),
        "gn_g": row(rw_gn_g[l]), "gn_b": row(rw_gn_b[l]), "g2": rw_g2[l].astype(BF16),
        "router_w": jnp.stack([rw_hi, rw_lo]), "router_b": rb_pad,
        "w_gu": w_gu[l].astype(F32), "b_gu": b_gu[l].reshape(N_EXPERTS, 1, -1).astype(F32),
        "w_down": w_down[l].astype(F32), "b_down": b_down[l].reshape(N_EXPERTS, 1, -1).astype(F32),
    }


def kernel(x_prompt, x_sample, state_wkv_fwd, state_wkv_bwd, c, c_ctx, ada_w, ada_b, pre_mix_g, post_mix_g, pre_ffn_g, post_ffn_g, w_in, w_out, conv_w, conv_b, conv_ln_g, conv_ln_b, rw_w0, rw_w1, rw_w2, rw_a0, rw_a1, rw_a2, rw_g1, rw_g2, rw_k_k, rw_k_a, rw_r_k, rw_gn_g, rw_gn_b, router_w, router_b, w_gu, b_gu, w_down, b_down):
    depth = ada_w.shape[0]
    dec_batch = c.shape[0]
    mod_rows = 16
    c_rows = jnp.concatenate([c, c_ctx[None, :],
                              jnp.zeros((mod_rows - dec_batch - 1, D_MODEL), F32)], axis=0)
    zero_state = jnp.zeros((WKV_SEQS, N_PAIRS, LANES, LANES), F32)
    y_prompt, y_sample = x_prompt, x_sample
    new_f, new_b = [], []
    for l in range(depth):
        wts = _prep_weights(l, pre_mix_g, post_mix_g, pre_ffn_g, post_ffn_g, w_in, w_out, conv_w,
                            conv_b, conv_ln_g, conv_ln_b, rw_w0, rw_w1, rw_w2, rw_a0, rw_a1, rw_a2,
                            rw_g1, rw_g2, rw_k_k, rw_k_a, rw_r_k, rw_gn_g, rw_gn_b, router_w,
                            router_b, w_gu, b_gu, w_down, b_down)
        mod = _mod_call(c_rows, ada_w[l], ada_b[l]).reshape(mod_rows, N_MOD, D_MODEL)
        mod_lat = mod[:dec_batch]
        mod_ctx = mod[dec_batch:dec_batch + 1]
        ctx = _mix_sublayer(y_prompt, mod_ctx, zero_state, zero_state, y_prompt.shape[1], wts)
        lat = _mix_sublayer(y_sample, mod_lat, _pack_state(state_wkv_fwd[:, l]),
                            _pack_state(state_wkv_bwd[:, l]), GRID_W, wts)
        new_f.append(ctx[4])
        new_b.append(ctx[5])
        y4_ctx, y4_lat = _routed_ffn([ctx[1], lat[1]], [ctx[6], lat[6]], [ctx[3], lat[3]], wts)
        y_prompt = _final_call(ctx[0], y4_ctx, ctx[2], mod_ctx, wts["post_ffn_g"],
                               y_prompt.shape[1]).reshape(y_prompt.shape)
        y_sample = _final_call(lat[0], y4_lat, lat[2], mod_lat, wts["post_ffn_g"],
                               y_sample.shape[1]).reshape(y_sample.shape)
    return (y_prompt, y_sample, jnp.stack(new_f, axis=1), jnp.stack(new_b, axis=1))
```

```python
import functools
import math

import jax
import jax.numpy as jnp
from jax import lax
from jax.experimental import pallas as pl
from jax.experimental.pallas import tpu as pltpu
from jax.experimental.pallas import tpu_sc as plsc

F32 = jnp.float32
BF16 = jnp.bfloat16

D_MODEL = 1024
CONV_CH = 512
RWKV_DIM = 512
HEAD_DIM = 64
RWKV_HEADS = 8
N_PAIRS = RWKV_HEADS // 2
CONV_WIDTH = 31
N_EXPERTS = 32
TOP_K = 4
D_FF = 1024
SWIGLU_LIMIT = 7.0
SWIGLU_ALPHA = 1.702
RMS_EPS = 1e-6
LN_EPS = 1e-5
GN_EPS = 64e-5
N_MOD = 6
GRID_W = 64

LANES = 128
SUBLANES = 8
CHUNK = 64
WKV_SEQS = 4
LORA_COLS = 384
TM_IN = 1024
TM_CONV = 512
TM_OUT = 512
TM_FIN = 1024
ROUTE_ROWS = 16
MOE_BLOCK = 512
SC_CORES = 2
SC_SUBCORES = 16
SC_ROWS = 128
NEG_BIG = -1e30
EXP_M05 = math.exp(-0.5)
VMEM_LIMIT = 56 * 1024 * 1024


def _sigmoid(x):
    return 1.0 / (1.0 + jnp.exp(-x))


def _mm(a, b):
    return jnp.dot(a.astype(BF16), b.astype(BF16), preferred_element_type=F32)


def _mm_nt(a, b):
    return lax.dot_general(a.astype(BF16), b.astype(BF16), (((1,), (1,)), ((), ())),
                           preferred_element_type=F32)


def _mm_tn(a, b):
    return lax.dot_general(a.astype(BF16), b.astype(BF16), (((0,), (0,)), ((), ())),
                           preferred_element_type=F32)


def _split2(x):
    hi = x.astype(BF16)
    lo = (x - hi.astype(F32)).astype(BF16)
    return hi, lo


def _split3(x):
    h1 = x.astype(BF16)
    r1 = x - h1.astype(F32)
    h2 = r1.astype(BF16)
    h3 = (r1 - h2.astype(F32)).astype(BF16)
    return h1, h2, h3


def _pack_rows(x):
    n = x.shape[1] // 2
    bits = lambda v: lax.bitcast_convert_type(v.astype(BF16).astype(F32), jnp.uint32)
    word = (bits(x[:, n:]) & jnp.uint32(0xFFFF0000)) | (bits(x[:, :n]) >> 16)
    return lax.bitcast_convert_type(word, jnp.int32)


def _unpack_rows(w):
    u = lax.bitcast_convert_type(w, jnp.uint32)
    lo = lax.bitcast_convert_type(u << 16, F32)
    hi = lax.bitcast_convert_type(u & jnp.uint32(0xFFFF0000), F32)
    return jnp.concatenate([lo, hi], axis=1)


def _rms(x, g):
    return x * lax.rsqrt(jnp.mean(x * x, axis=-1, keepdims=True) + RMS_EPS) * g


def _params(*sem):
    return pltpu.CompilerParams(dimension_semantics=sem, vmem_limit_bytes=VMEM_LIMIT)


def _mod_kernel(c_ref, w_ref, b_ref, o_ref):
    c = c_ref[...]
    s1, s2 = _split2(c * _sigmoid(c))
    w1, w2 = _split2(w_ref[...])
    dot = functools.partial(jnp.dot, preferred_element_type=F32)
    o_ref[...] = dot(s1, w1) + (dot(s1, w2) + dot(s2, w1)) + b_ref[...]


def _mod_call(c_rows, ada_w, ada_b):
    m, d = c_rows.shape
    n = ada_w.shape[1]
    tn = D_MODEL
    return pl.pallas_call(
        _mod_kernel,
        grid=(n // tn,),
        in_specs=[pl.BlockSpec((m, d), lambda j: (0, 0)),
                  pl.BlockSpec((d, tn), lambda j: (0, j)),
                  pl.BlockSpec((1, tn), lambda j: (0, j))],
        out_specs=pl.BlockSpec((m, tn), lambda j: (0, j)),
        out_shape=jax.ShapeDtypeStruct((m, n), F32),
        compiler_params=_params("arbitrary"),
        name="mod",
    )(c_rows, ada_w, ada_b.reshape(1, n))


def _in_kernel(x_ref, mod_ref, g_ref, w_ref, cv_ref, rkv_ref, lo_ref):
    m = mod_ref[0]
    h = _rms(x_ref[...], g_ref[...]) * (1.0 + m[1:2]) + m[0:1]
    proj = jnp.dot(h.astype(BF16), w_ref[...], preferred_element_type=F32)
    cv_ref[...] = proj[:, :CONV_CH] * _sigmoid(proj[:, CONV_CH:2 * CONV_CH])
    rkv_ref[...] = proj[:, 2 * CONV_CH:2 * CONV_CH + 3 * RWKV_DIM]
    lo_ref[...] = proj[:, 2 * CONV_CH + 3 * RWKV_DIM:]


def _mod_index(tile, seq_len, n_mod_rows):
    def index(i):
        return ((i * tile) // seq_len) % n_mod_rows, 0, 0
    return index


def _in_call(x, mod, g, w_all, seq_len):
    n, d = x.shape
    ncol = w_all.shape[1]
    return pl.pallas_call(
        _in_kernel,
        grid=(n // TM_IN,),
        in_specs=[pl.BlockSpec((TM_IN, d), lambda i: (i, 0)),
                  pl.BlockSpec((1, N_MOD, d), _mod_index(TM_IN, seq_len, mod.shape[0])),
                  pl.BlockSpec((1, d), lambda i: (0, 0)),
                  pl.BlockSpec((d, ncol), lambda i: (0, 0))],
        out_specs=[pl.BlockSpec((TM_IN, CONV_CH), lambda i: (i, 0)),
                   pl.BlockSpec((TM_IN, 3 * RWKV_DIM), lambda i: (i, 0)),
                   pl.BlockSpec((TM_IN, LORA_COLS), lambda i: (i, 0))],
        out_shape=[jax.ShapeDtypeStruct((n, CONV_CH), F32),
                   jax.ShapeDtypeStruct((n, 3 * RWKV_DIM), F32),
                   jax.ShapeDtypeStruct((n, LORA_COLS), F32)],
        compiler_params=_params("arbitrary"),
        name="in_proj",
    )(x, mod, g, w_all)


CONV_HALO = 16
CONV_ROWS = 32


def _conv_kernel(cv_ref, w_ref, b_ref, g_ref, be_ref, o_ref, pad_ref, acc_ref, *, seg_len):
    nseg = TM_CONV // seg_len
    stride = seg_len + 2 * CONV_HALO
    rows = nseg * stride
    u = cv_ref[...]
    pad_ref[0] = jnp.zeros(pad_ref.shape[1:], F32)
    for s in range(nseg):
        pad_ref[0, s * stride + CONV_HALO:s * stride + CONV_HALO + seg_len, :] = (
            u[s * seg_len:(s + 1) * seg_len])
    for r in range(1, SUBLANES):
        pad_ref[r, 0:rows, :] = pad_ref[0, r:r + rows, :]
    first = CONV_HALO - CONV_WIDTH // 2
    chunks_per_seg = seg_len // CONV_ROWS

    def row_chunk(i, carry):
        pad0 = i * CONV_ROWS + (i // chunks_per_seg) * (2 * CONV_HALO)
        accs = [jnp.zeros((SUBLANES, CONV_CH), F32) for _ in range(CONV_ROWS // SUBLANES)]
        for t in range(CONV_WIDTH):
            q, r = divmod(first + t, SUBLANES)
            w_t = w_ref[t]
            for j in range(len(accs)):
                start = pl.multiple_of(pad0 + (q + j) * SUBLANES, SUBLANES)
                accs[j] = accs[j] + w_t * pad_ref[r, pl.ds(start, SUBLANES), :]
        for j, acc in enumerate(accs):
            row0 = pl.multiple_of(i * CONV_ROWS + j * SUBLANES, SUBLANES)
            acc_ref[pl.ds(row0, SUBLANES), :] = acc
        return carry

    lax.fori_loop(0, TM_CONV // CONV_ROWS, row_chunk, 0)
    y = acc_ref[...] + b_ref[...]
    mu = jnp.mean(y, axis=-1, keepdims=True)
    yc = y - mu
    var = jnp.mean(yc * yc, axis=-1, keepdims=True)
    z = yc * lax.rsqrt(var + LN_EPS) * g_ref[...] + be_ref[...]
    o_ref[...] = z * _sigmoid(z)


def _conv_call(cv, conv_w, conv_b, ln_g, ln_b, seg_len):
    n = cv.shape[0]
    nseg = TM_CONV // seg_len
    row = lambda i: (i, 0)
    const = lambda i: (0, 0)
    return pl.pallas_call(
        functools.partial(_conv_kernel, seg_len=seg_len),
        grid=(n // TM_CONV,),
        in_specs=[pl.BlockSpec((TM_CONV, CONV_CH), row),
                  pl.BlockSpec((CONV_WIDTH, SUBLANES, CONV_CH), lambda i: (0, 0, 0)),
                  pl.BlockSpec((1, CONV_CH), const),
                  pl.BlockSpec((1, CONV_CH), const),
                  pl.BlockSpec((1, CONV_CH), const)],
        out_specs=pl.BlockSpec((TM_CONV, CONV_CH), row),
        out_shape=jax.ShapeDtypeStruct((n, CONV_CH), F32),
        scratch_shapes=[pltpu.VMEM((SUBLANES, nseg * (seg_len + 2 * CONV_HALO) + SUBLANES, CONV_CH),
                                   F32),
                        pltpu.VMEM((TM_CONV, CONV_CH), F32)],
        compiler_params=_params("arbitrary"),
        name="conv_module",
    )(cv, conv_w, conv_b, ln_g, ln_b)


def _head_sums(x, ones_bd):
    hi, lo = _split2(x)
    return jnp.dot(jnp.concatenate([hi, lo], axis=1), jnp.concatenate([ones_bd, ones_bd], axis=0),
                   preferred_element_type=F32)


def _wkv_masks():
    c = CHUNK
    r2 = lax.broadcasted_iota(jnp.int32, (2 * c, 2 * c), 0)
    c2 = lax.broadcasted_iota(jnp.int32, (2 * c, 2 * c), 1)
    same = (r2 // c) == (c2 // c)
    rr, cc = r2 % c, c2 % c
    r1 = lax.broadcasted_iota(jnp.int32, (c, 2 * c), 0)
    c1 = lax.broadcasted_iota(jnp.int32, (c, 2 * c), 1)
    cc1 = c1 % c
    rs = lax.broadcasted_iota(jnp.int32, (c, c), 0)
    cs = lax.broadcasted_iota(jnp.int32, (c, c), 1)
    masks = {"same": same, "eye": r2 == c2, "left": c1 < c,
             "ones_bd": same.astype(BF16)}
    for name, lt in (("f", lambda a, b: a < b), ("b", lambda a, b: a > b)):
        le = (lambda a, b, lt=lt: lt(a, b) | (a == b))
        masks["strict_" + name] = same & lt(cc, rr)
        masks["inc_" + name] = le(cc1, r1)
        masks["ak0_" + name] = (c1 >= c) & lt(cc1, r1)
        masks["ak1_" + name] = (c1 < c) & lt(cc1, r1)
        masks["tri_" + name] = le(cs, rs).astype(BF16)
    return masks


def _wkv_chains(chains, mk):
    c = CHUNK
    left = mk["left"]
    for ch in chains:
        zero = jnp.zeros_like(ch["at"])
        ch["at0"], ch["at1"] = jnp.where(left, ch["at"], zero), jnp.where(left, zero, ch["at"])
        rt0, rt1 = jnp.where(left, ch["rt"], zero), jnp.where(left, zero, ch["rt"])
        bt, kt = ch["bt"].astype(BF16), ch["kt"].astype(BF16)
        ch["g0"] = _mm_nt(jnp.concatenate([ch["at0"], rt0], axis=0), jnp.concatenate([bt, kt], axis=0))
        ch["g1"] = _mm_nt(jnp.concatenate([ch["at1"], rt1], axis=0), jnp.concatenate([kt, bt], axis=0))
    yield
    for ch in chains:
        g0, g1 = ch["g0"], ch["g1"]
        lmat = jnp.where(mk["strict_" + ch["d"]], jnp.concatenate([g0[:c], g1[:c]], axis=0), 0.0)
        ch["tinv"] = jnp.where(mk["eye"], 1.0, lmat)
        ch["lpow"] = _mm(lmat, lmat)
    for step in range(5):
        yield
        for ch in chains:
            lp = ch["lpow"].astype(BF16)
            if step < 4:
                prod = jnp.dot(lp, jnp.concatenate([lp, ch["tinv"].astype(BF16)], axis=1),
                               preferred_element_type=F32)
                ch["lpow"] = prod[:, :LANES]
                ch["tinv"] = ch["tinv"] + prod[:, LANES:]
            else:
                ch["tinv"] = ch["tinv"] + _mm(lp, ch["tinv"])
    yield
    for ch in chains:
        d, v = ch["d"], ch["v"]
        zv = jnp.zeros_like(v)
        ak0 = jnp.where(mk["ak0_" + d], ch["g0"][:c], 0.0)
        ak1 = jnp.where(mk["ak1_" + d], ch["g1"][:c], 0.0)
        ch["x0"] = _mm(ak0, jnp.concatenate([zv, v], axis=0))
        ch["x1"] = _mm(ak1, jnp.concatenate([v, zv], axis=0))
    yield
    for ch in chains:
        zero = jnp.zeros_like(ch["x0"])
        rhs = jnp.concatenate([
            jnp.concatenate([ch["at0"], jnp.where(left, ch["x0"], zero)], axis=1),
            jnp.concatenate([ch["at1"], jnp.where(left, zero, ch["x1"])], axis=1)], axis=0)
        ch["aw_blk"] = _mm(ch["tinv"], rhs)
    yield
    for ch in chains:
        d, v, aw_blk = ch["d"], ch["v"], ch["aw_blk"]
        zv = jnp.zeros_like(v)
        aw = aw_blk[:c] + aw_blk[c:]
        inc = mk["inc_" + d]
        g0, g1 = ch["g0"], ch["g1"]
        rb = jnp.where(inc, jnp.where(left, g0[c:], g1[c:]), 0.0)
        rk = jnp.where(inc, jnp.where(left, g1[c:], g0[c:]), 0.0)
        v0, v1 = jnp.where(left, v, zv), jnp.where(left, zv, v)
        vblk = jnp.concatenate([jnp.zeros((2 * c, LANES), F32),
                                jnp.concatenate([v1, v0], axis=0)], axis=1)
        qz = _mm(jnp.concatenate([rb, rk], axis=1), jnp.concatenate([aw_blk, vblk], axis=0))
        ch["q"] = ch["rt"] + qz[:, :LANES]
        ch["zz"] = qz[:, LANES:]
        rhs3 = jnp.concatenate([aw, jnp.concatenate([zv, v], axis=1)], axis=0)
        ch["mn"] = _mm_tn(jnp.concatenate([ch["bh"], ch["kh"]], axis=0), rhs3)
    yield
    same = mk["same"]
    for ch in chains:
        mn = ch["mn"]
        mt = jnp.where(same, mn[:, :LANES], 0.0) + jnp.where(mk["eye"], ch["ptot"], 0.0)
        nt = jnp.where(same, mn[:, LANES:], 0.0)
        both = _mm(jnp.concatenate([ch["q"], mt], axis=0), ch["st"])
        ch["y"] = both[:c] + ch["zz"]
        ch["st_new"] = both[c:] + nt


WKV_OPERANDS = ("at", "rt", "bt", "kt", "bh", "kh", "v")


def _wkv_prep(*args):
    for e in range(WKV_SEQS):
        yield from _wkv_prep_seq(e, *args)


def _wkv_prep_seq(e, rkv_refs, lo_refs, w2_ref, w0_ref, a2_ref, a0_ref, kk_ref, ka_ref, mk,
                  ops_ref, ptot_ref, slot):
    c = CHUNK
    for di, (d, rkv_ref, lo_ref) in enumerate(zip("fb", rkv_refs, lo_refs)):
        rkv = rkv_ref[e]
        lo = lo_ref[e]
        r = rkv[:, :RWKV_DIM]
        k = rkv[:, RWKV_DIM:2 * RWKV_DIM]
        v = rkv[:, 2 * RWKV_DIM:]
        w_raw = w0_ref[di:di + 1, :] + _mm(jnp.tanh(lo[:, :LANES]), w2_ref[di])
        lw = -EXP_M05 * _sigmoid(w_raw)
        yield
        ag = _sigmoid(a0_ref[di:di + 1, :] + _mm(lo[:, LANES:2 * LANES], a2_ref[di]))
        kd = k * (1.0 + (ag - 1.0) * ka_ref[...])
        kk_raw = k * kk_ref[...]
        h1, h2, h3 = _split3(lw)
        cs = jnp.dot(mk["tri_" + d], jnp.concatenate([h1, h2, h3], axis=1),
                     preferred_element_type=F32)
        logp = cs[:, :RWKV_DIM] + cs[:, RWKV_DIM:2 * RWKV_DIM] + cs[:, 2 * RWKV_DIM:]
        yield
        tot = logp[0:1] if d == "b" else logp[c - 1:c]
        e_inv = jnp.exp(-logp)
        e_hat = jnp.exp(tot - logp)
        rt = r * jnp.exp(logp)
        e_ex = jnp.exp(logp - lw)
        kt, kh = kd * e_inv, kd * e_hat
        ptot = jnp.exp(tot)
        for p in range(N_PAIRS):
            yield
            cols = slice(p * LANES, (p + 1) * LANES)
            kk = kk_raw[:, cols]
            kk = kk * lax.rsqrt(jnp.maximum(_head_sums(kk * kk, mk["ones_bd"]), 1e-24))
            kb = kk * ag[:, cols]
            tiles = {"at": -kk * e_ex[:, cols], "rt": rt[:, cols], "bt": kb * e_inv[:, cols],
                     "kt": kt[:, cols], "bh": kb * e_hat[:, cols], "kh": kh[:, cols],
                     "v": v[:, cols]}
            chain = (e * 2 + di) * N_PAIRS + p
            for j, name in enumerate(WKV_OPERANDS):
                ops_ref[slot, chain, j] = tiles[name]
            ptot_ref[slot, chain] = jnp.broadcast_to(ptot[:, cols], (SUBLANES, LANES))


def _wkv_kernel(rkv_f0_ref, lo_f0_ref, rkv_b0_ref, lo_b0_ref,
                rkv_f1_ref, lo_f1_ref, rkv_b1_ref, lo_b1_ref, s0f_ref, s0b_ref,
                w2_ref, w0_ref, a2_ref, a0_ref, kk_ref, ka_ref,
                yf_ref, yb_ref, sf_ref, sb_ref, st_ref, ops_ref, ptot_ref, *, nc):
    g = pl.program_id(0)
    ci = g % nc
    slot = g % 2
    mk = _wkv_masks()
    prm = (w2_ref, w0_ref, a2_ref, a0_ref, kk_ref, ka_ref, mk, ops_ref, ptot_ref)

    @pl.when(g == 0)
    def _():
        for _ in _wkv_prep((rkv_f0_ref, rkv_b0_ref), (lo_f0_ref, lo_b0_ref), *prm, 0):
            pass

    @pl.when(ci == 0)
    def _():
        for e in range(WKV_SEQS):
            st_ref[e, 0] = s0f_ref[e]
            st_ref[e, 1] = s0b_ref[e]

    chains = []
    for e in range(WKV_SEQS):
        for di, (d, y_ref) in enumerate((("f", yf_ref), ("b", yb_ref))):
            for p in range(N_PAIRS):
                chain = (e * 2 + di) * N_PAIRS + p
                ch = {"d": d, "e": e, "di": di, "p": p, "y_ref": y_ref, "st": st_ref[e, di, p],
                      "ptot": ptot_ref[slot, chain][0:1]}
                for j, name in enumerate(WKV_OPERANDS):
                    ch[name] = ops_ref[slot, chain, j]
                chains.append(ch)
    prep = _wkv_prep((rkv_f1_ref, rkv_b1_ref), (lo_f1_ref, lo_b1_ref), *prm, 1 - slot)
    for _ in _wkv_chains(chains, mk):
        for _ in range(WKV_SEQS):
            next(prep, None)
    for _ in prep:
        pass
    for ch in chains:
        ch["y_ref"][ch["e"], :, ch["p"] * LANES:(ch["p"] + 1) * LANES] = ch["y"]
        st_ref[ch["e"], ch["di"], ch["p"]] = ch["st_new"]

    @pl.when(ci == nc - 1)
    def _():
        for e in range(WKV_SEQS):
            for di, s_ref in enumerate((sf_ref, sb_ref)):
                for p in range(N_PAIRS):
                    st_t = st_ref[e, di, p].T
                    s_ref[e, 2 * p] = st_t[:HEAD_DIM, :HEAD_DIM]
                    s_ref[e, 2 * p + 1] = st_t[HEAD_DIM:, HEAD_DIM:]


def _wkv_call(rkv, lora, s0f, s0b, w2p, w0, a2p, a0, k_k, k_a, batch, seq_len):
    n = rkv.shape[0]
    nc = seq_len // CHUNK
    assert batch % WKV_SEQS == 0
    steps = (batch // WKV_SEQS) * nc
    rkv3 = rkv.reshape(batch, seq_len, 3 * RWKV_DIM)
    lora3 = lora.reshape(batch, seq_len, LORA_COLS)
    at = lambda g, back: (g // nc, (nc - 1 - g % nc) if back else g % nc, 0)
    fwd = lambda g: at(g, False)
    bwd = lambda g: at(g, True)
    fwd_first = lambda g: (0, 0, 0)
    bwd_first = lambda g: (0, nc - 1, 0)
    fwd_next = lambda g: at(jnp.minimum(g + 1, steps - 1), False)
    bwd_next = lambda g: at(jnp.minimum(g + 1, steps - 1), True)
    st = lambda g: (g // nc, 0, 0, 0)
    st_in = lambda g: ((g // nc) % (s0f.shape[0] // WKV_SEQS), 0, 0, 0)
    c2 = lambda g: (0, 0)
    c3 = lambda g: (0, 0, 0)
    st_shape = jax.ShapeDtypeStruct((batch, RWKV_HEADS, HEAD_DIM, HEAD_DIM), F32)
    y_shape = jax.ShapeDtypeStruct((batch, seq_len, RWKV_DIM), F32)
    n_chains = WKV_SEQS * 2 * N_PAIRS
    rkv_blk = (WKV_SEQS, CHUNK, 3 * RWKV_DIM)
    lora_blk = (WKV_SEQS, CHUNK, LORA_COLS)
    yf, yb, sf, sb = pl.pallas_call(
        functools.partial(_wkv_kernel, nc=nc),
        grid=(steps,),
        in_specs=[pl.BlockSpec(rkv_blk, fwd_first),
                  pl.BlockSpec(lora_blk, fwd_first),
                  pl.BlockSpec(rkv_blk, bwd_first),
                  pl.BlockSpec(lora_blk, bwd_first),
                  pl.BlockSpec(rkv_blk, fwd_next),
                  pl.BlockSpec(lora_blk, fwd_next),
                  pl.BlockSpec(rkv_blk, bwd_next),
                  pl.BlockSpec(lora_blk, bwd_next),
                  pl.BlockSpec((WKV_SEQS, N_PAIRS, LANES, LANES), st_in),
                  pl.BlockSpec((WKV_SEQS, N_PAIRS, LANES, LANES), st_in),
                  pl.BlockSpec((2, LANES, RWKV_DIM), c3),
                  pl.BlockSpec((2, RWKV_DIM), c2),
                  pl.BlockSpec((2, LANES, RWKV_DIM), c3),
                  pl.BlockSpec((2, RWKV_DIM), c2),
                  pl.BlockSpec((1, RWKV_DIM), c2),
                  pl.BlockSpec((1, RWKV_DIM), c2)],
        out_specs=[pl.BlockSpec((WKV_SEQS, CHUNK, RWKV_DIM), fwd),
                   pl.BlockSpec((WKV_SEQS, CHUNK, RWKV_DIM), bwd),
                   pl.BlockSpec((WKV_SEQS, RWKV_HEADS, HEAD_DIM, HEAD_DIM), st),
                   pl.BlockSpec((WKV_SEQS, RWKV_HEADS, HEAD_DIM, HEAD_DIM), st)],
        out_shape=[y_shape, y_shape, st_shape, st_shape],
        scratch_shapes=[pltpu.VMEM((WKV_SEQS, 2, N_PAIRS, LANES, LANES), F32),
                        pltpu.VMEM((2, n_chains, len(WKV_OPERANDS), CHUNK, LANES), F32),
                        pltpu.VMEM((2, n_chains, SUBLANES, LANES), F32)],
        compiler_params=_params("arbitrary"),
        name="wkv_chunked",
    )(rkv3, lora3, rkv3, lora3, rkv3, lora3, rkv3, lora3, s0f, s0b, w2p, w0, a2p, a0, k_k, k_a)
    return yf.reshape(n, RWKV_DIM), yb.reshape(n, RWKV_DIM), sf, sb


def _out_kernel(x_ref, mod_ref, conv_ref, yf_ref, yb_ref, rkv_ref, lo_ref,
                a2_ref, a0_ref, ka_ref, rk_ref, gng_ref, gnb_ref, g2_ref, wout_ref,
                postg_ref, preg_ref, rw_ref, rb_ref,
                x1_ref, h2_ref, route_ref, route_t_ref, cnt_ref):
    @pl.when(pl.program_id(0) == 0)
    def _():
        cnt_ref[...] = jnp.zeros(cnt_ref.shape, F32)

    m = mod_ref[0]
    rkv = rkv_ref[...]
    lo = lo_ref[...]
    r = rkv[:, :RWKV_DIM]
    k = rkv[:, RWKV_DIM:2 * RWKV_DIM]
    v = rkv[:, 2 * RWKV_DIM:]
    ha = lo[:, LANES:2 * LANES]
    ag_f = _sigmoid(a0_ref[0:1, :] + _mm(ha, a2_ref[0]))
    ag_b = _sigmoid(a0_ref[1:2, :] + _mm(ha, a2_ref[1]))
    rkk = r * k * rk_ref[...] * (2.0 + (ag_f + ag_b - 2.0) * ka_ref[...])
    o = yf_ref[...] + yb_ref[...]
    gate = _mm(_sigmoid(lo[:, 2 * LANES:]), g2_ref[...])
    r2 = lax.broadcasted_iota(jnp.int32, (LANES, LANES), 0)
    c2 = lax.broadcasted_iota(jnp.int32, (LANES, LANES), 1)
    ones_bd = ((r2 // HEAD_DIM) == (c2 // HEAD_DIM)).astype(BF16)
    parts = []
    for p in range(N_PAIRS):
        cols = slice(p * LANES, (p + 1) * LANES)
        op = o[:, cols]
        mu = _head_sums(op, ones_bd) * (1.0 / HEAD_DIM)
        oc = op - mu
        var = _head_sums(oc * oc, ones_bd) * (1.0 / HEAD_DIM)
        on = oc * lax.rsqrt(var + GN_EPS) * gng_ref[:, cols] + gnb_ref[:, cols]
        bonus = _head_sums(rkk[:, cols], ones_bd) * v[:, cols]
        parts.append((on + bonus) * gate[:, cols])
    mix_in = jnp.concatenate([conv_ref[...]] + parts, axis=1)
    mix = jnp.dot(mix_in.astype(BF16), wout_ref[...], preferred_element_type=F32)
    x1 = x_ref[...] + m[2:3] * _rms(mix, postg_ref[...])
    x1_ref[...] = x1
    h2 = _rms(x1, preg_ref[...]) * (1.0 + m[4:5]) + m[3:4]
    h2_ref[...] = _pack_rows(h2)
    hh, hl = _split2(h2)
    dot = functools.partial(jnp.dot, preferred_element_type=F32)
    logits = dot(hh, rw_ref[0]) + (dot(hh, rw_ref[1]) + dot(hl, rw_ref[0])) + rb_ref[...]
    lane =lax.broadcasted_iota(jnp.int32, logits.shape, 1)
    work = logits
    picks, vals, idxs = [], [], []
    for _ in range(TOP_K):
        mx = jnp.max(work, axis=-1, keepdims=True)
        idx = jnp.min(jnp.where(work == mx, lane, LANES), axis=-1, keepdims=True)
        pick = lane == idx
        picks.append(pick)
        vals.append(mx)
        idxs.append(idx)
        work = jnp.where(pick, 2.0 * NEG_BIG, work)
    exps = [jnp.exp(val - vals[0]) for val in vals]
    den = exps[0] + exps[1] + exps[2] + exps[3]
    sel = jnp.zeros(logits.shape, F32)
    for pick in picks:
        sel = jnp.where(pick, 1.0, sel)
    tr = lax.broadcasted_iota(jnp.int32, (TM_OUT, TM_OUT), 0)
    tc = lax.broadcasted_iota(jnp.int32, (TM_OUT, TM_OUT), 1)
    before = (tc < tr).astype(BF16)
    rank = dot(before, sel.astype(BF16)) + cnt_ref[...]
    cnt_ref[...] = cnt_ref[...] + jnp.sum(sel, axis=0, keepdims=True)
    route = jnp.zeros(logits.shape, F32)
    for q, (pick, idx, e) in enumerate(zip(picks, idxs, exps)):
        rank_q = jnp.sum(jnp.where(pick, rank, 0.0), axis=-1, keepdims=True)
        route = jnp.where(lane == q, idx.astype(F32), route)
        route = jnp.where(lane == TOP_K + q, e / den, route)
        route = jnp.where(lane == 2 * TOP_K + q, rank_q, route)
    route_ref[...] = route
    route_t_ref[...] = route.T[:ROUTE_ROWS, :]


def _out_call(x, mod, conv_out, yf, yb, rkv, lora, wts, seq_len):
    n, d = x.shape
    row = lambda i: (i, 0)
    c2 = lambda i: (0, 0)
    c3 = lambda i: (0, 0, 0)
    full = lambda a: pl.BlockSpec(a.shape, c3 if a.ndim == 3 else c2)
    consts = [wts["a2p"], wts["a0"], wts["k_a"], wts["r_k"], wts["gn_g"], wts["gn_b"], wts["g2"],
              wts["w_out"], wts["post_mix_g"], wts["pre_ffn_g"], wts["router_w"], wts["router_b"]]
    return pl.pallas_call(
        _out_kernel,
        grid=(n // TM_OUT,),
        in_specs=[pl.BlockSpec((TM_OUT, d), row),
                  pl.BlockSpec((1, N_MOD, d), _mod_index(TM_OUT, seq_len, mod.shape[0])),
                  pl.BlockSpec((TM_OUT, CONV_CH), row),
                  pl.BlockSpec((TM_OUT, RWKV_DIM), row),
                  pl.BlockSpec((TM_OUT, RWKV_DIM), row),
                  pl.BlockSpec((TM_OUT, 3 * RWKV_DIM), row),
                  pl.BlockSpec((TM_OUT, LORA_COLS), row)] + [full(a) for a in consts],
        out_specs=[pl.BlockSpec((TM_OUT, d), row),
                   pl.BlockSpec((TM_OUT, d // 2), row),
                   pl.BlockSpec((TM_OUT, LANES), row),
                   pl.BlockSpec((ROUTE_ROWS, TM_OUT), lambda i: (0, i)),
                   pl.BlockSpec((1, LANES), c2)],
        out_shape=[jax.ShapeDtypeStruct((n, d), F32),
                   jax.ShapeDtypeStruct((n, d // 2), jnp.int32),
                   jax.ShapeDtypeStruct((n, LANES), F32),
                   jax.ShapeDtypeStruct((ROUTE_ROWS, n), F32),
                   jax.ShapeDtypeStruct((1, LANES), F32)],
        compiler_params=_params("arbitrary"),
        name="out_router",
    )(x, mod, conv_out, yf, yb, rkv, lora, *consts)


def _sc_mesh():
    return plsc.VectorSubcoreMesh(core_axis_name="c", subcore_axis_name="s",
                                  num_cores=SC_CORES, num_subcores=SC_SUBCORES)


def _sc_worker_id():
    return lax.axis_index("s") * SC_CORES + lax.axis_index("c")


def _sc_scatter_rows(src_a, src_b, idx, n_out):
    (n_a, d), n_b = src_a.shape, src_b.shape[0]
    n = n_a + n_b
    workers = SC_CORES * SC_SUBCORES
    per_worker = n // workers
    assert idx.shape[0] == TOP_K * n and per_worker * workers == n
    assert per_worker % SC_ROWS == 0 and n_a % per_worker == 0

    @functools.partial(
        pl.kernel, mesh=_sc_mesh(),
        out_type=jax.ShapeDtypeStruct((n_out, d), src_a.dtype),
        scratch_types=[pltpu.VMEM((SC_ROWS,), jnp.int32),
                       pltpu.VMEM((SC_ROWS, d), src_a.dtype),
                       pltpu.SemaphoreType.DMA],
        name="sc_scatter_rows")
    def scatter(a_hbm, b_hbm, idx_hbm, out_hbm, idx_v, rows_v, sem):
        base = _sc_worker_id() * per_worker

        @pl.loop(0, per_worker // SC_ROWS)
        def _(j):
            tok = base + j * SC_ROWS

            @pl.when(tok < n_a)
            def _():
                pltpu.sync_copy(a_hbm.at[pl.ds(tok, SC_ROWS)], rows_v)

            @pl.when(tok >= n_a)
            def _():
                pltpu.sync_copy(b_hbm.at[pl.ds(tok - n_a, SC_ROWS)], rows_v)

            for q in range(TOP_K):
                pltpu.sync_copy(idx_hbm.at[pl.ds(q * n + tok, SC_ROWS)], idx_v)
                pltpu.async_copy(rows_v, out_hbm.at[idx_v], sem).wait()

    return scatter(src_a, src_b, idx)


def _sc_gather_rows(src, idx):
    n_idx, d = idx.shape[0], src.shape[1]
    workers = SC_CORES * SC_SUBCORES
    per_worker = n_idx // workers
    assert per_worker * workers == n_idx and per_worker % SC_ROWS == 0

    @functools.partial(
        pl.kernel, mesh=_sc_mesh(),
        out_type=jax.ShapeDtypeStruct((n_idx, d), src.dtype),
        scratch_types=[pltpu.VMEM((SC_ROWS,), jnp.int32),
                       pltpu.VMEM((SC_ROWS, d), src.dtype),
                       pltpu.SemaphoreType.DMA],
        name="sc_gather_rows")
    def gather(src_hbm, idx_hbm, out_hbm, idx_v, rows_v, sem):
        base = _sc_worker_id() * per_worker

        @pl.loop(0, per_worker // SC_ROWS)
        def _(j):
            off = base + j * SC_ROWS
            pltpu.sync_copy(idx_hbm.at[pl.ds(off, SC_ROWS)], idx_v)
            pltpu.async_copy(src_hbm.at[idx_v], rows_v, sem).wait()
            pltpu.sync_copy(rows_v, out_hbm.at[pl.ds(off, SC_ROWS)])

    return gather(src, idx)


def _moe_kernel(blk_e_ref, blk_first_ref, blk_rows_ref, blk_next_ref,
                xs_ref, wgu_hbm, bgu_ref, wd_hbm, bd_ref,
                o_ref, wgu_f32_ref, wd_f32_ref, wgu_bf_ref, wd_bf_ref, sem):
    b = pl.program_id(0)
    n_rows = blk_rows_ref[b]

    def weight_copies(e):
        return (pltpu.make_async_copy(wgu_hbm.at[e], wgu_f32_ref, sem.at[0]),
                pltpu.make_async_copy(wd_hbm.at[e], wd_f32_ref, sem.at[1]))

    @pl.when(b == 0)
    def _():
        for copy in weight_copies(blk_e_ref[0]):
            copy.start()

    @pl.when(blk_first_ref[b] == 1)
    def _():
        for copy in weight_copies(blk_e_ref[b]):
            copy.wait()
        wgu_bf_ref[...] = wgu_f32_ref[...].astype(BF16)
        wd_bf_ref[...] = wd_f32_ref[...].astype(BF16)

        @pl.when(blk_next_ref[b] >= 0)
        def _():
            for copy in weight_copies(blk_next_ref[b]):
                copy.start()

    @pl.when(n_rows > 0)
    def _():
        live = lax.broadcasted_iota(jnp.int32, xs_ref.shape, 0) < n_rows
        xs = _unpack_rows(jnp.where(live, xs_ref[...], 0))
        gu = jnp.dot(xs.astype(BF16), wgu_bf_ref[...],
                     preferred_element_type=F32) + bgu_ref[0]
        x_glu = jnp.minimum(gu[:, :D_FF], SWIGLU_LIMIT)
        x_lin = jnp.clip(gu[:, D_FF:], -SWIGLU_LIMIT, SWIGLU_LIMIT)
        act = x_glu * _sigmoid(SWIGLU_ALPHA * x_glu) * (x_lin + 1.0)
        o_ref[...] = _pack_rows(jnp.dot(act.astype(BF16), wd_bf_ref[...],
                                        preferred_element_type=F32) + bd_ref[0])

    @pl.when(n_rows == 0)
    def _():
        o_ref[...] = jnp.zeros(o_ref.shape, jnp.int32)


def _moe_call(blk_e, blk_first, blk_rows, blk_next, xs, wgu, bgu, wd, bd):
    n_slots, half = xs.shape
    d = 2 * half
    ex = lambda b, be, bf, br, bn: (be[b], 0, 0)
    row = lambda b, be, bf, br, bn: (b, 0)
    grid_spec = pltpu.PrefetchScalarGridSpec(
        num_scalar_prefetch=4,
        grid=(n_slots // MOE_BLOCK,),
        in_specs=[pl.BlockSpec((MOE_BLOCK, half), row),
                  pl.BlockSpec(memory_space=pl.ANY),
                  pl.BlockSpec((1, 1, 2 * D_FF), ex),
                  pl.BlockSpec(memory_space=pl.ANY),
                  pl.BlockSpec((1, 1, d), ex)],
        out_specs=pl.BlockSpec((MOE_BLOCK, half), row),
        scratch_shapes=[pltpu.VMEM((d, 2 * D_FF), F32), pltpu.VMEM((D_FF, d), F32),
                        pltpu.VMEM((d, 2 * D_FF), BF16), pltpu.VMEM((D_FF, d), BF16),
                        pltpu.SemaphoreType.DMA((2,))])
    return pl.pallas_call(
        _moe_kernel,
        grid_spec=grid_spec,
        out_shape=jax.ShapeDtypeStruct((n_slots, half), jnp.int32),
        compiler_params=_params("arbitrary"),
        name="moe_experts",
    )(blk_e, blk_first, blk_rows, blk_next, xs, wgu, bgu, wd, bd)


def _final_kernel(x1_ref, y4_ref, route_ref, mod_ref, g_ref, o_ref):
    m = mod_ref[0]
    route = route_ref[...]
    y = jnp.zeros(x1_ref.shape, F32)
    for q in range(TOP_K):
        y = y + route[:, TOP_K + q:TOP_K + q + 1] * _unpack_rows(y4_ref[q])
    o_ref[...] = x1_ref[...] + m[5:6] * _rms(y, g_ref[...])


def _final_call(x1, y4, route, mod, g, seq_len):
    n, d = x1.shape
    row = lambda i: (i, 0)
    return pl.pallas_call(
        _final_kernel,
        grid=(n // TM_FIN,),
        in_specs=[pl.BlockSpec((TM_FIN, d), row),
                  pl.BlockSpec((TOP_K, TM_FIN, d // 2), lambda i: (0, i, 0)),
                  pl.BlockSpec((TM_FIN, LANES), row),
                  pl.BlockSpec((1, N_MOD, d), _mod_index(TM_FIN, seq_len, mod.shape[0])),
                  pl.BlockSpec((1, d), lambda i: (0, 0))],
        out_specs=pl.BlockSpec((TM_FIN, d), row),
        out_shape=jax.ShapeDtypeStruct((n, d), F32),
        compiler_params=_params("arbitrary"),
        name="final_residual",
    )(x1, y4, route, mod, g)


def _pack_state(s):
    b = s.shape[0]
    st = jnp.swapaxes(s.astype(F32), -1, -2).reshape(b, N_PAIRS, 2, HEAD_DIM, HEAD_DIM)
    z = jnp.zeros_like(st[:, :, 0])
    top = jnp.concatenate([st[:, :, 0], z], axis=-1)
    bot = jnp.concatenate([z, st[:, :, 1]], axis=-1)
    return jnp.concatenate([top, bot], axis=-2)


def _mix_sublayer(x, mod, s0f, s0b, seg_len, wts):
    batch, seq_len, d = x.shape
    x2 = x.reshape(batch * seq_len, d)
    cv, rkv, lora = _in_call(x2, mod, wts["pre_mix_g"], wts["w_all"], seq_len)
    conv_out = _conv_call(cv, wts["conv_w"], wts["conv_b"], wts["conv_ln_g"], wts["conv_ln_b"],
                          seg_len)
    yf, yb, sf, sb = _wkv_call(rkv, lora, s0f, s0b, wts["w2p"], wts["w0"], wts["a2p"], wts["a0"],
                               wts["k_k"], wts["k_a"], batch, seq_len)
    x1, h2, route, route_t, cnt = _out_call(x2, mod, conv_out, yf, yb, rkv, lora, wts, seq_len)
    return x1, h2, route, cnt, sf, sb, route_t


def _dispatch_plan(routes, cnts):
    experts = jnp.arange(N_EXPERTS, dtype=jnp.int32)
    counts = [c[0, :N_EXPERTS].astype(jnp.int32) for c in cnts]
    total = sum(counts)
    padded = (total + MOE_BLOCK - 1) // MOE_BLOCK * MOE_BLOCK
    pad_end = jnp.cumsum(padded)
    pad_start = pad_end - padded
    dests = []
    prior = jnp.zeros_like(total)
    for route, count in zip(routes, counts):
        idx = route[:TOP_K].astype(jnp.int32)
        rank = route[2 * TOP_K:3 * TOP_K].astype(jnp.int32)
        first = jnp.sum(jnp.where(idx[..., None] == experts, pad_start + prior, 0), axis=-1)
        dests.append(first + rank)
        prior = prior + count
    dest = jnp.concatenate(dests, axis=1).reshape(-1)
    n_slots = dest.shape[0] + N_EXPERTS * MOE_BLOCK
    blk_row0 = jnp.arange(n_slots // MOE_BLOCK, dtype=jnp.int32) * MOE_BLOCK
    blk_e = jnp.sum((pad_end[None, :] <= blk_row0[:, None]).astype(jnp.int32), axis=-1)
    blk_e = jnp.minimum(blk_e, N_EXPERTS - 1)
    blk_first = jnp.concatenate([jnp.ones((1,), jnp.int32),
                                 (blk_e[1:] != blk_e[:-1]).astype(jnp.int32)])
    of_blk = lambda per_expert: jnp.sum(
        jnp.where(blk_e[:, None] == experts[None, :], per_expert[None, :], 0), axis=-1)
    blk_rows = jnp.clip(of_blk(pad_start + total) - blk_row0, 0, MOE_BLOCK).astype(jnp.int32)
    pos = jnp.arange(blk_e.shape[0], dtype=jnp.int32)
    later_first = (pos[None, :] > pos[:, None]) & (blk_first[None, :] == 1)
    next_pos = jnp.min(jnp.where(later_first, pos[None, :], pos.shape[0]), axis=-1)
    blk_next = jnp.where(next_pos < pos.shape[0],
                         blk_e[jnp.minimum(next_pos, pos.shape[0] - 1)], -1).astype(jnp.int32)
    return dest, n_slots, blk_e, blk_first, blk_rows, blk_next


def _routed_ffn(h2s, routes, cnts, wts):
    dest, n_slots, blk_e, blk_first, blk_rows, blk_next = _dispatch_plan(routes, cnts)
    xs = _sc_scatter_rows(h2s[0], h2s[1], dest, n_slots)
    out_sorted = _moe_call(blk_e, blk_first, blk_rows, blk_next, xs, wts["w_gu"], wts["b_gu"],
                           wts["w_down"], wts["b_down"])
    dest_qt = dest.reshape(TOP_K, -1)
    y4s, row0 = [], 0
    for h2 in h2s:
        n_path = h2.shape[0]
        y4 = _sc_gather_rows(out_sorted, dest_qt[:, row0:row0 + n_path].reshape(-1))
        y4s.append(y4.reshape(TOP_K, n_path, xs.shape[1]))
        row0 += n_path
    return y4s


def _prep_weights(l, pre_mix_g, post_mix_g, pre_ffn_g, post_ffn_g, w_in, w_out, conv_w, conv_b,
                  conv_ln_g, conv_ln_b, rw_w0, rw_w1, rw_w2, rw_a0, rw_a1, rw_a2, rw_g1, rw_g2,
                  rw_k_k, rw_k_a, rw_r_k, rw_gn_g, rw_gn_b, router_w, router_b, w_gu, b_gu,
                  w_down, b_down):
    row = lambda a: a.reshape(1, -1).astype(F32)
    zpad = jnp.zeros((HEAD_DIM, RWKV_DIM), F32)
    w_all = jnp.concatenate([w_in[l], rw_w1[l, 0], rw_w1[l, 1], rw_a1[l, 0], rw_a1[l, 1],
                             rw_g1[l]], axis=1).astype(BF16)
    w2p = jnp.stack([jnp.concatenate([rw_w2[l, 0], zpad], axis=0),
                     jnp.concatenate([zpad, rw_w2[l, 1]], axis=0)]).astype(BF16)
    a2p = jnp.stack([jnp.concatenate([rw_a2[l, 0], zpad], axis=0),
                     jnp.concatenate([zpad, rw_a2[l, 1]], axis=0)]).astype(BF16)
    rw_pad = jnp.pad(router_w[l].astype(F32), ((0, 0), (0, LANES - N_EXPERTS)))
    rw_hi = rw_pad.astype(BF16)
    rw_lo = (rw_pad - rw_hi.astype(F32)).astype(BF16)
    rb_pad = jnp.concatenate([router_b[l].astype(F32),
                              jnp.full((LANES - N_EXPERTS,), NEG_BIG, F32)]).reshape(1, LANES)
    return {
        "pre_mix_g": row(pre_mix_g[l]), "post_mix_g": row(post_mix_g[l]),
        "pre_ffn_g": row(pre_ffn_g[l]), "post_ffn_g": row(post_ffn_g[l]),
        "w_all": w_all, "w_out": w_out[l].astype(BF16),
        "conv_w": jnp.broadcast_to(conv_w[l].astype(F32)[:, None, :],
                                   (CONV_WIDTH, SUBLANES, CONV_CH)),
        "conv_b": row(conv_b[l]),
        "conv_ln_g": row(conv_ln_g[l]), "conv_ln_b": row(conv_ln_b[l]),
        "w2p": w2p, "w0": rw_w0[l].astype(F32), "a2p": a2p, "a0": rw_a0[l].astype(F32),
        "k_k": row(rw_k_k[l]), "k_a": row(rw_k_a[l]), "r_k": row(rw_r_k[l]),
        "gn_g": row(rw_gn_g[l]), "gn_b": row(rw_gn_b[l]), "g2": rw_g2[l].astype(BF16),
        "router_w": jnp.stack([rw_hi, rw_lo]), "router_b": rb_pad,
        "w_gu": w_gu[l].astype(F32), "b_gu": b_gu[l].reshape(N_EXPERTS, 1, -1).astype(F32),
        "w_down": w_down[l].astype(F32), "b_down": b_down[l].reshape(N_EXPERTS, 1, -1).astype(F32),
    }


def kernel(x_prompt, x_sample, state_wkv_fwd, state_wkv_bwd, c, c_ctx, ada_w, ada_b, pre_mix_g, post_mix_g, pre_ffn_g, post_ffn_g, w_in, w_out, conv_w, conv_b, conv_ln_g, conv_ln_b, rw_w0, rw_w1, rw_w2, rw_a0, rw_a1, rw_a2, rw_g1, rw_g2, rw_k_k, rw_k_a, rw_r_k, rw_gn_g, rw_gn_b, router_w, router_b, w_gu, b_gu, w_down, b_down):
    depth = ada_w.shape[0]
    dec_batch = c.shape[0]
    mod_rows = 16
    c_rows = jnp.concatenate([c, c_ctx[None, :],
                              jnp.zeros((mod_rows - dec_batch - 1, D_MODEL), F32)], axis=0)
    zero_state = jnp.zeros((WKV_SEQS, N_PAIRS, LANES, LANES), F32)
    y_prompt, y_sample = x_prompt, x_sample
    new_f, new_b = [], []
    for l in range(depth):
        wts = _prep_weights(l, pre_mix_g, post_mix_g, pre_ffn_g, post_ffn_g, w_in, w_out, conv_w,
                            conv_b, conv_ln_g, conv_ln_b, rw_w0, rw_w1, rw_w2, rw_a0, rw_a1, rw_a2,
                            rw_g1, rw_g2, rw_k_k, rw_k_a, rw_r_k, rw_gn_g, rw_gn_b, router_w,
                            router_b, w_gu, b_gu, w_down, b_down)
        mod = _mod_call(c_rows, ada_w[l], ada_b[l]).reshape(mod_rows, N_MOD, D_MODEL)
        mod_lat = mod[:dec_batch]
        mod_ctx = mod[dec_batch:dec_batch + 1]
        ctx = _mix_sublayer(y_prompt, mod_ctx, zero_state, zero_state, y_prompt.shape[1], wts)
        lat = _mix_sublayer(y_sample, mod_lat, _pack_state(state_wkv_fwd[:, l]),
                            _pack_state(state_wkv_bwd[:, l]), GRID_W, wts)
        new_f.append(ctx[4])
        new_b.append(ctx[5])
        y4_ctx, y4_lat = _routed_ffn([ctx[1], lat[1]], [ctx[6], lat[6]], [ctx[3], lat[3]], wts)
        y_prompt = _final_call(ctx[0], y4_ctx, ctx[2], mod_ctx, wts["post_ffn_g"],
                               y_prompt.shape[1]).reshape(y_prompt.shape)
        y_sample = _final_call(lat[0], y4_lat, lat[2], mod_lat, wts["post_ffn_g"],
                               y_sample.shape[1]).reshape(y_sample.shape)
    return (y_prompt, y_sample, jnp.stack(new_f, axis=1), jnp.stack(new_b, axis=1))
```

```python
import functools
import math

import jax
import jax.numpy as jnp
from jax import lax
from jax.experimental import pallas as pl
from jax.experimental.pallas import tpu as pltpu
from jax.experimental.pallas import tpu_sc as plsc

F32 = jnp.float32
BF16 = jnp.bfloat16

D_MODEL = 1024
CONV_CH = 512
RWKV_DIM = 512
HEAD_DIM = 64
RWKV_HEADS = 8
N_PAIRS = RWKV_HEADS // 2
CONV_WIDTH = 31
N_EXPERTS = 32
TOP_K = 4
D_FF = 1024
SWIGLU_LIMIT = 7.0
SWIGLU_ALPHA = 1.702
RMS_EPS = 1e-6
LN_EPS = 1e-5
GN_EPS = 64e-5
N_MOD = 6
GRID_W = 64

LANES = 128
SUBLANES = 8
CHUNK = 64
WKV_SEQS = 4
LORA_COLS = 384
TM_IN = 1024
TM_CONV = 512
TM_OUT = 512
TM_FIN = 1024
ROUTE_ROWS = 16
MOE_BLOCK = 512
SC_CORES = 2
SC_SUBCORES = 16
SC_ROWS = 128
NEG_BIG = -1e30
EXP_M05 = math.exp(-0.5)
VMEM_LIMIT = 56 * 1024 * 1024


def _sigmoid(x):
    return 1.0 / (1.0 + jnp.exp(-x))


def _mm(a, b):
    return jnp.dot(a.astype(BF16), b.astype(BF16), preferred_element_type=F32)


def _mm_nt(a, b):
    return lax.dot_general(a.astype(BF16), b.astype(BF16), (((1,), (1,)), ((), ())),
                           preferred_element_type=F32)


def _mm_tn(a, b):
    return lax.dot_general(a.astype(BF16), b.astype(BF16), (((0,), (0,)), ((), ())),
                           preferred_element_type=F32)


def _split2(x):
    hi = x.astype(BF16)
    lo = (x - hi.astype(F32)).astype(BF16)
    return hi, lo


def _split3(x):
    h1 = x.astype(BF16)
    r1 = x - h1.astype(F32)
    h2 = r1.astype(BF16)
    h3 = (r1 - h2.astype(F32)).astype(BF16)
    return h1, h2, h3


def _pack_rows(x):
    n = x.shape[1] // 2
    bits = lambda v: lax.bitcast_convert_type(v.astype(BF16).astype(F32), jnp.uint32)
    word = (bits(x[:, n:]) & jnp.uint32(0xFFFF0000)) | (bits(x[:, :n]) >> 16)
    return lax.bitcast_convert_type(word, jnp.int32)


def _unpack_rows(w):
    u = lax.bitcast_convert_type(w, jnp.uint32)
    lo = lax.bitcast_convert_type(u << 16, F32)
    hi = lax.bitcast_convert_type(u & jnp.uint32(0xFFFF0000), F32)
    return jnp.concatenate([lo, hi], axis=1)


def _rms(x, g):
    return x * lax.rsqrt(jnp.mean(x * x, axis=-1, keepdims=True) + RMS_EPS) * g


def _params(*sem):
    return pltpu.CompilerParams(dimension_semantics=sem, vmem_limit_bytes=VMEM_LIMIT)


def _mod_kernel(c_ref, w_ref, b_ref, o_ref):
    c = c_ref[...]
    s1, s2 = _split2(c * _sigmoid(c))
    w1, w2 = _split2(w_ref[...])
    dot = functools.partial(jnp.dot, preferred_element_type=F32)
    o_ref[...] = dot(s1, w1) + (dot(s1, w2) + dot(s2, w1)) + b_ref[...]


def _mod_call(c_rows, ada_w, ada_b):
    m, d = c_rows.shape
    n = ada_w.shape[1]
    tn = 512
    return pl.pallas_call(
        _mod_kernel,
        grid=(n // tn,),
        in_specs=[pl.BlockSpec((m, d), lambda j: (0, 0)),
                  pl.BlockSpec((d, tn), lambda j: (0, j)),
                  pl.BlockSpec((1, tn), lambda j: (0, j))],
        out_specs=pl.BlockSpec((m, tn), lambda j: (0, j)),
        out_shape=jax.ShapeDtypeStruct((m, n), F32),
        compiler_params=_params("arbitrary"),
        name="mod",
    )(c_rows, ada_w, ada_b.reshape(1, n))


def _in_kernel(x_ref, mod_ref, g_ref, w_ref, cv_ref, rkv_ref, lo_ref):
    m = mod_ref[0]
    h = _rms(x_ref[...], g_ref[...]) * (1.0 + m[1:2]) + m[0:1]
    proj = jnp.dot(h.astype(BF16), w_ref[...], preferred_element_type=F32)
    cv_ref[...] = proj[:, :CONV_CH] * _sigmoid(proj[:, CONV_CH:2 * CONV_CH])
    rkv_ref[...] = proj[:, 2 * CONV_CH:2 * CONV_CH + 3 * RWKV_DIM]
    lo_ref[...] = proj[:, 2 * CONV_CH + 3 * RWKV_DIM:]


def _mod_index(tile, seq_len, n_mod_rows):
    def index(i):
        return ((i * tile) // seq_len) % n_mod_rows, 0, 0
    return index


def _in_call(x, mod, g, w_all, seq_len):
    n, d = x.shape
    ncol = w_all.shape[1]
    return pl.pallas_call(
        _in_kernel,
        grid=(n // TM_IN,),
        in_specs=[pl.BlockSpec((TM_IN, d), lambda i: (i, 0)),
                  pl.BlockSpec((1, N_MOD, d), _mod_index(TM_IN, seq_len, mod.shape[0])),
                  pl.BlockSpec((1, d), lambda i: (0, 0)),
                  pl.BlockSpec((d, ncol), lambda i: (0, 0))],
        out_specs=[pl.BlockSpec((TM_IN, CONV_CH), lambda i: (i, 0)),
                   pl.BlockSpec((TM_IN, 3 * RWKV_DIM), lambda i: (i, 0)),
                   pl.BlockSpec((TM_IN, LORA_COLS), lambda i: (i, 0))],
        out_shape=[jax.ShapeDtypeStruct((n, CONV_CH), F32),
                   jax.ShapeDtypeStruct((n, 3 * RWKV_DIM), F32),
                   jax.ShapeDtypeStruct((n, LORA_COLS), F32)],
        compiler_params=_params("arbitrary"),
        name="in_proj",
    )(x, mod, g, w_all)


CONV_HALO = 16
CONV_ROWS = 32


def _conv_kernel(cv_ref, w_ref, b_ref, g_ref, be_ref, o_ref, pad_ref, acc_ref, *, seg_len):
    nseg = TM_CONV // seg_len
    stride = seg_len + 2 * CONV_HALO
    rows = nseg * stride
    u = cv_ref[...]
    pad_ref[0] = jnp.zeros(pad_ref.shape[1:], F32)
    for s in range(nseg):
        pad_ref[0, s * stride + CONV_HALO:s * stride + CONV_HALO + seg_len, :] = (
            u[s * seg_len:(s + 1) * seg_len])
    for r in range(1, SUBLANES):
        pad_ref[r, 0:rows, :] = pad_ref[0, r:r + rows, :]
    first = CONV_HALO - CONV_WIDTH // 2
    chunks_per_seg = seg_len // CONV_ROWS

    def row_chunk(i, carry):
        pad0 = i * CONV_ROWS + (i // chunks_per_seg) * (2 * CONV_HALO)
        accs = [jnp.zeros((SUBLANES, CONV_CH), F32) for _ in range(CONV_ROWS // SUBLANES)]
        for t in range(CONV_WIDTH):
            q, r = divmod(first + t, SUBLANES)
            w_t = w_ref[t]
            for j in range(len(accs)):
                start = pl.multiple_of(pad0 + (q + j) * SUBLANES, SUBLANES)
                accs[j] = accs[j] + w_t * pad_ref[r, pl.ds(start, SUBLANES), :]
        for j, acc in enumerate(accs):
            row0 = pl.multiple_of(i * CONV_ROWS + j * SUBLANES, SUBLANES)
            acc_ref[pl.ds(row0, SUBLANES), :] = acc
        return carry

    lax.fori_loop(0, TM_CONV // CONV_ROWS, row_chunk, 0)
    y = acc_ref[...] + b_ref[...]
    mu = jnp.mean(y, axis=-1, keepdims=True)
    yc = y - mu
    var = jnp.mean(yc * yc, axis=-1, keepdims=True)
    z = yc * lax.rsqrt(var + LN_EPS) * g_ref[...] + be_ref[...]
    o_ref[...] = z * _sigmoid(z)


def _conv_call(cv, conv_w, conv_b, ln_g, ln_b, seg_len):
    n = cv.shape[0]
    nseg = TM_CONV // seg_len
    row = lambda i: (i, 0)
    const = lambda i: (0, 0)
    return pl.pallas_call(
        functools.partial(_conv_kernel, seg_len=seg_len),
        grid=(n // TM_CONV,),
        in_specs=[pl.BlockSpec((TM_CONV, CONV_CH), row),
                  pl.BlockSpec((CONV_WIDTH, SUBLANES, CONV_CH), lambda i: (0, 0, 0)),
                  pl.BlockSpec((1, CONV_CH), const),
                  pl.BlockSpec((1, CONV_CH), const),
                  pl.BlockSpec((1, CONV_CH), const)],
        out_specs=pl.BlockSpec((TM_CONV, CONV_CH), row),
        out_shape=jax.ShapeDtypeStruct((n, CONV_CH), F32),
        scratch_shapes=[pltpu.VMEM((SUBLANES, nseg * (seg_len + 2 * CONV_HALO) + SUBLANES, CONV_CH),
                                   F32),
                        pltpu.VMEM((TM_CONV, CONV_CH), F32)],
        compiler_params=_params("arbitrary"),
        name="conv_module",
    )(cv, conv_w, conv_b, ln_g, ln_b)


def _head_sums(x, ones_bd):
    hi, lo = _split2(x)
    return jnp.dot(jnp.concatenate([hi, lo], axis=1), jnp.concatenate([ones_bd, ones_bd], axis=0),
                   preferred_element_type=F32)


def _wkv_masks():
    c = CHUNK
    r2 = lax.broadcasted_iota(jnp.int32, (2 * c, 2 * c), 0)
    c2 = lax.broadcasted_iota(jnp.int32, (2 * c, 2 * c), 1)
    same = (r2 // c) == (c2 // c)
    rr, cc = r2 % c, c2 % c
    r1 = lax.broadcasted_iota(jnp.int32, (c, 2 * c), 0)
    c1 = lax.broadcasted_iota(jnp.int32, (c, 2 * c), 1)
    cc1 = c1 % c
    rs = lax.broadcasted_iota(jnp.int32, (c, c), 0)
    cs = lax.broadcasted_iota(jnp.int32, (c, c), 1)
    masks = {"same": same, "eye": r2 == c2, "left": c1 < c,
             "ones_bd": same.astype(BF16)}
    for name, lt in (("f", lambda a, b: a < b), ("b", lambda a, b: a > b)):
        le = (lambda a, b, lt=lt: lt(a, b) | (a == b))
        masks["strict_" + name] = same & lt(cc, rr)
        masks["inc_" + name] = le(cc1, r1)
        masks["ak0_" + name] = (c1 >= c) & lt(cc1, r1)
        masks["ak1_" + name] = (c1 < c) & lt(cc1, r1)
        masks["tri_" + name] = le(cs, rs).astype(BF16)
    return masks


def _wkv_chains(chains, mk):
    c = CHUNK
    left = mk["left"]
    for ch in chains:
        zero = jnp.zeros_like(ch["at"])
        ch["at0"], ch["at1"] = jnp.where(left, ch["at"], zero), jnp.where(left, zero, ch["at"])
        rt0, rt1 = jnp.where(left, ch["rt"], zero), jnp.where(left, zero, ch["rt"])
        bt, kt = ch["bt"].astype(BF16), ch["kt"].astype(BF16)
        ch["g0"] = _mm_nt(jnp.concatenate([ch["at0"], rt0], axis=0), jnp.concatenate([bt, kt], axis=0))
        ch["g1"] = _mm_nt(jnp.concatenate([ch["at1"], rt1], axis=0), jnp.concatenate([kt, bt], axis=0))
    yield
    for ch in chains:
        g0, g1 = ch["g0"], ch["g1"]
        lmat = jnp.where(mk["strict_" + ch["d"]], jnp.concatenate([g0[:c], g1[:c]], axis=0), 0.0)
        ch["tinv"] = jnp.where(mk["eye"], 1.0, lmat)
        ch["lpow"] = _mm(lmat, lmat)
    for step in range(5):
        yield
        for ch in chains:
            lp = ch["lpow"].astype(BF16)
            if step < 4:
                prod = jnp.dot(lp, jnp.concatenate([lp, ch["tinv"].astype(BF16)], axis=1),
                               preferred_element_type=F32)
                ch["lpow"] = prod[:, :LANES]
                ch["tinv"] = ch["tinv"] + prod[:, LANES:]
            else:
                ch["tinv"] = ch["tinv"] + _mm(lp, ch["tinv"])
    yield
    for ch in chains:
        d, v = ch["d"], ch["v"]
        zv = jnp.zeros_like(v)
        ak0 = jnp.where(mk["ak0_" + d], ch["g0"][:c], 0.0)
        ak1 = jnp.where(mk["ak1_" + d], ch["g1"][:c], 0.0)
        ch["x0"] = _mm(ak0, jnp.concatenate([zv, v], axis=0))
        ch["x1"] = _mm(ak1, jnp.concatenate([v, zv], axis=0))
    yield
    for ch in chains:
        zero = jnp.zeros_like(ch["x0"])
        rhs = jnp.concatenate([
            jnp.concatenate([ch["at0"], jnp.where(left, ch["x0"], zero)], axis=1),
            jnp.concatenate([ch["at1"], jnp.where(left, zero, ch["x1"])], axis=1)], axis=0)
        ch["aw_blk"] = _mm(ch["tinv"], rhs)
    yield
    for ch in chains:
        d, v, aw_blk = ch["d"], ch["v"], ch["aw_blk"]
        zv = jnp.zeros_like(v)
        aw = aw_blk[:c] + aw_blk[c:]
        inc = mk["inc_" + d]
        g0, g1 = ch["g0"], ch["g1"]
        rb = jnp.where(inc, jnp.where(left, g0[c:], g1[c:]), 0.0)
        rk = jnp.where(inc, jnp.where(left, g1[c:], g0[c:]), 0.0)
        v0, v1 = jnp.where(left, v, zv), jnp.where(left, zv, v)
        vblk = jnp.concatenate([jnp.zeros((2 * c, LANES), F32),
                                jnp.concatenate([v1, v0], axis=0)], axis=1)
        qz = _mm(jnp.concatenate([rb, rk], axis=1), jnp.concatenate([aw_blk, vblk], axis=0))
        ch["q"] = ch["rt"] + qz[:, :LANES]
        ch["zz"] = qz[:, LANES:]
        rhs3 = jnp.concatenate([aw, jnp.concatenate([zv, v], axis=1)], axis=0)
        ch["mn"] = _mm_tn(jnp.concatenate([ch["bh"], ch["kh"]], axis=0), rhs3)
    yield
    same = mk["same"]
    for ch in chains:
        mn = ch["mn"]
        mt = jnp.where(same, mn[:, :LANES], 0.0) + jnp.where(mk["eye"], ch["ptot"], 0.0)
        nt = jnp.where(same, mn[:, LANES:], 0.0)
        both = _mm(jnp.concatenate([ch["q"], mt], axis=0), ch["st"])
        ch["y"] = both[:c] + ch["zz"]
        ch["st_new"] = both[c:] + nt


WKV_OPERANDS = ("at", "rt", "bt", "kt", "bh", "kh", "v")


def _wkv_prep(*args):
    for e in range(WKV_SEQS):
        yield from _wkv_prep_seq(e, *args)


def _wkv_prep_seq(e, rkv_refs, lo_refs, w2_ref, w0_ref, a2_ref, a0_ref, kk_ref, ka_ref, mk,
                  ops_ref, ptot_ref, slot):
    c = CHUNK
    for di, (d, rkv_ref, lo_ref) in enumerate(zip("fb", rkv_refs, lo_refs)):
        rkv = rkv_ref[e]
        lo = lo_ref[e]
        r = rkv[:, :RWKV_DIM]
        k = rkv[:, RWKV_DIM:2 * RWKV_DIM]
        v = rkv[:, 2 * RWKV_DIM:]
        w_raw = w0_ref[di:di + 1, :] + _mm(jnp.tanh(lo[:, :LANES]), w2_ref[di])
        lw = -EXP_M05 * _sigmoid(w_raw)
        yield
        ag = _sigmoid(a0_ref[di:di + 1, :] + _mm(lo[:, LANES:2 * LANES], a2_ref[di]))
        kd = k * (1.0 + (ag - 1.0) * ka_ref[...])
        kk_raw = k * kk_ref[...]
        h1, h2, h3 = _split3(lw)
        cs = jnp.dot(mk["tri_" + d], jnp.concatenate([h1, h2, h3], axis=1),
                     preferred_element_type=F32)
        logp = cs[:, :RWKV_DIM] + cs[:, RWKV_DIM:2 * RWKV_DIM] + cs[:, 2 * RWKV_DIM:]
        yield
        tot = logp[0:1] if d == "b" else logp[c - 1:c]
        e_inv = jnp.exp(-logp)
        e_hat = jnp.exp(tot - logp)
        rt = r * jnp.exp(logp)
        e_ex = jnp.exp(logp - lw)
        kt, kh = kd * e_inv, kd * e_hat
        ptot = jnp.exp(tot)
        for p in range(N_PAIRS):
            yield
            cols = slice(p * LANES, (p + 1) * LANES)
            kk = kk_raw[:, cols]
            kk = kk * lax.rsqrt(jnp.maximum(_head_sums(kk * kk, mk["ones_bd"]), 1e-24))
            kb = kk * ag[:, cols]
            tiles = {"at": -kk * e_ex[:, cols], "rt": rt[:, cols], "bt": kb * e_inv[:, cols],
                     "kt": kt[:, cols], "bh": kb * e_hat[:, cols], "kh": kh[:, cols],
                     "v": v[:, cols]}
            chain = (e * 2 + di) * N_PAIRS + p
            for j, name in enumerate(WKV_OPERANDS):
                ops_ref[slot, chain, j] = tiles[name]
            ptot_ref[slot, chain] = jnp.broadcast_to(ptot[:, cols], (SUBLANES, LANES))


def _wkv_kernel(rkv_f0_ref, lo_f0_ref, rkv_b0_ref, lo_b0_ref,
                rkv_f1_ref, lo_f1_ref, rkv_b1_ref, lo_b1_ref, s0f_ref, s0b_ref,
                w2_ref, w0_ref, a2_ref, a0_ref, kk_ref, ka_ref,
                yf_ref, yb_ref, sf_ref, sb_ref, st_ref, ops_ref, ptot_ref, *, nc):
    g = pl.program_id(0)
    ci = g % nc
    slot = g % 2
    mk = _wkv_masks()
    prm = (w2_ref, w0_ref, a2_ref, a0_ref, kk_ref, ka_ref, mk, ops_ref, ptot_ref)

    @pl.when(g == 0)
    def _():
        for _ in _wkv_prep((rkv_f0_ref, rkv_b0_ref), (lo_f0_ref, lo_b0_ref), *prm, 0):
            pass

    @pl.when(ci == 0)
    def _():
        for e in range(WKV_SEQS):
            st_ref[e, 0] = s0f_ref[e]
            st_ref[e, 1] = s0b_ref[e]

    chains = []
    for e in range(WKV_SEQS):
        for di, (d, y_ref) in enumerate((("f", yf_ref), ("b", yb_ref))):
            for p in range(N_PAIRS):
                chain = (e * 2 + di) * N_PAIRS + p
                ch = {"d": d, "e": e, "di": di, "p": p, "y_ref": y_ref, "st": st_ref[e, di, p],
                      "ptot": ptot_ref[slot, chain][0:1]}
                for j, name in enumerate(WKV_OPERANDS):
                    ch[name] = ops_ref[slot, chain, j]
                chains.append(ch)
    prep = _wkv_prep((rkv_f1_ref, rkv_b1_ref), (lo_f1_ref, lo_b1_ref), *prm, 1 - slot)
    for _ in _wkv_chains(chains, mk):
        for _ in range(WKV_SEQS):
            next(prep, None)
    for _ in prep:
        pass
    for ch in chains:
        ch["y_ref"][ch["e"], :, ch["p"] * LANES:(ch["p"] + 1) * LANES] = ch["y"]
        st_ref[ch["e"], ch["di"], ch["p"]] = ch["st_new"]

    @pl.when(ci == nc - 1)
    def _():
        for e in range(WKV_SEQS):
            for di, s_ref in enumerate((sf_ref, sb_ref)):
                for p in range(N_PAIRS):
                    st_t = st_ref[e, di, p].T
                    s_ref[e, 2 * p] = st_t[:HEAD_DIM, :HEAD_DIM]
                    s_ref[e, 2 * p + 1] = st_t[HEAD_DIM:, HEAD_DIM:]


def _wkv_call(rkv, lora, s0f, s0b, w2p, w0, a2p, a0, k_k, k_a, batch, seq_len):
    n = rkv.shape[0]
    nc = seq_len // CHUNK
    assert batch % WKV_SEQS == 0
    steps = (batch // WKV_SEQS) * nc
    rkv3 = rkv.reshape(batch, seq_len, 3 * RWKV_DIM)
    lora3 = lora.reshape(batch, seq_len, LORA_COLS)
    at = lambda g, back: (g // nc, (nc - 1 - g % nc) if back else g % nc, 0)
    fwd = lambda g: at(g, False)
    bwd = lambda g: at(g, True)
    fwd_first = lambda g: (0, 0, 0)
    bwd_first = lambda g: (0, nc - 1, 0)
    fwd_next = lambda g: at(jnp.minimum(g + 1, steps - 1), False)
    bwd_next = lambda g: at(jnp.minimum(g + 1, steps - 1), True)
    st = lambda g: (g // nc, 0, 0, 0)
    st_in = lambda g: ((g // nc) % (s0f.shape[0] // WKV_SEQS), 0, 0, 0)
    c2 = lambda g: (0, 0)
    c3 = lambda g: (0, 0, 0)
    st_shape = jax.ShapeDtypeStruct((batch, RWKV_HEADS, HEAD_DIM, HEAD_DIM), F32)
    y_shape = jax.ShapeDtypeStruct((batch, seq_len, RWKV_DIM), F32)
    n_chains = WKV_SEQS * 2 * N_PAIRS
    rkv_blk = (WKV_SEQS, CHUNK, 3 * RWKV_DIM)
    lora_blk = (WKV_SEQS, CHUNK, LORA_COLS)
    yf, yb, sf, sb = pl.pallas_call(
        functools.partial(_wkv_kernel, nc=nc),
        grid=(steps,),
        in_specs=[pl.BlockSpec(rkv_blk, fwd_first),
                  pl.BlockSpec(lora_blk, fwd_first),
                  pl.BlockSpec(rkv_blk, bwd_first),
                  pl.BlockSpec(lora_blk, bwd_first),
                  pl.BlockSpec(rkv_blk, fwd_next),
                  pl.BlockSpec(lora_blk, fwd_next),
                  pl.BlockSpec(rkv_blk, bwd_next),
                  pl.BlockSpec(lora_blk, bwd_next),
                  pl.BlockSpec((WKV_SEQS, N_PAIRS, LANES, LANES), st_in),
                  pl.BlockSpec((WKV_SEQS, N_PAIRS, LANES, LANES), st_in),
                  pl.BlockSpec((2, LANES, RWKV_DIM), c3),
                  pl.BlockSpec((2, RWKV_DIM), c2),
                  pl.BlockSpec((2, LANES, RWKV_DIM), c3),
                  pl.BlockSpec((2, RWKV_DIM), c2),
                  pl.BlockSpec((1, RWKV_DIM), c2),
                  pl.BlockSpec((1, RWKV_DIM), c2)],
        out_specs=[pl.BlockSpec((WKV_SEQS, CHUNK, RWKV_DIM), fwd),
                   pl.BlockSpec((WKV_SEQS, CHUNK, RWKV_DIM), bwd),
                   pl.BlockSpec((WKV_SEQS, RWKV_HEADS, HEAD_DIM, HEAD_DIM), st),
                   pl.BlockSpec((WKV_SEQS, RWKV_HEADS, HEAD_DIM, HEAD_DIM), st)],
        out_shape=[y_shape, y_shape, st_shape, st_shape],
        scratch_shapes=[pltpu.VMEM((WKV_SEQS, 2, N_PAIRS, LANES, LANES), F32),
                        pltpu.VMEM((2, n_chains, len(WKV_OPERANDS), CHUNK, LANES), F32),
                        pltpu.VMEM((2, n_chains, SUBLANES, LANES), F32)],
        compiler_params=_params("arbitrary"),
        name="wkv_chunked",
    )(rkv3, lora3, rkv3, lora3, rkv3, lora3, rkv3, lora3, s0f, s0b, w2p, w0, a2p, a0, k_k, k_a)
    return yf.reshape(n, RWKV_DIM), yb.reshape(n, RWKV_DIM), sf, sb


def _out_kernel(x_ref, mod_ref, conv_ref, yf_ref, yb_ref, rkv_ref, lo_ref,
                a2_ref, a0_ref, ka_ref, rk_ref, gng_ref, gnb_ref, g2_ref, wout_ref,
                postg_ref, preg_ref, rw_ref, rb_ref,
                x1_ref, h2_ref, route_ref, route_t_ref, cnt_ref):
    @pl.when(pl.program_id(0) == 0)
    def _():
        cnt_ref[...] = jnp.zeros(cnt_ref.shape, F32)

    m = mod_ref[0]
    rkv = rkv_ref[...]
    lo = lo_ref[...]
    r = rkv[:, :RWKV_DIM]
    k = rkv[:, RWKV_DIM:2 * RWKV_DIM]
    v = rkv[:, 2 * RWKV_DIM:]
    ha = lo[:, LANES:2 * LANES]
    ag_f = _sigmoid(a0_ref[0:1, :] + _mm(ha, a2_ref[0]))
    ag_b = _sigmoid(a0_ref[1:2, :] + _mm(ha, a2_ref[1]))
    rkk = r * k * rk_ref[...] * (2.0 + (ag_f + ag_b - 2.0) * ka_ref[...])
    o = yf_ref[...] + yb_ref[...]
    gate = _mm(_sigmoid(lo[:, 2 * LANES:]), g2_ref[...])
    r2 = lax.broadcasted_iota(jnp.int32, (LANES, LANES), 0)
    c2 = lax.broadcasted_iota(jnp.int32, (LANES, LANES), 1)
    ones_bd = ((r2 // HEAD_DIM) == (c2 // HEAD_DIM)).astype(BF16)
    parts = []
    for p in range(N_PAIRS):
        cols = slice(p * LANES, (p + 1) * LANES)
        op = o[:, cols]
        mu = _head_sums(op, ones_bd) * (1.0 / HEAD_DIM)
        oc = op - mu
        var = _head_sums(oc * oc, ones_bd) * (1.0 / HEAD_DIM)
        on = oc * lax.rsqrt(var + GN_EPS) * gng_ref[:, cols] + gnb_ref[:, cols]
        bonus = _head_sums(rkk[:, cols], ones_bd) * v[:, cols]
        parts.append((on + bonus) * gate[:, cols])
    mix_in = jnp.concatenate([conv_ref[...]] + parts, axis=1)
    mix = jnp.dot(mix_in.astype(BF16), wout_ref[...], preferred_element_type=F32)
    x1 = x_ref[...] + m[2:3] * _rms(mix, postg_ref[...])
    x1_ref[...] = x1
    h2 = _rms(x1, preg_ref[...]) * (1.0 + m[4:5]) + m[3:4]
    h2_ref[...] = _pack_rows(h2)
    hh, hl = _split2(h2)
    dot = functools.partial(jnp.dot, preferred_element_type=F32)
    logits = dot(hh, rw_ref[0]) + (dot(hh, rw_ref[1]) + dot(hl, rw_ref[0])) + rb_ref[...]
    lane =lax.broadcasted_iota(jnp.int32, logits.shape, 1)
    work = logits
    picks, vals, idxs = [], [], []
    for _ in range(TOP_K):
        mx = jnp.max(work, axis=-1, keepdims=True)
        idx = jnp.min(jnp.where(work == mx, lane, LANES), axis=-1, keepdims=True)
        pick = lane == idx
        picks.append(pick)
        vals.append(mx)
        idxs.append(idx)
        work = jnp.where(pick, 2.0 * NEG_BIG, work)
    exps = [jnp.exp(val - vals[0]) for val in vals]
    den = exps[0] + exps[1] + exps[2] + exps[3]
    sel = jnp.zeros(logits.shape, F32)
    for pick in picks:
        sel = jnp.where(pick, 1.0, sel)
    tr = lax.broadcasted_iota(jnp.int32, (TM_OUT, TM_OUT), 0)
    tc = lax.broadcasted_iota(jnp.int32, (TM_OUT, TM_OUT), 1)
    before = (tc < tr).astype(BF16)
    rank = dot(before, sel.astype(BF16)) + cnt_ref[...]
    cnt_ref[...] = cnt_ref[...] + jnp.sum(sel, axis=0, keepdims=True)
    route = jnp.zeros(logits.shape, F32)
    for q, (pick, idx, e) in enumerate(zip(picks, idxs, exps)):
        rank_q = jnp.sum(jnp.where(pick, rank, 0.0), axis=-1, keepdims=True)
        route = jnp.where(lane == q, idx.astype(F32), route)
        route = jnp.where(lane == TOP_K + q, e / den, route)
        route = jnp.where(lane == 2 * TOP_K + q, rank_q, route)
    route_ref[...] = route
    route_t_ref[...] = route.T[:ROUTE_ROWS, :]


def _out_call(x, mod, conv_out, yf, yb, rkv, lora, wts, seq_len):
    n, d = x.shape
    row = lambda i: (i, 0)
    c2 = lambda i: (0, 0)
    c3 = lambda i: (0, 0, 0)
    full = lambda a: pl.BlockSpec(a.shape, c3 if a.ndim == 3 else c2)
    consts = [wts["a2p"], wts["a0"], wts["k_a"], wts["r_k"], wts["gn_g"], wts["gn_b"], wts["g2"],
              wts["w_out"], wts["post_mix_g"], wts["pre_ffn_g"], wts["router_w"], wts["router_b"]]
    return pl.pallas_call(
        _out_kernel,
        grid=(n // TM_OUT,),
        in_specs=[pl.BlockSpec((TM_OUT, d), row),
                  pl.BlockSpec((1, N_MOD, d), _mod_index(TM_OUT, seq_len, mod.shape[0])),
                  pl.BlockSpec((TM_OUT, CONV_CH), row),
                  pl.BlockSpec((TM_OUT, RWKV_DIM), row),
                  pl.BlockSpec((TM_OUT, RWKV_DIM), row),
                  pl.BlockSpec((TM_OUT, 3 * RWKV_DIM), row),
                  pl.BlockSpec((TM_OUT, LORA_COLS), row)] + [full(a) for a in consts],
        out_specs=[pl.BlockSpec((TM_OUT, d), row),
                   pl.BlockSpec((TM_OUT, d // 2), row),
                   pl.BlockSpec((TM_OUT, LANES), row),
                   pl.BlockSpec((ROUTE_ROWS, TM_OUT), lambda i: (0, i)),
                   pl.BlockSpec((1, LANES), c2)],
        out_shape=[jax.ShapeDtypeStruct((n, d), F32),
                   jax.ShapeDtypeStruct((n, d // 2), jnp.int32),
                   jax.ShapeDtypeStruct((n, LANES), F32),
                   jax.ShapeDtypeStruct((ROUTE_ROWS, n), F32),
                   jax.ShapeDtypeStruct((1, LANES), F32)],
        compiler_params=_params("arbitrary"),
        name="out_router",
    )(x, mod, conv_out, yf, yb, rkv, lora, *consts)


def _sc_mesh():
    return plsc.VectorSubcoreMesh(core_axis_name="c", subcore_axis_name="s",
                                  num_cores=SC_CORES, num_subcores=SC_SUBCORES)


def _sc_worker_id():
    return lax.axis_index("s") * SC_CORES + lax.axis_index("c")


def _sc_scatter_rows(src_a, src_b, idx, n_out):
    (n_a, d), n_b = src_a.shape, src_b.shape[0]
    n = n_a + n_b
    workers = SC_CORES * SC_SUBCORES
    per_worker = n // workers
    assert idx.shape[0] == TOP_K * n and per_worker * workers == n
    assert per_worker % SC_ROWS == 0 and n_a % per_worker == 0

    @functools.partial(
        pl.kernel, mesh=_sc_mesh(),
        out_type=jax.ShapeDtypeStruct((n_out, d), src_a.dtype),
        scratch_types=[pltpu.VMEM((SC_ROWS,), jnp.int32),
                       pltpu.VMEM((SC_ROWS, d), src_a.dtype),
                       pltpu.SemaphoreType.DMA],
        name="sc_scatter_rows")
    def scatter(a_hbm, b_hbm, idx_hbm, out_hbm, idx_v, rows_v, sem):
        base = _sc_worker_id() * per_worker

        @pl.loop(0, per_worker // SC_ROWS)
        def _(j):
            tok = base + j * SC_ROWS

            @pl.when(tok < n_a)
            def _():
                pltpu.sync_copy(a_hbm.at[pl.ds(tok, SC_ROWS)], rows_v)

            @pl.when(tok >= n_a)
            def _():
                pltpu.sync_copy(b_hbm.at[pl.ds(tok - n_a, SC_ROWS)], rows_v)

            for q in range(TOP_K):
                pltpu.sync_copy(idx_hbm.at[pl.ds(q * n + tok, SC_ROWS)], idx_v)
                pltpu.async_copy(rows_v, out_hbm.at[idx_v], sem).wait()

    return scatter(src_a, src_b, idx)


def _sc_gather_rows(src, idx):
    n_idx, d = idx.shape[0], src.shape[1]
    workers = SC_CORES * SC_SUBCORES
    per_worker = n_idx // workers
    assert per_worker * workers == n_idx and per_worker % SC_ROWS == 0

    @functools.partial(
        pl.kernel, mesh=_sc_mesh(),
        out_type=jax.ShapeDtypeStruct((n_idx, d), src.dtype),
        scratch_types=[pltpu.VMEM((SC_ROWS,), jnp.int32),
                       pltpu.VMEM((SC_ROWS, d), src.dtype),
                       pltpu.SemaphoreType.DMA],
        name="sc_gather_rows")
    def gather(src_hbm, idx_hbm, out_hbm, idx_v, rows_v, sem):
        base = _sc_worker_id() * per_worker

        @pl.loop(0, per_worker // SC_ROWS)
        def _(j):
            off = base + j * SC_ROWS
            pltpu.sync_copy(idx_hbm.at[pl.ds(off, SC_ROWS)], idx_v)
            pltpu.async_copy(src_hbm.at[idx_v], rows_v, sem).wait()
            pltpu.sync_copy(rows_v, out_hbm.at[pl.ds(off, SC_ROWS)])

    return gather(src, idx)


def _moe_kernel(blk_e_ref, blk_first_ref, blk_rows_ref, blk_next_ref,
                xs_ref, wgu_hbm, bgu_ref, wd_hbm, bd_ref,
                o_ref, wgu_f32_ref, wd_f32_ref, wgu_bf_ref, wd_bf_ref, sem):
    b = pl.program_id(0)
    n_rows = blk_rows_ref[b]

    def weight_copies(e):
        return (pltpu.make_async_copy(wgu_hbm.at[e], wgu_f32_ref, sem.at[0]),
                pltpu.make_async_copy(wd_hbm.at[e], wd_f32_ref, sem.at[1]))

    @pl.when(b == 0)
    def _():
        for copy in weight_copies(blk_e_ref[0]):
            copy.start()

    @pl.when(blk_first_ref[b] == 1)
    def _():
        for copy in weight_copies(blk_e_ref[b]):
            copy.wait()
        wgu_bf_ref[...] = wgu_f32_ref[...].astype(BF16)
        wd_bf_ref[...] = wd_f32_ref[...].astype(BF16)

        @pl.when(blk_next_ref[b] >= 0)
        def _():
            for copy in weight_copies(blk_next_ref[b]):
                copy.start()

    half = MOE_BLOCK // 2

    def ffn(row0):
        rows = slice(row0, row0 + half)
        live = lax.broadcasted_iota(jnp.int32, (half, xs_ref.shape[1]), 0) < n_rows - row0
        xs = _unpack_rows(jnp.where(live, xs_ref[rows, :], 0))
        gu = jnp.dot(xs.astype(BF16), wgu_bf_ref[...],
                     preferred_element_type=F32) + bgu_ref[0]
        x_glu = jnp.minimum(gu[:, :D_FF], SWIGLU_LIMIT)
        x_lin = jnp.clip(gu[:, D_FF:], -SWIGLU_LIMIT, SWIGLU_LIMIT)
        act = x_glu * _sigmoid(SWIGLU_ALPHA * x_glu) * (x_lin + 1.0)
        o_ref[rows, :] = _pack_rows(jnp.dot(act.astype(BF16), wd_bf_ref[...],
                                            preferred_element_type=F32) + bd_ref[0])

    for row0 in (0, half):
        @pl.when(n_rows > row0)
        def _(row0=row0):
            ffn(row0)

        @pl.when(n_rows <= row0)
        def _(row0=row0):
            o_ref[row0:row0 + half, :] = jnp.zeros((half, o_ref.shape[1]), jnp.int32)


def _moe_call(blk_e, blk_first, blk_rows, blk_next, xs, wgu, bgu, wd, bd):
    n_slots, half = xs.shape
    d = 2 * half
    ex = lambda b, be, bf, br, bn: (be[b], 0, 0)
    row = lambda b, be, bf, br, bn: (b, 0)
    grid_spec = pltpu.PrefetchScalarGridSpec(
        num_scalar_prefetch=4,
        grid=(n_slots // MOE_BLOCK,),
        in_specs=[pl.BlockSpec((MOE_BLOCK, half), row),
                  pl.BlockSpec(memory_space=pl.ANY),
                  pl.BlockSpec((1, 1, 2 * D_FF), ex),
                  pl.BlockSpec(memory_space=pl.ANY),
                  pl.BlockSpec((1, 1, d), ex)],
        out_specs=pl.BlockSpec((MOE_BLOCK, half), row),
        scratch_shapes=[pltpu.VMEM((d, 2 * D_FF), F32), pltpu.VMEM((D_FF, d), F32),
                        pltpu.VMEM((d, 2 * D_FF), BF16), pltpu.VMEM((D_FF, d), BF16),
                        pltpu.SemaphoreType.DMA((2,))])
    return pl.pallas_call(
        _moe_kernel,
        grid_spec=grid_spec,
        out_shape=jax.ShapeDtypeStruct((n_slots, half), jnp.int32),
        compiler_params=_params("arbitrary"),
        name="moe_experts",
    )(blk_e, blk_first, blk_rows, blk_next, xs, wgu, bgu, wd, bd)


def _final_kernel(x1_ref, y4_ref, route_ref, mod_ref, g_ref, o_ref):
    m = mod_ref[0]
    route = route_ref[...]
    y = jnp.zeros(x1_ref.shape, F32)
    for q in range(TOP_K):
        y = y + route[:, TOP_K + q:TOP_K + q + 1] * _unpack_rows(y4_ref[q])
    o_ref[...] = x1_ref[...] + m[5:6] * _rms(y, g_ref[...])


def _final_call(x1, y4, route, mod, g, seq_len):
    n, d = x1.shape
    row = lambda i: (i, 0)
    return pl.pallas_call(
        _final_kernel,
        grid=(n // TM_FIN,),
        in_specs=[pl.BlockSpec((TM_FIN, d), row),
                  pl.BlockSpec((TOP_K, TM_FIN, d // 2), lambda i: (0, i, 0)),
                  pl.BlockSpec((TM_FIN, LANES), row),
                  pl.BlockSpec((1, N_MOD, d), _mod_index(TM_FIN, seq_len, mod.shape[0])),
                  pl.BlockSpec((1, d), lambda i: (0, 0))],
        out_specs=pl.BlockSpec((TM_FIN, d), row),
        out_shape=jax.ShapeDtypeStruct((n, d), F32),
        compiler_params=_params("arbitrary"),
        name="final_residual",
    )(x1, y4, route, mod, g)


def _pack_state(s):
    b = s.shape[0]
    st = jnp.swapaxes(s.astype(F32), -1, -2).reshape(b, N_PAIRS, 2, HEAD_DIM, HEAD_DIM)
    z = jnp.zeros_like(st[:, :, 0])
    top = jnp.concatenate([st[:, :, 0], z], axis=-1)
    bot = jnp.concatenate([z, st[:, :, 1]], axis=-1)
    return jnp.concatenate([top, bot], axis=-2)


def _mix_sublayer(x, mod, s0f, s0b, seg_len, wts):
    batch, seq_len, d = x.shape
    x2 = x.reshape(batch * seq_len, d)
    cv, rkv, lora = _in_call(x2, mod, wts["pre_mix_g"], wts["w_all"], seq_len)
    conv_out = _conv_call(cv, wts["conv_w"], wts["conv_b"], wts["conv_ln_g"], wts["conv_ln_b"],
                          seg_len)
    yf, yb, sf, sb = _wkv_call(rkv, lora, s0f, s0b, wts["w2p"], wts["w0"], wts["a2p"], wts["a0"],
                               wts["k_k"], wts["k_a"], batch, seq_len)
    x1, h2, route, route_t, cnt = _out_call(x2, mod, conv_out, yf, yb, rkv, lora, wts, seq_len)
    return x1, h2, route, cnt, sf, sb, route_t


def _dispatch_plan(routes, cnts):
    experts = jnp.arange(N_EXPERTS, dtype=jnp.int32)
    counts = [c[0, :N_EXPERTS].astype(jnp.int32) for c in cnts]
    total = sum(counts)
    padded = (total + MOE_BLOCK - 1) // MOE_BLOCK * MOE_BLOCK
    pad_end = jnp.cumsum(padded)
    pad_start = pad_end - padded
    dests = []
    prior = jnp.zeros_like(total)
    for route, count in zip(routes, counts):
        idx = route[:TOP_K].astype(jnp.int32)
        rank = route[2 * TOP_K:3 * TOP_K].astype(jnp.int32)
        first = jnp.sum(jnp.where(idx[..., None] == experts, pad_start + prior, 0), axis=-1)
        dests.append(first + rank)
        prior = prior + count
    dest = jnp.concatenate(dests, axis=1).reshape(-1)
    n_slots = dest.shape[0] + N_EXPERTS * MOE_BLOCK
    blk_row0 = jnp.arange(n_slots // MOE_BLOCK, dtype=jnp.int32) * MOE_BLOCK
    blk_e = jnp.sum((pad_end[None, :] <= blk_row0[:, None]).astype(jnp.int32), axis=-1)
    blk_e = jnp.minimum(blk_e, N_EXPERTS - 1)
    blk_first = jnp.concatenate([jnp.ones((1,), jnp.int32),
                                 (blk_e[1:] != blk_e[:-1]).astype(jnp.int32)])
    of_blk = lambda per_expert: jnp.sum(
        jnp.where(blk_e[:, None] == experts[None, :], per_expert[None, :], 0), axis=-1)
    blk_rows = jnp.clip(of_blk(pad_start + total) - blk_row0, 0, MOE_BLOCK).astype(jnp.int32)
    pos = jnp.arange(blk_e.shape[0], dtype=jnp.int32)
    later_first = (pos[None, :] > pos[:, None]) & (blk_first[None, :] == 1)
    next_pos = jnp.min(jnp.where(later_first, pos[None, :], pos.shape[0]), axis=-1)
    blk_next = jnp.where(next_pos < pos.shape[0],
                         blk_e[jnp.minimum(next_pos, pos.shape[0] - 1)], -1).astype(jnp.int32)
    return dest, n_slots, blk_e, blk_first, blk_rows, blk_next


def _routed_ffn(h2s, routes, cnts, wts):
    dest, n_slots, blk_e, blk_first, blk_rows, blk_next = _dispatch_plan(routes, cnts)
    xs = _sc_scatter_rows(h2s[0], h2s[1], dest, n_slots)
    out_sorted = _moe_call(blk_e, blk_first, blk_rows, blk_next, xs, wts["w_gu"], wts["b_gu"],
                           wts["w_down"], wts["b_down"])
    dest_qt = dest.reshape(TOP_K, -1)
    y4s, row0 = [], 0
    for h2 in h2s:
        n_path = h2.shape[0]
        y4 = _sc_gather_rows(out_sorted, dest_qt[:, row0:row0 + n_path].reshape(-1))
        y4s.append(y4.reshape(TOP_K, n_path, xs.shape[1]))
        row0 += n_path
    return y4s


def _prep_weights(l, pre_mix_g, post_mix_g, pre_ffn_g, post_ffn_g, w_in, w_out, conv_w, conv_b,
                  conv_ln_g, conv_ln_b, rw_w0, rw_w1, rw_w2, rw_a0, rw_a1, rw_a2, rw_g1, rw_g2,
                  rw_k_k, rw_k_a, rw_r_k, rw_gn_g, rw_gn_b, router_w, router_b, w_gu, b_gu,
                  w_down, b_down):
    row = lambda a: a.reshape(1, -1).astype(F32)
    zpad = jnp.zeros((HEAD_DIM, RWKV_DIM), F32)
    w_all = jnp.concatenate([w_in[l], rw_w1[l, 0], rw_w1[l, 1], rw_a1[l, 0], rw_a1[l, 1],
                             rw_g1[l]], axis=1).astype(BF16)
    w2p = jnp.stack([jnp.concatenate([rw_w2[l, 0], zpad], axis=0),
                     jnp.concatenate([zpad, rw_w2[l, 1]], axis=0)]).astype(BF16)
    a2p = jnp.stack([jnp.concatenate([rw_a2[l, 0], zpad], axis=0),
                     jnp.concatenate([zpad, rw_a2[l, 1]], axis=0)]).astype(BF16)
    rw_pad = jnp.pad(router_w[l].astype(F32), ((0, 0), (0, LANES - N_EXPERTS)))
    rw_hi = rw_pad.astype(BF16)
    rw_lo = (rw_pad - rw_hi.astype(F32)).astype(BF16)
    rb_pad = jnp.concatenate([router_b[l].astype(F32),
                              jnp.full((LANES - N_EXPERTS,), NEG_BIG, F32)]).reshape(1, LANES)
    return {
        "pre_mix_g": row(pre_mix_g[l]), "post_mix_g": row(post_mix_g[l]),
        "pre_ffn_g": row(pre_ffn_g[l]), "post_ffn_g": row(post_ffn_g[l]),
        "w_all": w_all, "w_out": w_out[l].astype(BF16),
        "conv_w": jnp.broadcast_to(conv_w[l].astype(F32)[:, None, :],
                                   (CONV_WIDTH, SUBLANES, CONV_CH)),
        "conv_b": row(conv_b[l]),
        "conv_ln_g": row(conv_ln_g[l]), "conv_ln_b": row(conv_ln_b[l]),
        "w2p": w2p, "w0": rw_w0[l].astype(F32), "a2p": a2p, "a0": rw_a0[l].astype(F32),
        "k_k": row(rw_k_k[l]), "k_a": row(rw_k_a[l]), "r_k": row(rw_r_k[l]),
        "gn_g": row(rw_gn_g[l]), "gn_b": row(rw_gn_b[l]), "g2": rw_g2[l].astype(BF16),
        "router_w": jnp.stack([rw_hi, rw_lo]), "router_b": rb_pad,
        "w_gu": w_gu[l].astype(F32), "b_gu": b_gu[l].reshape(N_EXPERTS, 1, -1).astype(F32),
        "w_down": w_down[l].astype(F32), "b_down": b_down[l].reshape(N_EXPERTS, 1, -1).astype(F32),
    }


def kernel(x_prompt, x_sample, state_wkv_fwd, state_wkv_bwd, c, c_ctx, ada_w, ada_b, pre_mix_g, post_mix_g, pre_ffn_g, post_ffn_g, w_in, w_out, conv_w, conv_b, conv_ln_g, conv_ln_b, rw_w0, rw_w1, rw_w2, rw_a0, rw_a1, rw_a2, rw_g1, rw_g2, rw_k_k, rw_k_a, rw_r_k, rw_gn_g, rw_gn_b, router_w, router_b, w_gu, b_gu, w_down, b_down):
    depth = ada_w.shape[0]
    dec_batch = c.shape[0]
    mod_rows = 16
    c_rows = jnp.concatenate([c, c_ctx[None, :],
                              jnp.zeros((mod_rows - dec_batch - 1, D_MODEL), F32)], axis=0)
    zero_state = jnp.zeros((WKV_SEQS, N_PAIRS, LANES, LANES), F32)
    y_prompt, y_sample = x_prompt, x_sample
    new_f, new_b = [], []
    for l in range(depth):
        wts = _prep_weights(l, pre_mix_g, post_mix_g, pre_ffn_g, post_ffn_g, w_in, w_out, conv_w,
                            conv_b, conv_ln_g, conv_ln_b, rw_w0, rw_w1, rw_w2, rw_a0, rw_a1, rw_a2,
                            rw_g1, rw_g2, rw_k_k, rw_k_a, rw_r_k, rw_gn_g, rw_gn_b, router_w,
                            router_b, w_gu, b_gu, w_down, b_down)
        mod = _mod_call(c_rows, ada_w[l], ada_b[l]).reshape(mod_rows, N_MOD, D_MODEL)
        mod_lat = mod[:dec_batch]
        mod_ctx = mod[dec_batch:dec_batch + 1]
        ctx = _mix_sublayer(y_prompt, mod_ctx, zero_state, zero_state, y_prompt.shape[1], wts)
        lat = _mix_sublayer(y_sample, mod_lat, _pack_state(state_wkv_fwd[:, l]),
                            _pack_state(state_wkv_bwd[:, l]), GRID_W, wts)
        new_f.append(ctx[4])
        new_b.append(ctx[5])
        y4_ctx, y4_lat = _routed_ffn([ctx[1], lat[1]], [ctx[6], lat[6]], [ctx[3], lat[3]], wts)
        y_prompt = _final_call(ctx[0], y4_ctx, ctx[2], mod_ctx, wts["post_ffn_g"],
                               y_prompt.shape[1]).reshape(y_prompt.shape)
        y_sample = _final_call(lat[0], y4_lat, lat[2], mod_lat, wts["post_ffn_g"],
                               y_sample.shape[1]).reshape(y_sample.shape)
    return (y_prompt, y_sample, jnp.stack(new_f, axis=1), jnp.stack(new_b, axis=1))
```

```python
import functools
import math

import jax
import jax.numpy as jnp
from jax import lax
from jax.experimental import pallas as pl
from jax.experimental.pallas import tpu as pltpu
from jax.experimental.pallas import tpu_sc as plsc

F32 = jnp.float32
BF16 = jnp.bfloat16

D_MODEL = 1024
CONV_CH = 512
RWKV_DIM = 512
HEAD_DIM = 64
RWKV_HEADS = 8
N_PAIRS = RWKV_HEADS // 2
CONV_WIDTH = 31
N_EXPERTS = 32
TOP_K = 4
D_FF = 1024
SWIGLU_LIMIT = 7.0
SWIGLU_ALPHA = 1.702
RMS_EPS = 1e-6
LN_EPS = 1e-5
GN_EPS = 64e-5
N_MOD = 6
GRID_W = 64

LANES = 128
SUBLANES = 8
CHUNK = 64
WKV_SEQS = 4
LORA_COLS = 384
TM_IN = 1024
TM_CONV = 512
TM_OUT = 512
TM_FIN = 1024
ROUTE_ROWS = 16
MOE_BLOCK = 512
SC_CORES = 2
SC_SUBCORES = 16
SC_ROWS = 128
NEG_BIG = -1e30
EXP_M05 = math.exp(-0.5)
VMEM_LIMIT = 56 * 1024 * 1024


def _sigmoid(x):
    return 1.0 / (1.0 + jnp.exp(-x))


def _mm(a, b):
    return jnp.dot(a.astype(BF16), b.astype(BF16), preferred_element_type=F32)


def _mm_nt(a, b):
    return lax.dot_general(a.astype(BF16), b.astype(BF16), (((1,), (1,)), ((), ())),
                           preferred_element_type=F32)


def _mm_tn(a, b):
    return lax.dot_general(a.astype(BF16), b.astype(BF16), (((0,), (0,)), ((), ())),
                           preferred_element_type=F32)


def _split2(x):
    hi = x.astype(BF16)
    lo = (x - hi.astype(F32)).astype(BF16)
    return hi, lo


def _split3(x):
    h1 = x.astype(BF16)
    r1 = x - h1.astype(F32)
    h2 = r1.astype(BF16)
    h3 = (r1 - h2.astype(F32)).astype(BF16)
    return h1, h2, h3


def _pack_rows(x):
    n = x.shape[1] // 2
    bits = lambda v: lax.bitcast_convert_type(v.astype(BF16).astype(F32), jnp.uint32)
    word = (bits(x[:, n:]) & jnp.uint32(0xFFFF0000)) | (bits(x[:, :n]) >> 16)
    return lax.bitcast_convert_type(word, jnp.int32)


def _unpack_rows(w):
    u = lax.bitcast_convert_type(w, jnp.uint32)
    lo = lax.bitcast_convert_type(u << 16, F32)
    hi = lax.bitcast_convert_type(u & jnp.uint32(0xFFFF0000), F32)
    return jnp.concatenate([lo, hi], axis=1)


def _rms(x, g):
    return x * lax.rsqrt(jnp.mean(x * x, axis=-1, keepdims=True) + RMS_EPS) * g


def _params(*sem):
    return pltpu.CompilerParams(dimension_semantics=sem, vmem_limit_bytes=VMEM_LIMIT)


def _mod_kernel(c_ref, w_ref, b_ref, o_ref):
    c = c_ref[...]
    s1, s2 = _split2(c * _sigmoid(c))
    w1, w2 = _split2(w_ref[...])
    dot = functools.partial(jnp.dot, preferred_element_type=F32)
    o_ref[...] = dot(s1, w1) + (dot(s1, w2) + dot(s2, w1)) + b_ref[...]


def _mod_call(c_rows, ada_w, ada_b):
    m, d = c_rows.shape
    n = ada_w.shape[1]
    tn = 512
    return pl.pallas_call(
        _mod_kernel,
        grid=(n // tn,),
        in_specs=[pl.BlockSpec((m, d), lambda j: (0, 0)),
                  pl.BlockSpec((d, tn), lambda j: (0, j)),
                  pl.BlockSpec((1, tn), lambda j: (0, j))],
        out_specs=pl.BlockSpec((m, tn), lambda j: (0, j)),
        out_shape=jax.ShapeDtypeStruct((m, n), F32),
        compiler_params=_params("arbitrary"),
        name="mod",
    )(c_rows, ada_w, ada_b.reshape(1, n))


def _in_kernel(x_ref, mod_ref, g_ref, w_ref, cv_ref, rkv_ref, lo_ref):
    m = mod_ref[0]
    h = _rms(x_ref[...], g_ref[...]) * (1.0 + m[1:2]) + m[0:1]
    proj = jnp.dot(h.astype(BF16), w_ref[...], preferred_element_type=F32)
    cv_ref[...] = proj[:, :CONV_CH] * _sigmoid(proj[:, CONV_CH:2 * CONV_CH])
    rkv_ref[...] = proj[:, 2 * CONV_CH:2 * CONV_CH + 3 * RWKV_DIM]
    lo_ref[...] = proj[:, 2 * CONV_CH + 3 * RWKV_DIM:]


def _mod_index(tile, seq_len, n_mod_rows):
    def index(i):
        return ((i * tile) // seq_len) % n_mod_rows, 0, 0
    return index


def _in_call(x, mod, g, w_all, seq_len):
    n, d = x.shape
    ncol = w_all.shape[1]
    return pl.pallas_call(
        _in_kernel,
        grid=(n // TM_IN,),
        in_specs=[pl.BlockSpec((TM_IN, d), lambda i: (i, 0)),
                  pl.BlockSpec((1, N_MOD, d), _mod_index(TM_IN, seq_len, mod.shape[0])),
                  pl.BlockSpec((1, d), lambda i: (0, 0)),
                  pl.BlockSpec((d, ncol), lambda i: (0, 0))],
        out_specs=[pl.BlockSpec((TM_IN, CONV_CH), lambda i: (i, 0)),
                   pl.BlockSpec((TM_IN, 3 * RWKV_DIM), lambda i: (i, 0)),
                   pl.BlockSpec((TM_IN, LORA_COLS), lambda i: (i, 0))],
        out_shape=[jax.ShapeDtypeStruct((n, CONV_CH), F32),
                   jax.ShapeDtypeStruct((n, 3 * RWKV_DIM), F32),
                   jax.ShapeDtypeStruct((n, LORA_COLS), F32)],
        compiler_params=_params("arbitrary"),
        name="in_proj",
    )(x, mod, g, w_all)


CONV_HALO = 16
CONV_ROWS = 32


def _conv_kernel(cv_ref, w_ref, b_ref, g_ref, be_ref, o_ref, pad_ref, acc_ref, *, seg_len):
    nseg = TM_CONV // seg_len
    stride = seg_len + 2 * CONV_HALO
    rows = nseg * stride
    u = cv_ref[...]
    pad_ref[0] = jnp.zeros(pad_ref.shape[1:], F32)
    for s in range(nseg):
        pad_ref[0, s * stride + CONV_HALO:s * stride + CONV_HALO + seg_len, :] = (
            u[s * seg_len:(s + 1) * seg_len])
    for r in range(1, SUBLANES):
        pad_ref[r, 0:rows, :] = pad_ref[0, r:r + rows, :]
    first = CONV_HALO - CONV_WIDTH // 2
    chunks_per_seg = seg_len // CONV_ROWS

    def row_chunk(i, carry):
        pad0 = i * CONV_ROWS + (i // chunks_per_seg) * (2 * CONV_HALO)
        accs = [jnp.zeros((SUBLANES, CONV_CH), F32) for _ in range(CONV_ROWS // SUBLANES)]
        for t in range(CONV_WIDTH):
            q, r = divmod(first + t, SUBLANES)
            w_t = w_ref[t]
            for j in range(len(accs)):
                start = pl.multiple_of(pad0 + (q + j) * SUBLANES, SUBLANES)
                accs[j] = accs[j] + w_t * pad_ref[r, pl.ds(start, SUBLANES), :]
        for j, acc in enumerate(accs):
            row0 = pl.multiple_of(i * CONV_ROWS + j * SUBLANES, SUBLANES)
            acc_ref[pl.ds(row0, SUBLANES), :] = acc
        return carry

    lax.fori_loop(0, TM_CONV // CONV_ROWS, row_chunk, 0)
    y = acc_ref[...] + b_ref[...]
    mu = jnp.mean(y, axis=-1, keepdims=True)
    yc = y - mu
    var = jnp.mean(yc * yc, axis=-1, keepdims=True)
    z = yc * lax.rsqrt(var + LN_EPS) * g_ref[...] + be_ref[...]
    o_ref[...] = z * _sigmoid(z)


def _conv_call(cv, conv_w, conv_b, ln_g, ln_b, seg_len):
    n = cv.shape[0]
    nseg = TM_CONV // seg_len
    row = lambda i: (i, 0)
    const = lambda i: (0, 0)
    return pl.pallas_call(
        functools.partial(_conv_kernel, seg_len=seg_len),
        grid=(n // TM_CONV,),
        in_specs=[pl.BlockSpec((TM_CONV, CONV_CH), row),
                  pl.BlockSpec((CONV_WIDTH, SUBLANES, CONV_CH), lambda i: (0, 0, 0)),
                  pl.BlockSpec((1, CONV_CH), const),
                  pl.BlockSpec((1, CONV_CH), const),
                  pl.BlockSpec((1, CONV_CH), const)],
        out_specs=pl.BlockSpec((TM_CONV, CONV_CH), row),
        out_shape=jax.ShapeDtypeStruct((n, CONV_CH), F32),
        scratch_shapes=[pltpu.VMEM((SUBLANES, nseg * (seg_len + 2 * CONV_HALO) + SUBLANES, CONV_CH),
                                   F32),
                        pltpu.VMEM((TM_CONV, CONV_CH), F32)],
        compiler_params=_params("arbitrary"),
        name="conv_module",
    )(cv, conv_w, conv_b, ln_g, ln_b)


def _head_sums(x, ones_bd):
    hi, lo = _split2(x)
    return jnp.dot(jnp.concatenate([hi, lo], axis=1), jnp.concatenate([ones_bd, ones_bd], axis=0),
                   preferred_element_type=F32)


def _wkv_masks():
    c = CHUNK
    r2 = lax.broadcasted_iota(jnp.int32, (2 * c, 2 * c), 0)
    c2 = lax.broadcasted_iota(jnp.int32, (2 * c, 2 * c), 1)
    same = (r2 // c) == (c2 // c)
    rr, cc = r2 % c, c2 % c
    r1 = lax.broadcasted_iota(jnp.int32, (c, 2 * c), 0)
    c1 = lax.broadcasted_iota(jnp.int32, (c, 2 * c), 1)
    cc1 = c1 % c
    rs = lax.broadcasted_iota(jnp.int32, (c, c), 0)
    cs = lax.broadcasted_iota(jnp.int32, (c, c), 1)
    masks = {"same": same, "eye": r2 == c2, "left": c1 < c,
             "ones_bd": same.astype(BF16)}
    for name, lt in (("f", lambda a, b: a < b), ("b", lambda a, b: a > b)):
        le = (lambda a, b, lt=lt: lt(a, b) | (a == b))
        masks["strict_" + name] = same & lt(cc, rr)
        masks["inc_" + name] = le(cc1, r1)
        masks["ak0_" + name] = (c1 >= c) & lt(cc1, r1)
        masks["ak1_" + name] = (c1 < c) & lt(cc1, r1)
        masks["tri_" + name] = le(cs, rs).astype(BF16)
    return masks


def _wkv_chains(chains, mk):
    c = CHUNK
    left = mk["left"]
    for ch in chains:
        zero = jnp.zeros_like(ch["at"])
        ch["at0"], ch["at1"] = jnp.where(left, ch["at"], zero), jnp.where(left, zero, ch["at"])
        rt0, rt1 = jnp.where(left, ch["rt"], zero), jnp.where(left, zero, ch["rt"])
        bt, kt = ch["bt"].astype(BF16), ch["kt"].astype(BF16)
        ch["g0"] = _mm_nt(jnp.concatenate([ch["at0"], rt0], axis=0), jnp.concatenate([bt, kt], axis=0))
        ch["g1"] = _mm_nt(jnp.concatenate([ch["at1"], rt1], axis=0), jnp.concatenate([kt, bt], axis=0))
    yield
    for ch in chains:
        g0, g1 = ch["g0"], ch["g1"]
        lmat = jnp.where(mk["strict_" + ch["d"]], jnp.concatenate([g0[:c], g1[:c]], axis=0), 0.0)
        ch["tinv"] = jnp.where(mk["eye"], 1.0, lmat)
        ch["lpow"] = _mm(lmat, lmat)
    for step in range(5):
        yield
        for ch in chains:
            lp = ch["lpow"].astype(BF16)
            if step < 4:
                prod = jnp.dot(lp, jnp.concatenate([lp, ch["tinv"].astype(BF16)], axis=1),
                               preferred_element_type=F32)
                ch["lpow"] = prod[:, :LANES]
                ch["tinv"] = ch["tinv"] + prod[:, LANES:]
            else:
                ch["tinv"] = ch["tinv"] + _mm(lp, ch["tinv"])
    yield
    for ch in chains:
        d, v = ch["d"], ch["v"]
        zv = jnp.zeros_like(v)
        ak0 = jnp.where(mk["ak0_" + d], ch["g0"][:c], 0.0)
        ak1 = jnp.where(mk["ak1_" + d], ch["g1"][:c], 0.0)
        ch["x0"] = _mm(ak0, jnp.concatenate([zv, v], axis=0))
        ch["x1"] = _mm(ak1, jnp.concatenate([v, zv], axis=0))
    yield
    for ch in chains:
        zero = jnp.zeros_like(ch["x0"])
        rhs = jnp.concatenate([
            jnp.concatenate([ch["at0"], jnp.where(left, ch["x0"], zero)], axis=1),
            jnp.concatenate([ch["at1"], jnp.where(left, zero, ch["x1"])], axis=1)], axis=0)
        ch["aw_blk"] = _mm(ch["tinv"], rhs)
    yield
    for ch in chains:
        d, v, aw_blk = ch["d"], ch["v"], ch["aw_blk"]
        zv = jnp.zeros_like(v)
        aw = aw_blk[:c] + aw_blk[c:]
        inc = mk["inc_" + d]
        g0, g1 = ch["g0"], ch["g1"]
        rb = jnp.where(inc, jnp.where(left, g0[c:], g1[c:]), 0.0)
        rk = jnp.where(inc, jnp.where(left, g1[c:], g0[c:]), 0.0)
        v0, v1 = jnp.where(left, v, zv), jnp.where(left, zv, v)
        vblk = jnp.concatenate([jnp.zeros((2 * c, LANES), F32),
                                jnp.concatenate([v1, v0], axis=0)], axis=1)
        qz = _mm(jnp.concatenate([rb, rk], axis=1), jnp.concatenate([aw_blk, vblk], axis=0))
        ch["q"] = ch["rt"] + qz[:, :LANES]
        ch["zz"] = qz[:, LANES:]
        rhs3 = jnp.concatenate([aw, jnp.concatenate([zv, v], axis=1)], axis=0)
        ch["mn"] = _mm_tn(jnp.concatenate([ch["bh"], ch["kh"]], axis=0), rhs3)
    yield
    same = mk["same"]
    for ch in chains:
        mn = ch["mn"]
        mt = jnp.where(same, mn[:, :LANES], 0.0) + jnp.where(mk["eye"], ch["ptot"], 0.0)
        nt = jnp.where(same, mn[:, LANES:], 0.0)
        both = _mm(jnp.concatenate([ch["q"], mt], axis=0), ch["st"])
        ch["y"] = both[:c] + ch["zz"]
        ch["st_new"] = both[c:] + nt


WKV_OPERANDS = ("at", "rt", "bt", "kt", "bh", "kh", "v")


def _wkv_prep(*args):
    for e in range(WKV_SEQS):
        yield from _wkv_prep_seq(e, *args)


def _wkv_prep_seq(e, rkv_refs, lo_refs, w2_ref, w0_ref, a2_ref, a0_ref, kk_ref, ka_ref, mk,
                  ops_ref, ptot_ref, slot):
    c = CHUNK
    for di, (d, rkv_ref, lo_ref) in enumerate(zip("fb", rkv_refs, lo_refs)):
        rkv = rkv_ref[e]
        lo = lo_ref[e]
        r = rkv[:, :RWKV_DIM]
        k = rkv[:, RWKV_DIM:2 * RWKV_DIM]
        v = rkv[:, 2 * RWKV_DIM:]
        w_raw = w0_ref[di:di + 1, :] + _mm(jnp.tanh(lo[:, :LANES]), w2_ref[di])
        lw = -EXP_M05 * _sigmoid(w_raw)
        yield
        ag = _sigmoid(a0_ref[di:di + 1, :] + _mm(lo[:, LANES:2 * LANES], a2_ref[di]))
        kd = k * (1.0 + (ag - 1.0) * ka_ref[...])
        kk_raw = k * kk_ref[...]
        h1, h2, h3 = _split3(lw)
        cs = jnp.dot(mk["tri_" + d], jnp.concatenate([h1, h2, h3], axis=1),
                     preferred_element_type=F32)
        logp = cs[:, :RWKV_DIM] + cs[:, RWKV_DIM:2 * RWKV_DIM] + cs[:, 2 * RWKV_DIM:]
        yield
        tot = logp[0:1] if d == "b" else logp[c - 1:c]
        e_inv = jnp.exp(-logp)
        e_hat = jnp.exp(tot - logp)
        rt = r * jnp.exp(logp)
        e_ex = jnp.exp(logp - lw)
        kt, kh = kd * e_inv, kd * e_hat
        ptot = jnp.exp(tot)
        for p in range(N_PAIRS):
            yield
            cols = slice(p * LANES, (p + 1) * LANES)
            kk = kk_raw[:, cols]
            kk = kk * lax.rsqrt(jnp.maximum(_head_sums(kk * kk, mk["ones_bd"]), 1e-24))
            kb = kk * ag[:, cols]
            tiles = {"at": -kk * e_ex[:, cols], "rt": rt[:, cols], "bt": kb * e_inv[:, cols],
                     "kt": kt[:, cols], "bh": kb * e_hat[:, cols], "kh": kh[:, cols],
                     "v": v[:, cols]}
            chain = (e * 2 + di) * N_PAIRS + p
            for j, name in enumerate(WKV_OPERANDS):
                ops_ref[slot, chain, j] = tiles[name]
            ptot_ref[slot, chain] = jnp.broadcast_to(ptot[:, cols], (SUBLANES, LANES))


def _wkv_kernel(rkv_f0_ref, lo_f0_ref, rkv_b0_ref, lo_b0_ref,
                rkv_f1_ref, lo_f1_ref, rkv_b1_ref, lo_b1_ref, s0f_ref, s0b_ref,
                w2_ref, w0_ref, a2_ref, a0_ref, kk_ref, ka_ref,
                yf_ref, yb_ref, sf_ref, sb_ref, st_ref, ops_ref, ptot_ref, *, nc):
    g = pl.program_id(0)
    ci = g % nc
    slot = g % 2
    mk = _wkv_masks()
    prm = (w2_ref, w0_ref, a2_ref, a0_ref, kk_ref, ka_ref, mk, ops_ref, ptot_ref)

    @pl.when(g == 0)
    def _():
        for _ in _wkv_prep((rkv_f0_ref, rkv_b0_ref), (lo_f0_ref, lo_b0_ref), *prm, 0):
            pass

    @pl.when(ci == 0)
    def _():
        for e in range(WKV_SEQS):
            st_ref[e, 0] = s0f_ref[e]
            st_ref[e, 1] = s0b_ref[e]

    chains = []
    for e in range(WKV_SEQS):
        for di, (d, y_ref) in enumerate((("f", yf_ref), ("b", yb_ref))):
            for p in range(N_PAIRS):
                chain = (e * 2 + di) * N_PAIRS + p
                ch = {"d": d, "e": e, "di": di, "p": p, "y_ref": y_ref, "st": st_ref[e, di, p],
                      "ptot": ptot_ref[slot, chain][0:1]}
                for j, name in enumerate(WKV_OPERANDS):
                    ch[name] = ops_ref[slot, chain, j]
                chains.append(ch)
    prep = _wkv_prep((rkv_f1_ref, rkv_b1_ref), (lo_f1_ref, lo_b1_ref), *prm, 1 - slot)
    for _ in _wkv_chains(chains, mk):
        for _ in range(WKV_SEQS):
            next(prep, None)
    for _ in prep:
        pass
    for ch in chains:
        ch["y_ref"][ch["e"], :, ch["p"] * LANES:(ch["p"] + 1) * LANES] = ch["y"]
        st_ref[ch["e"], ch["di"], ch["p"]] = ch["st_new"]

    @pl.when(ci == nc - 1)
    def _():
        for e in range(WKV_SEQS):
            for di, s_ref in enumerate((sf_ref, sb_ref)):
                for p in range(N_PAIRS):
                    st_t = st_ref[e, di, p].T
                    s_ref[e, 2 * p] = st_t[:HEAD_DIM, :HEAD_DIM]
                    s_ref[e, 2 * p + 1] = st_t[HEAD_DIM:, HEAD_DIM:]


def _wkv_call(rkv, lora, s0f, s0b, w2p, w0, a2p, a0, k_k, k_a, batch, seq_len):
    n = rkv.shape[0]
    nc = seq_len // CHUNK
    assert batch % WKV_SEQS == 0
    steps = (batch // WKV_SEQS) * nc
    rkv3 = rkv.reshape(batch, seq_len, 3 * RWKV_DIM)
    lora3 = lora.reshape(batch, seq_len, LORA_COLS)
    at = lambda g, back: (g // nc, (nc - 1 - g % nc) if back else g % nc, 0)
    fwd = lambda g: at(g, False)
    bwd = lambda g: at(g, True)
    fwd_first = lambda g: (0, 0, 0)
    bwd_first = lambda g: (0, nc - 1, 0)
    fwd_next = lambda g: at(jnp.minimum(g + 1, steps - 1), False)
    bwd_next = lambda g: at(jnp.minimum(g + 1, steps - 1), True)
    st = lambda g: (g // nc, 0, 0, 0)
    st_in = lambda g: ((g // nc) % (s0f.shape[0] // WKV_SEQS), 0, 0, 0)
    c2 = lambda g: (0, 0)
    c3 = lambda g: (0, 0, 0)
    st_shape = jax.ShapeDtypeStruct((batch, RWKV_HEADS, HEAD_DIM, HEAD_DIM), F32)
    y_shape = jax.ShapeDtypeStruct((batch, seq_len, RWKV_DIM), F32)
    n_chains = WKV_SEQS * 2 * N_PAIRS
    rkv_blk = (WKV_SEQS, CHUNK, 3 * RWKV_DIM)
    lora_blk = (WKV_SEQS, CHUNK, LORA_COLS)
    yf, yb, sf, sb = pl.pallas_call(
        functools.partial(_wkv_kernel, nc=nc),
        grid=(steps,),
        in_specs=[pl.BlockSpec(rkv_blk, fwd_first),
                  pl.BlockSpec(lora_blk, fwd_first),
                  pl.BlockSpec(rkv_blk, bwd_first),
                  pl.BlockSpec(lora_blk, bwd_first),
                  pl.BlockSpec(rkv_blk, fwd_next),
                  pl.BlockSpec(lora_blk, fwd_next),
                  pl.BlockSpec(rkv_blk, bwd_next),
                  pl.BlockSpec(lora_blk, bwd_next),
                  pl.BlockSpec((WKV_SEQS, N_PAIRS, LANES, LANES), st_in),
                  pl.BlockSpec((WKV_SEQS, N_PAIRS, LANES, LANES), st_in),
                  pl.BlockSpec((2, LANES, RWKV_DIM), c3),
                  pl.BlockSpec((2, RWKV_DIM), c2),
                  pl.BlockSpec((2, LANES, RWKV_DIM), c3),
                  pl.BlockSpec((2, RWKV_DIM), c2),
                  pl.BlockSpec((1, RWKV_DIM), c2),
                  pl.BlockSpec((1, RWKV_DIM), c2)],
        out_specs=[pl.BlockSpec((WKV_SEQS, CHUNK, RWKV_DIM), fwd),
                   pl.BlockSpec((WKV_SEQS, CHUNK, RWKV_DIM), bwd),
                   pl.BlockSpec((WKV_SEQS, RWKV_HEADS, HEAD_DIM, HEAD_DIM), st),
                   pl.BlockSpec((WKV_SEQS, RWKV_HEADS, HEAD_DIM, HEAD_DIM), st)],
        out_shape=[y_shape, y_shape, st_shape, st_shape],
        scratch_shapes=[pltpu.VMEM((WKV_SEQS, 2, N_PAIRS, LANES, LANES), F32),
                        pltpu.VMEM((2, n_chains, len(WKV_OPERANDS), CHUNK, LANES), F32),
                        pltpu.VMEM((2, n_chains, SUBLANES, LANES), F32)],
        compiler_params=_params("arbitrary"),
        name="wkv_chunked",
    )(rkv3, lora3, rkv3, lora3, rkv3, lora3, rkv3, lora3, s0f, s0b, w2p, w0, a2p, a0, k_k, k_a)
    return yf.reshape(n, RWKV_DIM), yb.reshape(n, RWKV_DIM), sf, sb


def _out_kernel(x_ref, mod_ref, conv_ref, yf_ref, yb_ref, rkv_ref, lo_ref,
                a2_ref, a0_ref, ka_ref, rk_ref, gng_ref, gnb_ref, g2_ref, wout_ref,
                postg_ref, preg_ref, rw_ref, rb_ref,
                x1_ref, h2_ref, route_ref, route_t_ref, cnt_ref):
    @pl.when(pl.program_id(0) == 0)
    def _():
        cnt_ref[...] = jnp.zeros(cnt_ref.shape, F32)

    m = mod_ref[0]
    rkv = rkv_ref[...]
    lo = lo_ref[...]
    r = rkv[:, :RWKV_DIM]
    k = rkv[:, RWKV_DIM:2 * RWKV_DIM]
    v = rkv[:, 2 * RWKV_DIM:]
    ha = lo[:, LANES:2 * LANES]
    ag_f = _sigmoid(a0_ref[0:1, :] + _mm(ha, a2_ref[0]))
    ag_b = _sigmoid(a0_ref[1:2, :] + _mm(ha, a2_ref[1]))
    rkk = r * k * rk_ref[...] * (2.0 + (ag_f + ag_b - 2.0) * ka_ref[...])
    o = yf_ref[...] + yb_ref[...]
    gate = _mm(_sigmoid(lo[:, 2 * LANES:]), g2_ref[...])
    r2 = lax.broadcasted_iota(jnp.int32, (LANES, LANES), 0)
    c2 = lax.broadcasted_iota(jnp.int32, (LANES, LANES), 1)
    ones_bd = ((r2 // HEAD_DIM) == (c2 // HEAD_DIM)).astype(BF16)
    parts = []
    for p in range(N_PAIRS):
        cols = slice(p * LANES, (p + 1) * LANES)
        op = o[:, cols]
        mu = _head_sums(op, ones_bd) * (1.0 / HEAD_DIM)
        oc = op - mu
        var = _head_sums(oc * oc, ones_bd) * (1.0 / HEAD_DIM)
        on = oc * lax.rsqrt(var + GN_EPS) * gng_ref[:, cols] + gnb_ref[:, cols]
        bonus = _head_sums(rkk[:, cols], ones_bd) * v[:, cols]
        parts.append((on + bonus) * gate[:, cols])
    mix_in = jnp.concatenate([conv_ref[...]] + parts, axis=1)
    mix = jnp.dot(mix_in.astype(BF16), wout_ref[...], preferred_element_type=F32)
    x1 = x_ref[...] + m[2:3] * _rms(mix, postg_ref[...])
    x1_ref[...] = x1
    h2 = _rms(x1, preg_ref[...]) * (1.0 + m[4:5]) + m[3:4]
    h2_ref[...] = _pack_rows(h2)
    hh, hl = _split2(h2)
    dot = functools.partial(jnp.dot, preferred_element_type=F32)
    logits = dot(hh, rw_ref[0]) + (dot(hh, rw_ref[1]) + dot(hl, rw_ref[0])) + rb_ref[...]
    lane =lax.broadcasted_iota(jnp.int32, logits.shape, 1)
    work = logits
    picks, vals, idxs = [], [], []
    for _ in range(TOP_K):
        mx = jnp.max(work, axis=-1, keepdims=True)
        idx = jnp.min(jnp.where(work == mx, lane, LANES), axis=-1, keepdims=True)
        pick = lane == idx
        picks.append(pick)
        vals.append(mx)
        idxs.append(idx)
        work = jnp.where(pick, 2.0 * NEG_BIG, work)
    exps = [jnp.exp(val - vals[0]) for val in vals]
    den = exps[0] + exps[1] + exps[2] + exps[3]
    sel = jnp.zeros(logits.shape, F32)
    for pick in picks:
        sel = jnp.where(pick, 1.0, sel)
    tr = lax.broadcasted_iota(jnp.int32, (TM_OUT, TM_OUT), 0)
    tc = lax.broadcasted_iota(jnp.int32, (TM_OUT, TM_OUT), 1)
    before = (tc < tr).astype(BF16)
    rank = dot(before, sel.astype(BF16)) + cnt_ref[...]
    cnt_ref[...] = cnt_ref[...] + jnp.sum(sel, axis=0, keepdims=True)
    route = jnp.zeros(logits.shape, F32)
    for q, (pick, idx, e) in enumerate(zip(picks, idxs, exps)):
        rank_q = jnp.sum(jnp.where(pick, rank, 0.0), axis=-1, keepdims=True)
        route = jnp.where(lane == q, idx.astype(F32), route)
        route = jnp.where(lane == TOP_K + q, e / den, route)
        route = jnp.where(lane == 2 * TOP_K + q, rank_q, route)
    route_ref[...] = route
    route_t_ref[...] = route.T[:ROUTE_ROWS, :]


def _out_call(x, mod, conv_out, yf, yb, rkv, lora, wts, seq_len):
    n, d = x.shape
    row = lambda i: (i, 0)
    c2 = lambda i: (0, 0)
    c3 = lambda i: (0, 0, 0)
    full = lambda a: pl.BlockSpec(a.shape, c3 if a.ndim == 3 else c2)
    consts = [wts["a2p"], wts["a0"], wts["k_a"], wts["r_k"], wts["gn_g"], wts["gn_b"], wts["g2"],
              wts["w_out"], wts["post_mix_g"], wts["pre_ffn_g"], wts["router_w"], wts["router_b"]]
    return pl.pallas_call(
        _out_kernel,
        grid=(n // TM_OUT,),
        in_specs=[pl.BlockSpec((TM_OUT, d), row),
                  pl.BlockSpec((1, N_MOD, d), _mod_index(TM_OUT, seq_len, mod.shape[0])),
                  pl.BlockSpec((TM_OUT, CONV_CH), row),
                  pl.BlockSpec((TM_OUT, RWKV_DIM), row),
                  pl.BlockSpec((TM_OUT, RWKV_DIM), row),
                  pl.BlockSpec((TM_OUT, 3 * RWKV_DIM), row),
                  pl.BlockSpec((TM_OUT, LORA_COLS), row)] + [full(a) for a in consts],
        out_specs=[pl.BlockSpec((TM_OUT, d), row),
                   pl.BlockSpec((TM_OUT, d // 2), row),
                   pl.BlockSpec((TM_OUT, LANES), row),
                   pl.BlockSpec((ROUTE_ROWS, TM_OUT), lambda i: (0, i)),
                   pl.BlockSpec((1, LANES), c2)],
        out_shape=[jax.ShapeDtypeStruct((n, d), F32),
                   jax.ShapeDtypeStruct((n, d // 2), jnp.int32),
                   jax.ShapeDtypeStruct((n, LANES), F32),
                   jax.ShapeDtypeStruct((ROUTE_ROWS, n), F32),
                   jax.ShapeDtypeStruct((1, LANES), F32)],
        compiler_params=_params("arbitrary"),
        name="out_router",
    )(x, mod, conv_out, yf, yb, rkv, lora, *consts)


def _sc_mesh():
    return plsc.VectorSubcoreMesh(core_axis_name="c", subcore_axis_name="s",
                                  num_cores=SC_CORES, num_subcores=SC_SUBCORES)


def _sc_worker_id():
    return lax.axis_index("s") * SC_CORES + lax.axis_index("c")


def _sc_scatter_rows(src_a, src_b, idx, n_out):
    (n_a, d), n_b = src_a.shape, src_b.shape[0]
    n = n_a + n_b
    workers = SC_CORES * SC_SUBCORES
    per_worker = n // workers
    assert idx.shape[0] == TOP_K * n and per_worker * workers == n
    assert per_worker % SC_ROWS == 0 and n_a % per_worker == 0

    @functools.partial(
        pl.kernel, mesh=_sc_mesh(),
        out_type=jax.ShapeDtypeStruct((n_out, d), src_a.dtype),
        scratch_types=[pltpu.VMEM((SC_ROWS,), jnp.int32),
                       pltpu.VMEM((SC_ROWS, d), src_a.dtype),
                       pltpu.SemaphoreType.DMA],
        name="sc_scatter_rows")
    def scatter(a_hbm, b_hbm, idx_hbm, out_hbm, idx_v, rows_v, sem):
        base = _sc_worker_id() * per_worker

        @pl.loop(0, per_worker // SC_ROWS)
        def _(j):
            tok = base + j * SC_ROWS

            @pl.when(tok < n_a)
            def _():
                pltpu.sync_copy(a_hbm.at[pl.ds(tok, SC_ROWS)], rows_v)

            @pl.when(tok >= n_a)
            def _():
                pltpu.sync_copy(b_hbm.at[pl.ds(tok - n_a, SC_ROWS)], rows_v)

            for q in range(TOP_K):
                pltpu.sync_copy(idx_hbm.at[pl.ds(q * n + tok, SC_ROWS)], idx_v)
                pltpu.async_copy(rows_v, out_hbm.at[idx_v], sem).wait()

    return scatter(src_a, src_b, idx)


def _sc_gather_rows(src, idx):
    n_idx, d = idx.shape[0], src.shape[1]
    workers = SC_CORES * SC_SUBCORES
    per_worker = n_idx // workers
    half = SC_ROWS // 2
    assert per_worker * workers == n_idx and per_worker % SC_ROWS == 0

    @functools.partial(
        pl.kernel, mesh=_sc_mesh(),
        out_type=jax.ShapeDtypeStruct((n_idx, d), src.dtype),
        scratch_types=[pltpu.VMEM((half,), jnp.int32), pltpu.VMEM((half,), jnp.int32),
                       pltpu.VMEM((half, d), src.dtype), pltpu.VMEM((half, d), src.dtype),
                       pltpu.SemaphoreType.DMA, pltpu.SemaphoreType.DMA,
                       pltpu.SemaphoreType.DMA, pltpu.SemaphoreType.DMA],
        name="sc_gather_rows")
    def gather(src_hbm, idx_hbm, out_hbm, idx_a, idx_b, rows_a, rows_b, ga, gb, wa, wb):
        base = _sc_worker_id() * per_worker

        @pl.loop(0, per_worker // SC_ROWS)
        def _(j):
            off_a = base + j * SC_ROWS
            off_b = off_a + half
            pltpu.sync_copy(idx_hbm.at[pl.ds(off_a, half)], idx_a)
            gather_a = pltpu.async_copy(src_hbm.at[idx_a], rows_a, ga)
            pltpu.sync_copy(idx_hbm.at[pl.ds(off_b, half)], idx_b)
            gather_b = pltpu.async_copy(src_hbm.at[idx_b], rows_b, gb)
            gather_a.wait()
            write_a = pltpu.async_copy(rows_a, out_hbm.at[pl.ds(off_a, half)], wa)
            gather_b.wait()
            write_b = pltpu.async_copy(rows_b, out_hbm.at[pl.ds(off_b, half)], wb)
            write_a.wait()
            write_b.wait()

    return gather(src, idx)


def _moe_kernel(blk_e_ref, blk_first_ref, blk_rows_ref, blk_next_ref,
                xs_ref, wgu_hbm, bgu_ref, wd_hbm, bd_ref,
                o_ref, wgu_f32_ref, wd_f32_ref, wgu_bf_ref, wd_bf_ref, sem):
    b = pl.program_id(0)
    n_rows = blk_rows_ref[b]

    def weight_copies(e):
        return (pltpu.make_async_copy(wgu_hbm.at[e], wgu_f32_ref, sem.at[0]),
                pltpu.make_async_copy(wd_hbm.at[e], wd_f32_ref, sem.at[1]))

    @pl.when(b == 0)
    def _():
        for copy in weight_copies(blk_e_ref[0]):
            copy.start()

    @pl.when(blk_first_ref[b] == 1)
    def _():
        for copy in weight_copies(blk_e_ref[b]):
            copy.wait()
        wgu_bf_ref[...] = wgu_f32_ref[...].astype(BF16)
        wd_bf_ref[...] = wd_f32_ref[...].astype(BF16)

        @pl.when(blk_next_ref[b] >= 0)
        def _():
            for copy in weight_copies(blk_next_ref[b]):
                copy.start()

    half = MOE_BLOCK // 2

    def ffn(row0):
        rows = slice(row0, row0 + half)
        live = lax.broadcasted_iota(jnp.int32, (half, xs_ref.shape[1]), 0) < n_rows - row0
        xs = _unpack_rows(jnp.where(live, xs_ref[rows, :], 0))
        gu = jnp.dot(xs.astype(BF16), wgu_bf_ref[...],
                     preferred_element_type=F32) + bgu_ref[0]
        x_glu = jnp.minimum(gu[:, :D_FF], SWIGLU_LIMIT)
        x_lin = jnp.clip(gu[:, D_FF:], -SWIGLU_LIMIT, SWIGLU_LIMIT)
        act = x_glu * _sigmoid(SWIGLU_ALPHA * x_glu) * (x_lin + 1.0)
        o_ref[rows, :] = _pack_rows(jnp.dot(act.astype(BF16), wd_bf_ref[...],
                                            preferred_element_type=F32) + bd_ref[0])

    for row0 in (0, half):
        @pl.when(n_rows > row0)
        def _(row0=row0):
            ffn(row0)

        @pl.when(n_rows <= row0)
        def _(row0=row0):
            o_ref[row0:row0 + half, :] = jnp.zeros((half, o_ref.shape[1]), jnp.int32)


def _moe_call(blk_e, blk_first, blk_rows, blk_next, xs, wgu, bgu, wd, bd):
    n_slots, half = xs.shape
    d = 2 * half
    ex = lambda b, be, bf, br, bn: (be[b], 0, 0)
    row = lambda b, be, bf, br, bn: (b, 0)
    grid_spec = pltpu.PrefetchScalarGridSpec(
        num_scalar_prefetch=4,
        grid=(n_slots // MOE_BLOCK,),
        in_specs=[pl.BlockSpec((MOE_BLOCK, half), row),
                  pl.BlockSpec(memory_space=pl.ANY),
                  pl.BlockSpec((1, 1, 2 * D_FF), ex),
                  pl.BlockSpec(memory_space=pl.ANY),
                  pl.BlockSpec((1, 1, d), ex)],
        out_specs=pl.BlockSpec((MOE_BLOCK, half), row),
        scratch_shapes=[pltpu.VMEM((d, 2 * D_FF), F32), pltpu.VMEM((D_FF, d), F32),
                        pltpu.VMEM((d, 2 * D_FF), BF16), pltpu.VMEM((D_FF, d), BF16),
                        pltpu.SemaphoreType.DMA((2,))])
    return pl.pallas_call(
        _moe_kernel,
        grid_spec=grid_spec,
        out_shape=jax.ShapeDtypeStruct((n_slots, half), jnp.int32),
        compiler_params=_params("arbitrary"),
        name="moe_experts",
    )(blk_e, blk_first, blk_rows, blk_next, xs, wgu, bgu, wd, bd)


def _final_kernel(x1_ref, y4_ref, route_ref, mod_ref, g_ref, o_ref):
    m = mod_ref[0]
    route = route_ref[...]
    y = jnp.zeros(x1_ref.shape, F32)
    for q in range(TOP_K):
        y = y + route[:, TOP_K + q:TOP_K + q + 1] * _unpack_rows(y4_ref[q])
    o_ref[...] = x1_ref[...] + m[5:6] * _rms(y, g_ref[...])


def _final_call(x1, y4, route, mod, g, seq_len):
    n, d = x1.shape
    row = lambda i: (i, 0)
    return pl.pallas_call(
        _final_kernel,
        grid=(n // TM_FIN,),
        in_specs=[pl.BlockSpec((TM_FIN, d), row),
                  pl.BlockSpec((TOP_K, TM_FIN, d // 2), lambda i: (0, i, 0)),
                  pl.BlockSpec((TM_FIN, LANES), row),
                  pl.BlockSpec((1, N_MOD, d), _mod_index(TM_FIN, seq_len, mod.shape[0])),
                  pl.BlockSpec((1, d), lambda i: (0, 0))],
        out_specs=pl.BlockSpec((TM_FIN, d), row),
        out_shape=jax.ShapeDtypeStruct((n, d), F32),
        compiler_params=_params("arbitrary"),
        name="final_residual",
    )(x1, y4, route, mod, g)


def _pack_state(s):
    b = s.shape[0]
    st = jnp.swapaxes(s.astype(F32), -1, -2).reshape(b, N_PAIRS, 2, HEAD_DIM, HEAD_DIM)
    z = jnp.zeros_like(st[:, :, 0])
    top = jnp.concatenate([st[:, :, 0], z], axis=-1)
    bot = jnp.concatenate([z, st[:, :, 1]], axis=-1)
    return jnp.concatenate([top, bot], axis=-2)


def _mix_sublayer(x, mod, s0f, s0b, seg_len, wts):
    batch, seq_len, d = x.shape
    x2 = x.reshape(batch * seq_len, d)
    cv, rkv, lora = _in_call(x2, mod, wts["pre_mix_g"], wts["w_all"], seq_len)
    conv_out = _conv_call(cv, wts["conv_w"], wts["conv_b"], wts["conv_ln_g"], wts["conv_ln_b"],
                          seg_len)
    yf, yb, sf, sb = _wkv_call(rkv, lora, s0f, s0b, wts["w2p"], wts["w0"], wts["a2p"], wts["a0"],
                               wts["k_k"], wts["k_a"], batch, seq_len)
    x1, h2, route, route_t, cnt = _out_call(x2, mod, conv_out, yf, yb, rkv, lora, wts, seq_len)
    return x1, h2, route, cnt, sf, sb, route_t


def _dispatch_plan(routes, cnts):
    experts = jnp.arange(N_EXPERTS, dtype=jnp.int32)
    counts = [c[0, :N_EXPERTS].astype(jnp.int32) for c in cnts]
    total = sum(counts)
    padded = (total + MOE_BLOCK - 1) // MOE_BLOCK * MOE_BLOCK
    pad_end = jnp.cumsum(padded)
    pad_start = pad_end - padded
    dests = []
    prior = jnp.zeros_like(total)
    for route, count in zip(routes, counts):
        idx = route[:TOP_K].astype(jnp.int32)
        rank = route[2 * TOP_K:3 * TOP_K].astype(jnp.int32)
        first = jnp.sum(jnp.where(idx[..., None] == experts, pad_start + prior, 0), axis=-1)
        dests.append(first + rank)
        prior = prior + count
    dest = jnp.concatenate(dests, axis=1).reshape(-1)
    n_slots = dest.shape[0] + N_EXPERTS * MOE_BLOCK
    blk_row0 = jnp.arange(n_slots // MOE_BLOCK, dtype=jnp.int32) * MOE_BLOCK
    blk_e = jnp.sum((pad_end[None, :] <= blk_row0[:, None]).astype(jnp.int32), axis=-1)
    blk_e = jnp.minimum(blk_e, N_EXPERTS - 1)
    blk_first = jnp.concatenate([jnp.ones((1,), jnp.int32),
                                 (blk_e[1:] != blk_e[:-1]).astype(jnp.int32)])
    of_blk = lambda per_expert: jnp.sum(
        jnp.where(blk_e[:, None] == experts[None, :], per_expert[None, :], 0), axis=-1)
    blk_rows = jnp.clip(of_blk(pad_start + total) - blk_row0, 0, MOE_BLOCK).astype(jnp.int32)
    pos = jnp.arange(blk_e.shape[0], dtype=jnp.int32)
    later_first = (pos[None, :] > pos[:, None]) & (blk_first[None, :] == 1)
    next_pos = jnp.min(jnp.where(later_first, pos[None, :], pos.shape[0]), axis=-1)
    blk_next = jnp.where(next_pos < pos.shape[0],
                         blk_e[jnp.minimum(next_pos, pos.shape[0] - 1)], -1).astype(jnp.int32)
    return dest, n_slots, blk_e, blk_first, blk_rows, blk_next


def _routed_ffn(h2s, routes, cnts, wts):
    dest, n_slots, blk_e, blk_first, blk_rows, blk_next = _dispatch_plan(routes, cnts)
    xs = _sc_scatter_rows(h2s[0], h2s[1], dest, n_slots)
    out_sorted = _moe_call(blk_e, blk_first, blk_rows, blk_next, xs, wts["w_gu"], wts["b_gu"],
                           wts["w_down"], wts["b_down"])
    dest_qt = dest.reshape(TOP_K, -1)
    y4s, row0 = [], 0
    for h2 in h2s:
        n_path = h2.shape[0]
        y4 = _sc_gather_rows(out_sorted, dest_qt[:, row0:row0 + n_path].reshape(-1))
        y4s.append(y4.reshape(TOP_K, n_path, xs.shape[1]))
        row0 += n_path
    return y4s


def _prep_weights(l, pre_mix_g, post_mix_g, pre_ffn_g, post_ffn_g, w_in, w_out, conv_w, conv_b,
                  conv_ln_g, conv_ln_b, rw_w0, rw_w1, rw_w2, rw_a0, rw_a1, rw_a2, rw_g1, rw_g2,
                  rw_k_k, rw_k_a, rw_r_k, rw_gn_g, rw_gn_b, router_w, router_b, w_gu, b_gu,
                  w_down, b_down):
    row = lambda a: a.reshape(1, -1).astype(F32)
    zpad = jnp.zeros((HEAD_DIM, RWKV_DIM), F32)
    w_all = jnp.concatenate([w_in[l], rw_w1[l, 0], rw_w1[l, 1], rw_a1[l, 0], rw_a1[l, 1],
                             rw_g1[l]], axis=1).astype(BF16)
    w2p = jnp.stack([jnp.concatenate([rw_w2[l, 0], zpad], axis=0),
                     jnp.concatenate([zpad, rw_w2[l, 1]], axis=0)]).astype(BF16)
    a2p = jnp.stack([jnp.concatenate([rw_a2[l, 0], zpad], axis=0),
                     jnp.concatenate([zpad, rw_a2[l, 1]], axis=0)]).astype(BF16)
    rw_pad = jnp.pad(router_w[l].astype(F32), ((0, 0), (0, LANES - N_EXPERTS)))
    rw_hi = rw_pad.astype(BF16)
    rw_lo = (rw_pad - rw_hi.astype(F32)).astype(BF16)
    rb_pad = jnp.concatenate([router_b[l].astype(F32),
                              jnp.full((LANES - N_EXPERTS,), NEG_BIG, F32)]).reshape(1, LANES)
    return {
        "pre_mix_g": row(pre_mix_g[l]), "post_mix_g": row(post_mix_g[l]),
        "pre_ffn_g": row(pre_ffn_g[l]), "post_ffn_g": row(post_ffn_g[l]),
        "w_all": w_all, "w_out": w_out[l].astype(BF16),
        "conv_w": jnp.broadcast_to(conv_w[l].astype(F32)[:, None, :],
                                   (CONV_WIDTH, SUBLANES, CONV_CH)),
        "conv_b": row(conv_b[l]),
        "conv_ln_g": row(conv_ln_g[l]), "conv_ln_b": row(conv_ln_b[l]),
        "w2p": w2p, "w0": rw_w0[l].astype(F32), "a2p": a2p, "a0": rw_a0[l].astype(F32),
        "k_k": row(rw_k_k[l]), "k_a": row(rw_k_a[l]), "r_k": row(rw_r_k[l]),
        "gn_g": row(rw_gn_g[l]), "gn_b": row(rw_gn_b[l]), "g2": rw_g2[l].astype(BF16),
        "router_w": jnp.stack([rw_hi, rw_lo]), "router_b": rb_pad,
        "w_gu": w_gu[l].astype(F32), "b_gu": b_gu[l].reshape(N_EXPERTS, 1, -1).astype(F32),
        "w_down": w_down[l].astype(F32), "b_down": b_down[l].reshape(N_EXPERTS, 1, -1).astype(F32),
    }


def kernel(x_prompt, x_sample, state_wkv_fwd, state_wkv_bwd, c, c_ctx, ada_w, ada_b, pre_mix_g, post_mix_g, pre_ffn_g, post_ffn_g, w_in, w_out, conv_w, conv_b, conv_ln_g, conv_ln_b, rw_w0, rw_w1, rw_w2, rw_a0, rw_a1, rw_a2, rw_g1, rw_g2, rw_k_k, rw_k_a, rw_r_k, rw_gn_g, rw_gn_b, router_w, router_b, w_gu, b_gu, w_down, b_down):
    depth = ada_w.shape[0]
    dec_batch = c.shape[0]
    mod_rows = 16
    c_rows = jnp.concatenate([c, c_ctx[None, :],
                              jnp.zeros((mod_rows - dec_batch - 1, D_MODEL), F32)], axis=0)
    zero_state = jnp.zeros((WKV_SEQS, N_PAIRS, LANES, LANES), F32)
    y_prompt, y_sample = x_prompt, x_sample
    new_f, new_b = [], []
    for l in range(depth):
        wts = _prep_weights(l, pre_mix_g, post_mix_g, pre_ffn_g, post_ffn_g, w_in, w_out, conv_w,
                            conv_b, conv_ln_g, conv_ln_b, rw_w0, rw_w1, rw_w2, rw_a0, rw_a1, rw_a2,
                            rw_g1, rw_g2, rw_k_k, rw_k_a, rw_r_k, rw_gn_g, rw_gn_b, router_w,
                            router_b, w_gu, b_gu, w_down, b_down)
        mod = _mod_call(c_rows, ada_w[l], ada_b[l]).reshape(mod_rows, N_MOD, D_MODEL)
        mod_lat = mod[:dec_batch]
        mod_ctx = mod[dec_batch:dec_batch + 1]
        ctx = _mix_sublayer(y_prompt, mod_ctx, zero_state, zero_state, y_prompt.shape[1], wts)
        lat = _mix_sublayer(y_sample, mod_lat, _pack_state(state_wkv_fwd[:, l]),
                            _pack_state(state_wkv_bwd[:, l]), GRID_W, wts)
        new_f.append(ctx[4])
        new_b.append(ctx[5])
        y4_ctx, y4_lat = _routed_ffn([ctx[1], lat[1]], [ctx[6], lat[6]], [ctx[3], lat[3]], wts)
        y_prompt = _final_call(ctx[0], y4_ctx, ctx[2], mod_ctx, wts["post_ffn_g"],
                               y_prompt.shape[1]).reshape(y_prompt.shape)
        y_sample = _final_call(lat[0], y4_lat, lat[2], mod_lat, wts["post_ffn_g"],
                               y_sample.shape[1]).reshape(y_sample.shape)
    return (y_prompt, y_sample, jnp.stack(new_f, axis=1), jnp.stack(new_b, axis=1))
```
